```python
import math
import jax, jax.numpy as jnp
from jax import lax
import numpy as np

D_MODEL = 1024
BATCH = 4
SEQ = 8192
DEPTH = 4
DEC_BATCH = 2
DEC_SEQ = 8192
PAST_LEN = 128

D_MIX = D_MODEL
N_GROUPS = 4
GROUP_W = D_MIX // N_GROUPS
HEAD_DIM = 64
A_HEADS = GROUP_W // HEAD_DIM
CHUNK = 128
CONV_W = 3
C_HEADS = GROUP_W // HEAD_DIM
GRID_W = 64
WIN_R = 8
WIN_C = 16
D_HEADS = 4
Q_RANK = 256
KV_RANK = 128
NOPE_DIM = 64
ROPE_DIM = 32
V_DIM = 64
ROPE_THETA = 10000.0
Q_BLOCK = 128
N_EXPERTS = 16
D_EXPERT = 1024
CAPACITY_FACTOR = 2
EPS = 1e-6

A_COLS = 2 * GROUP_W
B_COLS = 3 * GROUP_W
C_COLS = 3 * C_HEADS * HEAD_DIM
D_COLS = Q_RANK + KV_RANK + ROPE_DIM
N_IN = A_COLS + B_COLS + C_COLS + D_COLS

kernel_name = "hybrid_parallel_group_encoder"


def rmsnorm(x, g):
    xf = x.astype(jnp.float32)
    y = xf * lax.rsqrt(jnp.mean(xf * xf, axis=-1, keepdims=True) + EPS)
    return (y * g.astype(jnp.float32)).astype(x.dtype)


def mixer_a(z, vnorm_g, ws, bs):
    b, l, _ = z.shape
    z = jax.nn.gelu(z)
    u, v = jnp.split(z, 2, axis=-1)
    v = rmsnorm(v, vnorm_g).reshape(b, l // CHUNK, CHUNK, A_HEADS, HEAD_DIM)
    sv = jnp.einsum('hpq,bnqhd->bnphd', ws, v) + bs.T[:, :, None]
    return u * sv.reshape(b, l, GROUP_W)


def mixer_b(z, conv_w):
    l = z.shape[1]
    bg, cg, h = jnp.split(z, 3, axis=-1)
    y = cg * h
    half = CONV_W // 2
    yp = jnp.pad(y, ((0, 0), (half, half), (0, 0)))
    yc = sum(conv_w[j] * yp[:, j:j + l] for j in range(CONV_W))
    return bg * yc


def natten_geometry(rows):
    wr = min(WIN_R, rows)
    r = np.arange(rows)
    rs = np.clip(r - wr // 2, 0, rows - wr)
    row_idx = rs[:, None] + np.arange(wr)[None, :]
    dr = row_idx - r[:, None] + (WIN_R - 1)
    c = np.arange(GRID_W)
    cs = np.clip(c - WIN_C // 2, 0, GRID_W - WIN_C)
    kc = np.arange(GRID_W)
    col_mask = (kc[None, :] >= cs[:, None]) & (kc[None, :] < cs[:, None] + WIN_C)
    dc = np.clip(kc[None, :] - c[:, None], -(WIN_C - 1), WIN_C - 1) + (WIN_C - 1)
    return wr, row_idx, dr, col_mask, dc


def mixer_c(z, rpb):
    b, l, _ = z.shape
    rows = l // GRID_W
    wr, row_idx, dr, col_mask, dc = natten_geometry(rows)
    q, k, v = jnp.split(z, 3, axis=-1)
    shp = (b, rows, GRID_W, C_HEADS, HEAD_DIM)
    q = q.reshape(shp) * (HEAD_DIM ** -0.5)
    k = k.reshape(shp)[:, row_idx]
    v = v.reshape(shp)[:, row_idx]
    s = jnp.einsum('brchd,brwkhd->bhrcwk', q, k).astype(jnp.float32)
    bias = rpb[:, dr[:, None, :, None], dc[None, :, None, :]].astype(jnp.float32)
    bias = jnp.where(col_mask[None, None, :, None, :], bias, -jnp.inf)
    p = jax.nn.softmax(s + bias[None], axis=(-2, -1)).astype(v.dtype)
    o = jnp.einsum('bhrcwk,brwkhd->brchd', p, v)
    return o.reshape(b, l, C_HEADS * HEAD_DIM)


def rope_tables(l, dtype):
    inv = 1.0 / (ROPE_THETA ** (jnp.arange(0, ROPE_DIM, 2, dtype=jnp.float32) / ROPE_DIM))
    ang = jnp.arange(l, dtype=jnp.float32)[:, None] * inv[None, :]
    return jnp.cos(ang).astype(dtype), jnp.sin(ang).astype(dtype)


def apply_rope(x, cos, sin):
    x1, x2 = jnp.split(x, 2, axis=-1)
    return jnp.concatenate([x1 * cos - x2 * sin, x2 * cos + x1 * sin], axis=-1)


def mixer_d(z, qnorm_g, kvnorm_g, w_uq, w_ukv):
    b, l, _ = z.shape
    cq, ckv, kr = jnp.split(z, [Q_RANK, Q_RANK + KV_RANK], axis=-1)
    q = (rmsnorm(cq, qnorm_g) @ w_uq).reshape(b, l, D_HEADS, NOPE_DIM + ROPE_DIM)
    kv = (rmsnorm(ckv, kvnorm_g) @ w_ukv).reshape(b, l, D_HEADS, NOPE_DIM + V_DIM)
    q_nope, q_rope = jnp.split(q, [NOPE_DIM], axis=-1)
    k_nope, v = jnp.split(kv, [NOPE_DIM], axis=-1)
    cos, sin = rope_tables(l, z.dtype)
    q_rope = apply_rope(q_rope, cos[:, None, :], sin[:, None, :])
    k_rope = apply_rope(kr, cos, sin)
    scale = (NOPE_DIM + ROPE_DIM) ** -0.5
    nblk = l // Q_BLOCK

    def block(args):
        qn, qr = args
        s = (jnp.einsum('bqhd,bkhd->bhqk', qn, k_nope)
             + jnp.einsum('bqhr,bkr->bhqk', qr, k_rope)).astype(jnp.float32) * scale
        p = jax.nn.softmax(s, axis=-1).astype(v.dtype)
        return jnp.einsum('bhqk,bkhd->bqhd', p, v)

    qn_b = q_nope.reshape(b, nblk, Q_BLOCK, D_HEADS, NOPE_DIM).swapaxes(0, 1)
    qr_b = q_rope.reshape(b, nblk, Q_BLOCK, D_HEADS, ROPE_DIM).swapaxes(0, 1)
    o = lax.map(block, (qn_b, qr_b))
    return o.swapaxes(0, 1).reshape(b, l, D_HEADS * V_DIM)


def expert_choice_ffn(x, router_w, w_gate, w_up, w_down):
    b, l, d = x.shape
    n = b * l
    cap = CAPACITY_FACTOR * n // N_EXPERTS
    xt = x.reshape(n, d)
    aff = jax.nn.softmax((xt @ router_w).astype(jnp.float32), axis=-1)
    g, idx = lax.top_k(aff.T, cap)
    xe = xt[idx]
    h = jax.nn.silu(jnp.einsum('ecd,edf->ecf', xe, w_gate)) * jnp.einsum('ecd,edf->ecf', xe, w_up)
    ye = jnp.einsum('ecf,efd->ecd', h, w_down) * g[..., None].astype(x.dtype)
    out = jnp.zeros_like(xt).at[idx.reshape(-1)].add(ye.reshape(-1, d))
    return out.reshape(b, l, d)


def layer(x, p):
    b, l, _ = x.shape
    h = rmsnorm(x, p['norm1_g'])
    z = h @ p['w_in']
    za, zb, zc, zd = jnp.split(z, [A_COLS, A_COLS + B_COLS, A_COLS + B_COLS + C_COLS], axis=-1)
    ya = mixer_a(za, p['a_vnorm_g'], p['a_ws'], p['a_bs'])
    yb = mixer_b(zb, p['b_conv'])
    yc = mixer_c(zc, p['c_rpb'])
    yd = mixer_d(zd, p['d_qnorm_g'], p['d_kvnorm_g'], p['d_w_uq'], p['d_w_ukv'])
    y = jnp.concatenate([ya, yb, yc, yd], axis=-1).reshape(b, l, N_GROUPS, GROUP_W)
    y = rmsnorm(y, p['out_norm_g'].reshape(N_GROUPS, GROUP_W)).reshape(b, l, D_MIX)
    x = x + y @ p['w_o']
    x = x + expert_choice_ffn(rmsnorm(x, p['norm2_g']), p['router_w'],
                              p['e_w_gate'], p['e_w_up'], p['e_w_down'])
    return x


def setup_inputs(seed: int = 0) -> dict:
    key = jax.random.key(seed)
    ks = jax.random.split(key, 24)
    f32 = jnp.float32

    def nrm(k, shape, scale):
        return jax.random.normal(k, shape, f32) * scale

    def gain(k, shape):
        return 1.0 + 0.05 * jax.random.normal(k, shape, f32)

    return {
        "x_prompt": nrm(ks[0], (BATCH, SEQ, D_MODEL), 1.0),
        "x_sample": nrm(ks[1], (DEC_BATCH, DEC_SEQ, D_MODEL), 1.0),
        "norm1_g": gain(ks[2], (DEPTH, D_MODEL)),
        "w_in": nrm(ks[3], (DEPTH, D_MODEL, N_IN), D_MODEL ** -0.5),
        "a_vnorm_g": gain(ks[4], (DEPTH, GROUP_W)),
        "a_ws": nrm(ks[5], (DEPTH, A_HEADS, CHUNK, CHUNK), CHUNK ** -0.5),
        "a_bs": gain(ks[6], (DEPTH, A_HEADS, CHUNK)),
        "b_conv": nrm(ks[7], (DEPTH, CONV_W, GROUP_W), CONV_W ** -0.5),
        "c_rpb": nrm(ks[8], (DEPTH, C_HEADS, 2 * WIN_R - 1, 2 * WIN_C - 1), 0.1),
        "d_qnorm_g": gain(ks[9], (DEPTH, Q_RANK)),
        "d_kvnorm_g": gain(ks[10], (DEPTH, KV_RANK)),
        "d_w_uq": nrm(ks[11], (DEPTH, Q_RANK, D_HEADS * (NOPE_DIM + ROPE_DIM)), Q_RANK ** -0.5),
        "d_w_ukv": nrm(ks[12], (DEPTH, KV_RANK, D_HEADS * (NOPE_DIM + V_DIM)), KV_RANK ** -0.5),
        "out_norm_g": gain(ks[13], (DEPTH, D_MIX)),
        "w_o": nrm(ks[14], (DEPTH, D_MIX, D_MODEL), D_MIX ** -0.5),
        "norm2_g": gain(ks[15], (DEPTH, D_MODEL)),
        "router_w": nrm(ks[16], (DEPTH, D_MODEL, N_EXPERTS), D_MODEL ** -0.5),
        "e_w_gate": nrm(ks[17], (DEPTH, N_EXPERTS, D_MODEL, D_EXPERT), D_MODEL ** -0.5),
        "e_w_up": nrm(ks[18], (DEPTH, N_EXPERTS, D_MODEL, D_EXPERT), D_MODEL ** -0.5),
        "e_w_down": nrm(ks[19], (DEPTH, N_EXPERTS, D_EXPERT, D_MODEL), D_EXPERT ** -0.5),
        "final_norm_g": gain(ks[20], (D_MODEL,)),
    }


def reference(x_prompt, x_sample, norm1_g, w_in, a_vnorm_g, a_ws, a_bs, b_conv, c_rpb,
              d_qnorm_g, d_kvnorm_g, d_w_uq, d_w_ukv, out_norm_g, w_o, norm2_g,
              router_w, e_w_gate, e_w_up, e_w_down, final_norm_g):
    def trunk(x):
        for i in range(DEPTH):
            p = {
                'norm1_g': norm1_g[i], 'w_in': w_in[i],
                'a_vnorm_g': a_vnorm_g[i], 'a_ws': a_ws[i], 'a_bs': a_bs[i],
                'b_conv': b_conv[i], 'c_rpb': c_rpb[i],
                'd_qnorm_g': d_qnorm_g[i], 'd_kvnorm_g': d_kvnorm_g[i],
                'd_w_uq': d_w_uq[i], 'd_w_ukv': d_w_ukv[i],
                'out_norm_g': out_norm_g[i], 'w_o': w_o[i], 'norm2_g': norm2_g[i],
                'router_w': router_w[i], 'e_w_gate': e_w_gate[i],
                'e_w_up': e_w_up[i], 'e_w_down': e_w_down[i],
            }
            x = layer(x, p)
        return rmsnorm(x, final_norm_g)

    y_prompt = trunk(x_prompt)
    y_sample = trunk(x_sample)
    return (y_prompt, y_sample)
```

```python
import functools
import math

import numpy as np
import jax
import jax.numpy as jnp
from jax import lax
from jax.experimental import pallas as pl
from jax.experimental.pallas import tpu as pltpu

D_MODEL = 1024
BATCH = 4
SEQ = 8192
DEPTH = 4
DEC_BATCH = 2
GROUP_W = 256
HEAD_DIM = 64
A_HEADS = 4
CHUNK = 128
C_HEADS = 4
GRID_W = 64
WIN_R = 8
WIN_C = 16
D_HEADS = 4
Q_RANK = 256
KV_RANK = 128
NOPE_DIM = 64
ROPE_DIM = 32
V_DIM = 64
ROPE_THETA = 10000.0
N_EXPERTS = 16
D_EXPERT = 1024
CAPACITY_FACTOR = 2
EPS = 1e-6

NSEQ = BATCH + DEC_BATCH
T_ALL = NSEQ * SEQ
ROWS = SEQ // GRID_W
LANES = 128
NEG = -1e30

A_COLS = 2 * GROUP_W
B_COLS = 3 * GROUP_W
C_COLS = 3 * GROUP_W
OFF_B = A_COLS
OFF_C = OFF_B + B_COLS
OFF_D = OFF_C + C_COLS
D_PACK = Q_RANK + KV_RANK + 2 * LANES
N_IN_PACK = OFF_D + D_PACK

TM = 512
TQ = 256
TK = 512
ROWS_PER_STEP = 8
TME = 512
VMEM_LIMIT = 56 * 1024 * 1024

f32 = jnp.float32
bf16 = jnp.bfloat16


def _rms(x, g):
    return x * lax.rsqrt(jnp.mean(x * x, axis=-1, keepdims=True) + EPS) * g


def _dot(a, b):
    return jnp.dot(a, b, preferred_element_type=f32)


def _dot_nt(a, b):
    return lax.dot_general(a, b, (((1,), (1,)), ((), ())), preferred_element_type=f32)


def _inproj_kernel(x_ref, xp_ref, xn_ref, g1_ref, win_ref, avg_ref, aws_ref, abias_ref,
                   bconv_ref, qg_ref, kvg_ref, wuq_ref, wuqr_ref, wk_ref, wv_ref,
                   cos_ref, sin_ref,
                   ya_ref, yb_ref, zc_ref, q4_ref, k4_ref, v2_ref):
    i = pl.program_id(0)
    j = i % (SEQ // TM)
    g1 = g1_ref[...]
    h = _rms(x_ref[...], g1).astype(bf16)

    za = jax.nn.gelu(_dot(h, win_ref[:, 0:A_COLS]))
    u = za[:, :GROUP_W]
    v = _rms(za[:, GROUP_W:], avg_ref[...]).astype(bf16)
    lane = lax.broadcasted_iota(jnp.int32, (CHUNK, GROUP_W), 1)
    for c in range(TM // CHUNK):
        vc = v[c * CHUNK:(c + 1) * CHUNK, :]
        sv = abias_ref[...]
        for hd in range(A_HEADS):
            r = _dot(aws_ref[hd], vc)
            sv = sv + jnp.where((lane >= hd * HEAD_DIM) & (lane < (hd + 1) * HEAD_DIM), r, 0.0)
        ya_ref[c * CHUNK:(c + 1) * CHUNK, :] = u[c * CHUNK:(c + 1) * CHUNK, :] * sv

    zb = _dot(h, win_ref[:, OFF_B:OFF_B + B_COLS])
    bg = zb[:, :GROUP_W]
    y = zb[:, GROUP_W:2 * GROUP_W] * zb[:, 2 * GROUP_W:]
    xh = jnp.concatenate([xp_ref[...], xn_ref[...]], axis=0)
    hh = _rms(xh, g1).astype(bf16)
    zh = _dot(hh, win_ref[:, OFF_B + GROUP_W:OFF_B + B_COLS])
    yh = zh[:, :GROUP_W] * zh[:, GROUP_W:]
    y_before = jnp.where(j == 0, 0.0, yh[7:8, :])
    y_after = jnp.where(j == SEQ // TM - 1, 0.0, yh[8:9, :])
    row = lax.broadcasted_iota(jnp.int32, (TM, GROUP_W), 0)
    y_m1 = jnp.where(row == 0, y_before, pltpu.roll(y, 1, axis=0))
    y_p1 = jnp.where(row == TM - 1, y_after, pltpu.roll(y, TM - 1, axis=0))
    wc = bconv_ref[...]
    yb_ref[...] = bg * (wc[0:1, :] * y_m1 + wc[1:2, :] * y + wc[2:3, :] * y_p1)

    zc = _dot(h, win_ref[:, OFF_C:OFF_C + C_COLS])
    zc_ref[:, 0:GROUP_W] = (zc[:, 0:GROUP_W] * (HEAD_DIM ** -0.5)).astype(bf16)
    zc_ref[:, GROUP_W:] = zc[:, GROUP_W:].astype(bf16)

    zd = _dot(h, win_ref[:, OFF_D:OFF_D + D_PACK])
    cqn = _rms(zd[:, 0:Q_RANK], qg_ref[...]).astype(bf16)
    ckvn = _rms(zd[:, Q_RANK:Q_RANK + KV_RANK], kvg_ref[...]).astype(bf16)
    cos = cos_ref[...]
    sin = sin_ref[...]
    o = Q_RANK + KV_RANK
    kr = zd[:, o:o + LANES] * cos + zd[:, o + LANES:o + 2 * LANES] * sin
    q_pre = _dot(cqn, wuq_ref[...])
    q_rot = _dot(cqn, wuqr_ref[...])
    kn = _dot(ckvn, wk_ref[...])
    for hd in range(D_HEADS):
        sl = slice(hd * LANES, (hd + 1) * LANES)
        q4_ref[hd] = (q_pre[:, sl] * cos + q_rot[:, sl] * sin).astype(bf16)
        k4_ref[hd] = (kn[:, sl] + kr).astype(bf16)
    vv = _dot(ckvn, wv_ref[...])
    v2_ref[0] = vv[:, :LANES].astype(bf16)
    v2_ref[1] = vv[:, LANES:].astype(bf16)


def _inproj(x, lw, cosq, sinq):
    nt = T_ALL // TM
    tps = SEQ // TM
    full = lambda shape: pl.BlockSpec(shape, lambda i: (0,) * len(shape))
    in_specs = [
        pl.BlockSpec((TM, D_MODEL), lambda i: (i, 0)),
        pl.BlockSpec((8, D_MODEL), lambda i: (jnp.maximum(i * (TM // 8) - 1, 0), 0)),
        pl.BlockSpec((8, D_MODEL), lambda i: (jnp.minimum((i + 1) * (TM // 8), T_ALL // 8 - 1), 0)),
        full((1, D_MODEL)),
        full((D_MODEL, N_IN_PACK)),
        full((1, GROUP_W)),
        full((A_HEADS, CHUNK, CHUNK)),
        full((CHUNK, GROUP_W)),
        full((3, GROUP_W)),
        full((1, Q_RANK)),
        full((1, KV_RANK)),
        full((Q_RANK, D_HEADS * LANES)),
        full((Q_RANK, D_HEADS * LANES)),
        full((KV_RANK, D_HEADS * LANES)),
        full((KV_RANK, D_HEADS * V_DIM)),
        pl.BlockSpec((TM, LANES), lambda i: (i % tps, 0)),
        pl.BlockSpec((TM, LANES), lambda i: (i % tps, 0)),
    ]
    out_specs = [
        pl.BlockSpec((TM, GROUP_W), lambda i: (i, 0)),
        pl.BlockSpec((TM, GROUP_W), lambda i: (i, 0)),
        pl.BlockSpec((TM, C_COLS), lambda i: (i, 0)),
        pl.BlockSpec((None, D_HEADS, TM, LANES), lambda i: (i // tps, 0, i % tps, 0)),
        pl.BlockSpec((None, D_HEADS, TM, LANES), lambda i: (i // tps, 0, i % tps, 0)),
        pl.BlockSpec((None, 2, TM, LANES), lambda i: (i // tps, 0, i % tps, 0)),
    ]
    out_shape = [
        jax.ShapeDtypeStruct((T_ALL, GROUP_W), f32),
        jax.ShapeDtypeStruct((T_ALL, GROUP_W), f32),
        jax.ShapeDtypeStruct((T_ALL, C_COLS), bf16),
        jax.ShapeDtypeStruct((NSEQ, D_HEADS, SEQ, LANES), bf16),
        jax.ShapeDtypeStruct((NSEQ, D_HEADS, SEQ, LANES), bf16),
        jax.ShapeDtypeStruct((NSEQ, 2, SEQ, LANES), bf16),
    ]
    return pl.pallas_call(
        _inproj_kernel, grid=(nt,), in_specs=in_specs, out_specs=out_specs, out_shape=out_shape,
        name="inproj",
        compiler_params=pltpu.CompilerParams(dimension_semantics=("parallel",),
                                             vmem_limit_bytes=VMEM_LIMIT),
    )(x, x, x, lw["g1"], lw["w_in"], lw["a_vg"], lw["a_ws"], lw["a_bias"], lw["b_conv"],
      lw["d_qg"], lw["d_kvg"], lw["w_uq"], lw["w_uq_rot"], lw["w_k"], lw["w_v"], cosq, sinq)


def _natten_kernel(q_ref, k_ref, v_ref, bias_ref, o_ref):
    jb = pl.program_id(1)
    lane = lax.broadcasted_iota(jnp.int32, (GRID_W, LANES), 1)
    lo = lane < HEAD_DIM
    nkeys = WIN_R * GRID_W

    def row_body(i, carry):
        r = jb * ROWS_PER_STEP + i
        rs = jnp.clip(r - WIN_R // 2, 0, ROWS - WIN_R)
        d = rs - r + (WIN_R - 1)
        kstart = pl.multiple_of(rs * GRID_W, GRID_W)
        qrow = pl.multiple_of(i * GRID_W, GRID_W)
        for s in range(2):
            sl = slice(s * LANES, (s + 1) * LANES)
            qs = q_ref[pl.ds(qrow, GRID_W), sl]
            ks = k_ref[pl.ds(kstart, nkeys), sl]
            vs = v_ref[pl.ds(kstart, nkeys), sl]
            zero = jnp.zeros_like(qs)
            qq = jnp.concatenate([jnp.where(lo, qs, zero), jnp.where(lo, zero, qs)], axis=0)
            sc = _dot_nt(qq, ks) + bias_ref[d, s]
            m = jnp.max(sc, axis=-1, keepdims=True)
            p = jnp.exp(sc - m)
            l = jnp.sum(p, axis=-1, keepdims=True)
            o = _dot(p.astype(bf16), vs) / l
            o_ref[pl.ds(qrow, GRID_W), sl] = jnp.where(lo, o[:GRID_W], o[GRID_W:])
        return carry

    lax.fori_loop(0, ROWS_PER_STEP, row_body, 0)


def _natten(zc, bias):
    nb = ROWS // ROWS_PER_STEP
    tq = ROWS_PER_STEP * GRID_W
    return pl.pallas_call(
        _natten_kernel, grid=(NSEQ, nb),
        in_specs=[
            pl.BlockSpec((tq, GROUP_W), lambda b, j: (b * nb + j, 0)),
            pl.BlockSpec((SEQ, GROUP_W), lambda b, j: (b, 1)),
            pl.BlockSpec((SEQ, GROUP_W), lambda b, j: (b, 2)),
            pl.BlockSpec((WIN_R, 2, 2 * GRID_W, WIN_R * GRID_W), lambda b, j: (0, 0, 0, 0)),
        ],
        out_specs=pl.BlockSpec((tq, GROUP_W), lambda b, j: (b * nb + j, 0)),
        out_shape=jax.ShapeDtypeStruct((T_ALL, GROUP_W), f32),
        name="natten",
        compiler_params=pltpu.CompilerParams(dimension_semantics=("parallel", "arbitrary"),
                                             vmem_limit_bytes=VMEM_LIMIT),
    )(zc, zc, zc, bias)


def _mla_kernel(q_ref, k_ref, v_ref, o_ref):
    scale = (NOPE_DIM + ROPE_DIM) ** -0.5
    outs = []
    for hh in range(2):
        q = q_ref[hh]

        def body(c, carry):
            m, l, acc = carry
            ks = pl.multiple_of(c * TK, TK)
            k = k_ref[hh, pl.ds(ks, TK), :]
            v = v_ref[pl.ds(ks, TK), :]
            s = _dot_nt(q, k) * scale
            m_new = jnp.maximum(m, jnp.max(s, axis=-1, keepdims=True))
            alpha = jnp.exp(m - m_new)
            p = jnp.exp(s - m_new)
            l = l * alpha + jnp.sum(p, axis=-1, keepdims=True)
            acc = acc * alpha + _dot(p.astype(bf16), v)
            return m_new, l, acc

        init = (jnp.full((TQ, 1), NEG, f32), jnp.zeros((TQ, 1), f32), jnp.zeros((TQ, LANES), f32))
        m, l, acc = lax.fori_loop(0, SEQ // TK, body, init)
        outs.append(acc / l)
    lane = lax.broadcasted_iota(jnp.int32, (TQ, LANES), 1)
    o_ref[...] = jnp.where(lane < V_DIM, outs[0], outs[1])


def _mla(q4, k4, v2):
    nq = SEQ // TQ
    return pl.pallas_call(
        _mla_kernel, grid=(NSEQ, 2, nq),
        in_specs=[
            pl.BlockSpec((None, 2, TQ, LANES), lambda b, p, i: (b, p, i, 0)),
            pl.BlockSpec((None, 2, SEQ, LANES), lambda b, p, i: (b, p, 0, 0)),
            pl.BlockSpec((None, None, SEQ, LANES), lambda b, p, i: (b, p, 0, 0)),
        ],
        out_specs=pl.BlockSpec((TQ, LANES), lambda b, p, i: (b * nq + i, p)),
        out_shape=jax.ShapeDtypeStruct((T_ALL, GROUP_W), f32),
        name="mla",
        compiler_params=pltpu.CompilerParams(
            dimension_semantics=("parallel", "parallel", "arbitrary"),
            vmem_limit_bytes=VMEM_LIMIT),
    )(q4, k4, v2)


def _outproj_kernel(ya_ref, yb_ref, yc_ref, yd_ref, x_ref, og_ref, wo_ref, g2_ref, rw_ref,
                    xn_ref, h2_ref, aff_ref):
    acc = x_ref[...]
    for g, y_ref in enumerate((ya_ref, yb_ref, yc_ref, yd_ref)):
        sl = slice(g * GROUP_W, (g + 1) * GROUP_W)
        yg = _rms(y_ref[...], og_ref[:, sl]).astype(bf16)
        acc = acc + _dot(yg, wo_ref[sl, :])
    xn_ref[...] = acc
    h2 = _rms(acc, g2_ref[...]).astype(bf16)
    h2_ref[...] = h2
    logits = _dot(h2, rw_ref[...])
    lane = lax.broadcasted_iota(jnp.int32, logits.shape, 1)
    logits = jnp.where(lane < N_EXPERTS, logits, NEG)
    m = jnp.max(logits, axis=-1, keepdims=True)
    e = jnp.exp(logits - m)
    aff = e / jnp.sum(e, axis=-1, keepdims=True)
    aff_ref[...] = aff.T[:N_EXPERTS, :]


def _outproj(ya, yb, yc, yd, x, lw):
    nt = T_ALL // TM
    full = lambda shape: pl.BlockSpec(shape, lambda i: (0,) * len(shape))
    yspec = pl.BlockSpec((TM, GROUP_W), lambda i: (i, 0))
    xspec = pl.BlockSpec((TM, D_MODEL), lambda i: (i, 0))
    return pl.pallas_call(
        _outproj_kernel, grid=(nt,),
        in_specs=[yspec, yspec, yspec, yspec, xspec, full((1, D_MODEL)), full((D_MODEL, D_MODEL)),
                  full((1, D_MODEL)), full((D_MODEL, LANES))],
        out_specs=[xspec, xspec, pl.BlockSpec((N_EXPERTS, TM), lambda i: (0, i))],
        out_shape=[jax.ShapeDtypeStruct((T_ALL, D_MODEL), f32),
                   jax.ShapeDtypeStruct((T_ALL, D_MODEL), bf16),
                   jax.ShapeDtypeStruct((N_EXPERTS, T_ALL), f32)],
        name="outproj",
        compiler_params=pltpu.CompilerParams(dimension_semantics=("parallel",),
                                             vmem_limit_bytes=VMEM_LIMIT),
    )(ya, yb, yc, yd, x, lw["og"], lw["w_o"], lw["g2"], lw["rw"])


def _ffn_kernel(xe_ref, g_ref, wg_ref, wu_ref, wd_ref, ye_ref):
    x = xe_ref[...]
    hg = _dot(x, wg_ref[...])
    hu = _dot(x, wu_ref[...])
    h = (hg * (1.0 / (1.0 + jnp.exp(-hg))) * hu).astype(bf16)
    ye_ref[...] = _dot(h, wd_ref[...]) * g_ref[...]


def _ffn(xe, g, lw):
    cap = xe.shape[1]
    wspec = pl.BlockSpec((None, D_MODEL, D_EXPERT), lambda e, t: (e, 0, 0))
    return pl.pallas_call(
        _ffn_kernel, grid=(N_EXPERTS, cap // TME),
        in_specs=[
            pl.BlockSpec((None, TME, D_MODEL), lambda e, t: (e, t, 0)),
            pl.BlockSpec((None, TME, 1), lambda e, t: (e, t, 0)),
            wspec, wspec,
            pl.BlockSpec((None, D_EXPERT, D_MODEL), lambda e, t: (e, 0, 0)),
        ],
        out_specs=pl.BlockSpec((None, TME, D_MODEL), lambda e, t: (e, t, 0)),
        out_shape=jax.ShapeDtypeStruct((N_EXPERTS, cap, D_MODEL), f32),
        name="expert_ffn",
        compiler_params=pltpu.CompilerParams(dimension_semantics=("parallel", "arbitrary"),
                                             vmem_limit_bytes=VMEM_LIMIT),
    )(xe, g, lw["e_wg"], lw["e_wu"], lw["e_wd"])


def _final_kernel(x_ref, g_ref, o_ref):
    o_ref[...] = _rms(x_ref[...], g_ref[...])


def _final_norm(x, g, row0, nrows):
    off = row0 // TM
    return pl.pallas_call(
        _final_kernel, grid=(nrows // TM,),
        in_specs=[pl.BlockSpec((TM, D_MODEL), lambda i: (i + off, 0)),
                  pl.BlockSpec((1, D_MODEL), lambda i: (0, 0))],
        out_specs=pl.BlockSpec((TM, D_MODEL), lambda i: (i, 0)),
        out_shape=jax.ShapeDtypeStruct((nrows, D_MODEL), f32),
        name="final_norm",
        compiler_params=pltpu.CompilerParams(dimension_semantics=("parallel",)),
    )(x, g)


def _natten_bias_table(rpb):
    c = np.arange(GRID_W)
    cs = np.clip(c - WIN_C // 2, 0, GRID_W - WIN_C)
    kc = np.arange(GRID_W)
    col_mask = (kc[None, :] >= cs[:, None]) & (kc[None, :] < cs[:, None] + WIN_C)
    dc = np.clip(kc[None, :] - c[:, None], -(WIN_C - 1), WIN_C - 1) + (WIN_C - 1)
    tb = jnp.where(col_mask[None, None], rpb[:, :, dc].astype(f32), NEG)
    dw = np.arange(WIN_R)[:, None] + np.arange(WIN_R)[None, :]
    t = tb[:, dw]
    t = t.transpose(1, 0, 3, 2, 4).reshape(WIN_R, 2, 2 * GRID_W, WIN_R * GRID_W)
    return t


def _rope_slabs():
    inv = 1.0 / (ROPE_THETA ** (jnp.arange(0, ROPE_DIM, 2, dtype=f32) / ROPE_DIM))
    ang = jnp.arange(SEQ, dtype=f32)[:, None] * inv[None, :]
    cos, sin = jnp.cos(ang), jnp.sin(ang)
    ones = jnp.ones((SEQ, NOPE_DIM), f32)
    zpad = jnp.zeros((SEQ, LANES - NOPE_DIM - ROPE_DIM), f32)
    cosq = jnp.concatenate([ones, cos, cos, zpad], axis=1)
    sinq = jnp.concatenate([0.0 * ones, sin, sin, zpad], axis=1)
    return cosq, sinq


def _pack_layer(i, norm1_g, w_in, a_vnorm_g, a_ws, a_bs, b_conv, c_rpb, d_qnorm_g, d_kvnorm_g,
                d_w_uq, d_w_ukv, out_norm_g, w_o, norm2_g, router_w, e_w_gate, e_w_up, e_w_down):
    half = ROPE_DIM // 2
    w = w_in[i]
    kr0 = OFF_D + Q_RANK + KV_RANK
    kr = w[:, kr0:kr0 + ROPE_DIM]
    z64 = jnp.zeros((D_MODEL, NOPE_DIM), f32)
    z32 = jnp.zeros((D_MODEL, LANES - NOPE_DIM - ROPE_DIM), f32)
    kr_slab = jnp.concatenate([z64, kr, z32], axis=1)
    kr_rot = jnp.concatenate([z64, -kr[:, half:], kr[:, :half], z32], axis=1)
    w_pack = jnp.concatenate([w[:, :kr0], kr_slab, kr_rot], axis=1).astype(bf16)

    uq = d_w_uq[i].reshape(Q_RANK, D_HEADS, NOPE_DIM + ROPE_DIM)
    nope, rope = uq[..., :NOPE_DIM], uq[..., NOPE_DIM:]
    zq = jnp.zeros((Q_RANK, D_HEADS, LANES - NOPE_DIM - ROPE_DIM), f32)
    w_uq = jnp.concatenate([nope, rope, zq], axis=-1).reshape(Q_RANK, D_HEADS * LANES)
    w_uq_rot = jnp.concatenate([0.0 * nope, -rope[..., half:], rope[..., :half], zq],
                               axis=-1).reshape(Q_RANK, D_HEADS * LANES)
    ukv = d_w_ukv[i].reshape(KV_RANK, D_HEADS, NOPE_DIM + V_DIM)
    w_k = jnp.concatenate([ukv[..., :NOPE_DIM], jnp.zeros((KV_RANK, D_HEADS, LANES - NOPE_DIM), f32)],
                          axis=-1).reshape(KV_RANK, D_HEADS * LANES)
    w_v = ukv[..., NOPE_DIM:].reshape(KV_RANK, D_HEADS * V_DIM)
    rw = jnp.concatenate([router_w[i], jnp.zeros((D_MODEL, LANES - N_EXPERTS), f32)], axis=1)
    return {
        "g1": norm1_g[i].reshape(1, D_MODEL),
        "w_in": w_pack,
        "a_vg": a_vnorm_g[i].reshape(1, GROUP_W),
        "a_ws": a_ws[i].astype(bf16),
        "a_bias": jnp.repeat(a_bs[i].T, HEAD_DIM, axis=1),
        "b_conv": b_conv[i],
        "c_bias": _natten_bias_table(c_rpb[i]),
        "d_qg": d_qnorm_g[i].reshape(1, Q_RANK),
        "d_kvg": d_kvnorm_g[i].reshape(1, KV_RANK),
        "w_uq": w_uq.astype(bf16),
        "w_uq_rot": w_uq_rot.astype(bf16),
        "w_k": w_k.astype(bf16),
        "w_v": w_v.astype(bf16),
        "og": out_norm_g[i].reshape(1, D_MODEL),
        "w_o": w_o[i].astype(bf16),
        "g2": norm2_g[i].reshape(1, D_MODEL),
        "rw": rw.astype(bf16),
        "e_wg": e_w_gate[i].astype(bf16),
        "e_wu": e_w_up[i].astype(bf16),
        "e_wd": e_w_down[i].astype(bf16),
    }


def _moe(xn, h2, aff, lw, row0, nrows):
    cap = CAPACITY_FACTOR * nrows // N_EXPERTS
    g, idx = lax.top_k(aff[:, row0:row0 + nrows], cap)
    xe = h2[row0:row0 + nrows][idx]
    ye = _ffn(xe, g[..., None], lw)
    return xn[row0:row0 + nrows].at[idx.reshape(-1)].add(ye.reshape(-1, D_MODEL))


def kernel(x_prompt, x_sample, norm1_g, w_in, a_vnorm_g, a_ws, a_bs, b_conv, c_rpb, d_qnorm_g, d_kvnorm_g, d_w_uq, d_w_ukv, out_norm_g, w_o, norm2_g, router_w, e_w_gate, e_w_up, e_w_down, final_norm_g):
    n_p = BATCH * SEQ
    n_s = DEC_BATCH * SEQ
    x = jnp.concatenate([x_prompt.reshape(n_p, D_MODEL), x_sample.reshape(n_s, D_MODEL)], axis=0)
    cosq, sinq = _rope_slabs()
    for i in range(DEPTH):
        lw = _pack_layer(i, norm1_g, w_in, a_vnorm_g, a_ws, a_bs, b_conv, c_rpb, d_qnorm_g, d_kvnorm_g,
                         d_w_uq, d_w_ukv, out_norm_g, w_o, norm2_g, router_w, e_w_gate, e_w_up, e_w_down)
        ya, yb, zc, q4, k4, v2 = _inproj(x, lw, cosq, sinq)
        yc = _natten(zc, lw["c_bias"])
        yd = _mla(q4, k4, v2)
        xn, h2, aff = _outproj(ya, yb, yc, yd, x, lw)
        x = jnp.concatenate([_moe(xn, h2, aff, lw, 0, n_p), _moe(xn, h2, aff, lw, n_p, n_s)], axis=0)
    g = final_norm_g.reshape(1, D_MODEL)
    y_p = _final_norm(x, g, 0, n_p).reshape(BATCH, SEQ, D_MODEL)
    y_s = _final_norm(x, g, n_p, n_s).reshape(DEC_BATCH, SEQ, D_MODEL)
    return (y_p, y_s)
```

```python
import functools
import math

import numpy as np
import jax
import jax.numpy as jnp
from jax import lax
from jax.experimental import pallas as pl
from jax.experimental.pallas import tpu as pltpu

D_MODEL = 1024
BATCH = 4
SEQ = 8192
DEPTH = 4
DEC_BATCH = 2
GROUP_W = 256
HEAD_DIM = 64
A_HEADS = 4
CHUNK = 128
C_HEADS = 4
GRID_W = 64
WIN_R = 8
WIN_C = 16
D_HEADS = 4
Q_RANK = 256
KV_RANK = 128
NOPE_DIM = 64
ROPE_DIM = 32
V_DIM = 64
ROPE_THETA = 10000.0
N_EXPERTS = 16
D_EXPERT = 1024
CAPACITY_FACTOR = 2
EPS = 1e-6

NSEQ = BATCH + DEC_BATCH
T_ALL = NSEQ * SEQ
ROWS = SEQ // GRID_W
LANES = 128
NEG = -1e30

A_COLS = 2 * GROUP_W
B_COLS = 3 * GROUP_W
C_COLS = 3 * GROUP_W
OFF_B = A_COLS
OFF_C = OFF_B + B_COLS
OFF_D = OFF_C + C_COLS
D_PACK = Q_RANK + KV_RANK + 2 * LANES
N_IN_PACK = OFF_D + D_PACK

TM = 512
TQ = 256
TK = 512
L_ROWS = 16
MLA_LAG = 1
MLA_QSCALE = (NOPE_DIM + ROPE_DIM) ** -0.5 * math.log2(math.e)
ROWS_PER_STEP = 8
TME = 512
VMEM_LIMIT = 56 * 1024 * 1024

f32 = jnp.float32
bf16 = jnp.bfloat16


def _rms(x, g):
    return x * lax.rsqrt(jnp.mean(x * x, axis=-1, keepdims=True) + EPS) * g


def _dot(a, b):
    return jnp.dot(a, b, preferred_element_type=f32)


def _dot_nt(a, b):
    return lax.dot_general(a, b, (((1,), (1,)), ((), ())), preferred_element_type=f32)


def _inproj_kernel(x_ref, xp_ref, xn_ref, g1_ref, win_ref, avg_ref, aws_ref, abias_ref,
                   bconv_ref, qg_ref, kvg_ref, wuq_ref, wuqr_ref, wk_ref, wv_ref,
                   cos_ref, sin_ref,
                   ya_ref, yb_ref, zc_ref, qt_ref, k4_ref, vt_ref):
    i = pl.program_id(0)
    j = i % (SEQ // TM)
    g1 = g1_ref[...]
    h = _rms(x_ref[...], g1).astype(bf16)

    za = jax.nn.gelu(_dot(h, win_ref[:, 0:A_COLS]))
    u = za[:, :GROUP_W]
    v = _rms(za[:, GROUP_W:], avg_ref[...]).astype(bf16)
    lane = lax.broadcasted_iota(jnp.int32, (CHUNK, GROUP_W), 1)
    for c in range(TM // CHUNK):
        vc = v[c * CHUNK:(c + 1) * CHUNK, :]
        sv = abias_ref[...]
        for hd in range(A_HEADS):
            r = _dot(aws_ref[hd], vc)
            sv = sv + jnp.where((lane >= hd * HEAD_DIM) & (lane < (hd + 1) * HEAD_DIM), r, 0.0)
        ya_ref[c * CHUNK:(c + 1) * CHUNK, :] = u[c * CHUNK:(c + 1) * CHUNK, :] * sv

    zb = _dot(h, win_ref[:, OFF_B:OFF_B + B_COLS])
    bg = zb[:, :GROUP_W]
    y = zb[:, GROUP_W:2 * GROUP_W] * zb[:, 2 * GROUP_W:]
    xh = jnp.concatenate([xp_ref[...], xn_ref[...]], axis=0)
    hh = _rms(xh, g1).astype(bf16)
    zh = _dot(hh, win_ref[:, OFF_B + GROUP_W:OFF_B + B_COLS])
    yh = zh[:, :GROUP_W] * zh[:, GROUP_W:]
    y_before = jnp.where(j == 0, 0.0, yh[7:8, :])
    y_after = jnp.where(j == SEQ // TM - 1, 0.0, yh[8:9, :])
    row = lax.broadcasted_iota(jnp.int32, (TM, GROUP_W), 0)
    y_m1 = jnp.where(row == 0, y_before, pltpu.roll(y, 1, axis=0))
    y_p1 = jnp.where(row == TM - 1, y_after, pltpu.roll(y, TM - 1, axis=0))
    wc = bconv_ref[...]
    yb_ref[...] = bg * (wc[0:1, :] * y_m1 + wc[1:2, :] * y + wc[2:3, :] * y_p1)

    zc = _dot(h, win_ref[:, OFF_C:OFF_C + C_COLS])
    zc_ref[:, 0:GROUP_W] = (zc[:, 0:GROUP_W] * (HEAD_DIM ** -0.5)).astype(bf16)
    zc_ref[:, GROUP_W:] = zc[:, GROUP_W:].astype(bf16)

    zd = _dot(h, win_ref[:, OFF_D:OFF_D + D_PACK])
    cqn = _rms(zd[:, 0:Q_RANK], qg_ref[...]).astype(bf16)
    ckvn = _rms(zd[:, Q_RANK:Q_RANK + KV_RANK], kvg_ref[...]).astype(bf16)
    cos = cos_ref[...]
    sin = sin_ref[...]
    o = Q_RANK + KV_RANK
    kr = zd[:, o:o + LANES] * cos + zd[:, o + LANES:o + 2 * LANES] * sin
    q_pre = _dot(cqn, wuq_ref[...])
    q_rot = _dot(cqn, wuqr_ref[...])
    kn = _dot(ckvn, wk_ref[...])
    for hd in range(D_HEADS):
        sl = slice(hd * LANES, (hd + 1) * LANES)
        qt_ref[hd] = ((q_pre[:, sl] * cos + q_rot[:, sl] * sin) * MLA_QSCALE).T.astype(bf16)
        k4_ref[hd] = (kn[:, sl] + kr).astype(bf16)
    vt = _dot(ckvn, wv_ref[...]).T.astype(bf16)
    for hd in range(D_HEADS):
        vt_ref[hd] = vt[hd * V_DIM:(hd + 1) * V_DIM, :]


def _inproj(x, lw, cosq, sinq):
    nt = T_ALL // TM
    tps = SEQ // TM
    full = lambda shape: pl.BlockSpec(shape, lambda i: (0,) * len(shape))
    in_specs = [
        pl.BlockSpec((TM, D_MODEL), lambda i: (i, 0)),
        pl.BlockSpec((8, D_MODEL), lambda i: (jnp.maximum(i * (TM // 8) - 1, 0), 0)),
        pl.BlockSpec((8, D_MODEL), lambda i: (jnp.minimum((i + 1) * (TM // 8), T_ALL // 8 - 1), 0)),
        full((1, D_MODEL)),
        full((D_MODEL, N_IN_PACK)),
        full((1, GROUP_W)),
        full((A_HEADS, CHUNK, CHUNK)),
        full((CHUNK, GROUP_W)),
        full((3, GROUP_W)),
        full((1, Q_RANK)),
        full((1, KV_RANK)),
        full((Q_RANK, D_HEADS * LANES)),
        full((Q_RANK, D_HEADS * LANES)),
        full((KV_RANK, D_HEADS * LANES)),
        full((KV_RANK, D_HEADS * V_DIM)),
        pl.BlockSpec((TM, LANES), lambda i: (i % tps, 0)),
        pl.BlockSpec((TM, LANES), lambda i: (i % tps, 0)),
    ]
    out_specs = [
        pl.BlockSpec((TM, GROUP_W), lambda i: (i, 0)),
        pl.BlockSpec((TM, GROUP_W), lambda i: (i, 0)),
        pl.BlockSpec((TM, C_COLS), lambda i: (i, 0)),
        pl.BlockSpec((None, D_HEADS, LANES, TM), lambda i: (i // tps, 0, 0, i % tps)),
        pl.BlockSpec((None, D_HEADS, TM, LANES), lambda i: (i // tps, 0, i % tps, 0)),
        pl.BlockSpec((None, D_HEADS, V_DIM, TM), lambda i: (i // tps, 0, 0, i % tps)),
    ]
    out_shape = [
        jax.ShapeDtypeStruct((T_ALL, GROUP_W), f32),
        jax.ShapeDtypeStruct((T_ALL, GROUP_W), f32),
        jax.ShapeDtypeStruct((T_ALL, C_COLS), bf16),
        jax.ShapeDtypeStruct((NSEQ, D_HEADS, LANES, SEQ), bf16),
        jax.ShapeDtypeStruct((NSEQ, D_HEADS, SEQ, LANES), bf16),
        jax.ShapeDtypeStruct((NSEQ, D_HEADS, V_DIM, SEQ), bf16),
    ]
    return pl.pallas_call(
        _inproj_kernel, grid=(nt,), in_specs=in_specs, out_specs=out_specs, out_shape=out_shape,
        name="inproj",
        compiler_params=pltpu.CompilerParams(dimension_semantics=("parallel",),
                                             vmem_limit_bytes=VMEM_LIMIT),
    )(x, x, x, lw["g1"], lw["w_in"], lw["a_vg"], lw["a_ws"], lw["a_bias"], lw["b_conv"],
      lw["d_qg"], lw["d_kvg"], lw["w_uq"], lw["w_uq_rot"], lw["w_k"], lw["w_v"], cosq, sinq)


def _natten_kernel(q_ref, k_ref, v_ref, bias_ref, o_ref):
    jb = pl.program_id(1)
    lane = lax.broadcasted_iota(jnp.int32, (GRID_W, LANES), 1)
    lo = lane < HEAD_DIM
    nkeys = WIN_R * GRID_W

    def row_body(i, carry):
        r = jb * ROWS_PER_STEP + i
        rs = jnp.clip(r - WIN_R // 2, 0, ROWS - WIN_R)
        d = rs - r + (WIN_R - 1)
        kstart = pl.multiple_of(rs * GRID_W, GRID_W)
        qrow = pl.multiple_of(i * GRID_W, GRID_W)
        for s in range(2):
            sl = slice(s * LANES, (s + 1) * LANES)
            qs = q_ref[pl.ds(qrow, GRID_W), sl]
            ks = k_ref[pl.ds(kstart, nkeys), sl]
            vs = v_ref[pl.ds(kstart, nkeys), sl]
            zero = jnp.zeros_like(qs)
            qq = jnp.concatenate([jnp.where(lo, qs, zero), jnp.where(lo, zero, qs)], axis=0)
            sc = _dot_nt(qq, ks) + bias_ref[d, s]
            m = jnp.max(sc, axis=-1, keepdims=True)
            p = jnp.exp(sc - m)
            l = jnp.sum(p, axis=-1, keepdims=True)
            o = _dot(p.astype(bf16), vs) / l
            o_ref[pl.ds(qrow, GRID_W), sl] = jnp.where(lo, o[:GRID_W], o[GRID_W:])
        return carry

    lax.fori_loop(0, ROWS_PER_STEP, row_body, 0)


def _natten(zc, bias):
    nb = ROWS // ROWS_PER_STEP
    tq = ROWS_PER_STEP * GRID_W
    return pl.pallas_call(
        _natten_kernel, grid=(NSEQ, nb),
        in_specs=[
            pl.BlockSpec((tq, GROUP_W), lambda b, j: (b * nb + j, 0)),
            pl.BlockSpec((SEQ, GROUP_W), lambda b, j: (b, 1)),
            pl.BlockSpec((SEQ, GROUP_W), lambda b, j: (b, 2)),
            pl.BlockSpec((WIN_R, 2, 2 * GRID_W, WIN_R * GRID_W), lambda b, j: (0, 0, 0, 0)),
        ],
        out_specs=pl.BlockSpec((tq, GROUP_W), lambda b, j: (b * nb + j, 0)),
        out_shape=jax.ShapeDtypeStruct((T_ALL, GROUP_W), f32),
        name="natten",
        compiler_params=pltpu.CompilerParams(dimension_semantics=("parallel", "arbitrary"),
                                             vmem_limit_bytes=VMEM_LIMIT),
    )(zc, zc, zc, bias)


def _mla_kernel(qt_ref, k_ref, vt_ref, o_ref, s_scr):
    nchunk = SEQ // TK
    heads = list(range(D_HEADS))
    qts = [qt_ref[h] for h in heads]

    def qk(slot, h, ci):
        ks = pl.multiple_of(ci * TK, TK)
        s_scr[slot, h] = _dot(k_ref[h, pl.ds(ks, TK), :], qts[h])

    ones = jnp.ones((L_ROWS, TK), bf16)

    def softmax_pv(slot, h, ci, stat):
        m, acc = stat
        ks = pl.multiple_of(ci * TK, TK)
        vt = jnp.concatenate([vt_ref[h, :, pl.ds(ks, TK)], ones], axis=0)
        s = s_scr[slot, h]
        m_new = jnp.maximum(m, jnp.max(s, axis=0, keepdims=True))
        alpha = jnp.exp2(m - m_new)
        p = jnp.exp2(s - m_new)
        acc = acc * alpha + _dot(vt, p.astype(bf16))
        return m_new, acc

    def half(cur, ci, stats, prefetch=True):
        out = []
        for h in heads:
            if prefetch:
                qk(1 - cur, h, ci + 1)
            if h >= MLA_LAG:
                out.append(softmax_pv(cur, h - MLA_LAG, ci, stats[h - MLA_LAG]))
        for h in range(D_HEADS - MLA_LAG, D_HEADS):
            out.append(softmax_pv(cur, h, ci, stats[h]))
        return tuple(out)

    def body(cp, stats):
        stats = half(0, 2 * cp, stats)
        return half(1, 2 * cp + 1, stats)

    for h in heads:
        qk(0, h, 0)
    init = tuple((jnp.full((1, TQ), NEG, f32), jnp.zeros((V_DIM + L_ROWS, TQ), f32)) for _ in heads)
    res = lax.fori_loop(0, nchunk // 2 - 1, body, init)
    res = half(0, nchunk - 2, res)
    res = half(1, nchunk - 1, res, prefetch=False)
    o_ref[...] = jnp.concatenate([acc[:V_DIM] / acc[V_DIM:V_DIM + 1] for (_, acc) in res], axis=0).T


def _mla(qt, k4, vt):
    nq = SEQ // TQ
    return pl.pallas_call(
        _mla_kernel, grid=(NSEQ, nq),
        in_specs=[
            pl.BlockSpec((None, D_HEADS, LANES, TQ), lambda b, i: (b, 0, 0, i)),
            pl.BlockSpec((None, D_HEADS, SEQ, LANES), lambda b, i: (b, 0, 0, 0)),
            pl.BlockSpec((None, D_HEADS, V_DIM, SEQ), lambda b, i: (b, 0, 0, 0)),
        ],
        out_specs=pl.BlockSpec((TQ, GROUP_W), lambda b, i: (b * nq + i, 0)),
        out_shape=jax.ShapeDtypeStruct((T_ALL, GROUP_W), f32),
        scratch_shapes=[pltpu.VMEM((2, D_HEADS, TK, TQ), f32)],
        name="mla",
        compiler_params=pltpu.CompilerParams(
            dimension_semantics=("parallel", "arbitrary"),
            vmem_limit_bytes=VMEM_LIMIT),
    )(qt, k4, vt)


def _outproj_kernel(ya_ref, yb_ref, yc_ref, yd_ref, x_ref, og_ref, wo_ref, g2_ref, rw_ref,
                    xn_ref, h2_ref, aff_ref):
    acc = x_ref[...]
    for g, y_ref in enumerate((ya_ref, yb_ref, yc_ref, yd_ref)):
        sl = slice(g * GROUP_W, (g + 1) * GROUP_W)
        yg = _rms(y_ref[...], og_ref[:, sl]).astype(bf16)
        acc = acc + _dot(yg, wo_ref[sl, :])
    xn_ref[...] = acc
    h2 = _rms(acc, g2_ref[...]).astype(bf16)
    h2_ref[...] = h2
    logits = _dot(h2, rw_ref[...])
    lane = lax.broadcasted_iota(jnp.int32, logits.shape, 1)
    logits = jnp.where(lane < N_EXPERTS, logits, NEG)
    m = jnp.max(logits, axis=-1, keepdims=True)
    e = jnp.exp(logits - m)
    aff = e / jnp.sum(e, axis=-1, keepdims=True)
    aff_ref[...] = aff.T[:N_EXPERTS, :]


def _outproj(ya, yb, yc, yd, x, lw):
    nt = T_ALL // TM
    full = lambda shape: pl.BlockSpec(shape, lambda i: (0,) * len(shape))
    yspec = pl.BlockSpec((TM, GROUP_W), lambda i: (i, 0))
    xspec = pl.BlockSpec((TM, D_MODEL), lambda i: (i, 0))
    return pl.pallas_call(
        _outproj_kernel, grid=(nt,),
        in_specs=[yspec, yspec, yspec, yspec, xspec, full((1, D_MODEL)), full((D_MODEL, D_MODEL)),
                  full((1, D_MODEL)), full((D_MODEL, LANES))],
        out_specs=[xspec, xspec, pl.BlockSpec((N_EXPERTS, TM), lambda i: (0, i))],
        out_shape=[jax.ShapeDtypeStruct((T_ALL, D_MODEL), f32),
                   jax.ShapeDtypeStruct((T_ALL, D_MODEL), bf16),
                   jax.ShapeDtypeStruct((N_EXPERTS, T_ALL), f32)],
        name="outproj",
        compiler_params=pltpu.CompilerParams(dimension_semantics=("parallel",),
                                             vmem_limit_bytes=VMEM_LIMIT),
    )(ya, yb, yc, yd, x, lw["og"], lw["w_o"], lw["g2"], lw["rw"])


def _ffn_kernel(xe_ref, g_ref, wg_ref, wu_ref, wd_ref, ye_ref):
    x = xe_ref[...]
    hg = _dot(x, wg_ref[...])
    hu = _dot(x, wu_ref[...])
    h = (hg * (1.0 / (1.0 + jnp.exp(-hg))) * hu).astype(bf16)
    ye_ref[...] = _dot(h, wd_ref[...]) * g_ref[...]


def _ffn(xe, g, lw):
    cap = xe.shape[1]
    wspec = pl.BlockSpec((None, D_MODEL, D_EXPERT), lambda e, t: (e, 0, 0))
    return pl.pallas_call(
        _ffn_kernel, grid=(N_EXPERTS, cap // TME),
        in_specs=[
            pl.BlockSpec((None, TME, D_MODEL), lambda e, t: (e, t, 0)),
            pl.BlockSpec((None, TME, 1), lambda e, t: (e, t, 0)),
            wspec, wspec,
            pl.BlockSpec((None, D_EXPERT, D_MODEL), lambda e, t: (e, 0, 0)),
        ],
        out_specs=pl.BlockSpec((None, TME, D_MODEL), lambda e, t: (e, t, 0)),
        out_shape=jax.ShapeDtypeStruct((N_EXPERTS, cap, D_MODEL), f32),
        name="expert_ffn",
        compiler_params=pltpu.CompilerParams(dimension_semantics=("parallel", "arbitrary"),
                                             vmem_limit_bytes=VMEM_LIMIT),
    )(xe, g, lw["e_wg"], lw["e_wu"], lw["e_wd"])


def _final_kernel(x_ref, g_ref, o_ref):
    o_ref[...] = _rms(x_ref[...], g_ref[...])


def _final_norm(x, g, row0, nrows):
    off = row0 // TM
    return pl.pallas_call(
        _final_kernel, grid=(nrows // TM,),
        in_specs=[pl.BlockSpec((TM, D_MODEL), lambda i: (i + off, 0)),
                  pl.BlockSpec((1, D_MODEL), lambda i: (0, 0))],
        out_specs=pl.BlockSpec((TM, D_MODEL), lambda i: (i, 0)),
        out_shape=jax.ShapeDtypeStruct((nrows, D_MODEL), f32),
        name="final_norm",
        compiler_params=pltpu.CompilerParams(dimension_semantics=("parallel",)),
    )(x, g)


def _natten_bias_table(rpb):
    c = np.arange(GRID_W)
    cs = np.clip(c - WIN_C // 2, 0, GRID_W - WIN_C)
    kc = np.arange(GRID_W)
    col_mask = (kc[None, :] >= cs[:, None]) & (kc[None, :] < cs[:, None] + WIN_C)
    dc = np.clip(kc[None, :] - c[:, None], -(WIN_C - 1), WIN_C - 1) + (WIN_C - 1)
    tb = jnp.where(col_mask[None, None], rpb[:, :, dc].astype(f32), NEG)
    dw = np.arange(WIN_R)[:, None] + np.arange(WIN_R)[None, :]
    t = tb[:, dw]
    t = t.transpose(1, 0, 3, 2, 4).reshape(WIN_R, 2, 2 * GRID_W, WIN_R * GRID_W)
    return t


def _rope_slabs():
    inv = 1.0 / (ROPE_THETA ** (jnp.arange(0, ROPE_DIM, 2, dtype=f32) / ROPE_DIM))
    ang = jnp.arange(SEQ, dtype=f32)[:, None] * inv[None, :]
    cos, sin = jnp.cos(ang), jnp.sin(ang)
    ones = jnp.ones((SEQ, NOPE_DIM), f32)
    zpad = jnp.zeros((SEQ, LANES - NOPE_DIM - ROPE_DIM), f32)
    cosq = jnp.concatenate([ones, cos, cos, zpad], axis=1)
    sinq = jnp.concatenate([0.0 * ones, sin, sin, zpad], axis=1)
    return cosq, sinq


def _pack_layer(i, norm1_g, w_in, a_vnorm_g, a_ws, a_bs, b_conv, c_rpb, d_qnorm_g, d_kvnorm_g,
                d_w_uq, d_w_ukv, out_norm_g, w_o, norm2_g, router_w, e_w_gate, e_w_up, e_w_down):
    half = ROPE_DIM // 2
    w = w_in[i]
    kr0 = OFF_D + Q_RANK + KV_RANK
    kr = w[:, kr0:kr0 + ROPE_DIM]
    z64 = jnp.zeros((D_MODEL, NOPE_DIM), f32)
    z32 = jnp.zeros((D_MODEL, LANES - NOPE_DIM - ROPE_DIM), f32)
    kr_slab = jnp.concatenate([z64, kr, z32], axis=1)
    kr_rot = jnp.concatenate([z64, -kr[:, half:], kr[:, :half], z32], axis=1)
    w_pack = jnp.concatenate([w[:, :kr0], kr_slab, kr_rot], axis=1).astype(bf16)

    uq = d_w_uq[i].reshape(Q_RANK, D_HEADS, NOPE_DIM + ROPE_DIM)
    nope, rope = uq[..., :NOPE_DIM], uq[..., NOPE_DIM:]
    zq = jnp.zeros((Q_RANK, D_HEADS, LANES - NOPE_DIM - ROPE_DIM), f32)
    w_uq = jnp.concatenate([nope, rope, zq], axis=-1).reshape(Q_RANK, D_HEADS * LANES)
    w_uq_rot = jnp.concatenate([0.0 * nope, -rope[..., half:], rope[..., :half], zq],
                               axis=-1).reshape(Q_RANK, D_HEADS * LANES)
    ukv = d_w_ukv[i].reshape(KV_RANK, D_HEADS, NOPE_DIM + V_DIM)
    w_k = jnp.concatenate([ukv[..., :NOPE_DIM], jnp.zeros((KV_RANK, D_HEADS, LANES - NOPE_DIM), f32)],
                          axis=-1).reshape(KV_RANK, D_HEADS * LANES)
    w_v = ukv[..., NOPE_DIM:].reshape(KV_RANK, D_HEADS * V_DIM)
    rw = jnp.concatenate([router_w[i], jnp.zeros((D_MODEL, LANES - N_EXPERTS), f32)], axis=1)
    return {
        "g1": norm1_g[i].reshape(1, D_MODEL),
        "w_in": w_pack,
        "a_vg": a_vnorm_g[i].reshape(1, GROUP_W),
        "a_ws": a_ws[i].astype(bf16),
        "a_bias": jnp.repeat(a_bs[i].T, HEAD_DIM, axis=1),
        "b_conv": b_conv[i],
        "c_bias": _natten_bias_table(c_rpb[i]),
        "d_qg": d_qnorm_g[i].reshape(1, Q_RANK),
        "d_kvg": d_kvnorm_g[i].reshape(1, KV_RANK),
        "w_uq": w_uq.astype(bf16),
        "w_uq_rot": w_uq_rot.astype(bf16),
        "w_k": w_k.astype(bf16),
        "w_v": w_v.astype(bf16),
        "og": out_norm_g[i].reshape(1, D_MODEL),
        "w_o": w_o[i].astype(bf16),
        "g2": norm2_g[i].reshape(1, D_MODEL),
        "rw": rw.astype(bf16),
        "e_wg": e_w_gate[i].astype(bf16),
        "e_wu": e_w_up[i].astype(bf16),
        "e_wd": e_w_down[i].astype(bf16),
    }


def _moe(xn, h2, aff, lw, row0, nrows):
    cap = CAPACITY_FACTOR * nrows // N_EXPERTS
    g, idx = lax.top_k(aff[:, row0:row0 + nrows], cap)
    xe = h2[row0:row0 + nrows][idx]
    ye = _ffn(xe, g[..., None], lw)
    return xn[row0:row0 + nrows].at[idx.reshape(-1)].add(ye.reshape(-1, D_MODEL))


def kernel(x_prompt, x_sample, norm1_g, w_in, a_vnorm_g, a_ws, a_bs, b_conv, c_rpb, d_qnorm_g, d_kvnorm_g, d_w_uq, d_w_ukv, out_norm_g, w_o, norm2_g, router_w, e_w_gate, e_w_up, e_w_down, final_norm_g):
    n_p = BATCH * SEQ
    n_s = DEC_BATCH * SEQ
    x = jnp.concatenate([x_prompt.reshape(n_p, D_MODEL), x_sample.reshape(n_s, D_MODEL)], axis=0)
    cosq, sinq = _rope_slabs()
    for i in range(DEPTH):
        lw = _pack_layer(i, norm1_g, w_in, a_vnorm_g, a_ws, a_bs, b_conv, c_rpb, d_qnorm_g, d_kvnorm_g,
                         d_w_uq, d_w_ukv, out_norm_g, w_o, norm2_g, router_w, e_w_gate, e_w_up, e_w_down)
        ya, yb, zc, qt, k4, vt = _inproj(x, lw, cosq, sinq)
        yc = _natten(zc, lw["c_bias"])
        yd = _mla(qt, k4, vt)
        xn, h2, aff = _outproj(ya, yb, yc, yd, x, lw)
        x = jnp.concatenate([_moe(xn, h2, aff, lw, 0, n_p), _moe(xn, h2, aff, lw, n_p, n_s)], axis=0)
    g = final_norm_g.reshape(1, D_MODEL)
    y_p = _final_norm(x, g, 0, n_p).reshape(BATCH, SEQ, D_MODEL)
    y_s = _final_norm(x, g, n_p, n_s).reshape(DEC_BATCH, SEQ, D_MODEL)
    return (y_p, y_s)
```

```python
import functools
import math

import numpy as np
import jax
import jax.numpy as jnp
from jax import lax
from jax.experimental import pallas as pl
from jax.experimental.pallas import tpu as pltpu

D_MODEL = 1024
BATCH = 4
SEQ = 8192
DEPTH = 4
DEC_BATCH = 2
GROUP_W = 256
HEAD_DIM = 64
A_HEADS = 4
CHUNK = 128
C_HEADS = 4
GRID_W = 64
WIN_R = 8
WIN_C = 16
D_HEADS = 4
Q_RANK = 256
KV_RANK = 128
NOPE_DIM = 64
ROPE_DIM = 32
V_DIM = 64
ROPE_THETA = 10000.0
N_EXPERTS = 16
D_EXPERT = 1024
CAPACITY_FACTOR = 2
EPS = 1e-6

NSEQ = BATCH + DEC_BATCH
T_ALL = NSEQ * SEQ
ROWS = SEQ // GRID_W
LANES = 128
NEG = -1e30

A_COLS = 2 * GROUP_W
B_COLS = 3 * GROUP_W
C_COLS = 3 * GROUP_W
OFF_B = A_COLS
OFF_C = OFF_B + B_COLS
OFF_D = OFF_C + C_COLS
D_PACK = Q_RANK + KV_RANK + 2 * LANES
N_IN_PACK = OFF_D + D_PACK

TM = 512
TQ = 256
TK = 512
L_ROWS = 16
MLA_LAG = 1
MLA_QSCALE = (NOPE_DIM + ROPE_DIM) ** -0.5 * math.log2(math.e)
ROWS_PER_STEP = 8
TME = 256
TT = 256
WIN_SMALL = 72
WIN_LARGE = TT + 8
VMEM_LIMIT = 56 * 1024 * 1024

f32 = jnp.float32
bf16 = jnp.bfloat16


def _rms(x, g):
    return x * lax.rsqrt(jnp.mean(x * x, axis=-1, keepdims=True) + EPS) * g


def _dot(a, b):
    return jnp.dot(a, b, preferred_element_type=f32)


def _dot_nt(a, b):
    return lax.dot_general(a, b, (((1,), (1,)), ((), ())), preferred_element_type=f32)


def _dot_tn(a, b):
    return lax.dot_general(a, b, (((0,), (0,)), ((), ())), preferred_element_type=f32)


def _inproj_kernel(x_ref, xp_ref, xn_ref, g1_ref, win_ref, avg_ref, aws_ref, abias_ref,
                   bconv_ref, qg_ref, kvg_ref, wuq_ref, wuqr_ref, wk_ref, wv_ref,
                   cos_ref, sin_ref,
                   ya_ref, yb_ref, zc_ref, qt_ref, k4_ref, vt_ref):
    i = pl.program_id(0)
    j = i % (SEQ // TM)
    g1 = g1_ref[...]
    h = _rms(x_ref[...], g1).astype(bf16)

    za = jax.nn.gelu(_dot(h, win_ref[:, 0:A_COLS]))
    u = za[:, :GROUP_W]
    v = _rms(za[:, GROUP_W:], avg_ref[...]).astype(bf16)
    lane = lax.broadcasted_iota(jnp.int32, (CHUNK, GROUP_W), 1)
    for c in range(TM // CHUNK):
        vc = v[c * CHUNK:(c + 1) * CHUNK, :]
        sv = abias_ref[...]
        for hd in range(A_HEADS):
            r = _dot(aws_ref[hd], vc)
            sv = sv + jnp.where((lane >= hd * HEAD_DIM) & (lane < (hd + 1) * HEAD_DIM), r, 0.0)
        ya_ref[c * CHUNK:(c + 1) * CHUNK, :] = u[c * CHUNK:(c + 1) * CHUNK, :] * sv

    zb = _dot(h, win_ref[:, OFF_B:OFF_B + B_COLS])
    bg = zb[:, :GROUP_W]
    y = zb[:, GROUP_W:2 * GROUP_W] * zb[:, 2 * GROUP_W:]
    xh = jnp.concatenate([xp_ref[...], xn_ref[...]], axis=0)
    hh = _rms(xh, g1).astype(bf16)
    zh = _dot(hh, win_ref[:, OFF_B + GROUP_W:OFF_B + B_COLS])
    yh = zh[:, :GROUP_W] * zh[:, GROUP_W:]
    y_before = jnp.where(j == 0, 0.0, yh[7:8, :])
    y_after = jnp.where(j == SEQ // TM - 1, 0.0, yh[8:9, :])
    row = lax.broadcasted_iota(jnp.int32, (TM, GROUP_W), 0)
    y_m1 = jnp.where(row == 0, y_before, pltpu.roll(y, 1, axis=0))
    y_p1 = jnp.where(row == TM - 1, y_after, pltpu.roll(y, TM - 1, axis=0))
    wc = bconv_ref[...]
    yb_ref[...] = bg * (wc[0:1, :] * y_m1 + wc[1:2, :] * y + wc[2:3, :] * y_p1)

    zc = _dot(h, win_ref[:, OFF_C:OFF_C + C_COLS])
    zc_ref[:, 0:GROUP_W] = (zc[:, 0:GROUP_W] * (HEAD_DIM ** -0.5)).astype(bf16)
    zc_ref[:, GROUP_W:] = zc[:, GROUP_W:].astype(bf16)

    zd = _dot(h, win_ref[:, OFF_D:OFF_D + D_PACK])
    cqn = _rms(zd[:, 0:Q_RANK], qg_ref[...]).astype(bf16)
    ckvn = _rms(zd[:, Q_RANK:Q_RANK + KV_RANK], kvg_ref[...]).astype(bf16)
    cos = cos_ref[...]
    sin = sin_ref[...]
    o = Q_RANK + KV_RANK
    kr = zd[:, o:o + LANES] * cos + zd[:, o + LANES:o + 2 * LANES] * sin
    q_pre = _dot(cqn, wuq_ref[...])
    q_rot = _dot(cqn, wuqr_ref[...])
    kn = _dot(ckvn, wk_ref[...])
    for hd in range(D_HEADS):
        sl = slice(hd * LANES, (hd + 1) * LANES)
        qt_ref[hd] = ((q_pre[:, sl] * cos + q_rot[:, sl] * sin) * MLA_QSCALE).T.astype(bf16)
        k4_ref[hd] = (kn[:, sl] + kr).astype(bf16)
    vt = _dot(ckvn, wv_ref[...]).T.astype(bf16)
    for hd in range(D_HEADS):
        vt_ref[hd] = vt[hd * V_DIM:(hd + 1) * V_DIM, :]


def _inproj(x, lw, cosq, sinq):
    nt = T_ALL // TM
    tps = SEQ // TM
    full = lambda shape: pl.BlockSpec(shape, lambda i: (0,) * len(shape))
    in_specs = [
        pl.BlockSpec((TM, D_MODEL), lambda i: (i, 0)),
        pl.BlockSpec((8, D_MODEL), lambda i: (jnp.maximum(i * (TM // 8) - 1, 0), 0)),
        pl.BlockSpec((8, D_MODEL), lambda i: (jnp.minimum((i + 1) * (TM // 8), T_ALL // 8 - 1), 0)),
        full((1, D_MODEL)),
        full((D_MODEL, N_IN_PACK)),
        full((1, GROUP_W)),
        full((A_HEADS, CHUNK, CHUNK)),
        full((CHUNK, GROUP_W)),
        full((3, GROUP_W)),
        full((1, Q_RANK)),
        full((1, KV_RANK)),
        full((Q_RANK, D_HEADS * LANES)),
        full((Q_RANK, D_HEADS * LANES)),
        full((KV_RANK, D_HEADS * LANES)),
        full((KV_RANK, D_HEADS * V_DIM)),
        pl.BlockSpec((TM, LANES), lambda i: (i % tps, 0)),
        pl.BlockSpec((TM, LANES), lambda i: (i % tps, 0)),
    ]
    out_specs = [
        pl.BlockSpec((TM, GROUP_W), lambda i: (i, 0)),
        pl.BlockSpec((TM, GROUP_W), lambda i: (i, 0)),
        pl.BlockSpec((TM, C_COLS), lambda i: (i, 0)),
        pl.BlockSpec((None, D_HEADS, LANES, TM), lambda i: (i // tps, 0, 0, i % tps)),
        pl.BlockSpec((None, D_HEADS, TM, LANES), lambda i: (i // tps, 0, i % tps, 0)),
        pl.BlockSpec((None, D_HEADS, V_DIM, TM), lambda i: (i // tps, 0, 0, i % tps)),
    ]
    out_shape = [
        jax.ShapeDtypeStruct((T_ALL, GROUP_W), f32),
        jax.ShapeDtypeStruct((T_ALL, GROUP_W), f32),
        jax.ShapeDtypeStruct((T_ALL, C_COLS), bf16),
        jax.ShapeDtypeStruct((NSEQ, D_HEADS, LANES, SEQ), bf16),
        jax.ShapeDtypeStruct((NSEQ, D_HEADS, SEQ, LANES), bf16),
        jax.ShapeDtypeStruct((NSEQ, D_HEADS, V_DIM, SEQ), bf16),
    ]
    return pl.pallas_call(
        _inproj_kernel, grid=(nt,), in_specs=in_specs, out_specs=out_specs, out_shape=out_shape,
        name="inproj",
        compiler_params=pltpu.CompilerParams(dimension_semantics=("parallel",),
                                             vmem_limit_bytes=VMEM_LIMIT),
    )(x, x, x, lw["g1"], lw["w_in"], lw["a_vg"], lw["a_ws"], lw["a_bias"], lw["b_conv"],
      lw["d_qg"], lw["d_kvg"], lw["w_uq"], lw["w_uq_rot"], lw["w_k"], lw["w_v"], cosq, sinq)


def _natten_kernel(q_ref, k_ref, v_ref, bias_ref, o_ref):
    jb = pl.program_id(1)
    lane = lax.broadcasted_iota(jnp.int32, (GRID_W, LANES), 1)
    lo = lane < HEAD_DIM
    nkeys = WIN_R * GRID_W

    def row_body(i, carry):
        r = jb * ROWS_PER_STEP + i
        rs = jnp.clip(r - WIN_R // 2, 0, ROWS - WIN_R)
        d = rs - r + (WIN_R - 1)
        kstart = pl.multiple_of(rs * GRID_W, GRID_W)
        qrow = pl.multiple_of(i * GRID_W, GRID_W)
        for s in range(2):
            sl = slice(s * LANES, (s + 1) * LANES)
            qs = q_ref[pl.ds(qrow, GRID_W), sl]
            ks = k_ref[pl.ds(kstart, nkeys), sl]
            vs = v_ref[pl.ds(kstart, nkeys), sl]
            zero = jnp.zeros_like(qs)
            qq = jnp.concatenate([jnp.where(lo, qs, zero), jnp.where(lo, zero, qs)], axis=0)
            sc = _dot_nt(qq, ks) + bias_ref[d, s]
            m = jnp.max(sc, axis=-1, keepdims=True)
            p = jnp.exp(sc - m)
            l = jnp.sum(p, axis=-1, keepdims=True)
            o = _dot(p.astype(bf16), vs) / l
            o_ref[pl.ds(qrow, GRID_W), sl] = jnp.where(lo, o[:GRID_W], o[GRID_W:])
        return carry

    lax.fori_loop(0, ROWS_PER_STEP, row_body, 0)


def _natten(zc, bias):
    nb = ROWS // ROWS_PER_STEP
    tq = ROWS_PER_STEP * GRID_W
    return pl.pallas_call(
        _natten_kernel, grid=(NSEQ, nb),
        in_specs=[
            pl.BlockSpec((tq, GROUP_W), lambda b, j: (b * nb + j, 0)),
            pl.BlockSpec((SEQ, GROUP_W), lambda b, j: (b, 1)),
            pl.BlockSpec((SEQ, GROUP_W), lambda b, j: (b, 2)),
            pl.BlockSpec((WIN_R, 2, 2 * GRID_W, WIN_R * GRID_W), lambda b, j: (0, 0, 0, 0)),
        ],
        out_specs=pl.BlockSpec((tq, GROUP_W), lambda b, j: (b * nb + j, 0)),
        out_shape=jax.ShapeDtypeStruct((T_ALL, GROUP_W), f32),
        name="natten",
        compiler_params=pltpu.CompilerParams(dimension_semantics=("parallel", "arbitrary"),
                                             vmem_limit_bytes=VMEM_LIMIT),
    )(zc, zc, zc, bias)


def _mla_kernel(qt_ref, k_ref, vt_ref, o_ref, s_scr):
    nchunk = SEQ // TK
    heads = list(range(D_HEADS))
    qts = [qt_ref[h] for h in heads]

    def qk(slot, h, ci):
        ks = pl.multiple_of(ci * TK, TK)
        s_scr[slot, h] = _dot(k_ref[h, pl.ds(ks, TK), :], qts[h])

    ones = jnp.ones((L_ROWS, TK), bf16)

    def softmax_pv(slot, h, ci, stat):
        m, acc = stat
        ks = pl.multiple_of(ci * TK, TK)
        vt = jnp.concatenate([vt_ref[h, :, pl.ds(ks, TK)], ones], axis=0)
        s = s_scr[slot, h]
        m_new = jnp.maximum(m, jnp.max(s, axis=0, keepdims=True))
        alpha = jnp.exp2(m - m_new)
        p = jnp.exp2(s - m_new)
        acc = acc * alpha + _dot(vt, p.astype(bf16))
        return m_new, acc

    def half(cur, ci, stats, prefetch=True):
        out = []
        for h in heads:
            if prefetch:
                qk(1 - cur, h, ci + 1)
            if h >= MLA_LAG:
                out.append(softmax_pv(cur, h - MLA_LAG, ci, stats[h - MLA_LAG]))
        for h in range(D_HEADS - MLA_LAG, D_HEADS):
            out.append(softmax_pv(cur, h, ci, stats[h]))
        return tuple(out)

    def body(cp, stats):
        stats = half(0, 2 * cp, stats)
        return half(1, 2 * cp + 1, stats)

    for h in heads:
        qk(0, h, 0)
    init = tuple((jnp.full((1, TQ), NEG, f32), jnp.zeros((V_DIM + L_ROWS, TQ), f32)) for _ in heads)
    res = lax.fori_loop(0, nchunk // 2 - 1, body, init)
    res = half(0, nchunk - 2, res)
    res = half(1, nchunk - 1, res, prefetch=False)
    o_ref[...] = jnp.concatenate([acc[:V_DIM] / acc[V_DIM:V_DIM + 1] for (_, acc) in res], axis=0).T


def _mla(qt, k4, vt):
    nq = SEQ // TQ
    return pl.pallas_call(
        _mla_kernel, grid=(NSEQ, nq),
        in_specs=[
            pl.BlockSpec((None, D_HEADS, LANES, TQ), lambda b, i: (b, 0, 0, i)),
            pl.BlockSpec((None, D_HEADS, SEQ, LANES), lambda b, i: (b, 0, 0, 0)),
            pl.BlockSpec((None, D_HEADS, V_DIM, SEQ), lambda b, i: (b, 0, 0, 0)),
        ],
        out_specs=pl.BlockSpec((TQ, GROUP_W), lambda b, i: (b * nq + i, 0)),
        out_shape=jax.ShapeDtypeStruct((T_ALL, GROUP_W), f32),
        scratch_shapes=[pltpu.VMEM((2, D_HEADS, TK, TQ), f32)],
        name="mla",
        compiler_params=pltpu.CompilerParams(
            dimension_semantics=("parallel", "arbitrary"),
            vmem_limit_bytes=VMEM_LIMIT),
    )(qt, k4, vt)


def _outproj_kernel(ya_ref, yb_ref, yc_ref, yd_ref, x_ref, og_ref, wo_ref, g2_ref, rw_ref,
                    xn_ref, h2_ref, aff_ref):
    acc = x_ref[...]
    for g, y_ref in enumerate((ya_ref, yb_ref, yc_ref, yd_ref)):
        sl = slice(g * GROUP_W, (g + 1) * GROUP_W)
        yg = _rms(y_ref[...], og_ref[:, sl]).astype(bf16)
        acc = acc + _dot(yg, wo_ref[sl, :])
    xn_ref[...] = acc
    h2 = _rms(acc, g2_ref[...]).astype(bf16)
    h2_ref[...] = h2
    logits = _dot(h2, rw_ref[...])
    lane = lax.broadcasted_iota(jnp.int32, logits.shape, 1)
    logits = jnp.where(lane < N_EXPERTS, logits, NEG)
    m = jnp.max(logits, axis=-1, keepdims=True)
    e = jnp.exp(logits - m)
    aff = e / jnp.sum(e, axis=-1, keepdims=True)
    aff_ref[...] = aff.T[:N_EXPERTS, :]


def _outproj(ya, yb, yc, yd, x, lw):
    nt = T_ALL // TM
    full = lambda shape: pl.BlockSpec(shape, lambda i: (0,) * len(shape))
    yspec = pl.BlockSpec((TM, GROUP_W), lambda i: (i, 0))
    xspec = pl.BlockSpec((TM, D_MODEL), lambda i: (i, 0))
    return pl.pallas_call(
        _outproj_kernel, grid=(nt,),
        in_specs=[yspec, yspec, yspec, yspec, xspec, full((1, D_MODEL)), full((D_MODEL, D_MODEL)),
                  full((1, D_MODEL)), full((D_MODEL, LANES))],
        out_specs=[xspec, xspec, pl.BlockSpec((N_EXPERTS, TM), lambda i: (0, i))],
        out_shape=[jax.ShapeDtypeStruct((T_ALL, D_MODEL), f32),
                   jax.ShapeDtypeStruct((T_ALL, D_MODEL), bf16),
                   jax.ShapeDtypeStruct((N_EXPERTS, T_ALL), f32)],
        name="outproj",
        compiler_params=pltpu.CompilerParams(dimension_semantics=("parallel",),
                                             vmem_limit_bytes=VMEM_LIMIT),
    )(ya, yb, yc, yd, x, lw["og"], lw["w_o"], lw["g2"], lw["rw"])


def _route_kernel(aff_ref, tri_ref, posm_ref, base_ref, thr_scr, need_scr, runeq_scr, runsel_scr, *, cap):
    j = pl.program_id(0)

    @pl.when(j == 0)
    def _():
        def refine(i, thr):
            cand = thr | jnp.left_shift(jnp.int32(1), 30 - i)
            bits = pltpu.bitcast(aff_ref[...], jnp.int32)
            cnt = jnp.sum(jnp.where(bits >= cand, 1.0, 0.0), axis=1, keepdims=True)
            return jnp.where(cnt >= cap, cand, thr)

        thr = lax.fori_loop(0, 31, refine, jnp.zeros((N_EXPERTS, 1), jnp.int32))
        bits = pltpu.bitcast(aff_ref[...], jnp.int32)
        above = jnp.sum(jnp.where(bits > thr, 1.0, 0.0), axis=1, keepdims=True)
        thr_scr[...] = jnp.broadcast_to(thr, (N_EXPERTS, LANES))
        need_scr[...] = jnp.broadcast_to(cap - above, (N_EXPERTS, LANES))
        runeq_scr[...] = jnp.zeros((N_EXPERTS, LANES), f32)
        runsel_scr[...] = jnp.zeros((N_EXPERTS, LANES), f32)

    t0 = pl.multiple_of(j * TT, TT)
    bits = pltpu.bitcast(aff_ref[:, pl.ds(t0, TT)], jnp.int32)
    thr = thr_scr[:, 0:1]
    eq = bits == thr
    eq_rank = runeq_scr[:, 0:1] + _dot(jnp.where(eq, 1.0, 0.0).astype(bf16), tri_ref[...])
    sel = (bits > thr) | (eq & (eq_rank < need_scr[:, 0:1]))
    sel_f = jnp.where(sel, 1.0, 0.0)
    base = runsel_scr[...]
    pos = base[:, 0:1] + _dot(sel_f.astype(bf16), tri_ref[...])
    posm_ref[...] = jnp.where(sel, pos, -1.0)
    base_ref[...] = base
    runeq_scr[...] += jnp.sum(jnp.where(eq, 1.0, 0.0), axis=1, keepdims=True)
    runsel_scr[...] += jnp.sum(sel_f, axis=1, keepdims=True)


def _route(aff, row0, nrows, tri):
    cap = CAPACITY_FACTOR * nrows // N_EXPERTS
    nt = nrows // TT
    blk0 = row0 // nrows
    vec = pltpu.VMEM((N_EXPERTS, LANES), f32)
    return pl.pallas_call(
        functools.partial(_route_kernel, cap=cap), grid=(nt,),
        in_specs=[pl.BlockSpec((N_EXPERTS, nrows), lambda j: (0, blk0)),
                  pl.BlockSpec((TT, TT), lambda j: (0, 0))],
        out_specs=[pl.BlockSpec((N_EXPERTS, TT), lambda j: (0, j)),
                   pl.BlockSpec((None, N_EXPERTS, LANES), lambda j: (j, 0, 0))],
        out_shape=[jax.ShapeDtypeStruct((N_EXPERTS, nrows), f32),
                   jax.ShapeDtypeStruct((nt, N_EXPERTS, LANES), f32)],
        scratch_shapes=[pltpu.VMEM((N_EXPERTS, LANES), jnp.int32), vec, vec, vec],
        name="route",
        compiler_params=pltpu.CompilerParams(dimension_semantics=("arbitrary",)),
    )(aff, tri)


def _ffn_kernel(c0_ref, c1_ref, posm_ref, h2_hbm, rw_ref, wg_ref, wu_ref, wd_ref, ye_ref,
                buf, sem, acc, *, row0, ngroups):
    e = pl.program_id(0)
    g = pl.program_id(1)
    lo = c0_ref[e * ngroups + g]
    hi = c1_ref[e * ngroups + g]

    def tile_copy(c, slot):
        r = pl.multiple_of(row0 + c * TT, TT)
        return pltpu.make_async_copy(h2_hbm.at[pl.ds(r, TT)], buf.at[slot], sem.at[slot])

    tile_copy(lo, 0).start()
    acc[...] = jnp.zeros_like(acc)
    slot_id = (lax.broadcasted_iota(jnp.int32, (TME, TT), 0) + g * TME).astype(f32)

    def body(c, carry):
        slot = (c - lo) % 2
        tile_copy(c, slot).wait()

        @pl.when(c + 1 < hi)
        def _():
            tile_copy(c + 1, 1 - slot).start()

        pm = posm_ref[:, pl.ds(pl.multiple_of(c * TT, TT), TT)]
        onehot = jnp.where(slot_id == pm, 1.0, 0.0).astype(bf16)
        acc[...] += _dot(onehot, buf[slot])
        return carry

    lax.fori_loop(lo, hi, body, 0)

    x = acc[...].astype(bf16)
    logits = _dot(x, rw_ref[...])
    lane = lax.broadcasted_iota(jnp.int32, logits.shape, 1)
    logits = jnp.where(lane < N_EXPERTS, logits, NEG)
    ex = jnp.exp(logits - jnp.max(logits, axis=-1, keepdims=True))
    gate = jnp.sum(jnp.where(lane == e, ex, 0.0), axis=-1, keepdims=True) / jnp.sum(ex, axis=-1, keepdims=True)
    hg = _dot(x, wg_ref[...])
    hu = _dot(x, wu_ref[...])
    h = (hg * (1.0 / (1.0 + jnp.exp(-hg))) * hu).astype(bf16)
    ye_ref[...] = _dot(h, wd_ref[...]) * gate


def _ffn(h2, posm, base, lw, row0, nrows):
    cap = CAPACITY_FACTOR * nrows // N_EXPERTS
    ng = cap // TME
    base = base[:, :, 0]
    ends = jnp.concatenate([base[1:], jnp.full((1, N_EXPERTS), cap, f32)], axis=0)
    edges = (jnp.arange(ng, dtype=f32) * TME)[:, None, None]
    c0 = jnp.sum(ends[None] <= edges, axis=1).astype(jnp.int32).T.reshape(-1)
    c1 = jnp.sum(base[None] < edges + TME, axis=1).astype(jnp.int32).T.reshape(-1)
    wspec = pl.BlockSpec((None, D_MODEL, D_EXPERT), lambda e, g, c0, c1: (e, 0, 0))
    return pl.pallas_call(
        functools.partial(_ffn_kernel, row0=row0, ngroups=ng),
        grid_spec=pltpu.PrefetchScalarGridSpec(
            num_scalar_prefetch=2, grid=(N_EXPERTS, ng),
            in_specs=[
                pl.BlockSpec((None, 1, nrows), lambda e, g, c0, c1: (e, 0, 0)),
                pl.BlockSpec(memory_space=pl.ANY),
                pl.BlockSpec((D_MODEL, LANES), lambda e, g, c0, c1: (0, 0)),
                wspec, wspec,
                pl.BlockSpec((None, D_EXPERT, D_MODEL), lambda e, g, c0, c1: (e, 0, 0)),
            ],
            out_specs=pl.BlockSpec((None, TME, D_MODEL), lambda e, g, c0, c1: (e, g, 0)),
            scratch_shapes=[pltpu.VMEM((2, TT, D_MODEL), bf16), pltpu.SemaphoreType.DMA((2,)),
                            pltpu.VMEM((TME, D_MODEL), f32)],
        ),
        out_shape=jax.ShapeDtypeStruct((N_EXPERTS, cap, D_MODEL), f32),
        name="expert_ffn",
        compiler_params=pltpu.CompilerParams(dimension_semantics=("arbitrary", "arbitrary"),
                                             vmem_limit_bytes=VMEM_LIMIT),
    )(c0, c1, posm.reshape(N_EXPERTS, 1, nrows), h2, lw["rw"], lw["e_wg"], lw["e_wu"], lw["e_wd"])


def _combine_kernel(a_ref, posm_ref, x_ref, ye_hbm, o_ref, buf, sem, *, win, ntiles):
    j = pl.program_id(0)

    def window_copy(jj, e, slot):
        a = pl.multiple_of(a_ref[jj * N_EXPERTS + e], 8)
        return pltpu.make_async_copy(ye_hbm.at[e, pl.ds(a, win)], buf.at[slot, pl.ds(e * win, win)],
                                     sem.at[slot])

    @pl.when(j == 0)
    def _():
        for e in range(N_EXPERTS):
            window_copy(0, e, 0).start()

    slot = j % 2
    for e in range(N_EXPERTS):
        window_copy(j, e, slot).wait()

    @pl.when(j + 1 < ntiles)
    def _():
        for e in range(N_EXPERTS):
            window_copy(j + 1, e, 1 - slot).start()

    row = lax.broadcasted_iota(jnp.int32, (win, TT), 0)
    onehot = jnp.concatenate(
        [jnp.where((row + a_ref[j * N_EXPERTS + e]).astype(f32) == posm_ref[e:e + 1, :], 1.0, 0.0)
         for e in range(N_EXPERTS)], axis=0).astype(bf16)
    y = buf[slot]
    hi = y.astype(bf16)
    lo = (y - hi.astype(f32)).astype(bf16)
    o_ref[...] = x_ref[...] + _dot_tn(onehot, hi) + _dot_tn(onehot, lo)


def _combine(x, ye, posm, starts, row0, nrows, win):
    nt = nrows // TT
    blk0 = row0 // TT
    return pl.pallas_call(
        functools.partial(_combine_kernel, win=win, ntiles=nt),
        grid_spec=pltpu.PrefetchScalarGridSpec(
            num_scalar_prefetch=1, grid=(nt,),
            in_specs=[pl.BlockSpec((N_EXPERTS, TT), lambda j, a: (0, j)),
                      pl.BlockSpec((TT, D_MODEL), lambda j, a: (blk0 + j, 0)),
                      pl.BlockSpec(memory_space=pl.ANY)],
            out_specs=pl.BlockSpec((TT, D_MODEL), lambda j, a: (blk0 + j, 0)),
            scratch_shapes=[pltpu.VMEM((2, N_EXPERTS * win, D_MODEL), f32), pltpu.SemaphoreType.DMA((2,))],
        ),
        out_shape=jax.ShapeDtypeStruct(x.shape, f32),
        input_output_aliases={2: 0},
        name="combine",
        compiler_params=pltpu.CompilerParams(dimension_semantics=("arbitrary",), vmem_limit_bytes=VMEM_LIMIT),
    )(starts, posm, x, ye)


def _final_kernel(x_ref, g_ref, o_ref):
    o_ref[...] = _rms(x_ref[...], g_ref[...])


def _final_norm(x, g, row0, nrows):
    off = row0 // TM
    return pl.pallas_call(
        _final_kernel, grid=(nrows // TM,),
        in_specs=[pl.BlockSpec((TM, D_MODEL), lambda i: (i + off, 0)),
                  pl.BlockSpec((1, D_MODEL), lambda i: (0, 0))],
        out_specs=pl.BlockSpec((TM, D_MODEL), lambda i: (i, 0)),
        out_shape=jax.ShapeDtypeStruct((nrows, D_MODEL), f32),
        name="final_norm",
        compiler_params=pltpu.CompilerParams(dimension_semantics=("parallel",)),
    )(x, g)


def _natten_bias_table(rpb):
    c = np.arange(GRID_W)
    cs = np.clip(c - WIN_C // 2, 0, GRID_W - WIN_C)
    kc = np.arange(GRID_W)
    col_mask = (kc[None, :] >= cs[:, None]) & (kc[None, :] < cs[:, None] + WIN_C)
    dc = np.clip(kc[None, :] - c[:, None], -(WIN_C - 1), WIN_C - 1) + (WIN_C - 1)
    tb = jnp.where(col_mask[None, None], rpb[:, :, dc].astype(f32), NEG)
    dw = np.arange(WIN_R)[:, None] + np.arange(WIN_R)[None, :]
    t = tb[:, dw]
    t = t.transpose(1, 0, 3, 2, 4).reshape(WIN_R, 2, 2 * GRID_W, WIN_R * GRID_W)
    return t


def _rope_slabs():
    inv = 1.0 / (ROPE_THETA ** (jnp.arange(0, ROPE_DIM, 2, dtype=f32) / ROPE_DIM))
    ang = jnp.arange(SEQ, dtype=f32)[:, None] * inv[None, :]
    cos, sin = jnp.cos(ang), jnp.sin(ang)
    ones = jnp.ones((SEQ, NOPE_DIM), f32)
    zpad = jnp.zeros((SEQ, LANES - NOPE_DIM - ROPE_DIM), f32)
    cosq = jnp.concatenate([ones, cos, cos, zpad], axis=1)
    sinq = jnp.concatenate([0.0 * ones, sin, sin, zpad], axis=1)
    return cosq, sinq


def _pack_layer(i, norm1_g, w_in, a_vnorm_g, a_ws, a_bs, b_conv, c_rpb, d_qnorm_g, d_kvnorm_g,
                d_w_uq, d_w_ukv, out_norm_g, w_o, norm2_g, router_w, e_w_gate, e_w_up, e_w_down):
    half = ROPE_DIM // 2
    w = w_in[i]
    kr0 = OFF_D + Q_RANK + KV_RANK
    kr = w[:, kr0:kr0 + ROPE_DIM]
    z64 = jnp.zeros((D_MODEL, NOPE_DIM), f32)
    z32 = jnp.zeros((D_MODEL, LANES - NOPE_DIM - ROPE_DIM), f32)
    kr_slab = jnp.concatenate([z64, kr, z32], axis=1)
    kr_rot = jnp.concatenate([z64, -kr[:, half:], kr[:, :half], z32], axis=1)
    w_pack = jnp.concatenate([w[:, :kr0], kr_slab, kr_rot], axis=1).astype(bf16)

    uq = d_w_uq[i].reshape(Q_RANK, D_HEADS, NOPE_DIM + ROPE_DIM)
    nope, rope = uq[..., :NOPE_DIM], uq[..., NOPE_DIM:]
    zq = jnp.zeros((Q_RANK, D_HEADS, LANES - NOPE_DIM - ROPE_DIM), f32)
    w_uq = jnp.concatenate([nope, rope, zq], axis=-1).reshape(Q_RANK, D_HEADS * LANES)
    w_uq_rot = jnp.concatenate([0.0 * nope, -rope[..., half:], rope[..., :half], zq],
                               axis=-1).reshape(Q_RANK, D_HEADS * LANES)
    ukv = d_w_ukv[i].reshape(KV_RANK, D_HEADS, NOPE_DIM + V_DIM)
    w_k = jnp.concatenate([ukv[..., :NOPE_DIM], jnp.zeros((KV_RANK, D_HEADS, LANES - NOPE_DIM), f32)],
                          axis=-1).reshape(KV_RANK, D_HEADS * LANES)
    w_v = ukv[..., NOPE_DIM:].reshape(KV_RANK, D_HEADS * V_DIM)
    rw = jnp.concatenate([router_w[i], jnp.zeros((D_MODEL, LANES - N_EXPERTS), f32)], axis=1)
    return {
        "g1": norm1_g[i].reshape(1, D_MODEL),
        "w_in": w_pack,
        "a_vg": a_vnorm_g[i].reshape(1, GROUP_W),
        "a_ws": a_ws[i].astype(bf16),
        "a_bias": jnp.repeat(a_bs[i].T, HEAD_DIM, axis=1),
        "b_conv": b_conv[i],
        "c_bias": _natten_bias_table(c_rpb[i]),
        "d_qg": d_qnorm_g[i].reshape(1, Q_RANK),
        "d_kvg": d_kvnorm_g[i].reshape(1, KV_RANK),
        "w_uq": w_uq.astype(bf16),
        "w_uq_rot": w_uq_rot.astype(bf16),
        "w_k": w_k.astype(bf16),
        "w_v": w_v.astype(bf16),
        "og": out_norm_g[i].reshape(1, D_MODEL),
        "w_o": w_o[i].astype(bf16),
        "g2": norm2_g[i].reshape(1, D_MODEL),
        "rw": rw.astype(bf16),
        "e_wg": e_w_gate[i].astype(bf16),
        "e_wu": e_w_up[i].astype(bf16),
        "e_wd": e_w_down[i].astype(bf16),
    }


def _moe(xn, h2, aff, lw, tri):
    x = xn
    for row0, nrows in ((0, BATCH * SEQ), (BATCH * SEQ, DEC_BATCH * SEQ)):
        cap = CAPACITY_FACTOR * nrows // N_EXPERTS
        posm, base = _route(aff, row0, nrows, tri)
        ye = _ffn(h2, posm, base, lw, row0, nrows)
        first = base[:, :, 0].astype(jnp.int32)
        count = jnp.diff(first, axis=0, append=jnp.full((1, N_EXPERTS), cap, jnp.int32))

        def run(win, x=x, ye=ye, posm=posm, first=first, row0=row0, nrows=nrows, cap=cap):
            starts = jnp.minimum(first // 8 * 8, cap - win).reshape(-1)
            return _combine(x, ye, posm, starts, row0, nrows, win)

        fits = jnp.max(count) <= WIN_SMALL - 7
        x = lax.cond(fits, lambda: run(WIN_SMALL), lambda: run(WIN_LARGE))
    return x


def kernel(x_prompt, x_sample, norm1_g, w_in, a_vnorm_g, a_ws, a_bs, b_conv, c_rpb, d_qnorm_g, d_kvnorm_g, d_w_uq, d_w_ukv, out_norm_g, w_o, norm2_g, router_w, e_w_gate, e_w_up, e_w_down, final_norm_g):
    n_p = BATCH * SEQ
    n_s = DEC_BATCH * SEQ
    x = jnp.concatenate([x_prompt.reshape(n_p, D_MODEL), x_sample.reshape(n_s, D_MODEL)], axis=0)
    cosq, sinq = _rope_slabs()
    tri = jnp.asarray(np.triu(np.ones((TT, TT), np.float32), 1), bf16)
    for i in range(DEPTH):
        lw = _pack_layer(i, norm1_g, w_in, a_vnorm_g, a_ws, a_bs, b_conv, c_rpb, d_qnorm_g, d_kvnorm_g,
                         d_w_uq, d_w_ukv, out_norm_g, w_o, norm2_g, router_w, e_w_gate, e_w_up, e_w_down)
        ya, yb, zc, qt, k4, vt = _inproj(x, lw, cosq, sinq)
        yc = _natten(zc, lw["c_bias"])
        yd = _mla(qt, k4, vt)
        xn, h2, aff = _outproj(ya, yb, yc, yd, x, lw)
        x = _moe(xn, h2, aff, lw, tri)
    g = final_norm_g.reshape(1, D_MODEL)
    y_p = _final_norm(x, g, 0, n_p).reshape(BATCH, SEQ, D_MODEL)
    y_s = _final_norm(x, g, n_p, n_s).reshape(DEC_BATCH, SEQ, D_MODEL)
    return (y_p, y_s)
```

```python
import functools
import math

import numpy as np
import jax
import jax.numpy as jnp
from jax import lax
from jax.experimental import pallas as pl
from jax.experimental.pallas import tpu as pltpu

D_MODEL = 1024
BATCH = 4
SEQ = 8192
DEPTH = 4
DEC_BATCH = 2
GROUP_W = 256
HEAD_DIM = 64
A_HEADS = 4
CHUNK = 128
C_HEADS = 4
GRID_W = 64
WIN_R = 8
WIN_C = 16
D_HEADS = 4
Q_RANK = 256
KV_RANK = 128
NOPE_DIM = 64
ROPE_DIM = 32
V_DIM = 64
ROPE_THETA = 10000.0
N_EXPERTS = 16
D_EXPERT = 1024
CAPACITY_FACTOR = 2
EPS = 1e-6

NSEQ = BATCH + DEC_BATCH
T_ALL = NSEQ * SEQ
ROWS = SEQ // GRID_W
LANES = 128
NEG = -1e30

A_COLS = 2 * GROUP_W
B_COLS = 3 * GROUP_W
C_COLS = 3 * GROUP_W
OFF_B = A_COLS
OFF_C = OFF_B + B_COLS
OFF_D = OFF_C + C_COLS
D_PACK = Q_RANK + KV_RANK + 2 * LANES
N_IN_PACK = OFF_D + D_PACK

TM = 512
TQ = 256
TK = 512
L_ROWS = 16
MLA_LAG = 1
MLA_QSCALE = (NOPE_DIM + ROPE_DIM) ** -0.5 * math.log2(math.e)
ROWS_PER_STEP = 8
TME = 512
TT = 256
ROW_ALIGN = 16
DISP_SMALL = 80
DISP_LARGE = TT + ROW_ALIGN
WIN_SMALL = 72
WIN_LARGE = TT + 8
VMEM_LIMIT = 56 * 1024 * 1024

f32 = jnp.float32
bf16 = jnp.bfloat16


def _rms(x, g):
    return x * lax.rsqrt(jnp.mean(x * x, axis=-1, keepdims=True) + EPS) * g


def _dot(a, b):
    return jnp.dot(a, b, preferred_element_type=f32)


def _dot_nt(a, b):
    return lax.dot_general(a, b, (((1,), (1,)), ((), ())), preferred_element_type=f32)


def _dot_tn(a, b):
    return lax.dot_general(a, b, (((0,), (0,)), ((), ())), preferred_element_type=f32)


def _inproj_kernel(x_ref, xp_ref, xn_ref, g1_ref, win_ref, avg_ref, aws_ref, abias_ref,
                   bconv_ref, qg_ref, kvg_ref, wuq_ref, wuqr_ref, wk_ref, wv_ref,
                   cos_ref, sin_ref,
                   ya_ref, yb_ref, zc_ref, qt_ref, k4_ref, vt_ref):
    i = pl.program_id(0)
    j = i % (SEQ // TM)
    g1 = g1_ref[...]
    h = _rms(x_ref[...], g1).astype(bf16)

    za = jax.nn.gelu(_dot(h, win_ref[:, 0:A_COLS]))
    u = za[:, :GROUP_W]
    v = _rms(za[:, GROUP_W:], avg_ref[...]).astype(bf16)
    lane = lax.broadcasted_iota(jnp.int32, (CHUNK, GROUP_W), 1)
    for c in range(TM // CHUNK):
        vc = v[c * CHUNK:(c + 1) * CHUNK, :]
        sv = abias_ref[...]
        for hd in range(A_HEADS):
            r = _dot(aws_ref[hd], vc)
            sv = sv + jnp.where((lane >= hd * HEAD_DIM) & (lane < (hd + 1) * HEAD_DIM), r, 0.0)
        ya_ref[c * CHUNK:(c + 1) * CHUNK, :] = u[c * CHUNK:(c + 1) * CHUNK, :] * sv

    zb = _dot(h, win_ref[:, OFF_B:OFF_B + B_COLS])
    bg = zb[:, :GROUP_W]
    y = zb[:, GROUP_W:2 * GROUP_W] * zb[:, 2 * GROUP_W:]
    xh = jnp.concatenate([xp_ref[...], xn_ref[...]], axis=0)
    hh = _rms(xh, g1).astype(bf16)
    zh = _dot(hh, win_ref[:, OFF_B + GROUP_W:OFF_B + B_COLS])
    yh = zh[:, :GROUP_W] * zh[:, GROUP_W:]
    y_before = jnp.where(j == 0, 0.0, yh[7:8, :])
    y_after = jnp.where(j == SEQ // TM - 1, 0.0, yh[8:9, :])
    row = lax.broadcasted_iota(jnp.int32, (TM, GROUP_W), 0)
    y_m1 = jnp.where(row == 0, y_before, pltpu.roll(y, 1, axis=0))
    y_p1 = jnp.where(row == TM - 1, y_after, pltpu.roll(y, TM - 1, axis=0))
    wc = bconv_ref[...]
    yb_ref[...] = bg * (wc[0:1, :] * y_m1 + wc[1:2, :] * y + wc[2:3, :] * y_p1)

    zc = _dot(h, win_ref[:, OFF_C:OFF_C + C_COLS])
    zc_ref[:, 0:GROUP_W] = (zc[:, 0:GROUP_W] * (HEAD_DIM ** -0.5)).astype(bf16)
    zc_ref[:, GROUP_W:] = zc[:, GROUP_W:].astype(bf16)

    zd = _dot(h, win_ref[:, OFF_D:OFF_D + D_PACK])
    cqn = _rms(zd[:, 0:Q_RANK], qg_ref[...]).astype(bf16)
    ckvn = _rms(zd[:, Q_RANK:Q_RANK + KV_RANK], kvg_ref[...]).astype(bf16)
    cos = cos_ref[...]
    sin = sin_ref[...]
    o = Q_RANK + KV_RANK
    kr = zd[:, o:o + LANES] * cos + zd[:, o + LANES:o + 2 * LANES] * sin
    q_pre = _dot(cqn, wuq_ref[...])
    q_rot = _dot(cqn, wuqr_ref[...])
    kn = _dot(ckvn, wk_ref[...])
    for hd in range(D_HEADS):
        sl = slice(hd * LANES, (hd + 1) * LANES)
        qt_ref[hd] = ((q_pre[:, sl] * cos + q_rot[:, sl] * sin) * MLA_QSCALE).T.astype(bf16)
        k4_ref[hd] = (kn[:, sl] + kr).astype(bf16)
    vt = _dot(ckvn, wv_ref[...]).T.astype(bf16)
    for hd in range(D_HEADS):
        vt_ref[hd] = vt[hd * V_DIM:(hd + 1) * V_DIM, :]


def _inproj(x, lw, cosq, sinq):
    nt = T_ALL // TM
    tps = SEQ // TM
    full = lambda shape: pl.BlockSpec(shape, lambda i: (0,) * len(shape))
    in_specs = [
        pl.BlockSpec((TM, D_MODEL), lambda i: (i, 0)),
        pl.BlockSpec((8, D_MODEL), lambda i: (jnp.maximum(i * (TM // 8) - 1, 0), 0)),
        pl.BlockSpec((8, D_MODEL), lambda i: (jnp.minimum((i + 1) * (TM // 8), T_ALL // 8 - 1), 0)),
        full((1, D_MODEL)),
        full((D_MODEL, N_IN_PACK)),
        full((1, GROUP_W)),
        full((A_HEADS, CHUNK, CHUNK)),
        full((CHUNK, GROUP_W)),
        full((3, GROUP_W)),
        full((1, Q_RANK)),
        full((1, KV_RANK)),
        full((Q_RANK, D_HEADS * LANES)),
        full((Q_RANK, D_HEADS * LANES)),
        full((KV_RANK, D_HEADS * LANES)),
        full((KV_RANK, D_HEADS * V_DIM)),
        pl.BlockSpec((TM, LANES), lambda i: (i % tps, 0)),
        pl.BlockSpec((TM, LANES), lambda i: (i % tps, 0)),
    ]
    out_specs = [
        pl.BlockSpec((TM, GROUP_W), lambda i: (i, 0)),
        pl.BlockSpec((TM, GROUP_W), lambda i: (i, 0)),
        pl.BlockSpec((TM, C_COLS), lambda i: (i, 0)),
        pl.BlockSpec((None, D_HEADS, LANES, TM), lambda i: (i // tps, 0, 0, i % tps)),
        pl.BlockSpec((None, D_HEADS, TM, LANES), lambda i: (i // tps, 0, i % tps, 0)),
        pl.BlockSpec((None, D_HEADS, V_DIM, TM), lambda i: (i // tps, 0, 0, i % tps)),
    ]
    out_shape = [
        jax.ShapeDtypeStruct((T_ALL, GROUP_W), f32),
        jax.ShapeDtypeStruct((T_ALL, GROUP_W), f32),
        jax.ShapeDtypeStruct((T_ALL, C_COLS), bf16),
        jax.ShapeDtypeStruct((NSEQ, D_HEADS, LANES, SEQ), bf16),
        jax.ShapeDtypeStruct((NSEQ, D_HEADS, SEQ, LANES), bf16),
        jax.ShapeDtypeStruct((NSEQ, D_HEADS, V_DIM, SEQ), bf16),
    ]
    return pl.pallas_call(
        _inproj_kernel, grid=(nt,), in_specs=in_specs, out_specs=out_specs, out_shape=out_shape,
        name="inproj",
        compiler_params=pltpu.CompilerParams(dimension_semantics=("parallel",),
                                             vmem_limit_bytes=VMEM_LIMIT),
    )(x, x, x, lw["g1"], lw["w_in"], lw["a_vg"], lw["a_ws"], lw["a_bias"], lw["b_conv"],
      lw["d_qg"], lw["d_kvg"], lw["w_uq"], lw["w_uq_rot"], lw["w_k"], lw["w_v"], cosq, sinq)


def _natten_kernel(q_ref, k_ref, v_ref, bias_ref, o_ref):
    jb = pl.program_id(1)
    lane = lax.broadcasted_iota(jnp.int32, (GRID_W, LANES), 1)
    lo = lane < HEAD_DIM
    nkeys = WIN_R * GRID_W

    def row_body(i, carry):
        r = jb * ROWS_PER_STEP + i
        rs = jnp.clip(r - WIN_R // 2, 0, ROWS - WIN_R)
        d = rs - r + (WIN_R - 1)
        kstart = pl.multiple_of(rs * GRID_W, GRID_W)
        qrow = pl.multiple_of(i * GRID_W, GRID_W)
        for s in range(2):
            sl = slice(s * LANES, (s + 1) * LANES)
            qs = q_ref[pl.ds(qrow, GRID_W), sl]
            ks = k_ref[pl.ds(kstart, nkeys), sl]
            vs = v_ref[pl.ds(kstart, nkeys), sl]
            zero = jnp.zeros_like(qs)
            qq = jnp.concatenate([jnp.where(lo, qs, zero), jnp.where(lo, zero, qs)], axis=0)
            sc = _dot_nt(qq, ks) + bias_ref[d, s]
            m = jnp.max(sc, axis=-1, keepdims=True)
            p = jnp.exp(sc - m)
            l = jnp.sum(p, axis=-1, keepdims=True)
            o = _dot(p.astype(bf16), vs) / l
            o_ref[pl.ds(qrow, GRID_W), sl] = jnp.where(lo, o[:GRID_W], o[GRID_W:])
        return carry

    lax.fori_loop(0, ROWS_PER_STEP, row_body, 0)


def _natten(zc, bias):
    nb = ROWS // ROWS_PER_STEP
    tq = ROWS_PER_STEP * GRID_W
    return pl.pallas_call(
        _natten_kernel, grid=(NSEQ, nb),
        in_specs=[
            pl.BlockSpec((tq, GROUP_W), lambda b, j: (b * nb + j, 0)),
            pl.BlockSpec((SEQ, GROUP_W), lambda b, j: (b, 1)),
            pl.BlockSpec((SEQ, GROUP_W), lambda b, j: (b, 2)),
            pl.BlockSpec((WIN_R, 2, 2 * GRID_W, WIN_R * GRID_W), lambda b, j: (0, 0, 0, 0)),
        ],
        out_specs=pl.BlockSpec((tq, GROUP_W), lambda b, j: (b * nb + j, 0)),
        out_shape=jax.ShapeDtypeStruct((T_ALL, GROUP_W), f32),
        name="natten",
        compiler_params=pltpu.CompilerParams(dimension_semantics=("parallel", "arbitrary"),
                                             vmem_limit_bytes=VMEM_LIMIT),
    )(zc, zc, zc, bias)


def _mla_kernel(qt_ref, k_ref, vt_ref, o_ref, s_scr):
    nchunk = SEQ // TK
    heads = list(range(D_HEADS))
    qts = [qt_ref[h] for h in heads]

    def qk(slot, h, ci):
        ks = pl.multiple_of(ci * TK, TK)
        s_scr[slot, h] = _dot(k_ref[h, pl.ds(ks, TK), :], qts[h])

    ones = jnp.ones((L_ROWS, TK), bf16)

    def softmax_pv(slot, h, ci, stat):
        m, acc = stat
        ks = pl.multiple_of(ci * TK, TK)
        vt = jnp.concatenate([vt_ref[h, :, pl.ds(ks, TK)], ones], axis=0)
        s = s_scr[slot, h]
        m_new = jnp.maximum(m, jnp.max(s, axis=0, keepdims=True))
        alpha = jnp.exp2(m - m_new)
        p = jnp.exp2(s - m_new)
        acc = acc * alpha + _dot(vt, p.astype(bf16))
        return m_new, acc

    def half(cur, ci, stats, prefetch=True):
        out = []
        for h in heads:
            if prefetch:
                qk(1 - cur, h, ci + 1)
            if h >= MLA_LAG:
                out.append(softmax_pv(cur, h - MLA_LAG, ci, stats[h - MLA_LAG]))
        for h in range(D_HEADS - MLA_LAG, D_HEADS):
            out.append(softmax_pv(cur, h, ci, stats[h]))
        return tuple(out)

    def body(cp, stats):
        stats = half(0, 2 * cp, stats)
        return half(1, 2 * cp + 1, stats)

    for h in heads:
        qk(0, h, 0)
    init = tuple((jnp.full((1, TQ), NEG, f32), jnp.zeros((V_DIM + L_ROWS, TQ), f32)) for _ in heads)
    res = lax.fori_loop(0, nchunk // 2 - 1, body, init)
    res = half(0, nchunk - 2, res)
    res = half(1, nchunk - 1, res, prefetch=False)
    o_ref[...] = jnp.concatenate([acc[:V_DIM] / acc[V_DIM:V_DIM + 1] for (_, acc) in res], axis=0).T


def _mla(qt, k4, vt):
    nq = SEQ // TQ
    return pl.pallas_call(
        _mla_kernel, grid=(NSEQ, nq),
        in_specs=[
            pl.BlockSpec((None, D_HEADS, LANES, TQ), lambda b, i: (b, 0, 0, i)),
            pl.BlockSpec((None, D_HEADS, SEQ, LANES), lambda b, i: (b, 0, 0, 0)),
            pl.BlockSpec((None, D_HEADS, V_DIM, SEQ), lambda b, i: (b, 0, 0, 0)),
        ],
        out_specs=pl.BlockSpec((TQ, GROUP_W), lambda b, i: (b * nq + i, 0)),
        out_shape=jax.ShapeDtypeStruct((T_ALL, GROUP_W), f32),
        scratch_shapes=[pltpu.VMEM((2, D_HEADS, TK, TQ), f32)],
        name="mla",
        compiler_params=pltpu.CompilerParams(
            dimension_semantics=("parallel", "arbitrary"),
            vmem_limit_bytes=VMEM_LIMIT),
    )(qt, k4, vt)


def _outproj_kernel(ya_ref, yb_ref, yc_ref, yd_ref, x_ref, og_ref, wo_ref, g2_ref, rw_ref,
                    xn_ref, h2_ref, aff_ref):
    acc = x_ref[...]
    for g, y_ref in enumerate((ya_ref, yb_ref, yc_ref, yd_ref)):
        sl = slice(g * GROUP_W, (g + 1) * GROUP_W)
        yg = _rms(y_ref[...], og_ref[:, sl]).astype(bf16)
        acc = acc + _dot(yg, wo_ref[sl, :])
    xn_ref[...] = acc
    h2 = _rms(acc, g2_ref[...]).astype(bf16)
    h2_ref[...] = h2
    logits = _dot(h2, rw_ref[...])
    lane = lax.broadcasted_iota(jnp.int32, logits.shape, 1)
    logits = jnp.where(lane < N_EXPERTS, logits, NEG)
    m = jnp.max(logits, axis=-1, keepdims=True)
    e = jnp.exp(logits - m)
    aff = e / jnp.sum(e, axis=-1, keepdims=True)
    aff_ref[...] = aff.T[:N_EXPERTS, :]


def _outproj(ya, yb, yc, yd, x, lw):
    nt = T_ALL // TM
    full = lambda shape: pl.BlockSpec(shape, lambda i: (0,) * len(shape))
    yspec = pl.BlockSpec((TM, GROUP_W), lambda i: (i, 0))
    xspec = pl.BlockSpec((TM, D_MODEL), lambda i: (i, 0))
    return pl.pallas_call(
        _outproj_kernel, grid=(nt,),
        in_specs=[yspec, yspec, yspec, yspec, xspec, full((1, D_MODEL)), full((D_MODEL, D_MODEL)),
                  full((1, D_MODEL)), full((D_MODEL, LANES))],
        out_specs=[xspec, xspec, pl.BlockSpec((N_EXPERTS, TM), lambda i: (0, i))],
        out_shape=[jax.ShapeDtypeStruct((T_ALL, D_MODEL), f32),
                   jax.ShapeDtypeStruct((T_ALL, D_MODEL), bf16),
                   jax.ShapeDtypeStruct((N_EXPERTS, T_ALL), f32)],
        name="outproj",
        compiler_params=pltpu.CompilerParams(dimension_semantics=("parallel",),
                                             vmem_limit_bytes=VMEM_LIMIT),
    )(ya, yb, yc, yd, x, lw["og"], lw["w_o"], lw["g2"], lw["rw"])


def _route_kernel(aff_ref, tri_ref, posm_ref, base_ref, thr_scr, need_scr, runeq_scr, runsel_scr, *, cap):
    j = pl.program_id(0)

    @pl.when(j == 0)
    def _():
        def refine(i, thr):
            cand = thr | jnp.left_shift(jnp.int32(1), 30 - i)
            bits = pltpu.bitcast(aff_ref[...], jnp.int32)
            cnt = jnp.sum(jnp.where(bits >= cand, 1.0, 0.0), axis=1, keepdims=True)
            return jnp.where(cnt >= cap, cand, thr)

        thr = lax.fori_loop(0, 31, refine, jnp.zeros((N_EXPERTS, 1), jnp.int32))
        bits = pltpu.bitcast(aff_ref[...], jnp.int32)
        above = jnp.sum(jnp.where(bits > thr, 1.0, 0.0), axis=1, keepdims=True)
        thr_scr[...] = jnp.broadcast_to(thr, (N_EXPERTS, LANES))
        need_scr[...] = jnp.broadcast_to(cap - above, (N_EXPERTS, LANES))
        runeq_scr[...] = jnp.zeros((N_EXPERTS, LANES), f32)
        runsel_scr[...] = jnp.zeros((N_EXPERTS, LANES), f32)

    t0 = pl.multiple_of(j * TT, TT)
    bits = pltpu.bitcast(aff_ref[:, pl.ds(t0, TT)], jnp.int32)
    thr = thr_scr[:, 0:1]
    eq = bits == thr
    eq_rank = runeq_scr[:, 0:1] + _dot(jnp.where(eq, 1.0, 0.0).astype(bf16), tri_ref[...])
    sel = (bits > thr) | (eq & (eq_rank < need_scr[:, 0:1]))
    sel_f = jnp.where(sel, 1.0, 0.0)
    base = runsel_scr[...]
    pos = base[:, 0:1] + _dot(sel_f.astype(bf16), tri_ref[...])
    posm_ref[...] = jnp.where(sel, pos, -1.0)
    base_ref[...] = base
    runeq_scr[...] += jnp.sum(jnp.where(eq, 1.0, 0.0), axis=1, keepdims=True)
    runsel_scr[...] += jnp.sum(sel_f, axis=1, keepdims=True)


def _route(aff, row0, nrows, tri):
    cap = CAPACITY_FACTOR * nrows // N_EXPERTS
    nt = nrows // TT
    blk0 = row0 // nrows
    vec = pltpu.VMEM((N_EXPERTS, LANES), f32)
    return pl.pallas_call(
        functools.partial(_route_kernel, cap=cap), grid=(nt,),
        in_specs=[pl.BlockSpec((N_EXPERTS, nrows), lambda j: (0, blk0)),
                  pl.BlockSpec((TT, TT), lambda j: (0, 0))],
        out_specs=[pl.BlockSpec((N_EXPERTS, TT), lambda j: (0, j)),
                   pl.BlockSpec((None, N_EXPERTS, LANES), lambda j: (j, 0, 0))],
        out_shape=[jax.ShapeDtypeStruct((N_EXPERTS, nrows), f32),
                   jax.ShapeDtypeStruct((nt, N_EXPERTS, LANES), f32)],
        scratch_shapes=[pltpu.VMEM((N_EXPERTS, LANES), jnp.int32), vec, vec, vec],
        name="route",
        compiler_params=pltpu.CompilerParams(dimension_semantics=("arbitrary",)),
    )(aff, tri)


def _dispatch_kernel(big_ref, first_ref, posm_ref, h2_ref, xe_hbm, stage, sem, carry, *, cap, ntiles):
    j = pl.program_id(0)
    slot = j % 2

    @pl.when(j == 0)
    def _():
        carry[...] = jnp.zeros_like(carry)
        stage[1, 0:DISP_LARGE, :] = jnp.zeros((DISP_LARGE, D_MODEL), bf16)
        pads = [pltpu.make_async_copy(stage.at[1, pl.ds(0, DISP_LARGE)], xe_hbm.at[e, pl.ds(cap, DISP_LARGE)],
                                      sem.at[1]) for e in range(N_EXPERTS)]
        for p in pads:
            p.start()
        for p in pads:
            p.wait()

    def run_start(jj, e):
        return first_ref[jj * N_EXPERTS + e]

    def window_copy(jj, e, sl, win):
        a = pl.multiple_of(run_start(jj, e) // ROW_ALIGN * ROW_ALIGN, ROW_ALIGN)
        return pltpu.make_async_copy(stage.at[sl, pl.ds(e * win, win)], xe_hbm.at[e, pl.ds(a, win)], sem.at[sl])

    def for_tile_window(jj, fn):
        for flag, win in ((0, DISP_SMALL), (1, DISP_LARGE)):
            @pl.when(big_ref[jj] == flag)
            def _():
                fn(win)

    def wait_all(jj, sl):
        def go(win):
            for e in range(N_EXPERTS):
                window_copy(jj, e, sl, win).wait()
        for_tile_window(jj, go)

    def compact(win):
        row = lax.broadcasted_iota(jnp.int32, (win, TT), 0)
        starts = [run_start(j, e) // ROW_ALIGN * ROW_ALIGN for e in range(N_EXPERTS)]
        onehot = jnp.concatenate(
            [jnp.where((row + starts[e]).astype(f32) == posm_ref[e:e + 1, :], 1.0, 0.0)
             for e in range(N_EXPERTS)], axis=0).astype(bf16)
        rows = _dot(onehot, h2_ref[...])
        head = lax.broadcasted_iota(jnp.int32, (ROW_ALIGN, D_MODEL), 0)
        for e in range(N_EXPERTS):
            lead = run_start(j, e) - starts[e]
            w = rows[e * win:(e + 1) * win]
            top = w[:ROW_ALIGN] + jnp.where(head < lead, carry[e], 0.0)
            stage[slot, e * win:e * win + ROW_ALIGN, :] = top.astype(bf16)
            stage[slot, e * win + ROW_ALIGN:(e + 1) * win, :] = w[ROW_ALIGN:].astype(bf16)
        for e in range(N_EXPERTS):
            end = jnp.where(j + 1 < ntiles, run_start(jnp.minimum(j + 1, ntiles - 1), e), cap)
            off = jnp.minimum(end // ROW_ALIGN * ROW_ALIGN - starts[e], win - ROW_ALIGN)
            off = pl.multiple_of(off, ROW_ALIGN)
            carry[e] = stage[slot, pl.ds(e * win + off, ROW_ALIGN), :].astype(f32)

    for_tile_window(j, compact)

    @pl.when(j > 0)
    def _():
        wait_all(j - 1, 1 - slot)

    def start_all(win):
        for e in range(N_EXPERTS):
            window_copy(j, e, slot, win).start()
    for_tile_window(j, start_all)

    @pl.when(j == ntiles - 1)
    def _():
        wait_all(j, slot)


def _dispatch(h2, posm, first, count, row0, nrows):
    cap = CAPACITY_FACTOR * nrows // N_EXPERTS
    nt = nrows // TT
    blk0 = row0 // TT
    big = (jnp.max(count, axis=1) > DISP_SMALL - ROW_ALIGN + 1).astype(jnp.int32)
    return pl.pallas_call(
        functools.partial(_dispatch_kernel, cap=cap, ntiles=nt),
        grid_spec=pltpu.PrefetchScalarGridSpec(
            num_scalar_prefetch=2, grid=(nt,),
            in_specs=[pl.BlockSpec((N_EXPERTS, TT), lambda j, *_: (0, j)),
                      pl.BlockSpec((TT, D_MODEL), lambda j, *_: (blk0 + j, 0))],
            out_specs=pl.BlockSpec(memory_space=pl.ANY),
            scratch_shapes=[pltpu.VMEM((2, N_EXPERTS * DISP_LARGE, D_MODEL), bf16),
                            pltpu.SemaphoreType.DMA((2,)),
                            pltpu.VMEM((N_EXPERTS, ROW_ALIGN, D_MODEL), f32)],
        ),
        out_shape=jax.ShapeDtypeStruct((N_EXPERTS, cap + DISP_LARGE, D_MODEL), bf16),
        name="dispatch",
        compiler_params=pltpu.CompilerParams(dimension_semantics=("arbitrary",), vmem_limit_bytes=VMEM_LIMIT),
    )(big, first.reshape(-1), posm, h2)


def _ffn_kernel(xe_ref, rw_ref, wg_ref, wu_ref, wd_ref, ye_ref):
    e = pl.program_id(0)
    x = xe_ref[...]
    logits = _dot(x, rw_ref[...])
    lane = lax.broadcasted_iota(jnp.int32, logits.shape, 1)
    logits = jnp.where(lane < N_EXPERTS, logits, NEG)
    ex = jnp.exp(logits - jnp.max(logits, axis=-1, keepdims=True))
    gate = jnp.sum(jnp.where(lane == e, ex, 0.0), axis=-1, keepdims=True) / jnp.sum(ex, axis=-1, keepdims=True)
    hg = _dot(x, wg_ref[...])
    hu = _dot(x, wu_ref[...])
    h = (hg * (1.0 / (1.0 + jnp.exp(-hg))) * hu).astype(bf16)
    ye_ref[...] = _dot(h, wd_ref[...]) * gate


def _ffn(xe, lw, nrows):
    cap = CAPACITY_FACTOR * nrows // N_EXPERTS
    wspec = pl.BlockSpec((None, D_MODEL, D_EXPERT), lambda e, t: (e, 0, 0))
    return pl.pallas_call(
        _ffn_kernel, grid=(N_EXPERTS, cap // TME),
        in_specs=[
            pl.BlockSpec((None, TME, D_MODEL), lambda e, t: (e, t, 0)),
            pl.BlockSpec((D_MODEL, LANES), lambda e, t: (0, 0)),
            wspec, wspec,
            pl.BlockSpec((None, D_EXPERT, D_MODEL), lambda e, t: (e, 0, 0)),
        ],
        out_specs=pl.BlockSpec((None, TME, D_MODEL), lambda e, t: (e, t, 0)),
        out_shape=jax.ShapeDtypeStruct((N_EXPERTS, cap, D_MODEL), f32),
        name="expert_ffn",
        compiler_params=pltpu.CompilerParams(dimension_semantics=("parallel", "arbitrary"),
                                             vmem_limit_bytes=VMEM_LIMIT),
    )(xe, lw["rw"], lw["e_wg"], lw["e_wu"], lw["e_wd"])


def _combine_kernel(big_ref, first_ref, posm_ref, x_ref, ye_hbm, o_ref, buf, sem, *, cap, ntiles):
    j = pl.program_id(0)

    def window_start(jj, e, win):
        return jnp.minimum(first_ref[jj * N_EXPERTS + e] // 8 * 8, cap - win)

    def window_copy(jj, e, slot, win):
        a = pl.multiple_of(window_start(jj, e, win), 8)
        return pltpu.make_async_copy(ye_hbm.at[e, pl.ds(a, win)], buf.at[slot, pl.ds(e * win, win)],
                                     sem.at[slot])

    def for_tile_window(jj, fn):
        for flag, win in ((0, WIN_SMALL), (1, WIN_LARGE)):
            @pl.when(big_ref[jj] == flag)
            def _():
                fn(win)

    def start_all(jj, slot):
        def go(win):
            for e in range(N_EXPERTS):
                window_copy(jj, e, slot, win).start()
        for_tile_window(jj, go)

    @pl.when(j == 0)
    def _():
        start_all(0, 0)

    slot = j % 2

    def wait_all(win):
        for e in range(N_EXPERTS):
            window_copy(j, e, slot, win).wait()
    for_tile_window(j, wait_all)

    @pl.when(j + 1 < ntiles)
    def _():
        start_all(j + 1, 1 - slot)

    def add_rows(win):
        row = lax.broadcasted_iota(jnp.int32, (win, TT), 0)
        onehot = jnp.concatenate(
            [jnp.where((row + window_start(j, e, win)).astype(f32) == posm_ref[e:e + 1, :], 1.0, 0.0)
             for e in range(N_EXPERTS)], axis=0).astype(bf16)
        y = buf[slot, 0:N_EXPERTS * win, :]
        hi = y.astype(bf16)
        lo = (y - hi.astype(f32)).astype(bf16)
        o_ref[...] = x_ref[...] + _dot_tn(onehot, hi) + _dot_tn(onehot, lo)
    for_tile_window(j, add_rows)


def _combine(x, ye, posm, first, count, row0, nrows):
    cap = CAPACITY_FACTOR * nrows // N_EXPERTS
    nt = nrows // TT
    blk0 = row0 // TT
    big = (jnp.max(count, axis=1) > WIN_SMALL - 7).astype(jnp.int32)
    return pl.pallas_call(
        functools.partial(_combine_kernel, cap=cap, ntiles=nt),
        grid_spec=pltpu.PrefetchScalarGridSpec(
            num_scalar_prefetch=2, grid=(nt,),
            in_specs=[pl.BlockSpec((N_EXPERTS, TT), lambda j, *_: (0, j)),
                      pl.BlockSpec((TT, D_MODEL), lambda j, *_: (blk0 + j, 0)),
                      pl.BlockSpec(memory_space=pl.ANY)],
            out_specs=pl.BlockSpec((TT, D_MODEL), lambda j, *_: (blk0 + j, 0)),
            scratch_shapes=[pltpu.VMEM((2, N_EXPERTS * WIN_LARGE, D_MODEL), f32), pltpu.SemaphoreType.DMA((2,))],
        ),
        out_shape=jax.ShapeDtypeStruct(x.shape, f32),
        input_output_aliases={3: 0},
        name="combine",
        compiler_params=pltpu.CompilerParams(dimension_semantics=("arbitrary",), vmem_limit_bytes=VMEM_LIMIT),
    )(big, first.reshape(-1), posm, x, ye)


def _final_kernel(x_ref, g_ref, o_ref):
    o_ref[...] = _rms(x_ref[...], g_ref[...])


def _final_norm(x, g, row0, nrows):
    off = row0 // TM
    return pl.pallas_call(
        _final_kernel, grid=(nrows // TM,),
        in_specs=[pl.BlockSpec((TM, D_MODEL), lambda i: (i + off, 0)),
                  pl.BlockSpec((1, D_MODEL), lambda i: (0, 0))],
        out_specs=pl.BlockSpec((TM, D_MODEL), lambda i: (i, 0)),
        out_shape=jax.ShapeDtypeStruct((nrows, D_MODEL), f32),
        name="final_norm",
        compiler_params=pltpu.CompilerParams(dimension_semantics=("parallel",)),
    )(x, g)


def _natten_bias_table(rpb):
    c = np.arange(GRID_W)
    cs = np.clip(c - WIN_C // 2, 0, GRID_W - WIN_C)
    kc = np.arange(GRID_W)
    col_mask = (kc[None, :] >= cs[:, None]) & (kc[None, :] < cs[:, None] + WIN_C)
    dc = np.clip(kc[None, :] - c[:, None], -(WIN_C - 1), WIN_C - 1) + (WIN_C - 1)
    tb = jnp.where(col_mask[None, None], rpb[:, :, dc].astype(f32), NEG)
    dw = np.arange(WIN_R)[:, None] + np.arange(WIN_R)[None, :]
    t = tb[:, dw]
    t = t.transpose(1, 0, 3, 2, 4).reshape(WIN_R, 2, 2 * GRID_W, WIN_R * GRID_W)
    return t


def _rope_slabs():
    inv = 1.0 / (ROPE_THETA ** (jnp.arange(0, ROPE_DIM, 2, dtype=f32) / ROPE_DIM))
    ang = jnp.arange(SEQ, dtype=f32)[:, None] * inv[None, :]
    cos, sin = jnp.cos(ang), jnp.sin(ang)
    ones = jnp.ones((SEQ, NOPE_DIM), f32)
    zpad = jnp.zeros((SEQ, LANES - NOPE_DIM - ROPE_DIM), f32)
    cosq = jnp.concatenate([ones, cos, cos, zpad], axis=1)
    sinq = jnp.concatenate([0.0 * ones, sin, sin, zpad], axis=1)
    return cosq, sinq


def _pack_layer(i, norm1_g, w_in, a_vnorm_g, a_ws, a_bs, b_conv, c_rpb, d_qnorm_g, d_kvnorm_g,
                d_w_uq, d_w_ukv, out_norm_g, w_o, norm2_g, router_w, e_w_gate, e_w_up, e_w_down):
    half = ROPE_DIM // 2
    w = w_in[i]
    kr0 = OFF_D + Q_RANK + KV_RANK
    kr = w[:, kr0:kr0 + ROPE_DIM]
    z64 = jnp.zeros((D_MODEL, NOPE_DIM), f32)
    z32 = jnp.zeros((D_MODEL, LANES - NOPE_DIM - ROPE_DIM), f32)
    kr_slab = jnp.concatenate([z64, kr, z32], axis=1)
    kr_rot = jnp.concatenate([z64, -kr[:, half:], kr[:, :half], z32], axis=1)
    w_pack = jnp.concatenate([w[:, :kr0], kr_slab, kr_rot], axis=1).astype(bf16)

    uq = d_w_uq[i].reshape(Q_RANK, D_HEADS, NOPE_DIM + ROPE_DIM)
    nope, rope = uq[..., :NOPE_DIM], uq[..., NOPE_DIM:]
    zq = jnp.zeros((Q_RANK, D_HEADS, LANES - NOPE_DIM - ROPE_DIM), f32)
    w_uq = jnp.concatenate([nope, rope, zq], axis=-1).reshape(Q_RANK, D_HEADS * LANES)
    w_uq_rot = jnp.concatenate([0.0 * nope, -rope[..., half:], rope[..., :half], zq],
                               axis=-1).reshape(Q_RANK, D_HEADS * LANES)
    ukv = d_w_ukv[i].reshape(KV_RANK, D_HEADS, NOPE_DIM + V_DIM)
    w_k = jnp.concatenate([ukv[..., :NOPE_DIM], jnp.zeros((KV_RANK, D_HEADS, LANES - NOPE_DIM), f32)],
                          axis=-1).reshape(KV_RANK, D_HEADS * LANES)
    w_v = ukv[..., NOPE_DIM:].reshape(KV_RANK, D_HEADS * V_DIM)
    rw = jnp.concatenate([router_w[i], jnp.zeros((D_MODEL, LANES - N_EXPERTS), f32)], axis=1)
    return {
        "g1": norm1_g[i].reshape(1, D_MODEL),
        "w_in": w_pack,
        "a_vg": a_vnorm_g[i].reshape(1, GROUP_W),
        "a_ws": a_ws[i].astype(bf16),
        "a_bias": jnp.repeat(a_bs[i].T, HEAD_DIM, axis=1),
        "b_conv": b_conv[i],
        "c_bias": _natten_bias_table(c_rpb[i]),
        "d_qg": d_qnorm_g[i].reshape(1, Q_RANK),
        "d_kvg": d_kvnorm_g[i].reshape(1, KV_RANK),
        "w_uq": w_uq.astype(bf16),
        "w_uq_rot": w_uq_rot.astype(bf16),
        "w_k": w_k.astype(bf16),
        "w_v": w_v.astype(bf16),
        "og": out_norm_g[i].reshape(1, D_MODEL),
        "w_o": w_o[i].astype(bf16),
        "g2": norm2_g[i].reshape(1, D_MODEL),
        "rw": rw.astype(bf16),
        "e_wg": e_w_gate[i].astype(bf16),
        "e_wu": e_w_up[i].astype(bf16),
        "e_wd": e_w_down[i].astype(bf16),
    }


def _moe(xn, h2, aff, lw, tri):
    x = xn
    for row0, nrows in ((0, BATCH * SEQ), (BATCH * SEQ, DEC_BATCH * SEQ)):
        cap = CAPACITY_FACTOR * nrows // N_EXPERTS
        posm, base = _route(aff, row0, nrows, tri)
        first = base[:, :, 0].astype(jnp.int32)
        count = jnp.diff(first, axis=0, append=jnp.full((1, N_EXPERTS), cap, jnp.int32))
        xe = _dispatch(h2, posm, first, count, row0, nrows)
        ye = _ffn(xe, lw, nrows)
        x = _combine(x, ye, posm, first, count, row0, nrows)
    return x


def kernel(x_prompt, x_sample, norm1_g, w_in, a_vnorm_g, a_ws, a_bs, b_conv, c_rpb, d_qnorm_g, d_kvnorm_g, d_w_uq, d_w_ukv, out_norm_g, w_o, norm2_g, router_w, e_w_gate, e_w_up, e_w_down, final_norm_g):
    n_p = BATCH * SEQ
    n_s = DEC_BATCH * SEQ
    x = jnp.concatenate([x_prompt.reshape(n_p, D_MODEL), x_sample.reshape(n_s, D_MODEL)], axis=0)
    cosq, sinq = _rope_slabs()
    tri = jnp.asarray(np.triu(np.ones((TT, TT), np.float32), 1), bf16)
    for i in range(DEPTH):
        lw = _pack_layer(i, norm1_g, w_in, a_vnorm_g, a_ws, a_bs, b_conv, c_rpb, d_qnorm_g, d_kvnorm_g,
                         d_w_uq, d_w_ukv, out_norm_g, w_o, norm2_g, router_w, e_w_gate, e_w_up, e_w_down)
        ya, yb, zc, qt, k4, vt = _inproj(x, lw, cosq, sinq)
        yc = _natten(zc, lw["c_bias"])
        yd = _mla(qt, k4, vt)
        xn, h2, aff = _outproj(ya, yb, yc, yd, x, lw)
        x = _moe(xn, h2, aff, lw, tri)
    g = final_norm_g.reshape(1, D_MODEL)
    y_p = _final_norm(x, g, 0, n_p).reshape(BATCH, SEQ, D_MODEL)
    y_s = _final_norm(x, g, n_p, n_s).reshape(DEC_BATCH, SEQ, D_MODEL)
    return (y_p, y_s)
```

```python
import functools
import math

import numpy as np
import jax
import jax.numpy as jnp
from jax import lax
from jax.experimental import pallas as pl
from jax.experimental.pallas import tpu as pltpu

D_MODEL = 1024
BATCH = 4
SEQ = 8192
DEPTH = 4
DEC_BATCH = 2
GROUP_W = 256
HEAD_DIM = 64
A_HEADS = 4
CHUNK = 128
C_HEADS = 4
GRID_W = 64
WIN_R = 8
WIN_C = 16
D_HEADS = 4
Q_RANK = 256
KV_RANK = 128
NOPE_DIM = 64
ROPE_DIM = 32
V_DIM = 64
ROPE_THETA = 10000.0
N_EXPERTS = 16
D_EXPERT = 1024
CAPACITY_FACTOR = 2
EPS = 1e-6

NSEQ = BATCH + DEC_BATCH
T_ALL = NSEQ * SEQ
ROWS = SEQ // GRID_W
LANES = 128
NEG = -1e30

A_COLS = 2 * GROUP_W
B_COLS = 3 * GROUP_W
C_COLS = 3 * GROUP_W
OFF_B = A_COLS
OFF_C = OFF_B + B_COLS
OFF_D = OFF_C + C_COLS
D_PACK = Q_RANK + KV_RANK + 2 * LANES
N_IN_PACK = OFF_D + D_PACK

TM = 512
TQ = 256
TK = 512
L_ROWS = 16
MLA_LAG = 1
MLA_QSCALE = (NOPE_DIM + ROPE_DIM) ** -0.5 * math.log2(math.e)
ROWS_PER_STEP = 8
TME = 512
TT = 256
ROW_ALIGN = 16
DISP_WINDOWS = (80, 144, TT + ROW_ALIGN)
COMB_ALIGN = 8
COMB_WINDOWS = (72, 136, TT + COMB_ALIGN)
VMEM_LIMIT = 56 * 1024 * 1024

f32 = jnp.float32
bf16 = jnp.bfloat16


def _rms(x, g):
    return x * lax.rsqrt(jnp.mean(x * x, axis=-1, keepdims=True) + EPS) * g


def _dot(a, b):
    return jnp.dot(a, b, preferred_element_type=f32)


def _dot_nt(a, b):
    return lax.dot_general(a, b, (((1,), (1,)), ((), ())), preferred_element_type=f32)


def _dot_tn(a, b):
    return lax.dot_general(a, b, (((0,), (0,)), ((), ())), preferred_element_type=f32)


def _inproj_kernel(x_ref, xp_ref, xn_ref, g1_ref, win_ref, avg_ref, aws_ref, abias_ref,
                   bconv_ref, qg_ref, kvg_ref, wuq_ref, wuqr_ref, wk_ref, wv_ref,
                   cos_ref, sin_ref,
                   ya_ref, yb_ref, zc_ref, qt_ref, k4_ref, vt_ref):
    i = pl.program_id(0)
    j = i % (SEQ // TM)
    g1 = g1_ref[...]
    h = _rms(x_ref[...], g1).astype(bf16)

    za = jax.nn.gelu(_dot(h, win_ref[:, 0:A_COLS]))
    u = za[:, :GROUP_W]
    v = _rms(za[:, GROUP_W:], avg_ref[...]).astype(bf16)
    lane = lax.broadcasted_iota(jnp.int32, (CHUNK, GROUP_W), 1)
    for c in range(TM // CHUNK):
        vc = v[c * CHUNK:(c + 1) * CHUNK, :]
        sv = abias_ref[...]
        for hd in range(A_HEADS):
            r = _dot(aws_ref[hd], vc)
            sv = sv + jnp.where((lane >= hd * HEAD_DIM) & (lane < (hd + 1) * HEAD_DIM), r, 0.0)
        ya_ref[c * CHUNK:(c + 1) * CHUNK, :] = u[c * CHUNK:(c + 1) * CHUNK, :] * sv

    zb = _dot(h, win_ref[:, OFF_B:OFF_B + B_COLS])
    bg = zb[:, :GROUP_W]
    y = zb[:, GROUP_W:2 * GROUP_W] * zb[:, 2 * GROUP_W:]
    xh = jnp.concatenate([xp_ref[...], xn_ref[...]], axis=0)
    hh = _rms(xh, g1).astype(bf16)
    zh = _dot(hh, win_ref[:, OFF_B + GROUP_W:OFF_B + B_COLS])
    yh = zh[:, :GROUP_W] * zh[:, GROUP_W:]
    y_before = jnp.where(j == 0, 0.0, yh[7:8, :])
    y_after = jnp.where(j == SEQ // TM - 1, 0.0, yh[8:9, :])
    row = lax.broadcasted_iota(jnp.int32, (TM, GROUP_W), 0)
    y_m1 = jnp.where(row == 0, y_before, pltpu.roll(y, 1, axis=0))
    y_p1 = jnp.where(row == TM - 1, y_after, pltpu.roll(y, TM - 1, axis=0))
    wc = bconv_ref[...]
    yb_ref[...] = bg * (wc[0:1, :] * y_m1 + wc[1:2, :] * y + wc[2:3, :] * y_p1)

    zc = _dot(h, win_ref[:, OFF_C:OFF_C + C_COLS])
    zc_ref[:, 0:GROUP_W] = (zc[:, 0:GROUP_W] * (HEAD_DIM ** -0.5)).astype(bf16)
    zc_ref[:, GROUP_W:] = zc[:, GROUP_W:].astype(bf16)

    zd = _dot(h, win_ref[:, OFF_D:OFF_D + D_PACK])
    cqn = _rms(zd[:, 0:Q_RANK], qg_ref[...]).astype(bf16)
    ckvn = _rms(zd[:, Q_RANK:Q_RANK + KV_RANK], kvg_ref[...]).astype(bf16)
    cos = cos_ref[...]
    sin = sin_ref[...]
    o = Q_RANK + KV_RANK
    kr = zd[:, o:o + LANES] * cos + zd[:, o + LANES:o + 2 * LANES] * sin
    q_pre = _dot(cqn, wuq_ref[...])
    q_rot = _dot(cqn, wuqr_ref[...])
    kn = _dot(ckvn, wk_ref[...])
    for hd in range(D_HEADS):
        sl = slice(hd * LANES, (hd + 1) * LANES)
        qt_ref[hd] = ((q_pre[:, sl] * cos + q_rot[:, sl] * sin) * MLA_QSCALE).T.astype(bf16)
        k4_ref[hd] = (kn[:, sl] + kr).astype(bf16)
    vt = _dot(ckvn, wv_ref[...]).T.astype(bf16)
    for hd in range(D_HEADS):
        vt_ref[hd] = vt[hd * V_DIM:(hd + 1) * V_DIM, :]


def _inproj(x, lw, cosq, sinq):
    nt = T_ALL // TM
    tps = SEQ // TM
    full = lambda shape: pl.BlockSpec(shape, lambda i: (0,) * len(shape))
    in_specs = [
        pl.BlockSpec((TM, D_MODEL), lambda i: (i, 0)),
        pl.BlockSpec((8, D_MODEL), lambda i: (jnp.maximum(i * (TM // 8) - 1, 0), 0)),
        pl.BlockSpec((8, D_MODEL), lambda i: (jnp.minimum((i + 1) * (TM // 8), T_ALL // 8 - 1), 0)),
        full((1, D_MODEL)),
        full((D_MODEL, N_IN_PACK)),
        full((1, GROUP_W)),
        full((A_HEADS, CHUNK, CHUNK)),
        full((CHUNK, GROUP_W)),
        full((3, GROUP_W)),
        full((1, Q_RANK)),
        full((1, KV_RANK)),
        full((Q_RANK, D_HEADS * LANES)),
        full((Q_RANK, D_HEADS * LANES)),
        full((KV_RANK, D_HEADS * LANES)),
        full((KV_RANK, D_HEADS * V_DIM)),
        pl.BlockSpec((TM, LANES), lambda i: (i % tps, 0)),
        pl.BlockSpec((TM, LANES), lambda i: (i % tps, 0)),
    ]
    out_specs = [
        pl.BlockSpec((TM, GROUP_W), lambda i: (i, 0)),
        pl.BlockSpec((TM, GROUP_W), lambda i: (i, 0)),
        pl.BlockSpec((TM, C_COLS), lambda i: (i, 0)),
        pl.BlockSpec((None, D_HEADS, LANES, TM), lambda i: (i // tps, 0, 0, i % tps)),
        pl.BlockSpec((None, D_HEADS, TM, LANES), lambda i: (i // tps, 0, i % tps, 0)),
        pl.BlockSpec((None, D_HEADS, V_DIM, TM), lambda i: (i // tps, 0, 0, i % tps)),
    ]
    out_shape = [
        jax.ShapeDtypeStruct((T_ALL, GROUP_W), f32),
        jax.ShapeDtypeStruct((T_ALL, GROUP_W), f32),
        jax.ShapeDtypeStruct((T_ALL, C_COLS), bf16),
        jax.ShapeDtypeStruct((NSEQ, D_HEADS, LANES, SEQ), bf16),
        jax.ShapeDtypeStruct((NSEQ, D_HEADS, SEQ, LANES), bf16),
        jax.ShapeDtypeStruct((NSEQ, D_HEADS, V_DIM, SEQ), bf16),
    ]
    return pl.pallas_call(
        _inproj_kernel, grid=(nt,), in_specs=in_specs, out_specs=out_specs, out_shape=out_shape,
        name="inproj",
        compiler_params=pltpu.CompilerParams(dimension_semantics=("parallel",),
                                             vmem_limit_bytes=VMEM_LIMIT),
    )(x, x, x, lw["g1"], lw["w_in"], lw["a_vg"], lw["a_ws"], lw["a_bias"], lw["b_conv"],
      lw["d_qg"], lw["d_kvg"], lw["w_uq"], lw["w_uq_rot"], lw["w_k"], lw["w_v"], cosq, sinq)


def _natten_kernel(q_ref, k_ref, v_ref, bias_ref, o_ref):
    jb = pl.program_id(1)
    lane = lax.broadcasted_iota(jnp.int32, (GRID_W, GROUP_W), 1)
    head_of_lane = lane // HEAD_DIM
    nkeys = WIN_R * GRID_W

    jobs = []
    for i in range(ROWS_PER_STEP):
        r = jb * ROWS_PER_STEP + i
        rs = jnp.clip(r - WIN_R // 2, 0, ROWS - WIN_R)
        d = rs - r + (WIN_R - 1)
        kstart = pl.multiple_of(rs * GRID_W, GRID_W)
        q = q_ref[i * GRID_W:(i + 1) * GRID_W, :]
        zero = jnp.zeros_like(q)
        qq = jnp.concatenate([jnp.where(head_of_lane == h, q, zero) for h in range(C_HEADS)], axis=0)
        jobs.append((_dot_nt(qq, k_ref[pl.ds(kstart, nkeys), :]), d, kstart))
    for i, (sc, d, kstart) in enumerate(jobs):
        sc = sc + bias_ref[d]
        m = jnp.max(sc, axis=-1, keepdims=True)
        p = jnp.exp(sc - m)
        l = jnp.sum(p, axis=-1, keepdims=True)
        o = _dot(p.astype(bf16), v_ref[pl.ds(kstart, nkeys), :]) / l
        out = jnp.zeros((GRID_W, GROUP_W), f32)
        for h in range(C_HEADS):
            out = jnp.where(head_of_lane == h, o[h * GRID_W:(h + 1) * GRID_W], out)
        o_ref[i * GRID_W:(i + 1) * GRID_W, :] = out


def _natten(zc, bias):
    nb = ROWS // ROWS_PER_STEP
    tq = ROWS_PER_STEP * GRID_W
    return pl.pallas_call(
        _natten_kernel, grid=(NSEQ, nb),
        in_specs=[
            pl.BlockSpec((tq, GROUP_W), lambda b, j: (b * nb + j, 0)),
            pl.BlockSpec((SEQ, GROUP_W), lambda b, j: (b, 1)),
            pl.BlockSpec((SEQ, GROUP_W), lambda b, j: (b, 2)),
            pl.BlockSpec((WIN_R, C_HEADS * GRID_W, WIN_R * GRID_W), lambda b, j: (0, 0, 0)),
        ],
        out_specs=pl.BlockSpec((tq, GROUP_W), lambda b, j: (b * nb + j, 0)),
        out_shape=jax.ShapeDtypeStruct((T_ALL, GROUP_W), f32),
        name="natten",
        compiler_params=pltpu.CompilerParams(dimension_semantics=("parallel", "arbitrary"),
                                             vmem_limit_bytes=VMEM_LIMIT),
    )(zc, zc, zc, bias)


def _mla_kernel(qt_ref, k_ref, vt_ref, o_ref, s_scr):
    nchunk = SEQ // TK
    heads = list(range(D_HEADS))
    qts = [qt_ref[h] for h in heads]

    def qk(slot, h, ci):
        ks = pl.multiple_of(ci * TK, TK)
        s_scr[slot, h] = _dot(k_ref[h, pl.ds(ks, TK), :], qts[h])

    ones = jnp.ones((L_ROWS, TK), bf16)

    def softmax_pv(slot, h, ci, stat):
        m, acc = stat
        ks = pl.multiple_of(ci * TK, TK)
        vt = jnp.concatenate([vt_ref[h, :, pl.ds(ks, TK)], ones], axis=0)
        s = s_scr[slot, h]
        m_new = jnp.maximum(m, jnp.max(s, axis=0, keepdims=True))
        alpha = jnp.exp2(m - m_new)
        p = jnp.exp2(s - m_new)
        acc = acc * alpha + _dot(vt, p.astype(bf16))
        return m_new, acc

    def half(cur, ci, stats, prefetch=True):
        out = []
        for h in heads:
            if prefetch:
                qk(1 - cur, h, ci + 1)
            if h >= MLA_LAG:
                out.append(softmax_pv(cur, h - MLA_LAG, ci, stats[h - MLA_LAG]))
        for h in range(D_HEADS - MLA_LAG, D_HEADS):
            out.append(softmax_pv(cur, h, ci, stats[h]))
        return tuple(out)

    def body(cp, stats):
        stats = half(0, 2 * cp, stats)
        return half(1, 2 * cp + 1, stats)

    for h in heads:
        qk(0, h, 0)
    init = tuple((jnp.full((1, TQ), NEG, f32), jnp.zeros((V_DIM + L_ROWS, TQ), f32)) for _ in heads)
    res = lax.fori_loop(0, nchunk // 2 - 1, body, init)
    res = half(0, nchunk - 2, res)
    res = half(1, nchunk - 1, res, prefetch=False)
    o_ref[...] = jnp.concatenate([acc[:V_DIM] / acc[V_DIM:V_DIM + 1] for (_, acc) in res], axis=0).T


def _mla(qt, k4, vt):
    nq = SEQ // TQ
    return pl.pallas_call(
        _mla_kernel, grid=(NSEQ, nq),
        in_specs=[
            pl.BlockSpec((None, D_HEADS, LANES, TQ), lambda b, i: (b, 0, 0, i)),
            pl.BlockSpec((None, D_HEADS, SEQ, LANES), lambda b, i: (b, 0, 0, 0)),
            pl.BlockSpec((None, D_HEADS, V_DIM, SEQ), lambda b, i: (b, 0, 0, 0)),
        ],
        out_specs=pl.BlockSpec((TQ, GROUP_W), lambda b, i: (b * nq + i, 0)),
        out_shape=jax.ShapeDtypeStruct((T_ALL, GROUP_W), f32),
        scratch_shapes=[pltpu.VMEM((2, D_HEADS, TK, TQ), f32)],
        name="mla",
        compiler_params=pltpu.CompilerParams(
            dimension_semantics=("parallel", "arbitrary"),
            vmem_limit_bytes=VMEM_LIMIT),
    )(qt, k4, vt)


def _outproj_kernel(ya_ref, yb_ref, yc_ref, yd_ref, x_ref, og_ref, wo_ref, g2_ref, rw_ref,
                    xn_ref, h2_ref, aff_ref):
    acc = x_ref[...]
    for g, y_ref in enumerate((ya_ref, yb_ref, yc_ref, yd_ref)):
        sl = slice(g * GROUP_W, (g + 1) * GROUP_W)
        yg = _rms(y_ref[...], og_ref[:, sl]).astype(bf16)
        acc = acc + _dot(yg, wo_ref[sl, :])
    xn_ref[...] = acc
    h2 = _rms(acc, g2_ref[...]).astype(bf16)
    h2_ref[...] = h2
    logits = _dot(h2, rw_ref[...])
    lane = lax.broadcasted_iota(jnp.int32, logits.shape, 1)
    logits = jnp.where(lane < N_EXPERTS, logits, NEG)
    m = jnp.max(logits, axis=-1, keepdims=True)
    e = jnp.exp(logits - m)
    aff = e / jnp.sum(e, axis=-1, keepdims=True)
    aff_ref[...] = aff.T[:N_EXPERTS, :]


def _outproj(ya, yb, yc, yd, x, lw):
    nt = T_ALL // TM
    full = lambda shape: pl.BlockSpec(shape, lambda i: (0,) * len(shape))
    yspec = pl.BlockSpec((TM, GROUP_W), lambda i: (i, 0))
    xspec = pl.BlockSpec((TM, D_MODEL), lambda i: (i, 0))
    return pl.pallas_call(
        _outproj_kernel, grid=(nt,),
        in_specs=[yspec, yspec, yspec, yspec, xspec, full((1, D_MODEL)), full((D_MODEL, D_MODEL)),
                  full((1, D_MODEL)), full((D_MODEL, LANES))],
        out_specs=[xspec, xspec, pl.BlockSpec((N_EXPERTS, TM), lambda i: (0, i))],
        out_shape=[jax.ShapeDtypeStruct((T_ALL, D_MODEL), f32),
                   jax.ShapeDtypeStruct((T_ALL, D_MODEL), bf16),
                   jax.ShapeDtypeStruct((N_EXPERTS, T_ALL), f32)],
        name="outproj",
        compiler_params=pltpu.CompilerParams(dimension_semantics=("parallel",),
                                             vmem_limit_bytes=VMEM_LIMIT),
    )(ya, yb, yc, yd, x, lw["og"], lw["w_o"], lw["g2"], lw["rw"])


def _route_kernel(aff_ref, tri_ref, posm_ref, base_ref, thr_scr, need_scr, runeq_scr, runsel_scr, *, cap):
    j = pl.program_id(0)

    @pl.when(j == 0)
    def _():
        def refine(i, thr):
            cand = thr | jnp.left_shift(jnp.int32(1), 30 - i)
            bits = pltpu.bitcast(aff_ref[...], jnp.int32)
            cnt = jnp.sum(jnp.where(bits >= cand, 1.0, 0.0), axis=1, keepdims=True)
            return jnp.where(cnt >= cap, cand, thr)

        thr = lax.fori_loop(0, 31, refine, jnp.zeros((N_EXPERTS, 1), jnp.int32))
        bits = pltpu.bitcast(aff_ref[...], jnp.int32)
        above = jnp.sum(jnp.where(bits > thr, 1.0, 0.0), axis=1, keepdims=True)
        thr_scr[...] = jnp.broadcast_to(thr, (N_EXPERTS, LANES))
        need_scr[...] = jnp.broadcast_to(cap - above, (N_EXPERTS, LANES))
        runeq_scr[...] = jnp.zeros((N_EXPERTS, LANES), f32)
        runsel_scr[...] = jnp.zeros((N_EXPERTS, LANES), f32)

    t0 = pl.multiple_of(j * TT, TT)
    bits = pltpu.bitcast(aff_ref[:, pl.ds(t0, TT)], jnp.int32)
    thr = thr_scr[:, 0:1]
    eq = bits == thr
    eq_rank = runeq_scr[:, 0:1] + _dot(jnp.where(eq, 1.0, 0.0).astype(bf16), tri_ref[...])
    sel = (bits > thr) | (eq & (eq_rank < need_scr[:, 0:1]))
    sel_f = jnp.where(sel, 1.0, 0.0)
    base = runsel_scr[...]
    pos = base[:, 0:1] + _dot(sel_f.astype(bf16), tri_ref[...])
    posm_ref[...] = jnp.where(sel, pos, -1.0)
    base_ref[...] = base
    runeq_scr[...] += jnp.sum(jnp.where(eq, 1.0, 0.0), axis=1, keepdims=True)
    runsel_scr[...] += jnp.sum(sel_f, axis=1, keepdims=True)


def _route(aff, row0, nrows, tri):
    cap = CAPACITY_FACTOR * nrows // N_EXPERTS
    nt = nrows // TT
    blk0 = row0 // nrows
    vec = pltpu.VMEM((N_EXPERTS, LANES), f32)
    return pl.pallas_call(
        functools.partial(_route_kernel, cap=cap), grid=(nt,),
        in_specs=[pl.BlockSpec((N_EXPERTS, nrows), lambda j: (0, blk0)),
                  pl.BlockSpec((TT, TT), lambda j: (0, 0))],
        out_specs=[pl.BlockSpec((N_EXPERTS, TT), lambda j: (0, j)),
                   pl.BlockSpec((None, N_EXPERTS, LANES), lambda j: (j, 0, 0))],
        out_shape=[jax.ShapeDtypeStruct((N_EXPERTS, nrows), f32),
                   jax.ShapeDtypeStruct((nt, N_EXPERTS, LANES), f32)],
        scratch_shapes=[pltpu.VMEM((N_EXPERTS, LANES), jnp.int32), vec, vec, vec],
        name="route",
        compiler_params=pltpu.CompilerParams(dimension_semantics=("arbitrary",)),
    )(aff, tri)


def _window_class(count, windows, align):
    longest = jnp.max(count, axis=1) + (align - 1)
    return sum((longest > w).astype(jnp.int32) for w in windows[:-1])


def _dispatch_kernel(big_ref, first_ref, posm_ref, h2_ref, xe_hbm, stage, sem, carry, *, cap, ntiles):
    j = pl.program_id(0)
    slot = j % 2

    @pl.when(j == 0)
    def _():
        carry[...] = jnp.zeros_like(carry)
        pad = DISP_WINDOWS[-1]
        stage[1, 0:pad, :] = jnp.zeros((pad, D_MODEL), bf16)
        pads = [pltpu.make_async_copy(stage.at[1, pl.ds(0, pad)], xe_hbm.at[e, pl.ds(cap, pad)], sem.at[1])
                for e in range(N_EXPERTS)]
        for p in pads:
            p.start()
        for p in pads:
            p.wait()

    def run_start(jj, e):
        return first_ref[jj * N_EXPERTS + e]

    def window_copy(jj, e, sl, win):
        a = pl.multiple_of(run_start(jj, e) // ROW_ALIGN * ROW_ALIGN, ROW_ALIGN)
        return pltpu.make_async_copy(stage.at[sl, pl.ds(e * win, win)], xe_hbm.at[e, pl.ds(a, win)], sem.at[sl])

    def for_tile_window(jj, fn):
        for flag, win in enumerate(DISP_WINDOWS):
            @pl.when(big_ref[jj] == flag)
            def _():
                fn(win)

    def wait_all(jj, sl):
        def go(win):
            for e in range(N_EXPERTS):
                window_copy(jj, e, sl, win).wait()
        for_tile_window(jj, go)

    def compact(win):
        row = lax.broadcasted_iota(jnp.int32, (win, TT), 0)
        starts = [run_start(j, e) // ROW_ALIGN * ROW_ALIGN for e in range(N_EXPERTS)]
        onehot = jnp.concatenate(
            [jnp.where((row + starts[e]).astype(f32) == posm_ref[e:e + 1, :], 1.0, 0.0)
             for e in range(N_EXPERTS)], axis=0).astype(bf16)
        rows = _dot(onehot, h2_ref[...])
        head = lax.broadcasted_iota(jnp.int32, (ROW_ALIGN, D_MODEL), 0)
        for e in range(N_EXPERTS):
            lead = run_start(j, e) - starts[e]
            w = rows[e * win:(e + 1) * win]
            top = w[:ROW_ALIGN] + jnp.where(head < lead, carry[e], 0.0)
            stage[slot, e * win:e * win + ROW_ALIGN, :] = top.astype(bf16)
            stage[slot, e * win + ROW_ALIGN:(e + 1) * win, :] = w[ROW_ALIGN:].astype(bf16)
        for e in range(N_EXPERTS):
            end = jnp.where(j + 1 < ntiles, run_start(jnp.minimum(j + 1, ntiles - 1), e), cap)
            off = jnp.minimum(end // ROW_ALIGN * ROW_ALIGN - starts[e], win - ROW_ALIGN)
            off = pl.multiple_of(off, ROW_ALIGN)
            carry[e] = stage[slot, pl.ds(e * win + off, ROW_ALIGN), :].astype(f32)

    for_tile_window(j, compact)

    @pl.when(j > 0)
    def _():
        wait_all(j - 1, 1 - slot)

    def start_all(win):
        for e in range(N_EXPERTS):
            window_copy(j, e, slot, win).start()
    for_tile_window(j, start_all)

    @pl.when(j == ntiles - 1)
    def _():
        wait_all(j, slot)


def _dispatch(h2, posm, first, count, row0, nrows):
    cap = CAPACITY_FACTOR * nrows // N_EXPERTS
    nt = nrows // TT
    blk0 = row0 // TT
    big = _window_class(count, DISP_WINDOWS, ROW_ALIGN)
    return pl.pallas_call(
        functools.partial(_dispatch_kernel, cap=cap, ntiles=nt),
        grid_spec=pltpu.PrefetchScalarGridSpec(
            num_scalar_prefetch=2, grid=(nt,),
            in_specs=[pl.BlockSpec((N_EXPERTS, TT), lambda j, *_: (0, j)),
                      pl.BlockSpec((TT, D_MODEL), lambda j, *_: (blk0 + j, 0))],
            out_specs=pl.BlockSpec(memory_space=pl.ANY),
            scratch_shapes=[pltpu.VMEM((2, N_EXPERTS * DISP_WINDOWS[-1], D_MODEL), bf16),
                            pltpu.SemaphoreType.DMA((2,)),
                            pltpu.VMEM((N_EXPERTS, ROW_ALIGN, D_MODEL), f32)],
        ),
        out_shape=jax.ShapeDtypeStruct((N_EXPERTS, cap + DISP_WINDOWS[-1], D_MODEL), bf16),
        name="dispatch",
        compiler_params=pltpu.CompilerParams(dimension_semantics=("arbitrary",), vmem_limit_bytes=VMEM_LIMIT),
    )(big, first.reshape(-1), posm, h2)


def _ffn_kernel(xe_ref, rw_ref, wg_ref, wu_ref, wd_ref, ye_ref):
    e = pl.program_id(0)
    x = xe_ref[...]
    logits = _dot(x, rw_ref[...])
    lane = lax.broadcasted_iota(jnp.int32, logits.shape, 1)
    logits = jnp.where(lane < N_EXPERTS, logits, NEG)
    ex = jnp.exp(logits - jnp.max(logits, axis=-1, keepdims=True))
    gate = jnp.sum(jnp.where(lane == e, ex, 0.0), axis=-1, keepdims=True) / jnp.sum(ex, axis=-1, keepdims=True)
    hg = _dot(x, wg_ref[...])
    hu = _dot(x, wu_ref[...])
    h = (hg * (1.0 / (1.0 + jnp.exp(-hg))) * hu).astype(bf16)
    ye_ref[...] = _dot(h, wd_ref[...]) * gate


def _ffn(xe, lw, nrows):
    cap = CAPACITY_FACTOR * nrows // N_EXPERTS
    wspec = pl.BlockSpec((None, D_MODEL, D_EXPERT), lambda e, t: (e, 0, 0))
    return pl.pallas_call(
        _ffn_kernel, grid=(N_EXPERTS, cap // TME),
        in_specs=[
            pl.BlockSpec((None, TME, D_MODEL), lambda e, t: (e, t, 0)),
            pl.BlockSpec((D_MODEL, LANES), lambda e, t: (0, 0)),
            wspec, wspec,
            pl.BlockSpec((None, D_EXPERT, D_MODEL), lambda e, t: (e, 0, 0)),
        ],
        out_specs=pl.BlockSpec((None, TME, D_MODEL), lambda e, t: (e, t, 0)),
        out_shape=jax.ShapeDtypeStruct((N_EXPERTS, cap, D_MODEL), f32),
        name="expert_ffn",
        compiler_params=pltpu.CompilerParams(dimension_semantics=("parallel", "arbitrary"),
                                             vmem_limit_bytes=VMEM_LIMIT),
    )(xe, lw["rw"], lw["e_wg"], lw["e_wu"], lw["e_wd"])


def _combine_kernel(big_ref, first_ref, posm_ref, x_ref, ye_hbm, o_ref, buf, sem, *, cap, ntiles):
    j = pl.program_id(0)

    def window_start(jj, e, win):
        return jnp.minimum(first_ref[jj * N_EXPERTS + e] // COMB_ALIGN * COMB_ALIGN, cap - win)

    def window_copy(jj, e, slot, win):
        a = pl.multiple_of(window_start(jj, e, win), COMB_ALIGN)
        return pltpu.make_async_copy(ye_hbm.at[e, pl.ds(a, win)], buf.at[slot, pl.ds(e * win, win)],
                                     sem.at[slot])

    def for_tile_window(jj, fn):
        for flag, win in enumerate(COMB_WINDOWS):
            @pl.when(big_ref[jj] == flag)
            def _():
                fn(win)

    def start_all(jj, slot):
        def go(win):
            for e in range(N_EXPERTS):
                window_copy(jj, e, slot, win).start()
        for_tile_window(jj, go)

    @pl.when(j == 0)
    def _():
        start_all(0, 0)

    slot = j % 2

    def wait_all(win):
        for e in range(N_EXPERTS):
            window_copy(j, e, slot, win).wait()
    for_tile_window(j, wait_all)

    @pl.when(j + 1 < ntiles)
    def _():
        start_all(j + 1, 1 - slot)

    def add_rows(win):
        row = lax.broadcasted_iota(jnp.int32, (win, TT), 0)
        onehot = jnp.concatenate(
            [jnp.where((row + window_start(j, e, win)).astype(f32) == posm_ref[e:e + 1, :], 1.0, 0.0)
             for e in range(N_EXPERTS)], axis=0).astype(bf16)
        y = buf[slot, 0:N_EXPERTS * win, :]
        hi = y.astype(bf16)
        lo = (y - hi.astype(f32)).astype(bf16)
        o_ref[...] = x_ref[...] + _dot_tn(onehot, hi) + _dot_tn(onehot, lo)
    for_tile_window(j, add_rows)


def _combine(x, ye, posm, first, count, row0, nrows):
    cap = CAPACITY_FACTOR * nrows // N_EXPERTS
    nt = nrows // TT
    blk0 = row0 // TT
    big = _window_class(count, COMB_WINDOWS, COMB_ALIGN)
    return pl.pallas_call(
        functools.partial(_combine_kernel, cap=cap, ntiles=nt),
        grid_spec=pltpu.PrefetchScalarGridSpec(
            num_scalar_prefetch=2, grid=(nt,),
            in_specs=[pl.BlockSpec((N_EXPERTS, TT), lambda j, *_: (0, j)),
                      pl.BlockSpec((TT, D_MODEL), lambda j, *_: (blk0 + j, 0)),
                      pl.BlockSpec(memory_space=pl.ANY)],
            out_specs=pl.BlockSpec((TT, D_MODEL), lambda j, *_: (blk0 + j, 0)),
            scratch_shapes=[pltpu.VMEM((2, N_EXPERTS * COMB_WINDOWS[-1], D_MODEL), f32),
                            pltpu.SemaphoreType.DMA((2,))],
        ),
        out_shape=jax.ShapeDtypeStruct(x.shape, f32),
        input_output_aliases={3: 0},
        name="combine",
        compiler_params=pltpu.CompilerParams(dimension_semantics=("arbitrary",), vmem_limit_bytes=VMEM_LIMIT),
    )(big, first.reshape(-1), posm, x, ye)


def _final_kernel(x_ref, g_ref, o_ref):
    o_ref[...] = _rms(x_ref[...], g_ref[...])


def _final_norm(x, g, row0, nrows):
    off = row0 // TM
    return pl.pallas_call(
        _final_kernel, grid=(nrows // TM,),
        in_specs=[pl.BlockSpec((TM, D_MODEL), lambda i: (i + off, 0)),
                  pl.BlockSpec((1, D_MODEL), lambda i: (0, 0))],
        out_specs=pl.BlockSpec((TM, D_MODEL), lambda i: (i, 0)),
        out_shape=jax.ShapeDtypeStruct((nrows, D_MODEL), f32),
        name="final_norm",
        compiler_params=pltpu.CompilerParams(dimension_semantics=("parallel",)),
    )(x, g)


def _natten_bias_table(rpb):
    c = np.arange(GRID_W)
    cs = np.clip(c - WIN_C // 2, 0, GRID_W - WIN_C)
    kc = np.arange(GRID_W)
    col_mask = (kc[None, :] >= cs[:, None]) & (kc[None, :] < cs[:, None] + WIN_C)
    dc = np.clip(kc[None, :] - c[:, None], -(WIN_C - 1), WIN_C - 1) + (WIN_C - 1)
    tb = jnp.where(col_mask[None, None], rpb[:, :, dc].astype(f32), NEG)
    dw = np.arange(WIN_R)[:, None] + np.arange(WIN_R)[None, :]
    t = tb[:, dw]
    t = t.transpose(1, 0, 3, 2, 4).reshape(WIN_R, C_HEADS * GRID_W, WIN_R * GRID_W)
    return t


def _rope_slabs():
    inv = 1.0 / (ROPE_THETA ** (jnp.arange(0, ROPE_DIM, 2, dtype=f32) / ROPE_DIM))
    ang = jnp.arange(SEQ, dtype=f32)[:, None] * inv[None, :]
    cos, sin = jnp.cos(ang), jnp.sin(ang)
    ones = jnp.ones((SEQ, NOPE_DIM), f32)
    zpad = jnp.zeros((SEQ, LANES - NOPE_DIM - ROPE_DIM), f32)
    cosq = jnp.concatenate([ones, cos, cos, zpad], axis=1)
    sinq = jnp.concatenate([0.0 * ones, sin, sin, zpad], axis=1)
    return cosq, sinq


def _pack_layer(i, norm1_g, w_in, a_vnorm_g, a_ws, a_bs, b_conv, c_rpb, d_qnorm_g, d_kvnorm_g,
                d_w_uq, d_w_ukv, out_norm_g, w_o, norm2_g, router_w, e_w_gate, e_w_up, e_w_down):
    half = ROPE_DIM // 2
    w = w_in[i]
    kr0 = OFF_D + Q_RANK + KV_RANK
    kr = w[:, kr0:kr0 + ROPE_DIM]
    z64 = jnp.zeros((D_MODEL, NOPE_DIM), f32)
    z32 = jnp.zeros((D_MODEL, LANES - NOPE_DIM - ROPE_DIM), f32)
    kr_slab = jnp.concatenate([z64, kr, z32], axis=1)
    kr_rot = jnp.concatenate([z64, -kr[:, half:], kr[:, :half], z32], axis=1)
    w_pack = jnp.concatenate([w[:, :kr0], kr_slab, kr_rot], axis=1).astype(bf16)

    uq = d_w_uq[i].reshape(Q_RANK, D_HEADS, NOPE_DIM + ROPE_DIM)
    nope, rope = uq[..., :NOPE_DIM], uq[..., NOPE_DIM:]
    zq = jnp.zeros((Q_RANK, D_HEADS, LANES - NOPE_DIM - ROPE_DIM), f32)
    w_uq = jnp.concatenate([nope, rope, zq], axis=-1).reshape(Q_RANK, D_HEADS * LANES)
    w_uq_rot = jnp.concatenate([0.0 * nope, -rope[..., half:], rope[..., :half], zq],
                               axis=-1).reshape(Q_RANK, D_HEADS * LANES)
    ukv = d_w_ukv[i].reshape(KV_RANK, D_HEADS, NOPE_DIM + V_DIM)
    w_k = jnp.concatenate([ukv[..., :NOPE_DIM], jnp.zeros((KV_RANK, D_HEADS, LANES - NOPE_DIM), f32)],
                          axis=-1).reshape(KV_RANK, D_HEADS * LANES)
    w_v = ukv[..., NOPE_DIM:].reshape(KV_RANK, D_HEADS * V_DIM)
    rw = jnp.concatenate([router_w[i], jnp.zeros((D_MODEL, LANES - N_EXPERTS), f32)], axis=1)
    return {
        "g1": norm1_g[i].reshape(1, D_MODEL),
        "w_in": w_pack,
        "a_vg": a_vnorm_g[i].reshape(1, GROUP_W),
        "a_ws": a_ws[i].astype(bf16),
        "a_bias": jnp.repeat(a_bs[i].T, HEAD_DIM, axis=1),
        "b_conv": b_conv[i],
        "c_bias": _natten_bias_table(c_rpb[i]),
        "d_qg": d_qnorm_g[i].reshape(1, Q_RANK),
        "d_kvg": d_kvnorm_g[i].reshape(1, KV_RANK),
        "w_uq": w_uq.astype(bf16),
        "w_uq_rot": w_uq_rot.astype(bf16),
        "w_k": w_k.astype(bf16),
        "w_v": w_v.astype(bf16),
        "og": out_norm_g[i].reshape(1, D_MODEL),
        "w_o": w_o[i].astype(bf16),
        "g2": norm2_g[i].reshape(1, D_MODEL),
        "rw": rw.astype(bf16),
        "e_wg": e_w_gate[i].astype(bf16),
        "e_wu": e_w_up[i].astype(bf16),
        "e_wd": e_w_down[i].astype(bf16),
    }


def _moe(xn, h2, aff, lw, tri):
    x = xn
    for row0, nrows in ((0, BATCH * SEQ), (BATCH * SEQ, DEC_BATCH * SEQ)):
        cap = CAPACITY_FACTOR * nrows // N_EXPERTS
        posm, base = _route(aff, row0, nrows, tri)
        first = base[:, :, 0].astype(jnp.int32)
        count = jnp.diff(first, axis=0, append=jnp.full((1, N_EXPERTS), cap, jnp.int32))
        xe = _dispatch(h2, posm, first, count, row0, nrows)
        ye = _ffn(xe, lw, nrows)
        x = _combine(x, ye, posm, first, count, row0, nrows)
    return x


def kernel(x_prompt, x_sample, norm1_g, w_in, a_vnorm_g, a_ws, a_bs, b_conv, c_rpb, d_qnorm_g, d_kvnorm_g, d_w_uq, d_w_ukv, out_norm_g, w_o, norm2_g, router_w, e_w_gate, e_w_up, e_w_down, final_norm_g):
    n_p = BATCH * SEQ
    n_s = DEC_BATCH * SEQ
    x = jnp.concatenate([x_prompt.reshape(n_p, D_MODEL), x_sample.reshape(n_s, D_MODEL)], axis=0)
    cosq, sinq = _rope_slabs()
    tri = jnp.asarray(np.triu(np.ones((TT, TT), np.float32), 1), bf16)
    for i in range(DEPTH):
        lw = _pack_layer(i, norm1_g, w_in, a_vnorm_g, a_ws, a_bs, b_conv, c_rpb, d_qnorm_g, d_kvnorm_g,
                         d_w_uq, d_w_ukv, out_norm_g, w_o, norm2_g, router_w, e_w_gate, e_w_up, e_w_down)
        ya, yb, zc, qt, k4, vt = _inproj(x, lw, cosq, sinq)
        yc = _natten(zc, lw["c_bias"])
        yd = _mla(qt, k4, vt)
        xn, h2, aff = _outproj(ya, yb, yc, yd, x, lw)
        x = _moe(xn, h2, aff, lw, tri)
    g = final_norm_g.reshape(1, D_MODEL)
    y_p = _final_norm(x, g, 0, n_p).reshape(BATCH, SEQ, D_MODEL)
    y_s = _final_norm(x, g, n_p, n_s).reshape(DEC_BATCH, SEQ, D_MODEL)
    return (y_p, y_s)
```

```python
import functools
import math

import numpy as np
import jax
import jax.numpy as jnp
from jax import lax
from jax.experimental import pallas as pl
from jax.experimental.pallas import tpu as pltpu

D_MODEL = 1024
BATCH = 4
SEQ = 8192
DEPTH = 4
DEC_BATCH = 2
GROUP_W = 256
HEAD_DIM = 64
A_HEADS = 4
CHUNK = 128
C_HEADS = 4
GRID_W = 64
WIN_R = 8
WIN_C = 16
D_HEADS = 4
Q_RANK = 256
KV_RANK = 128
NOPE_DIM = 64
ROPE_DIM = 32
V_DIM = 64
ROPE_THETA = 10000.0
N_EXPERTS = 16
D_EXPERT = 1024
CAPACITY_FACTOR = 2
EPS = 1e-6

NSEQ = BATCH + DEC_BATCH
T_ALL = NSEQ * SEQ
ROWS = SEQ // GRID_W
LANES = 128
NEG = -1e30

A_COLS = 2 * GROUP_W
B_COLS = 3 * GROUP_W
C_COLS = 3 * GROUP_W
OFF_B = A_COLS
OFF_C = OFF_B + B_COLS
OFF_D = OFF_C + C_COLS
D_PACK = Q_RANK + KV_RANK + 2 * LANES
N_IN_PACK = OFF_D + D_PACK

TM = 512
TQ = 256
TK = 512
L_ROWS = 16
MLA_UNROLL = 4
MLA_LAG = 1
MLA_QSCALE = (NOPE_DIM + ROPE_DIM) ** -0.5 * math.log2(math.e)
ROWS_PER_STEP = 8
TME = 512
TT = 256
ROUTE_TILES = 4
ROW_ALIGN = 16
DISP_WINDOWS = (80, 144, TT + ROW_ALIGN)
COMB_ALIGN = 8
COMB_WINDOWS = (72, 136, TT + COMB_ALIGN)
VMEM_LIMIT = 56 * 1024 * 1024

f32 = jnp.float32
bf16 = jnp.bfloat16


def _rms(x, g):
    return x * lax.rsqrt(jnp.mean(x * x, axis=-1, keepdims=True) + EPS) * g


def _dot(a, b):
    return jnp.dot(a, b, preferred_element_type=f32)


def _dot_nt(a, b):
    return lax.dot_general(a, b, (((1,), (1,)), ((), ())), preferred_element_type=f32)


def _dot_tn(a, b):
    return lax.dot_general(a, b, (((0,), (0,)), ((), ())), preferred_element_type=f32)


def _inproj_kernel(x_ref, xp_ref, xn_ref, g1_ref, win_ref, avg_ref, aws_ref, abias_ref,
                   bconv_ref, qg_ref, kvg_ref, wuq_ref, wuqr_ref, wk_ref, wv_ref,
                   cos_ref, sin_ref,
                   ya_ref, yb_ref, zc_ref, qt_ref, k4_ref, vt_ref):
    i = pl.program_id(0)
    j = i % (SEQ // TM)
    g1 = g1_ref[...]
    h = _rms(x_ref[...], g1).astype(bf16)

    za = jax.nn.gelu(_dot(h, win_ref[:, 0:A_COLS]))
    u = za[:, :GROUP_W]
    v = _rms(za[:, GROUP_W:], avg_ref[...]).astype(bf16)
    lane = lax.broadcasted_iota(jnp.int32, (CHUNK, GROUP_W), 1)
    for c in range(TM // CHUNK):
        vc = v[c * CHUNK:(c + 1) * CHUNK, :]
        sv = abias_ref[...]
        for hd in range(A_HEADS):
            r = _dot(aws_ref[hd], vc)
            sv = sv + jnp.where((lane >= hd * HEAD_DIM) & (lane < (hd + 1) * HEAD_DIM), r, 0.0)
        ya_ref[c * CHUNK:(c + 1) * CHUNK, :] = u[c * CHUNK:(c + 1) * CHUNK, :] * sv

    zb = _dot(h, win_ref[:, OFF_B:OFF_B + B_COLS])
    bg = zb[:, :GROUP_W]
    y = zb[:, GROUP_W:2 * GROUP_W] * zb[:, 2 * GROUP_W:]
    xh = jnp.concatenate([xp_ref[...], xn_ref[...]], axis=0)
    hh = _rms(xh, g1).astype(bf16)
    zh = _dot(hh, win_ref[:, OFF_B + GROUP_W:OFF_B + B_COLS])
    yh = zh[:, :GROUP_W] * zh[:, GROUP_W:]
    y_before = jnp.where(j == 0, 0.0, yh[7:8, :])
    y_after = jnp.where(j == SEQ // TM - 1, 0.0, yh[8:9, :])
    row = lax.broadcasted_iota(jnp.int32, (TM, GROUP_W), 0)
    y_m1 = jnp.where(row == 0, y_before, pltpu.roll(y, 1, axis=0))
    y_p1 = jnp.where(row == TM - 1, y_after, pltpu.roll(y, TM - 1, axis=0))
    wc = bconv_ref[...]
    yb_ref[...] = bg * (wc[0:1, :] * y_m1 + wc[1:2, :] * y + wc[2:3, :] * y_p1)

    zc = _dot(h, win_ref[:, OFF_C:OFF_C + C_COLS])
    zc_ref[:, 0:GROUP_W] = (zc[:, 0:GROUP_W] * (HEAD_DIM ** -0.5)).astype(bf16)
    zc_ref[:, GROUP_W:] = zc[:, GROUP_W:].astype(bf16)

    zd = _dot(h, win_ref[:, OFF_D:OFF_D + D_PACK])
    cqn = _rms(zd[:, 0:Q_RANK], qg_ref[...]).astype(bf16)
    ckvn = _rms(zd[:, Q_RANK:Q_RANK + KV_RANK], kvg_ref[...]).astype(bf16)
    cos = cos_ref[...]
    sin = sin_ref[...]
    o = Q_RANK + KV_RANK
    kr = zd[:, o:o + LANES] * cos + zd[:, o + LANES:o + 2 * LANES] * sin
    q_pre = _dot(cqn, wuq_ref[...])
    q_rot = _dot(cqn, wuqr_ref[...])
    kn = _dot(ckvn, wk_ref[...])
    for hd in range(D_HEADS):
        sl = slice(hd * LANES, (hd + 1) * LANES)
        qt_ref[hd] = ((q_pre[:, sl] * cos + q_rot[:, sl] * sin) * MLA_QSCALE).T.astype(bf16)
        k4_ref[hd] = (kn[:, sl] + kr).astype(bf16)
    vt = _dot(ckvn, wv_ref[...]).T.astype(bf16)
    for hd in range(D_HEADS):
        vt_ref[hd] = vt[hd * V_DIM:(hd + 1) * V_DIM, :]


def _inproj(x, lw, cosq, sinq):
    nt = T_ALL // TM
    tps = SEQ // TM
    full = lambda shape: pl.BlockSpec(shape, lambda i: (0,) * len(shape))
    in_specs = [
        pl.BlockSpec((TM, D_MODEL), lambda i: (i, 0)),
        pl.BlockSpec((8, D_MODEL), lambda i: (jnp.maximum(i * (TM // 8) - 1, 0), 0)),
        pl.BlockSpec((8, D_MODEL), lambda i: (jnp.minimum((i + 1) * (TM // 8), T_ALL // 8 - 1), 0)),
        full((1, D_MODEL)),
        full((D_MODEL, N_IN_PACK)),
        full((1, GROUP_W)),
        full((A_HEADS, CHUNK, CHUNK)),
        full((CHUNK, GROUP_W)),
        full((3, GROUP_W)),
        full((1, Q_RANK)),
        full((1, KV_RANK)),
        full((Q_RANK, D_HEADS * LANES)),
        full((Q_RANK, D_HEADS * LANES)),
        full((KV_RANK, D_HEADS * LANES)),
        full((KV_RANK, D_HEADS * V_DIM)),
        pl.BlockSpec((TM, LANES), lambda i: (i % tps, 0)),
        pl.BlockSpec((TM, LANES), lambda i: (i % tps, 0)),
    ]
    out_specs = [
        pl.BlockSpec((TM, GROUP_W), lambda i: (i, 0)),
        pl.BlockSpec((TM, GROUP_W), lambda i: (i, 0)),
        pl.BlockSpec((TM, C_COLS), lambda i: (i, 0)),
        pl.BlockSpec((None, D_HEADS, LANES, TM), lambda i: (i // tps, 0, 0, i % tps)),
        pl.BlockSpec((None, D_HEADS, TM, LANES), lambda i: (i // tps, 0, i % tps, 0)),
        pl.BlockSpec((None, D_HEADS, V_DIM, TM), lambda i: (i // tps, 0, 0, i % tps)),
    ]
    out_shape = [
        jax.ShapeDtypeStruct((T_ALL, GROUP_W), f32),
        jax.ShapeDtypeStruct((T_ALL, GROUP_W), f32),
        jax.ShapeDtypeStruct((T_ALL, C_COLS), bf16),
        jax.ShapeDtypeStruct((NSEQ, D_HEADS, LANES, SEQ), bf16),
        jax.ShapeDtypeStruct((NSEQ, D_HEADS, SEQ, LANES), bf16),
        jax.ShapeDtypeStruct((NSEQ, D_HEADS, V_DIM, SEQ), bf16),
    ]
    return pl.pallas_call(
        _inproj_kernel, grid=(nt,), in_specs=in_specs, out_specs=out_specs, out_shape=out_shape,
        name="inproj",
        compiler_params=pltpu.CompilerParams(dimension_semantics=("parallel",),
                                             vmem_limit_bytes=VMEM_LIMIT),
    )(x, x, x, lw["g1"], lw["w_in"], lw["a_vg"], lw["a_ws"], lw["a_bias"], lw["b_conv"],
      lw["d_qg"], lw["d_kvg"], lw["w_uq"], lw["w_uq_rot"], lw["w_k"], lw["w_v"], cosq, sinq)


def _natten_kernel(q_ref, k_ref, v_ref, bias_ref, o_ref):
    jb = pl.program_id(1)
    lane = lax.broadcasted_iota(jnp.int32, (GRID_W, GROUP_W), 1)
    head_of_lane = lane // HEAD_DIM
    nkeys = WIN_R * GRID_W

    jobs = []
    for i in range(ROWS_PER_STEP):
        r = jb * ROWS_PER_STEP + i
        rs = jnp.clip(r - WIN_R // 2, 0, ROWS - WIN_R)
        d = rs - r + (WIN_R - 1)
        kstart = pl.multiple_of(rs * GRID_W, GRID_W)
        q = q_ref[i * GRID_W:(i + 1) * GRID_W, :]
        zero = jnp.zeros_like(q)
        qq = jnp.concatenate([jnp.where(head_of_lane == h, q, zero) for h in range(C_HEADS)], axis=0)
        jobs.append((_dot_nt(qq, k_ref[pl.ds(kstart, nkeys), :]), d, kstart))
    for i, (sc, d, kstart) in enumerate(jobs):
        sc = sc + bias_ref[d]
        m = jnp.max(sc, axis=-1, keepdims=True)
        p = jnp.exp(sc - m)
        l = jnp.sum(p, axis=-1, keepdims=True)
        o = _dot(p.astype(bf16), v_ref[pl.ds(kstart, nkeys), :]) / l
        out = jnp.zeros((GRID_W, GROUP_W), f32)
        for h in range(C_HEADS):
            out = jnp.where(head_of_lane == h, o[h * GRID_W:(h + 1) * GRID_W], out)
        o_ref[i * GRID_W:(i + 1) * GRID_W, :] = out


def _natten(zc, bias):
    nb = ROWS // ROWS_PER_STEP
    tq = ROWS_PER_STEP * GRID_W
    return pl.pallas_call(
        _natten_kernel, grid=(NSEQ, nb),
        in_specs=[
            pl.BlockSpec((tq, GROUP_W), lambda b, j: (b * nb + j, 0)),
            pl.BlockSpec((SEQ, GROUP_W), lambda b, j: (b, 1)),
            pl.BlockSpec((SEQ, GROUP_W), lambda b, j: (b, 2)),
            pl.BlockSpec((WIN_R, C_HEADS * GRID_W, WIN_R * GRID_W), lambda b, j: (0, 0, 0)),
        ],
        out_specs=pl.BlockSpec((tq, GROUP_W), lambda b, j: (b * nb + j, 0)),
        out_shape=jax.ShapeDtypeStruct((T_ALL, GROUP_W), f32),
        name="natten",
        compiler_params=pltpu.CompilerParams(dimension_semantics=("parallel", "arbitrary"),
                                             vmem_limit_bytes=VMEM_LIMIT),
    )(zc, zc, zc, bias)


def _mla_kernel(qt_ref, k_ref, vt_ref, o_ref, s_scr):
    nchunk = SEQ // TK
    heads = list(range(D_HEADS))
    qts = [qt_ref[h] for h in heads]

    def qk(slot, h, ci):
        ks = pl.multiple_of(ci * TK, TK)
        s_scr[slot, h] = _dot(k_ref[h, pl.ds(ks, TK), :], qts[h])

    ones = jnp.ones((L_ROWS, TK), bf16)

    def softmax_pv(slot, h, ci, stat):
        m, acc = stat
        ks = pl.multiple_of(ci * TK, TK)
        vt = jnp.concatenate([vt_ref[h, :, pl.ds(ks, TK)], ones], axis=0)
        s = s_scr[slot, h]
        m_new = jnp.maximum(m, jnp.max(s, axis=0, keepdims=True))
        alpha = jnp.exp2(m - m_new)
        p = jnp.exp2(s - m_new)
        acc = acc * alpha + _dot(vt, p.astype(bf16))
        return m_new, acc

    def half(cur, ci, stats, prefetch=True):
        out = []
        for h in heads:
            if prefetch:
                qk(1 - cur, h, ci + 1)
            if h >= MLA_LAG:
                out.append(softmax_pv(cur, h - MLA_LAG, ci, stats[h - MLA_LAG]))
        for h in range(D_HEADS - MLA_LAG, D_HEADS):
            out.append(softmax_pv(cur, h, ci, stats[h]))
        return tuple(out)

    def body(cp, stats):
        for u in range(MLA_UNROLL):
            stats = half(u % 2, MLA_UNROLL * cp + u, stats)
        return stats

    for h in heads:
        qk(0, h, 0)
    init = tuple((jnp.full((1, TQ), NEG, f32), jnp.zeros((V_DIM + L_ROWS, TQ), f32)) for _ in heads)
    res = lax.fori_loop(0, nchunk // MLA_UNROLL - 1, body, init)
    for u in range(MLA_UNROLL):
        res = half(u % 2, nchunk - MLA_UNROLL + u, res, prefetch=u < MLA_UNROLL - 1)
    o_ref[...] = jnp.concatenate([acc[:V_DIM] / acc[V_DIM:V_DIM + 1] for (_, acc) in res], axis=0).T


def _mla(qt, k4, vt):
    nq = SEQ // TQ
    return pl.pallas_call(
        _mla_kernel, grid=(NSEQ, nq),
        in_specs=[
            pl.BlockSpec((None, D_HEADS, LANES, TQ), lambda b, i: (b, 0, 0, i)),
            pl.BlockSpec((None, D_HEADS, SEQ, LANES), lambda b, i: (b, 0, 0, 0)),
            pl.BlockSpec((None, D_HEADS, V_DIM, SEQ), lambda b, i: (b, 0, 0, 0)),
        ],
        out_specs=pl.BlockSpec((TQ, GROUP_W), lambda b, i: (b * nq + i, 0)),
        out_shape=jax.ShapeDtypeStruct((T_ALL, GROUP_W), f32),
        scratch_shapes=[pltpu.VMEM((2, D_HEADS, TK, TQ), f32)],
        name="mla",
        compiler_params=pltpu.CompilerParams(
            dimension_semantics=("parallel", "arbitrary"),
            vmem_limit_bytes=VMEM_LIMIT),
    )(qt, k4, vt)


def _outproj_kernel(ya_ref, yb_ref, yc_ref, yd_ref, x_ref, og_ref, wo_ref, g2_ref, rwt_ref,
                    xn_ref, h2_ref, aff_ref):
    y = jnp.concatenate(
        [_rms(y_ref[...], og_ref[:, g * GROUP_W:(g + 1) * GROUP_W]).astype(bf16)
         for g, y_ref in enumerate((ya_ref, yb_ref, yc_ref, yd_ref))], axis=1)
    acc = x_ref[...] + _dot(y, wo_ref[...])
    xn_ref[...] = acc
    h2 = _rms(acc, g2_ref[...]).astype(bf16)
    h2_ref[...] = h2
    logits = _dot_nt(rwt_ref[...], h2)
    e = jnp.exp(logits - jnp.max(logits, axis=0, keepdims=True))
    aff_ref[...] = e / jnp.sum(e, axis=0, keepdims=True)


def _outproj(ya, yb, yc, yd, x, lw):
    nt = T_ALL // TM
    full = lambda shape: pl.BlockSpec(shape, lambda i: (0,) * len(shape))
    yspec = pl.BlockSpec((TM, GROUP_W), lambda i: (i, 0))
    xspec = pl.BlockSpec((TM, D_MODEL), lambda i: (i, 0))
    return pl.pallas_call(
        _outproj_kernel, grid=(nt,),
        in_specs=[yspec, yspec, yspec, yspec, xspec, full((1, D_MODEL)), full((D_MODEL, D_MODEL)),
                  full((1, D_MODEL)), full((N_EXPERTS, D_MODEL))],
        out_specs=[xspec, xspec, pl.BlockSpec((N_EXPERTS, TM), lambda i: (0, i))],
        out_shape=[jax.ShapeDtypeStruct((T_ALL, D_MODEL), f32),
                   jax.ShapeDtypeStruct((T_ALL, D_MODEL), bf16),
                   jax.ShapeDtypeStruct((N_EXPERTS, T_ALL), f32)],
        name="outproj",
        compiler_params=pltpu.CompilerParams(dimension_semantics=("parallel",),
                                             vmem_limit_bytes=VMEM_LIMIT),
    )(ya, yb, yc, yd, x, lw["og"], lw["w_o"], lw["g2"], lw["rwt"])


def _route_kernel(aff_ref, tri_ref, posm_ref, base_ref, thr_scr, need_scr, runeq_scr, runsel_scr, *, cap):
    j = pl.program_id(0)

    @pl.when(j == 0)
    def _():
        def refine(i, thr):
            cand = thr | jnp.left_shift(jnp.int32(1), 30 - i)
            bits = pltpu.bitcast(aff_ref[...], jnp.int32)
            cnt = jnp.sum(jnp.where(bits >= cand, 1.0, 0.0), axis=1, keepdims=True)
            return jnp.where(cnt >= cap, cand, thr)

        thr = lax.fori_loop(0, 31, refine, jnp.zeros((N_EXPERTS, 1), jnp.int32))
        bits = pltpu.bitcast(aff_ref[...], jnp.int32)
        above = jnp.sum(jnp.where(bits > thr, 1.0, 0.0), axis=1, keepdims=True)
        thr_scr[...] = jnp.broadcast_to(thr, (N_EXPERTS, LANES))
        need_scr[...] = jnp.broadcast_to(cap - above, (N_EXPERTS, LANES))
        runeq_scr[...] = jnp.zeros((N_EXPERTS, LANES), f32)
        runsel_scr[...] = jnp.zeros((N_EXPERTS, LANES), f32)

    thr = thr_scr[:, 0:1]
    need = need_scr[:, 0:1]
    tiles = []
    for u in range(ROUTE_TILES):
        t0 = pl.multiple_of((j * ROUTE_TILES + u) * TT, TT)
        bits = pltpu.bitcast(aff_ref[:, pl.ds(t0, TT)], jnp.int32)
        eq = bits == thr
        eq_f = jnp.where(eq, 1.0, 0.0)
        tiles.append((bits, eq, eq_f, _dot(eq_f.astype(bf16), tri_ref[...])))
    sels = []
    runeq = runeq_scr[:, 0:1]
    for bits, eq, eq_f, eq_before in tiles:
        sel = (bits > thr) | (eq & (runeq + eq_before < need))
        sel_f = jnp.where(sel, 1.0, 0.0)
        sels.append((sel, sel_f, _dot(sel_f.astype(bf16), tri_ref[...])))
        runeq = runeq + jnp.sum(eq_f, axis=1, keepdims=True)
    runeq_scr[...] = jnp.broadcast_to(runeq, (N_EXPERTS, LANES))
    base = runsel_scr[...]
    for u, (sel, sel_f, sel_before) in enumerate(sels):
        posm_ref[:, u * TT:(u + 1) * TT] = jnp.where(sel, base[:, 0:1] + sel_before, -1.0)
        base_ref[u] = base
        base = base + jnp.sum(sel_f, axis=1, keepdims=True)
    runsel_scr[...] = base


def _route(aff, row0, nrows, tri):
    cap = CAPACITY_FACTOR * nrows // N_EXPERTS
    nt = nrows // TT
    blk0 = row0 // nrows
    vec = pltpu.VMEM((N_EXPERTS, LANES), f32)
    return pl.pallas_call(
        functools.partial(_route_kernel, cap=cap), grid=(nt // ROUTE_TILES,),
        in_specs=[pl.BlockSpec((N_EXPERTS, nrows), lambda j: (0, blk0)),
                  pl.BlockSpec((TT, TT), lambda j: (0, 0))],
        out_specs=[pl.BlockSpec((N_EXPERTS, ROUTE_TILES * TT), lambda j: (0, j)),
                   pl.BlockSpec((ROUTE_TILES, N_EXPERTS, LANES), lambda j: (j, 0, 0))],
        out_shape=[jax.ShapeDtypeStruct((N_EXPERTS, nrows), f32),
                   jax.ShapeDtypeStruct((nt, N_EXPERTS, LANES), f32)],
        scratch_shapes=[pltpu.VMEM((N_EXPERTS, LANES), jnp.int32), vec, vec, vec],
        name="route",
        compiler_params=pltpu.CompilerParams(dimension_semantics=("arbitrary",)),
    )(aff, tri)


def _window_class(count, windows, align):
    longest = jnp.max(count, axis=1) + (align - 1)
    return sum((longest > w).astype(jnp.int32) for w in windows[:-1])


def _dispatch_kernel(big_ref, first_ref, posm_ref, h2_ref, xe_hbm, stage, sem, carry, *, cap, ntiles):
    j = pl.program_id(0)
    slot = j % 2

    @pl.when(j == 0)
    def _():
        carry[...] = jnp.zeros_like(carry)
        pad = DISP_WINDOWS[-1]
        stage[1, 0:pad, :] = jnp.zeros((pad, D_MODEL), bf16)
        pads = [pltpu.make_async_copy(stage.at[1, pl.ds(0, pad)], xe_hbm.at[e, pl.ds(cap, pad)], sem.at[1])
                for e in range(N_EXPERTS)]
        for p in pads:
            p.start()
        for p in pads:
            p.wait()

    def run_start(jj, e):
        return first_ref[jj * N_EXPERTS + e]

    def window_copy(jj, e, sl, win):
        a = pl.multiple_of(run_start(jj, e) // ROW_ALIGN * ROW_ALIGN, ROW_ALIGN)
        return pltpu.make_async_copy(stage.at[sl, pl.ds(e * win, win)], xe_hbm.at[e, pl.ds(a, win)], sem.at[sl])

    def for_tile_window(jj, fn):
        for flag, win in enumerate(DISP_WINDOWS):
            @pl.when(big_ref[jj] == flag)
            def _():
                fn(win)

    def wait_all(jj, sl):
        def go(win):
            for e in range(N_EXPERTS):
                window_copy(jj, e, sl, win).wait()
        for_tile_window(jj, go)

    def compact(win):
        row = lax.broadcasted_iota(jnp.int32, (win, TT), 0)
        starts = [run_start(j, e) // ROW_ALIGN * ROW_ALIGN for e in range(N_EXPERTS)]
        onehot = jnp.concatenate(
            [jnp.where((row + starts[e]).astype(f32) == posm_ref[e:e + 1, :], 1.0, 0.0)
             for e in range(N_EXPERTS)], axis=0).astype(bf16)
        rows = _dot(onehot, h2_ref[...])
        head = lax.broadcasted_iota(jnp.int32, (ROW_ALIGN, D_MODEL), 0)
        for e in range(N_EXPERTS):
            lead = run_start(j, e) - starts[e]
            w = rows[e * win:(e + 1) * win]
            top = w[:ROW_ALIGN] + jnp.where(head < lead, carry[e], 0.0)
            stage[slot, e * win:e * win + ROW_ALIGN, :] = top.astype(bf16)
            stage[slot, e * win + ROW_ALIGN:(e + 1) * win, :] = w[ROW_ALIGN:].astype(bf16)
        for e in range(N_EXPERTS):
            end = jnp.where(j + 1 < ntiles, run_start(jnp.minimum(j + 1, ntiles - 1), e), cap)
            off = jnp.minimum(end // ROW_ALIGN * ROW_ALIGN - starts[e], win - ROW_ALIGN)
            off = pl.multiple_of(off, ROW_ALIGN)
            carry[e] = stage[slot, pl.ds(e * win + off, ROW_ALIGN), :].astype(f32)

    for_tile_window(j, compact)

    @pl.when(j > 0)
    def _():
        wait_all(j - 1, 1 - slot)

    def start_all(win):
        for e in range(N_EXPERTS):
            window_copy(j, e, slot, win).start()
    for_tile_window(j, start_all)

    @pl.when(j == ntiles - 1)
    def _():
        wait_all(j, slot)


def _dispatch(h2, posm, first, count, row0, nrows):
    cap = CAPACITY_FACTOR * nrows // N_EXPERTS
    nt = nrows // TT
    blk0 = row0 // TT
    big = _window_class(count, DISP_WINDOWS, ROW_ALIGN)
    return pl.pallas_call(
        functools.partial(_dispatch_kernel, cap=cap, ntiles=nt),
        grid_spec=pltpu.PrefetchScalarGridSpec(
            num_scalar_prefetch=2, grid=(nt,),
            in_specs=[pl.BlockSpec((N_EXPERTS, TT), lambda j, *_: (0, j)),
                      pl.BlockSpec((TT, D_MODEL), lambda j, *_: (blk0 + j, 0))],
            out_specs=pl.BlockSpec(memory_space=pl.ANY),
            scratch_shapes=[pltpu.VMEM((2, N_EXPERTS * DISP_WINDOWS[-1], D_MODEL), bf16),
                            pltpu.SemaphoreType.DMA((2,)),
                            pltpu.VMEM((N_EXPERTS, ROW_ALIGN, D_MODEL), f32)],
        ),
        out_shape=jax.ShapeDtypeStruct((N_EXPERTS, cap + DISP_WINDOWS[-1], D_MODEL), bf16),
        name="dispatch",
        compiler_params=pltpu.CompilerParams(dimension_semantics=("arbitrary",), vmem_limit_bytes=VMEM_LIMIT),
    )(big, first.reshape(-1), posm, h2)


def _ffn_kernel(xe_ref, rw_ref, wg_ref, wu_ref, wd_ref, ye_ref):
    e = pl.program_id(0)
    x = xe_ref[...]
    logits = _dot(x, rw_ref[...])
    lane = lax.broadcasted_iota(jnp.int32, logits.shape, 1)
    logits = jnp.where(lane < N_EXPERTS, logits, NEG)
    ex = jnp.exp(logits - jnp.max(logits, axis=-1, keepdims=True))
    gate = jnp.sum(jnp.where(lane == e, ex, 0.0), axis=-1, keepdims=True) / jnp.sum(ex, axis=-1, keepdims=True)
    hg = _dot(x, wg_ref[...])
    hu = _dot(x, wu_ref[...])
    h = (hg * (1.0 / (1.0 + jnp.exp(-hg))) * hu).astype(bf16)
    ye_ref[...] = _dot(h, wd_ref[...]) * gate


def _ffn(xe, lw, nrows):
    cap = CAPACITY_FACTOR * nrows // N_EXPERTS
    wspec = pl.BlockSpec((None, D_MODEL, D_EXPERT), lambda e, t: (e, 0, 0))
    return pl.pallas_call(
        _ffn_kernel, grid=(N_EXPERTS, cap // TME),
        in_specs=[
            pl.BlockSpec((None, TME, D_MODEL), lambda e, t: (e, t, 0)),
            pl.BlockSpec((D_MODEL, LANES), lambda e, t: (0, 0)),
            wspec, wspec,
            pl.BlockSpec((None, D_EXPERT, D_MODEL), lambda e, t: (e, 0, 0)),
        ],
        out_specs=pl.BlockSpec((None, TME, D_MODEL), lambda e, t: (e, t, 0)),
        out_shape=jax.ShapeDtypeStruct((N_EXPERTS, cap, D_MODEL), f32),
        name="expert_ffn",
        compiler_params=pltpu.CompilerParams(dimension_semantics=("parallel", "arbitrary"),
                                             vmem_limit_bytes=VMEM_LIMIT),
    )(xe, lw["rw"], lw["e_wg"], lw["e_wu"], lw["e_wd"])


def _combine_kernel(big_ref, first_ref, posm_ref, x_ref, ye_hbm, o_ref, buf, sem, *, cap, ntiles):
    j = pl.program_id(0)

    def window_start(jj, e, win):
        return jnp.minimum(first_ref[jj * N_EXPERTS + e] // COMB_ALIGN * COMB_ALIGN, cap - win)

    def window_copy(jj, e, slot, win):
        a = pl.multiple_of(window_start(jj, e, win), COMB_ALIGN)
        return pltpu.make_async_copy(ye_hbm.at[e, pl.ds(a, win)], buf.at[slot, pl.ds(e * win, win)],
                                     sem.at[slot])

    def for_tile_window(jj, fn):
        for flag, win in enumerate(COMB_WINDOWS):
            @pl.when(big_ref[jj] == flag)
            def _():
                fn(win)

    def start_all(jj, slot):
        def go(win):
            for e in range(N_EXPERTS):
                window_copy(jj, e, slot, win).start()
        for_tile_window(jj, go)

    @pl.when(j == 0)
    def _():
        start_all(0, 0)

    slot = j % 2

    def wait_all(win):
        for e in range(N_EXPERTS):
            window_copy(j, e, slot, win).wait()
    for_tile_window(j, wait_all)

    @pl.when(j + 1 < ntiles)
    def _():
        start_all(j + 1, 1 - slot)

    def add_rows(win):
        row = lax.broadcasted_iota(jnp.int32, (win, TT), 0)
        onehot = jnp.concatenate(
            [jnp.where((row + window_start(j, e, win)).astype(f32) == posm_ref[e:e + 1, :], 1.0, 0.0)
             for e in range(N_EXPERTS)], axis=0).astype(bf16)
        y = buf[slot, 0:N_EXPERTS * win, :]
        hi = y.astype(bf16)
        lo = (y - hi.astype(f32)).astype(bf16)
        o_ref[...] = x_ref[...] + _dot_tn(onehot, hi) + _dot_tn(onehot, lo)
    for_tile_window(j, add_rows)


def _combine(x, ye, posm, first, count, row0, nrows):
    cap = CAPACITY_FACTOR * nrows // N_EXPERTS
    nt = nrows // TT
    blk0 = row0 // TT
    big = _window_class(count, COMB_WINDOWS, COMB_ALIGN)
    return pl.pallas_call(
        functools.partial(_combine_kernel, cap=cap, ntiles=nt),
        grid_spec=pltpu.PrefetchScalarGridSpec(
            num_scalar_prefetch=2, grid=(nt,),
            in_specs=[pl.BlockSpec((N_EXPERTS, TT), lambda j, *_: (0, j)),
                      pl.BlockSpec((TT, D_MODEL), lambda j, *_: (blk0 + j, 0)),
                      pl.BlockSpec(memory_space=pl.ANY)],
            out_specs=pl.BlockSpec((TT, D_MODEL), lambda j, *_: (blk0 + j, 0)),
            scratch_shapes=[pltpu.VMEM((2, N_EXPERTS * COMB_WINDOWS[-1], D_MODEL), f32),
                            pltpu.SemaphoreType.DMA((2,))],
        ),
        out_shape=jax.ShapeDtypeStruct(x.shape, f32),
        input_output_aliases={3: 0},
        name="combine",
        compiler_params=pltpu.CompilerParams(dimension_semantics=("arbitrary",), vmem_limit_bytes=VMEM_LIMIT),
    )(big, first.reshape(-1), posm, x, ye)


def _final_kernel(x_ref, g_ref, o_ref):
    o_ref[...] = _rms(x_ref[...], g_ref[...])


def _final_norm(x, g, row0, nrows):
    off = row0 // TM
    return pl.pallas_call(
        _final_kernel, grid=(nrows // TM,),
        in_specs=[pl.BlockSpec((TM, D_MODEL), lambda i: (i + off, 0)),
                  pl.BlockSpec((1, D_MODEL), lambda i: (0, 0))],
        out_specs=pl.BlockSpec((TM, D_MODEL), lambda i: (i, 0)),
        out_shape=jax.ShapeDtypeStruct((nrows, D_MODEL), f32),
        name="final_norm",
        compiler_params=pltpu.CompilerParams(dimension_semantics=("parallel",)),
    )(x, g)


def _natten_bias_table(rpb):
    c = np.arange(GRID_W)
    cs = np.clip(c - WIN_C // 2, 0, GRID_W - WIN_C)
    kc = np.arange(GRID_W)
    col_mask = (kc[None, :] >= cs[:, None]) & (kc[None, :] < cs[:, None] + WIN_C)
    dc = np.clip(kc[None, :] - c[:, None], -(WIN_C - 1), WIN_C - 1) + (WIN_C - 1)
    tb = jnp.where(col_mask[None, None], rpb[:, :, dc].astype(f32), NEG)
    dw = np.arange(WIN_R)[:, None] + np.arange(WIN_R)[None, :]
    t = tb[:, dw]
    t = t.transpose(1, 0, 3, 2, 4).reshape(WIN_R, C_HEADS * GRID_W, WIN_R * GRID_W)
    return t


def _rope_slabs():
    inv = 1.0 / (ROPE_THETA ** (jnp.arange(0, ROPE_DIM, 2, dtype=f32) / ROPE_DIM))
    ang = jnp.arange(SEQ, dtype=f32)[:, None] * inv[None, :]
    cos, sin = jnp.cos(ang), jnp.sin(ang)
    ones = jnp.ones((SEQ, NOPE_DIM), f32)
    zpad = jnp.zeros((SEQ, LANES - NOPE_DIM - ROPE_DIM), f32)
    cosq = jnp.concatenate([ones, cos, cos, zpad], axis=1)
    sinq = jnp.concatenate([0.0 * ones, sin, sin, zpad], axis=1)
    return cosq, sinq


def _pack_layer(i, norm1_g, w_in, a_vnorm_g, a_ws, a_bs, b_conv, c_rpb, d_qnorm_g, d_kvnorm_g,
                d_w_uq, d_w_ukv, out_norm_g, w_o, norm2_g, router_w, e_w_gate, e_w_up, e_w_down):
    half = ROPE_DIM // 2
    w = w_in[i]
    kr0 = OFF_D + Q_RANK + KV_RANK
    kr = w[:, kr0:kr0 + ROPE_DIM]
    z64 = jnp.zeros((D_MODEL, NOPE_DIM), f32)
    z32 = jnp.zeros((D_MODEL, LANES - NOPE_DIM - ROPE_DIM), f32)
    kr_slab = jnp.concatenate([z64, kr, z32], axis=1)
    kr_rot = jnp.concatenate([z64, -kr[:, half:], kr[:, :half], z32], axis=1)
    w_pack = jnp.concatenate([w[:, :kr0], kr_slab, kr_rot], axis=1).astype(bf16)

    uq = d_w_uq[i].reshape(Q_RANK, D_HEADS, NOPE_DIM + ROPE_DIM)
    nope, rope = uq[..., :NOPE_DIM], uq[..., NOPE_DIM:]
    zq = jnp.zeros((Q_RANK, D_HEADS, LANES - NOPE_DIM - ROPE_DIM), f32)
    w_uq = jnp.concatenate([nope, rope, zq], axis=-1).reshape(Q_RANK, D_HEADS * LANES)
    w_uq_rot = jnp.concatenate([0.0 * nope, -rope[..., half:], rope[..., :half], zq],
                               axis=-1).reshape(Q_RANK, D_HEADS * LANES)
    ukv = d_w_ukv[i].reshape(KV_RANK, D_HEADS, NOPE_DIM + V_DIM)
    w_k = jnp.concatenate([ukv[..., :NOPE_DIM], jnp.zeros((KV_RANK, D_HEADS, LANES - NOPE_DIM), f32)],
                          axis=-1).reshape(KV_RANK, D_HEADS * LANES)
    w_v = ukv[..., NOPE_DIM:].reshape(KV_RANK, D_HEADS * V_DIM)
    rw = jnp.concatenate([router_w[i], jnp.zeros((D_MODEL, LANES - N_EXPERTS), f32)], axis=1)
    return {
        "g1": norm1_g[i].reshape(1, D_MODEL),
        "w_in": w_pack,
        "a_vg": a_vnorm_g[i].reshape(1, GROUP_W),
        "a_ws": a_ws[i].astype(bf16),
        "a_bias": jnp.repeat(a_bs[i].T, HEAD_DIM, axis=1),
        "b_conv": b_conv[i],
        "c_bias": _natten_bias_table(c_rpb[i]),
        "d_qg": d_qnorm_g[i].reshape(1, Q_RANK),
        "d_kvg": d_kvnorm_g[i].reshape(1, KV_RANK),
        "w_uq": w_uq.astype(bf16),
        "w_uq_rot": w_uq_rot.astype(bf16),
        "w_k": w_k.astype(bf16),
        "w_v": w_v.astype(bf16),
        "og": out_norm_g[i].reshape(1, D_MODEL),
        "w_o": w_o[i].astype(bf16),
        "g2": norm2_g[i].reshape(1, D_MODEL),
        "rw": rw.astype(bf16),
        "rwt": router_w[i].T.astype(bf16),
        "e_wg": e_w_gate[i].astype(bf16),
        "e_wu": e_w_up[i].astype(bf16),
        "e_wd": e_w_down[i].astype(bf16),
    }


def _moe(xn, h2, aff, lw, tri):
    x = xn
    for row0, nrows in ((0, BATCH * SEQ), (BATCH * SEQ, DEC_BATCH * SEQ)):
        cap = CAPACITY_FACTOR * nrows // N_EXPERTS
        posm, base = _route(aff, row0, nrows, tri)
        first = base[:, :, 0].astype(jnp.int32)
        count = jnp.diff(first, axis=0, append=jnp.full((1, N_EXPERTS), cap, jnp.int32))
        xe = _dispatch(h2, posm, first, count, row0, nrows)
        ye = _ffn(xe, lw, nrows)
        x = _combine(x, ye, posm, first, count, row0, nrows)
    return x


def kernel(x_prompt, x_sample, norm1_g, w_in, a_vnorm_g, a_ws, a_bs, b_conv, c_rpb, d_qnorm_g, d_kvnorm_g, d_w_uq, d_w_ukv, out_norm_g, w_o, norm2_g, router_w, e_w_gate, e_w_up, e_w_down, final_norm_g):
    n_p = BATCH * SEQ
    n_s = DEC_BATCH * SEQ
    x = jnp.concatenate([x_prompt.reshape(n_p, D_MODEL), x_sample.reshape(n_s, D_MODEL)], axis=0)
    cosq, sinq = _rope_slabs()
    tri = jnp.asarray(np.triu(np.ones((TT, TT), np.float32), 1), bf16)
    for i in range(DEPTH):
        lw = _pack_layer(i, norm1_g, w_in, a_vnorm_g, a_ws, a_bs, b_conv, c_rpb, d_qnorm_g, d_kvnorm_g,
                         d_w_uq, d_w_ukv, out_norm_g, w_o, norm2_g, router_w, e_w_gate, e_w_up, e_w_down)
        ya, yb, zc, qt, k4, vt = _inproj(x, lw, cosq, sinq)
        yc = _natten(zc, lw["c_bias"])
        yd = _mla(qt, k4, vt)
        xn, h2, aff = _outproj(ya, yb, yc, yd, x, lw)
        x = _moe(xn, h2, aff, lw, tri)
    g = final_norm_g.reshape(1, D_MODEL)
    y_p = _final_norm(x, g, 0, n_p).reshape(BATCH, SEQ, D_MODEL)
    y_s = _final_norm(x, g, n_p, n_s).reshape(DEC_BATCH, SEQ, D_MODEL)
    return (y_p, y_s)
```

```python
import functools
import math

import numpy as np
import jax
import jax.numpy as jnp
from jax import lax
from jax.experimental import pallas as pl
from jax.experimental.pallas import tpu as pltpu

D_MODEL = 1024
BATCH = 4
SEQ = 8192
DEPTH = 4
DEC_BATCH = 2
GROUP_W = 256
HEAD_DIM = 64
A_HEADS = 4
CHUNK = 128
C_HEADS = 4
GRID_W = 64
WIN_R = 8
WIN_C = 16
D_HEADS = 4
Q_RANK = 256
KV_RANK = 128
NOPE_DIM = 64
ROPE_DIM = 32
V_DIM = 64
ROPE_THETA = 10000.0
N_EXPERTS = 16
D_EXPERT = 1024
CAPACITY_FACTOR = 2
EPS = 1e-6

NSEQ = BATCH + DEC_BATCH
T_ALL = NSEQ * SEQ
ROWS = SEQ // GRID_W
LANES = 128
NEG = -1e30

A_COLS = 2 * GROUP_W
B_COLS = 3 * GROUP_W
C_COLS = 3 * GROUP_W
OFF_B = A_COLS
OFF_C = OFF_B + B_COLS
OFF_D = OFF_C + C_COLS
D_PACK = Q_RANK + KV_RANK + 2 * LANES
N_IN_PACK = OFF_D + D_PACK

TM = 512
TQ = 256
TK = 512
L_ROWS = 16
MLA_UNROLL = 4
MLA_LAG = 1
MLA_QSCALE = (NOPE_DIM + ROPE_DIM) ** -0.5 * math.log2(math.e)
ROWS_PER_STEP = 8
TME = 512
TT = 256
ROUTE_TILES = 4
ROW_ALIGN = 16
DISP_WINDOWS = (80, 144, TT + ROW_ALIGN)
COMB_ALIGN = ROW_ALIGN
COMB_WINDOWS = (80, 144, TT + COMB_ALIGN)
VMEM_LIMIT = 56 * 1024 * 1024

f32 = jnp.float32
bf16 = jnp.bfloat16


def _rms(x, g):
    return x * lax.rsqrt(jnp.mean(x * x, axis=-1, keepdims=True) + EPS) * g


def _dot(a, b):
    return jnp.dot(a, b, preferred_element_type=f32)


def _dot_nt(a, b):
    return lax.dot_general(a, b, (((1,), (1,)), ((), ())), preferred_element_type=f32)


def _dot_tn(a, b):
    return lax.dot_general(a, b, (((0,), (0,)), ((), ())), preferred_element_type=f32)


def _inproj_kernel(x_ref, xp_ref, xn_ref, g1_ref, win_ref, avg_ref, aws_ref, abias_ref,
                   bconv_ref, qg_ref, kvg_ref, wuq_ref, wuqr_ref, wk_ref, wv_ref,
                   cos_ref, sin_ref,
                   ya_ref, yb_ref, zc_ref, qt_ref, k4_ref, vt_ref):
    i = pl.program_id(0)
    j = i % (SEQ // TM)
    g1 = g1_ref[...]
    h = _rms(x_ref[...], g1).astype(bf16)

    za = jax.nn.gelu(_dot(h, win_ref[:, 0:A_COLS]))
    u = za[:, :GROUP_W]
    v = _rms(za[:, GROUP_W:], avg_ref[...]).astype(bf16)
    lane = lax.broadcasted_iota(jnp.int32, (CHUNK, GROUP_W), 1)
    for c in range(TM // CHUNK):
        vc = v[c * CHUNK:(c + 1) * CHUNK, :]
        sv = abias_ref[...]
        for hd in range(A_HEADS):
            r = _dot(aws_ref[hd], vc)
            sv = sv + jnp.where((lane >= hd * HEAD_DIM) & (lane < (hd + 1) * HEAD_DIM), r, 0.0)
        ya_ref[c * CHUNK:(c + 1) * CHUNK, :] = u[c * CHUNK:(c + 1) * CHUNK, :] * sv

    zb = _dot(h, win_ref[:, OFF_B:OFF_B + B_COLS])
    bg = zb[:, :GROUP_W]
    y = zb[:, GROUP_W:2 * GROUP_W] * zb[:, 2 * GROUP_W:]
    xh = jnp.concatenate([xp_ref[...], xn_ref[...]], axis=0)
    hh = _rms(xh, g1).astype(bf16)
    zh = _dot(hh, win_ref[:, OFF_B + GROUP_W:OFF_B + B_COLS])
    yh = zh[:, :GROUP_W] * zh[:, GROUP_W:]
    y_before = jnp.where(j == 0, 0.0, yh[7:8, :])
    y_after = jnp.where(j == SEQ // TM - 1, 0.0, yh[8:9, :])
    row = lax.broadcasted_iota(jnp.int32, (TM, GROUP_W), 0)
    y_m1 = jnp.where(row == 0, y_before, pltpu.roll(y, 1, axis=0))
    y_p1 = jnp.where(row == TM - 1, y_after, pltpu.roll(y, TM - 1, axis=0))
    wc = bconv_ref[...]
    yb_ref[...] = bg * (wc[0:1, :] * y_m1 + wc[1:2, :] * y + wc[2:3, :] * y_p1)

    zc = _dot(h, win_ref[:, OFF_C:OFF_C + C_COLS])
    zc_ref[:, 0:GROUP_W] = (zc[:, 0:GROUP_W] * (HEAD_DIM ** -0.5)).astype(bf16)
    zc_ref[:, GROUP_W:] = zc[:, GROUP_W:].astype(bf16)

    zd = _dot(h, win_ref[:, OFF_D:OFF_D + D_PACK])
    cqn = _rms(zd[:, 0:Q_RANK], qg_ref[...]).astype(bf16)
    ckvn = _rms(zd[:, Q_RANK:Q_RANK + KV_RANK], kvg_ref[...]).astype(bf16)
    cos = cos_ref[...]
    sin = sin_ref[...]
    o = Q_RANK + KV_RANK
    kr = zd[:, o:o + LANES] * cos + zd[:, o + LANES:o + 2 * LANES] * sin
    q_pre = _dot(cqn, wuq_ref[...])
    q_rot = _dot(cqn, wuqr_ref[...])
    kn = _dot(ckvn, wk_ref[...])
    for hd in range(D_HEADS):
        sl = slice(hd * LANES, (hd + 1) * LANES)
        qt_ref[hd] = ((q_pre[:, sl] * cos + q_rot[:, sl] * sin) * MLA_QSCALE).T.astype(bf16)
        k4_ref[hd] = (kn[:, sl] + kr).astype(bf16)
    vt = _dot(ckvn, wv_ref[...]).T.astype(bf16)
    for hd in range(D_HEADS):
        vt_ref[hd] = vt[hd * V_DIM:(hd + 1) * V_DIM, :]


def _inproj(x, lw, cosq, sinq):
    nt = T_ALL // TM
    tps = SEQ // TM
    full = lambda shape: pl.BlockSpec(shape, lambda i: (0,) * len(shape))
    in_specs = [
        pl.BlockSpec((TM, D_MODEL), lambda i: (i, 0)),
        pl.BlockSpec((8, D_MODEL), lambda i: (jnp.maximum(i * (TM // 8) - 1, 0), 0)),
        pl.BlockSpec((8, D_MODEL), lambda i: (jnp.minimum((i + 1) * (TM // 8), T_ALL // 8 - 1), 0)),
        full((1, D_MODEL)),
        full((D_MODEL, N_IN_PACK)),
        full((1, GROUP_W)),
        full((A_HEADS, CHUNK, CHUNK)),
        full((CHUNK, GROUP_W)),
        full((3, GROUP_W)),
        full((1, Q_RANK)),
        full((1, KV_RANK)),
        full((Q_RANK, D_HEADS * LANES)),
        full((Q_RANK, D_HEADS * LANES)),
        full((KV_RANK, D_HEADS * LANES)),
        full((KV_RANK, D_HEADS * V_DIM)),
        pl.BlockSpec((TM, LANES), lambda i: (i % tps, 0)),
        pl.BlockSpec((TM, LANES), lambda i: (i % tps, 0)),
    ]
    out_specs = [
        pl.BlockSpec((TM, GROUP_W), lambda i: (i, 0)),
        pl.BlockSpec((TM, GROUP_W), lambda i: (i, 0)),
        pl.BlockSpec((TM, C_COLS), lambda i: (i, 0)),
        pl.BlockSpec((None, D_HEADS, LANES, TM), lambda i: (i // tps, 0, 0, i % tps)),
        pl.BlockSpec((None, D_HEADS, TM, LANES), lambda i: (i // tps, 0, i % tps, 0)),
        pl.BlockSpec((None, D_HEADS, V_DIM, TM), lambda i: (i // tps, 0, 0, i % tps)),
    ]
    out_shape = [
        jax.ShapeDtypeStruct((T_ALL, GROUP_W), f32),
        jax.ShapeDtypeStruct((T_ALL, GROUP_W), f32),
        jax.ShapeDtypeStruct((T_ALL, C_COLS), bf16),
        jax.ShapeDtypeStruct((NSEQ, D_HEADS, LANES, SEQ), bf16),
        jax.ShapeDtypeStruct((NSEQ, D_HEADS, SEQ, LANES), bf16),
        jax.ShapeDtypeStruct((NSEQ, D_HEADS, V_DIM, SEQ), bf16),
    ]
    return pl.pallas_call(
        _inproj_kernel, grid=(nt,), in_specs=in_specs, out_specs=out_specs, out_shape=out_shape,
        name="inproj",
        compiler_params=pltpu.CompilerParams(dimension_semantics=("parallel",),
                                             vmem_limit_bytes=VMEM_LIMIT),
    )(x, x, x, lw["g1"], lw["w_in"], lw["a_vg"], lw["a_ws"], lw["a_bias"], lw["b_conv"],
      lw["d_qg"], lw["d_kvg"], lw["w_uq"], lw["w_uq_rot"], lw["w_k"], lw["w_v"], cosq, sinq)


def _natten_kernel(q_ref, k_ref, v_ref, bias_ref, o_ref):
    jb = pl.program_id(1)
    lane = lax.broadcasted_iota(jnp.int32, (GRID_W, GROUP_W), 1)
    head_of_lane = lane // HEAD_DIM
    nkeys = WIN_R * GRID_W

    jobs = []
    for i in range(ROWS_PER_STEP):
        r = jb * ROWS_PER_STEP + i
        rs = jnp.clip(r - WIN_R // 2, 0, ROWS - WIN_R)
        d = rs - r + (WIN_R - 1)
        kstart = pl.multiple_of(rs * GRID_W, GRID_W)
        q = q_ref[i * GRID_W:(i + 1) * GRID_W, :]
        zero = jnp.zeros_like(q)
        qq = jnp.concatenate([jnp.where(head_of_lane == h, q, zero) for h in range(C_HEADS)], axis=0)
        jobs.append((_dot_nt(qq, k_ref[pl.ds(kstart, nkeys), :]), d, kstart))
    for i, (sc, d, kstart) in enumerate(jobs):
        sc = sc + bias_ref[d]
        m = jnp.max(sc, axis=-1, keepdims=True)
        p = jnp.exp(sc - m)
        l = jnp.sum(p, axis=-1, keepdims=True)
        o = _dot(p.astype(bf16), v_ref[pl.ds(kstart, nkeys), :]) / l
        out = jnp.zeros((GRID_W, GROUP_W), f32)
        for h in range(C_HEADS):
            out = jnp.where(head_of_lane == h, o[h * GRID_W:(h + 1) * GRID_W], out)
        o_ref[i * GRID_W:(i + 1) * GRID_W, :] = out


def _natten(zc, bias):
    nb = ROWS // ROWS_PER_STEP
    tq = ROWS_PER_STEP * GRID_W
    return pl.pallas_call(
        _natten_kernel, grid=(NSEQ, nb),
        in_specs=[
            pl.BlockSpec((tq, GROUP_W), lambda b, j: (b * nb + j, 0)),
            pl.BlockSpec((SEQ, GROUP_W), lambda b, j: (b, 1)),
            pl.BlockSpec((SEQ, GROUP_W), lambda b, j: (b, 2)),
            pl.BlockSpec((WIN_R, C_HEADS * GRID_W, WIN_R * GRID_W), lambda b, j: (0, 0, 0)),
        ],
        out_specs=pl.BlockSpec((tq, GROUP_W), lambda b, j: (b * nb + j, 0)),
        out_shape=jax.ShapeDtypeStruct((T_ALL, GROUP_W), f32),
        name="natten",
        compiler_params=pltpu.CompilerParams(dimension_semantics=("parallel", "arbitrary"),
                                             vmem_limit_bytes=VMEM_LIMIT),
    )(zc, zc, zc, bias)


def _mla_kernel(qt_ref, k_ref, vt_ref, o_ref, s_scr):
    nchunk = SEQ // TK
    heads = list(range(D_HEADS))
    qts = [qt_ref[h] for h in heads]

    def qk(slot, h, ci):
        ks = pl.multiple_of(ci * TK, TK)
        s_scr[slot, h] = _dot(k_ref[h, pl.ds(ks, TK), :], qts[h])

    ones = jnp.ones((L_ROWS, TK), bf16)

    def softmax_pv(slot, h, ci, stat):
        m, acc = stat
        ks = pl.multiple_of(ci * TK, TK)
        vt = jnp.concatenate([vt_ref[h, :, pl.ds(ks, TK)], ones], axis=0)
        s = s_scr[slot, h]
        m_new = jnp.maximum(m, jnp.max(s, axis=0, keepdims=True))
        alpha = jnp.exp2(m - m_new)
        p = jnp.exp2(s - m_new)
        acc = acc * alpha + _dot(vt, p.astype(bf16))
        return m_new, acc

    def half(cur, ci, stats, prefetch=True):
        out = []
        for h in heads:
            if prefetch:
                qk(1 - cur, h, ci + 1)
            if h >= MLA_LAG:
                out.append(softmax_pv(cur, h - MLA_LAG, ci, stats[h - MLA_LAG]))
        for h in range(D_HEADS - MLA_LAG, D_HEADS):
            out.append(softmax_pv(cur, h, ci, stats[h]))
        return tuple(out)

    def body(cp, stats):
        for u in range(MLA_UNROLL):
            stats = half(u % 2, MLA_UNROLL * cp + u, stats)
        return stats

    for h in heads:
        qk(0, h, 0)
    init = tuple((jnp.full((1, TQ), NEG, f32), jnp.zeros((V_DIM + L_ROWS, TQ), f32)) for _ in heads)
    res = lax.fori_loop(0, nchunk // MLA_UNROLL - 1, body, init)
    for u in range(MLA_UNROLL):
        res = half(u % 2, nchunk - MLA_UNROLL + u, res, prefetch=u < MLA_UNROLL - 1)
    o_ref[...] = jnp.concatenate([acc[:V_DIM] / acc[V_DIM:V_DIM + 1] for (_, acc) in res], axis=0).T


def _mla(qt, k4, vt):
    nq = SEQ // TQ
    return pl.pallas_call(
        _mla_kernel, grid=(NSEQ, nq),
        in_specs=[
            pl.BlockSpec((None, D_HEADS, LANES, TQ), lambda b, i: (b, 0, 0, i)),
            pl.BlockSpec((None, D_HEADS, SEQ, LANES), lambda b, i: (b, 0, 0, 0)),
            pl.BlockSpec((None, D_HEADS, V_DIM, SEQ), lambda b, i: (b, 0, 0, 0)),
        ],
        out_specs=pl.BlockSpec((TQ, GROUP_W), lambda b, i: (b * nq + i, 0)),
        out_shape=jax.ShapeDtypeStruct((T_ALL, GROUP_W), f32),
        scratch_shapes=[pltpu.VMEM((2, D_HEADS, TK, TQ), f32)],
        name="mla",
        compiler_params=pltpu.CompilerParams(
            dimension_semantics=("parallel", "arbitrary"),
            vmem_limit_bytes=VMEM_LIMIT),
    )(qt, k4, vt)


def _outproj_kernel(ya_ref, yb_ref, yc_ref, yd_ref, x_ref, og_ref, wo_ref, g2_ref, rwt_ref,
                    xn_ref, h2_ref, aff_ref):
    y = jnp.concatenate(
        [_rms(y_ref[...], og_ref[:, g * GROUP_W:(g + 1) * GROUP_W]).astype(bf16)
         for g, y_ref in enumerate((ya_ref, yb_ref, yc_ref, yd_ref))], axis=1)
    acc = x_ref[...] + _dot(y, wo_ref[...])
    xn_ref[...] = acc
    h2 = _rms(acc, g2_ref[...]).astype(bf16)
    h2_ref[...] = h2
    logits = _dot_nt(rwt_ref[...], h2)
    e = jnp.exp(logits - jnp.max(logits, axis=0, keepdims=True))
    aff_ref[...] = e / jnp.sum(e, axis=0, keepdims=True)


def _outproj(ya, yb, yc, yd, x, lw):
    nt = T_ALL // TM
    full = lambda shape: pl.BlockSpec(shape, lambda i: (0,) * len(shape))
    yspec = pl.BlockSpec((TM, GROUP_W), lambda i: (i, 0))
    xspec = pl.BlockSpec((TM, D_MODEL), lambda i: (i, 0))
    return pl.pallas_call(
        _outproj_kernel, grid=(nt,),
        in_specs=[yspec, yspec, yspec, yspec, xspec, full((1, D_MODEL)), full((D_MODEL, D_MODEL)),
                  full((1, D_MODEL)), full((N_EXPERTS, D_MODEL))],
        out_specs=[xspec, xspec, pl.BlockSpec((N_EXPERTS, TM), lambda i: (0, i))],
        out_shape=[jax.ShapeDtypeStruct((T_ALL, D_MODEL), f32),
                   jax.ShapeDtypeStruct((T_ALL, D_MODEL), bf16),
                   jax.ShapeDtypeStruct((N_EXPERTS, T_ALL), f32)],
        name="outproj",
        compiler_params=pltpu.CompilerParams(dimension_semantics=("parallel",),
                                             vmem_limit_bytes=VMEM_LIMIT),
    )(ya, yb, yc, yd, x, lw["og"], lw["w_o"], lw["g2"], lw["rwt"])


def _route_kernel(aff_ref, tri_ref, posm_ref, base_ref, thr_scr, need_scr, runeq_scr, runsel_scr, *, cap):
    j = pl.program_id(0)

    @pl.when(j == 0)
    def _():
        def refine(i, thr):
            cand = thr | jnp.left_shift(jnp.int32(1), 30 - i)
            bits = pltpu.bitcast(aff_ref[...], jnp.int32)
            cnt = jnp.sum(jnp.where(bits >= cand, 1.0, 0.0), axis=1, keepdims=True)
            return jnp.where(cnt >= cap, cand, thr)

        thr = lax.fori_loop(0, 31, refine, jnp.zeros((N_EXPERTS, 1), jnp.int32))
        bits = pltpu.bitcast(aff_ref[...], jnp.int32)
        above = jnp.sum(jnp.where(bits > thr, 1.0, 0.0), axis=1, keepdims=True)
        thr_scr[...] = jnp.broadcast_to(thr, (N_EXPERTS, LANES))
        need_scr[...] = jnp.broadcast_to(cap - above, (N_EXPERTS, LANES))
        runeq_scr[...] = jnp.zeros((N_EXPERTS, LANES), f32)
        runsel_scr[...] = jnp.zeros((N_EXPERTS, LANES), f32)

    thr = thr_scr[:, 0:1]
    need = need_scr[:, 0:1]
    tiles = []
    for u in range(ROUTE_TILES):
        t0 = pl.multiple_of((j * ROUTE_TILES + u) * TT, TT)
        bits = pltpu.bitcast(aff_ref[:, pl.ds(t0, TT)], jnp.int32)
        eq = bits == thr
        eq_f = jnp.where(eq, 1.0, 0.0)
        tiles.append((bits, eq, eq_f, _dot(eq_f.astype(bf16), tri_ref[...])))
    sels = []
    runeq = runeq_scr[:, 0:1]
    for bits, eq, eq_f, eq_before in tiles:
        sel = (bits > thr) | (eq & (runeq + eq_before < need))
        sel_f = jnp.where(sel, 1.0, 0.0)
        sels.append((sel, sel_f, _dot(sel_f.astype(bf16), tri_ref[...])))
        runeq = runeq + jnp.sum(eq_f, axis=1, keepdims=True)
    runeq_scr[...] = jnp.broadcast_to(runeq, (N_EXPERTS, LANES))
    base = runsel_scr[...]
    for u, (sel, sel_f, sel_before) in enumerate(sels):
        posm_ref[:, u * TT:(u + 1) * TT] = jnp.where(sel, base[:, 0:1] + sel_before, -1.0)
        base_ref[u] = base
        base = base + jnp.sum(sel_f, axis=1, keepdims=True)
    runsel_scr[...] = base


def _route(aff, row0, nrows, tri):
    cap = CAPACITY_FACTOR * nrows // N_EXPERTS
    nt = nrows // TT
    blk0 = row0 // nrows
    vec = pltpu.VMEM((N_EXPERTS, LANES), f32)
    return pl.pallas_call(
        functools.partial(_route_kernel, cap=cap), grid=(nt // ROUTE_TILES,),
        in_specs=[pl.BlockSpec((N_EXPERTS, nrows), lambda j: (0, blk0)),
                  pl.BlockSpec((TT, TT), lambda j: (0, 0))],
        out_specs=[pl.BlockSpec((N_EXPERTS, ROUTE_TILES * TT), lambda j: (0, j)),
                   pl.BlockSpec((ROUTE_TILES, N_EXPERTS, LANES), lambda j: (j, 0, 0))],
        out_shape=[jax.ShapeDtypeStruct((N_EXPERTS, nrows), f32),
                   jax.ShapeDtypeStruct((nt, N_EXPERTS, LANES), f32)],
        scratch_shapes=[pltpu.VMEM((N_EXPERTS, LANES), jnp.int32), vec, vec, vec],
        name="route",
        compiler_params=pltpu.CompilerParams(dimension_semantics=("arbitrary",)),
    )(aff, tri)


def _window_class(count, windows, align):
    longest = jnp.max(count, axis=1) + (align - 1)
    return sum((longest > w).astype(jnp.int32) for w in windows[:-1])


def _dispatch_kernel(big_ref, first_ref, posm_ref, h2_ref, xe_hbm, stage, sem, carry, *, cap, ntiles):
    j = pl.program_id(0)
    slot = j % 2

    @pl.when(j == 0)
    def _():
        carry[...] = jnp.zeros_like(carry)
        pad = DISP_WINDOWS[-1]
        stage[1, 0:pad, :] = jnp.zeros((pad, D_MODEL), bf16)
        pads = [pltpu.make_async_copy(stage.at[1, pl.ds(0, pad)], xe_hbm.at[e, pl.ds(cap, pad)], sem.at[1])
                for e in range(N_EXPERTS)]
        for p in pads:
            p.start()
        for p in pads:
            p.wait()

    def run_start(jj, e):
        return first_ref[jj * N_EXPERTS + e]

    def window_copy(jj, e, sl, win):
        a = pl.multiple_of(run_start(jj, e) // ROW_ALIGN * ROW_ALIGN, ROW_ALIGN)
        return pltpu.make_async_copy(stage.at[sl, pl.ds(e * win, win)], xe_hbm.at[e, pl.ds(a, win)], sem.at[sl])

    def for_tile_window(jj, fn):
        for flag, win in enumerate(DISP_WINDOWS):
            @pl.when(big_ref[jj] == flag)
            def _():
                fn(win)

    def wait_all(jj, sl):
        def go(win):
            for e in range(N_EXPERTS):
                window_copy(jj, e, sl, win).wait()
        for_tile_window(jj, go)

    def compact(win):
        row = lax.broadcasted_iota(jnp.int32, (win, TT), 0)
        starts = [run_start(j, e) // ROW_ALIGN * ROW_ALIGN for e in range(N_EXPERTS)]
        onehot = jnp.concatenate(
            [jnp.where((row + starts[e]).astype(f32) == posm_ref[e:e + 1, :], 1.0, 0.0)
             for e in range(N_EXPERTS)], axis=0).astype(bf16)
        rows = _dot(onehot, h2_ref[...])
        head = lax.broadcasted_iota(jnp.int32, (ROW_ALIGN, D_MODEL), 0)
        for e in range(N_EXPERTS):
            lead = run_start(j, e) - starts[e]
            w = rows[e * win:(e + 1) * win]
            top = w[:ROW_ALIGN] + jnp.where(head < lead, carry[e], 0.0)
            stage[slot, e * win:e * win + ROW_ALIGN, :] = top.astype(bf16)
            stage[slot, e * win + ROW_ALIGN:(e + 1) * win, :] = w[ROW_ALIGN:].astype(bf16)
        for e in range(N_EXPERTS):
            end = jnp.where(j + 1 < ntiles, run_start(jnp.minimum(j + 1, ntiles - 1), e), cap)
            off = jnp.minimum(end // ROW_ALIGN * ROW_ALIGN - starts[e], win - ROW_ALIGN)
            off = pl.multiple_of(off, ROW_ALIGN)
            carry[e] = stage[slot, pl.ds(e * win + off, ROW_ALIGN), :].astype(f32)

    for_tile_window(j, compact)

    @pl.when(j > 0)
    def _():
        wait_all(j - 1, 1 - slot)

    def start_all(win):
        for e in range(N_EXPERTS):
            window_copy(j, e, slot, win).start()
    for_tile_window(j, start_all)

    @pl.when(j == ntiles - 1)
    def _():
        wait_all(j, slot)


def _dispatch(h2, posm, first, count, row0, nrows):
    cap = CAPACITY_FACTOR * nrows // N_EXPERTS
    nt = nrows // TT
    blk0 = row0 // TT
    big = _window_class(count, DISP_WINDOWS, ROW_ALIGN)
    return pl.pallas_call(
        functools.partial(_dispatch_kernel, cap=cap, ntiles=nt),
        grid_spec=pltpu.PrefetchScalarGridSpec(
            num_scalar_prefetch=2, grid=(nt,),
            in_specs=[pl.BlockSpec((N_EXPERTS, TT), lambda j, *_: (0, j)),
                      pl.BlockSpec((TT, D_MODEL), lambda j, *_: (blk0 + j, 0))],
            out_specs=pl.BlockSpec(memory_space=pl.ANY),
            scratch_shapes=[pltpu.VMEM((2, N_EXPERTS * DISP_WINDOWS[-1], D_MODEL), bf16),
                            pltpu.SemaphoreType.DMA((2,)),
                            pltpu.VMEM((N_EXPERTS, ROW_ALIGN, D_MODEL), f32)],
        ),
        out_shape=jax.ShapeDtypeStruct((N_EXPERTS, cap + DISP_WINDOWS[-1], D_MODEL), bf16),
        name="dispatch",
        compiler_params=pltpu.CompilerParams(dimension_semantics=("arbitrary",), vmem_limit_bytes=VMEM_LIMIT),
    )(big, first.reshape(-1), posm, h2)


def _ffn_kernel(xe_ref, rw_ref, wg_ref, wu_ref, wd_ref, ye_ref):
    e = pl.program_id(0)
    x = xe_ref[...]
    logits = _dot(x, rw_ref[...])
    lane = lax.broadcasted_iota(jnp.int32, logits.shape, 1)
    logits = jnp.where(lane < N_EXPERTS, logits, NEG)
    ex = jnp.exp(logits - jnp.max(logits, axis=-1, keepdims=True))
    gate = jnp.sum(jnp.where(lane == e, ex, 0.0), axis=-1, keepdims=True) / jnp.sum(ex, axis=-1, keepdims=True)
    hg = _dot(x, wg_ref[...])
    hu = _dot(x, wu_ref[...])
    h = (hg * (1.0 / (1.0 + jnp.exp(-hg))) * hu).astype(bf16)
    ye_ref[...] = (_dot(h, wd_ref[...]) * gate).astype(bf16)


def _ffn(xe, lw, nrows):
    cap = CAPACITY_FACTOR * nrows // N_EXPERTS
    wspec = pl.BlockSpec((None, D_MODEL, D_EXPERT), lambda e, t: (e, 0, 0))
    return pl.pallas_call(
        _ffn_kernel, grid=(N_EXPERTS, cap // TME),
        in_specs=[
            pl.BlockSpec((None, TME, D_MODEL), lambda e, t: (e, t, 0)),
            pl.BlockSpec((D_MODEL, LANES), lambda e, t: (0, 0)),
            wspec, wspec,
            pl.BlockSpec((None, D_EXPERT, D_MODEL), lambda e, t: (e, 0, 0)),
        ],
        out_specs=pl.BlockSpec((None, TME, D_MODEL), lambda e, t: (e, t, 0)),
        out_shape=jax.ShapeDtypeStruct((N_EXPERTS, cap, D_MODEL), bf16),
        name="expert_ffn",
        compiler_params=pltpu.CompilerParams(dimension_semantics=("parallel", "arbitrary"),
                                             vmem_limit_bytes=VMEM_LIMIT),
    )(xe, lw["rw"], lw["e_wg"], lw["e_wu"], lw["e_wd"])


def _combine_kernel(big_ref, first_ref, posm_ref, x_ref, ye_hbm, o_ref, buf, sem, *, cap, ntiles):
    j = pl.program_id(0)

    def window_start(jj, e, win):
        return jnp.minimum(first_ref[jj * N_EXPERTS + e] // COMB_ALIGN * COMB_ALIGN, cap - win)

    def window_copy(jj, e, slot, win):
        a = pl.multiple_of(window_start(jj, e, win), COMB_ALIGN)
        return pltpu.make_async_copy(ye_hbm.at[e, pl.ds(a, win)], buf.at[slot, pl.ds(e * win, win)],
                                     sem.at[slot])

    def for_tile_window(jj, fn):
        for flag, win in enumerate(COMB_WINDOWS):
            @pl.when(big_ref[jj] == flag)
            def _():
                fn(win)

    def start_all(jj, slot):
        def go(win):
            for e in range(N_EXPERTS):
                window_copy(jj, e, slot, win).start()
        for_tile_window(jj, go)

    @pl.when(j == 0)
    def _():
        start_all(0, 0)

    slot = j % 2

    def wait_all(win):
        for e in range(N_EXPERTS):
            window_copy(j, e, slot, win).wait()
    for_tile_window(j, wait_all)

    @pl.when(j + 1 < ntiles)
    def _():
        start_all(j + 1, 1 - slot)

    def add_rows(win):
        row = lax.broadcasted_iota(jnp.int32, (win, TT), 0)
        onehot = jnp.concatenate(
            [jnp.where((row + window_start(j, e, win)).astype(f32) == posm_ref[e:e + 1, :], 1.0, 0.0)
             for e in range(N_EXPERTS)], axis=0).astype(bf16)
        o_ref[...] = x_ref[...] + _dot_tn(onehot, buf[slot, 0:N_EXPERTS * win, :])
    for_tile_window(j, add_rows)


def _combine(x, ye, posm, first, count, row0, nrows):
    cap = CAPACITY_FACTOR * nrows // N_EXPERTS
    nt = nrows // TT
    blk0 = row0 // TT
    big = _window_class(count, COMB_WINDOWS, COMB_ALIGN)
    return pl.pallas_call(
        functools.partial(_combine_kernel, cap=cap, ntiles=nt),
        grid_spec=pltpu.PrefetchScalarGridSpec(
            num_scalar_prefetch=2, grid=(nt,),
            in_specs=[pl.BlockSpec((N_EXPERTS, TT), lambda j, *_: (0, j)),
                      pl.BlockSpec((TT, D_MODEL), lambda j, *_: (blk0 + j, 0)),
                      pl.BlockSpec(memory_space=pl.ANY)],
            out_specs=pl.BlockSpec((TT, D_MODEL), lambda j, *_: (blk0 + j, 0)),
            scratch_shapes=[pltpu.VMEM((2, N_EXPERTS * COMB_WINDOWS[-1], D_MODEL), bf16),
                            pltpu.SemaphoreType.DMA((2,))],
        ),
        out_shape=jax.ShapeDtypeStruct(x.shape, f32),
        input_output_aliases={3: 0},
        name="combine",
        compiler_params=pltpu.CompilerParams(dimension_semantics=("arbitrary",), vmem_limit_bytes=VMEM_LIMIT),
    )(big, first.reshape(-1), posm, x, ye)


def _stack_kernel(a_hbm, b_hbm, o_hbm, sem):
    na = a_hbm.shape[0]
    copies = [pltpu.make_async_copy(a_hbm, o_hbm.at[pl.ds(0, na)], sem.at[0]),
              pltpu.make_async_copy(b_hbm, o_hbm.at[pl.ds(na, b_hbm.shape[0])], sem.at[1])]
    for c in copies:
        c.start()
    for c in copies:
        c.wait()


def _stack_rows(a, b):
    return pl.pallas_call(
        _stack_kernel,
        in_specs=[pl.BlockSpec(memory_space=pl.ANY), pl.BlockSpec(memory_space=pl.ANY)],
        out_specs=pl.BlockSpec(memory_space=pl.ANY),
        out_shape=jax.ShapeDtypeStruct((a.shape[0] + b.shape[0], a.shape[1]), a.dtype),
        scratch_shapes=[pltpu.SemaphoreType.DMA((2,))],
        name="stack_rows",
    )(a, b)


def _final_kernel(x_ref, g_ref, o_ref):
    o_ref[...] = _rms(x_ref[...], g_ref[...])


def _final_norm(x, g, row0, nrows):
    off = row0 // TM
    return pl.pallas_call(
        _final_kernel, grid=(nrows // TM,),
        in_specs=[pl.BlockSpec((TM, D_MODEL), lambda i: (i + off, 0)),
                  pl.BlockSpec((1, D_MODEL), lambda i: (0, 0))],
        out_specs=pl.BlockSpec((TM, D_MODEL), lambda i: (i, 0)),
        out_shape=jax.ShapeDtypeStruct((nrows, D_MODEL), f32),
        name="final_norm",
        compiler_params=pltpu.CompilerParams(dimension_semantics=("parallel",)),
    )(x, g)


def _natten_bias_table(rpb):
    c = np.arange(GRID_W)
    cs = np.clip(c - WIN_C // 2, 0, GRID_W - WIN_C)
    kc = np.arange(GRID_W)
    col_mask = (kc[None, :] >= cs[:, None]) & (kc[None, :] < cs[:, None] + WIN_C)
    dc = np.clip(kc[None, :] - c[:, None], -(WIN_C - 1), WIN_C - 1) + (WIN_C - 1)
    tb = jnp.where(col_mask[None, None], rpb[:, :, dc].astype(f32), NEG)
    dw = np.arange(WIN_R)[:, None] + np.arange(WIN_R)[None, :]
    t = tb[:, dw]
    t = t.transpose(1, 0, 3, 2, 4).reshape(WIN_R, C_HEADS * GRID_W, WIN_R * GRID_W)
    return t


def _rope_slabs():
    inv = 1.0 / (ROPE_THETA ** (jnp.arange(0, ROPE_DIM, 2, dtype=f32) / ROPE_DIM))
    ang = jnp.arange(SEQ, dtype=f32)[:, None] * inv[None, :]
    cos, sin = jnp.cos(ang), jnp.sin(ang)
    ones = jnp.ones((SEQ, NOPE_DIM), f32)
    zpad = jnp.zeros((SEQ, LANES - NOPE_DIM - ROPE_DIM), f32)
    cosq = jnp.concatenate([ones, cos, cos, zpad], axis=1)
    sinq = jnp.concatenate([0.0 * ones, sin, sin, zpad], axis=1)
    return cosq, sinq


def _pack_layer(i, norm1_g, w_in, a_vnorm_g, a_ws, a_bs, b_conv, c_rpb, d_qnorm_g, d_kvnorm_g,
                d_w_uq, d_w_ukv, out_norm_g, w_o, norm2_g, router_w, e_w_gate, e_w_up, e_w_down):
    half = ROPE_DIM // 2
    w = w_in[i]
    kr0 = OFF_D + Q_RANK + KV_RANK
    kr = w[:, kr0:kr0 + ROPE_DIM]
    z64 = jnp.zeros((D_MODEL, NOPE_DIM), f32)
    z32 = jnp.zeros((D_MODEL, LANES - NOPE_DIM - ROPE_DIM), f32)
    kr_slab = jnp.concatenate([z64, kr, z32], axis=1)
    kr_rot = jnp.concatenate([z64, -kr[:, half:], kr[:, :half], z32], axis=1)
    w_pack = jnp.concatenate([w[:, :kr0], kr_slab, kr_rot], axis=1).astype(bf16)

    uq = d_w_uq[i].reshape(Q_RANK, D_HEADS, NOPE_DIM + ROPE_DIM)
    nope, rope = uq[..., :NOPE_DIM], uq[..., NOPE_DIM:]
    zq = jnp.zeros((Q_RANK, D_HEADS, LANES - NOPE_DIM - ROPE_DIM), f32)
    w_uq = jnp.concatenate([nope, rope, zq], axis=-1).reshape(Q_RANK, D_HEADS * LANES)
    w_uq_rot = jnp.concatenate([0.0 * nope, -rope[..., half:], rope[..., :half], zq],
                               axis=-1).reshape(Q_RANK, D_HEADS * LANES)
    ukv = d_w_ukv[i].reshape(KV_RANK, D_HEADS, NOPE_DIM + V_DIM)
    w_k = jnp.concatenate([ukv[..., :NOPE_DIM], jnp.zeros((KV_RANK, D_HEADS, LANES - NOPE_DIM), f32)],
                          axis=-1).reshape(KV_RANK, D_HEADS * LANES)
    w_v = ukv[..., NOPE_DIM:].reshape(KV_RANK, D_HEADS * V_DIM)
    rw = jnp.concatenate([router_w[i], jnp.zeros((D_MODEL, LANES - N_EXPERTS), f32)], axis=1)
    return {
        "g1": norm1_g[i].reshape(1, D_MODEL),
        "w_in": w_pack,
        "a_vg": a_vnorm_g[i].reshape(1, GROUP_W),
        "a_ws": a_ws[i].astype(bf16),
        "a_bias": jnp.repeat(a_bs[i].T, HEAD_DIM, axis=1),
        "b_conv": b_conv[i],
        "c_bias": _natten_bias_table(c_rpb[i]),
        "d_qg": d_qnorm_g[i].reshape(1, Q_RANK),
        "d_kvg": d_kvnorm_g[i].reshape(1, KV_RANK),
        "w_uq": w_uq.astype(bf16),
        "w_uq_rot": w_uq_rot.astype(bf16),
        "w_k": w_k.astype(bf16),
        "w_v": w_v.astype(bf16),
        "og": out_norm_g[i].reshape(1, D_MODEL),
        "w_o": w_o[i].astype(bf16),
        "g2": norm2_g[i].reshape(1, D_MODEL),
        "rw": rw.astype(bf16),
        "rwt": router_w[i].T.astype(bf16),
        "e_wg": e_w_gate[i].astype(bf16),
        "e_wu": e_w_up[i].astype(bf16),
        "e_wd": e_w_down[i].astype(bf16),
    }


def _moe(xn, h2, aff, lw, tri):
    x = xn
    for row0, nrows in ((0, BATCH * SEQ), (BATCH * SEQ, DEC_BATCH * SEQ)):
        cap = CAPACITY_FACTOR * nrows // N_EXPERTS
        posm, base = _route(aff, row0, nrows, tri)
        first = base[:, :, 0].astype(jnp.int32)
        count = jnp.diff(first, axis=0, append=jnp.full((1, N_EXPERTS), cap, jnp.int32))
        xe = _dispatch(h2, posm, first, count, row0, nrows)
        ye = _ffn(xe, lw, nrows)
        x = _combine(x, ye, posm, first, count, row0, nrows)
    return x


def kernel(x_prompt, x_sample, norm1_g, w_in, a_vnorm_g, a_ws, a_bs, b_conv, c_rpb, d_qnorm_g, d_kvnorm_g, d_w_uq, d_w_ukv, out_norm_g, w_o, norm2_g, router_w, e_w_gate, e_w_up, e_w_down, final_norm_g):
    n_p = BATCH * SEQ
    n_s = DEC_BATCH * SEQ
    x = _stack_rows(x_prompt.reshape(n_p, D_MODEL), x_sample.reshape(n_s, D_MODEL))
    cosq, sinq = _rope_slabs()
    tri = jnp.asarray(np.triu(np.ones((TT, TT), np.float32), 1), bf16)
    for i in range(DEPTH):
        lw = _pack_layer(i, norm1_g, w_in, a_vnorm_g, a_ws, a_bs, b_conv, c_rpb, d_qnorm_g, d_kvnorm_g,
                         d_w_uq, d_w_ukv, out_norm_g, w_o, norm2_g, router_w, e_w_gate, e_w_up, e_w_down)
        ya, yb, zc, qt, k4, vt = _inproj(x, lw, cosq, sinq)
        yc = _natten(zc, lw["c_bias"])
        yd = _mla(qt, k4, vt)
        xn, h2, aff = _outproj(ya, yb, yc, yd, x, lw)
        x = _moe(xn, h2, aff, lw, tri)
    g = final_norm_g.reshape(1, D_MODEL)
    y_p = _final_norm(x, g, 0, n_p).reshape(BATCH, SEQ, D_MODEL)
    y_s = _final_norm(x, g, n_p, n_s).reshape(DEC_BATCH, SEQ, D_MODEL)
    return (y_p, y_s)
```

```python
import functools
import math

import numpy as np
import jax
import jax.numpy as jnp
from jax import lax
from jax.experimental import pallas as pl
from jax.experimental.pallas import tpu as pltpu

D_MODEL = 1024
BATCH = 4
SEQ = 8192
DEPTH = 4
DEC_BATCH = 2
GROUP_W = 256
HEAD_DIM = 64
A_HEADS = 4
CHUNK = 128
C_HEADS = 4
GRID_W = 64
WIN_R = 8
WIN_C = 16
D_HEADS = 4
Q_RANK = 256
KV_RANK = 128
NOPE_DIM = 64
ROPE_DIM = 32
V_DIM = 64
ROPE_THETA = 10000.0
N_EXPERTS = 16
D_EXPERT = 1024
CAPACITY_FACTOR = 2
EPS = 1e-6

NSEQ = BATCH + DEC_BATCH
T_ALL = NSEQ * SEQ
ROWS = SEQ // GRID_W
LANES = 128
NEG = -1e30

A_COLS = 2 * GROUP_W
B_COLS = 3 * GROUP_W
C_COLS = 3 * GROUP_W
OFF_B = A_COLS
OFF_C = OFF_B + B_COLS
OFF_D = OFF_C + C_COLS
D_PACK = Q_RANK + KV_RANK + 2 * LANES
N_IN_PACK = OFF_D + D_PACK

TM = 1024
TQ = 256
TK = 512
L_ROWS = 16
MLA_UNROLL = 4
MLA_LAG = 1
MLA_QSCALE = (NOPE_DIM + ROPE_DIM) ** -0.5 * math.log2(math.e)
ROWS_PER_STEP = 8
TME = 512
TT = 256
ROUTE_TILES = 4
ROW_ALIGN = 16
DISP_WINDOWS = (80, 144, TT + ROW_ALIGN)
COMB_ALIGN = ROW_ALIGN
COMB_WINDOWS = (80, 144, TT + COMB_ALIGN)
VMEM_LIMIT = 56 * 1024 * 1024

f32 = jnp.float32
bf16 = jnp.bfloat16


def _rms(x, g):
    return x * lax.rsqrt(jnp.mean(x * x, axis=-1, keepdims=True) + EPS) * g


def _dot(a, b):
    return jnp.dot(a, b, preferred_element_type=f32)


def _dot_nt(a, b):
    return lax.dot_general(a, b, (((1,), (1,)), ((), ())), preferred_element_type=f32)


def _dot_tn(a, b):
    return lax.dot_general(a, b, (((0,), (0,)), ((), ())), preferred_element_type=f32)


def _inproj_kernel(x_ref, xp_ref, xn_ref, g1_ref, win_ref, avg_ref, aws_ref, abias_ref,
                   bconv_ref, qg_ref, kvg_ref, wuq_ref, wuqr_ref, wk_ref, wv_ref,
                   cos_ref, sin_ref,
                   ya_ref, yb_ref, zc_ref, qt_ref, k4_ref, vt_ref):
    i = pl.program_id(0)
    j = i % (SEQ // TM)
    g1 = g1_ref[...]
    h = _rms(x_ref[...], g1).astype(bf16)

    za = jax.nn.gelu(_dot(h, win_ref[:, 0:A_COLS]))
    u = za[:, :GROUP_W]
    v = _rms(za[:, GROUP_W:], avg_ref[...]).astype(bf16)
    lane = lax.broadcasted_iota(jnp.int32, (CHUNK, GROUP_W), 1)
    for c in range(TM // CHUNK):
        vc = v[c * CHUNK:(c + 1) * CHUNK, :]
        sv = abias_ref[...]
        for hd in range(A_HEADS):
            r = _dot(aws_ref[hd], vc)
            sv = sv + jnp.where((lane >= hd * HEAD_DIM) & (lane < (hd + 1) * HEAD_DIM), r, 0.0)
        ya_ref[c * CHUNK:(c + 1) * CHUNK, :] = u[c * CHUNK:(c + 1) * CHUNK, :] * sv

    zb = _dot(h, win_ref[:, OFF_B:OFF_B + B_COLS])
    bg = zb[:, :GROUP_W]
    y = zb[:, GROUP_W:2 * GROUP_W] * zb[:, 2 * GROUP_W:]
    xh = jnp.concatenate([xp_ref[...], xn_ref[...]], axis=0)
    hh = _rms(xh, g1).astype(bf16)
    zh = _dot(hh, win_ref[:, OFF_B + GROUP_W:OFF_B + B_COLS])
    yh = zh[:, :GROUP_W] * zh[:, GROUP_W:]
    y_before = jnp.where(j == 0, 0.0, yh[7:8, :])
    y_after = jnp.where(j == SEQ // TM - 1, 0.0, yh[8:9, :])
    row = lax.broadcasted_iota(jnp.int32, (TM, GROUP_W), 0)
    y_m1 = jnp.where(row == 0, y_before, pltpu.roll(y, 1, axis=0))
    y_p1 = jnp.where(row == TM - 1, y_after, pltpu.roll(y, TM - 1, axis=0))
    wc = bconv_ref[...]
    yb_ref[...] = bg * (wc[0:1, :] * y_m1 + wc[1:2, :] * y + wc[2:3, :] * y_p1)

    zc = _dot(h, win_ref[:, OFF_C:OFF_C + C_COLS])
    zc_ref[:, 0:GROUP_W] = (zc[:, 0:GROUP_W] * (HEAD_DIM ** -0.5)).astype(bf16)
    zc_ref[:, GROUP_W:] = zc[:, GROUP_W:].astype(bf16)

    zd = _dot(h, win_ref[:, OFF_D:OFF_D + D_PACK])
    cqn = _rms(zd[:, 0:Q_RANK], qg_ref[...]).astype(bf16)
    ckvn = _rms(zd[:, Q_RANK:Q_RANK + KV_RANK], kvg_ref[...]).astype(bf16)
    cos = cos_ref[...]
    sin = sin_ref[...]
    o = Q_RANK + KV_RANK
    kr = zd[:, o:o + LANES] * cos + zd[:, o + LANES:o + 2 * LANES] * sin
    q_pre = _dot(cqn, wuq_ref[...])
    q_rot = _dot(cqn, wuqr_ref[...])
    kn = _dot(ckvn, wk_ref[...])
    for hd in range(D_HEADS):
        sl = slice(hd * LANES, (hd + 1) * LANES)
        qt_ref[hd] = ((q_pre[:, sl] * cos + q_rot[:, sl] * sin) * MLA_QSCALE).T.astype(bf16)
        k4_ref[hd] = (kn[:, sl] + kr).astype(bf16)
    vt = _dot(ckvn, wv_ref[...]).T.astype(bf16)
    for hd in range(D_HEADS):
        vt_ref[hd] = vt[hd * V_DIM:(hd + 1) * V_DIM, :]


def _inproj(x, lw, cosq, sinq):
    nt = T_ALL // TM
    tps = SEQ // TM
    full = lambda shape: pl.BlockSpec(shape, lambda i: (0,) * len(shape))
    in_specs = [
        pl.BlockSpec((TM, D_MODEL), lambda i: (i, 0)),
        pl.BlockSpec((8, D_MODEL), lambda i: (jnp.maximum(i * (TM // 8) - 1, 0), 0)),
        pl.BlockSpec((8, D_MODEL), lambda i: (jnp.minimum((i + 1) * (TM // 8), T_ALL // 8 - 1), 0)),
        full((1, D_MODEL)),
        full((D_MODEL, N_IN_PACK)),
        full((1, GROUP_W)),
        full((A_HEADS, CHUNK, CHUNK)),
        full((CHUNK, GROUP_W)),
        full((3, GROUP_W)),
        full((1, Q_RANK)),
        full((1, KV_RANK)),
        full((Q_RANK, D_HEADS * LANES)),
        full((Q_RANK, D_HEADS * LANES)),
        full((KV_RANK, D_HEADS * LANES)),
        full((KV_RANK, D_HEADS * V_DIM)),
        pl.BlockSpec((TM, LANES), lambda i: (i % tps, 0)),
        pl.BlockSpec((TM, LANES), lambda i: (i % tps, 0)),
    ]
    out_specs = [
        pl.BlockSpec((TM, GROUP_W), lambda i: (i, 0)),
        pl.BlockSpec((TM, GROUP_W), lambda i: (i, 0)),
        pl.BlockSpec((TM, C_COLS), lambda i: (i, 0)),
        pl.BlockSpec((None, D_HEADS, LANES, TM), lambda i: (i // tps, 0, 0, i % tps)),
        pl.BlockSpec((None, D_HEADS, TM, LANES), lambda i: (i // tps, 0, i % tps, 0)),
        pl.BlockSpec((None, D_HEADS, V_DIM, TM), lambda i: (i // tps, 0, 0, i % tps)),
    ]
    out_shape = [
        jax.ShapeDtypeStruct((T_ALL, GROUP_W), f32),
        jax.ShapeDtypeStruct((T_ALL, GROUP_W), f32),
        jax.ShapeDtypeStruct((T_ALL, C_COLS), bf16),
        jax.ShapeDtypeStruct((NSEQ, D_HEADS, LANES, SEQ), bf16),
        jax.ShapeDtypeStruct((NSEQ, D_HEADS, SEQ, LANES), bf16),
        jax.ShapeDtypeStruct((NSEQ, D_HEADS, V_DIM, SEQ), bf16),
    ]
    return pl.pallas_call(
        _inproj_kernel, grid=(nt,), in_specs=in_specs, out_specs=out_specs, out_shape=out_shape,
        name="inproj",
        compiler_params=pltpu.CompilerParams(dimension_semantics=("parallel",),
                                             vmem_limit_bytes=VMEM_LIMIT),
    )(x, x, x, lw["g1"], lw["w_in"], lw["a_vg"], lw["a_ws"], lw["a_bias"], lw["b_conv"],
      lw["d_qg"], lw["d_kvg"], lw["w_uq"], lw["w_uq_rot"], lw["w_k"], lw["w_v"], cosq, sinq)


def _natten_kernel(q_ref, k_ref, v_ref, bias_ref, o_ref):
    jb = pl.program_id(1)
    lane = lax.broadcasted_iota(jnp.int32, (GRID_W, GROUP_W), 1)
    head_of_lane = lane // HEAD_DIM
    nkeys = WIN_R * GRID_W

    jobs = []
    for i in range(ROWS_PER_STEP):
        r = jb * ROWS_PER_STEP + i
        rs = jnp.clip(r - WIN_R // 2, 0, ROWS - WIN_R)
        d = rs - r + (WIN_R - 1)
        kstart = pl.multiple_of(rs * GRID_W, GRID_W)
        q = q_ref[i * GRID_W:(i + 1) * GRID_W, :]
        zero = jnp.zeros_like(q)
        qq = jnp.concatenate([jnp.where(head_of_lane == h, q, zero) for h in range(C_HEADS)], axis=0)
        jobs.append((_dot_nt(qq, k_ref[pl.ds(kstart, nkeys), :]), d, kstart))
    for i, (sc, d, kstart) in enumerate(jobs):
        sc = sc + bias_ref[d]
        m = jnp.max(sc, axis=-1, keepdims=True)
        p = jnp.exp(sc - m)
        l = jnp.sum(p, axis=-1, keepdims=True)
        o = _dot(p.astype(bf16), v_ref[pl.ds(kstart, nkeys), :]) / l
        out = jnp.zeros((GRID_W, GROUP_W), f32)
        for h in range(C_HEADS):
            out = jnp.where(head_of_lane == h, o[h * GRID_W:(h + 1) * GRID_W], out)
        o_ref[i * GRID_W:(i + 1) * GRID_W, :] = out


def _natten(zc, bias):
    nb = ROWS // ROWS_PER_STEP
    tq = ROWS_PER_STEP * GRID_W
    return pl.pallas_call(
        _natten_kernel, grid=(NSEQ, nb),
        in_specs=[
            pl.BlockSpec((tq, GROUP_W), lambda b, j: (b * nb + j, 0)),
            pl.BlockSpec((SEQ, GROUP_W), lambda b, j: (b, 1)),
            pl.BlockSpec((SEQ, GROUP_W), lambda b, j: (b, 2)),
            pl.BlockSpec((WIN_R, C_HEADS * GRID_W, WIN_R * GRID_W), lambda b, j: (0, 0, 0)),
        ],
        out_specs=pl.BlockSpec((tq, GROUP_W), lambda b, j: (b * nb + j, 0)),
        out_shape=jax.ShapeDtypeStruct((T_ALL, GROUP_W), f32),
        name="natten",
        compiler_params=pltpu.CompilerParams(dimension_semantics=("parallel", "arbitrary"),
                                             vmem_limit_bytes=VMEM_LIMIT),
    )(zc, zc, zc, bias)


def _mla_kernel(qt_ref, k_ref, vt_ref, o_ref, s_scr):
    nchunk = SEQ // TK
    heads = list(range(D_HEADS))
    qts = [qt_ref[h] for h in heads]

    def qk(slot, h, ci):
        ks = pl.multiple_of(ci * TK, TK)
        s_scr[slot, h] = _dot(k_ref[h, pl.ds(ks, TK), :], qts[h])

    ones = jnp.ones((L_ROWS, TK), bf16)

    def softmax_pv(slot, h, ci, stat):
        m, acc = stat
        ks = pl.multiple_of(ci * TK, TK)
        vt = jnp.concatenate([vt_ref[h, :, pl.ds(ks, TK)], ones], axis=0)
        s = s_scr[slot, h]
        m_new = jnp.maximum(m, jnp.max(s, axis=0, keepdims=True))
        alpha = jnp.exp2(m - m_new)
        p = jnp.exp2(s - m_new)
        acc = acc * alpha + _dot(vt, p.astype(bf16))
        return m_new, acc

    def half(cur, ci, stats, prefetch=True):
        out = []
        for h in heads:
            if prefetch:
                qk(1 - cur, h, ci + 1)
            if h >= MLA_LAG:
                out.append(softmax_pv(cur, h - MLA_LAG, ci, stats[h - MLA_LAG]))
        for h in range(D_HEADS - MLA_LAG, D_HEADS):
            out.append(softmax_pv(cur, h, ci, stats[h]))
        return tuple(out)

    def body(cp, stats):
        for u in range(MLA_UNROLL):
            stats = half(u % 2, MLA_UNROLL * cp + u, stats)
        return stats

    for h in heads:
        qk(0, h, 0)
    init = tuple((jnp.full((1, TQ), NEG, f32), jnp.zeros((V_DIM + L_ROWS, TQ), f32)) for _ in heads)
    res = lax.fori_loop(0, nchunk // MLA_UNROLL - 1, body, init)
    for u in range(MLA_UNROLL):
        res = half(u % 2, nchunk - MLA_UNROLL + u, res, prefetch=u < MLA_UNROLL - 1)
    o_ref[...] = jnp.concatenate([acc[:V_DIM] / acc[V_DIM:V_DIM + 1] for (_, acc) in res], axis=0).T


def _mla(qt, k4, vt):
    nq = SEQ // TQ
    return pl.pallas_call(
        _mla_kernel, grid=(NSEQ, nq),
        in_specs=[
            pl.BlockSpec((None, D_HEADS, LANES, TQ), lambda b, i: (b, 0, 0, i)),
            pl.BlockSpec((None, D_HEADS, SEQ, LANES), lambda b, i: (b, 0, 0, 0)),
            pl.BlockSpec((None, D_HEADS, V_DIM, SEQ), lambda b, i: (b, 0, 0, 0)),
        ],
        out_specs=pl.BlockSpec((TQ, GROUP_W), lambda b, i: (b * nq + i, 0)),
        out_shape=jax.ShapeDtypeStruct((T_ALL, GROUP_W), f32),
        scratch_shapes=[pltpu.VMEM((2, D_HEADS, TK, TQ), f32)],
        name="mla",
        compiler_params=pltpu.CompilerParams(
            dimension_semantics=("parallel", "arbitrary"),
            vmem_limit_bytes=VMEM_LIMIT),
    )(qt, k4, vt)


def _outproj_kernel(ya_ref, yb_ref, yc_ref, yd_ref, x_ref, og_ref, wo_ref, g2_ref, rwt_ref,
                    xn_ref, h2_ref, aff_ref):
    y = jnp.concatenate(
        [_rms(y_ref[...], og_ref[:, g * GROUP_W:(g + 1) * GROUP_W]).astype(bf16)
         for g, y_ref in enumerate((ya_ref, yb_ref, yc_ref, yd_ref))], axis=1)
    acc = x_ref[...] + _dot(y, wo_ref[...])
    xn_ref[...] = acc
    h2 = _rms(acc, g2_ref[...]).astype(bf16)
    h2_ref[...] = h2
    logits = _dot_nt(rwt_ref[...], h2)
    e = jnp.exp(logits - jnp.max(logits, axis=0, keepdims=True))
    aff_ref[...] = e / jnp.sum(e, axis=0, keepdims=True)


def _outproj(ya, yb, yc, yd, x, lw):
    nt = T_ALL // TM
    full = lambda shape: pl.BlockSpec(shape, lambda i: (0,) * len(shape))
    yspec = pl.BlockSpec((TM, GROUP_W), lambda i: (i, 0))
    xspec = pl.BlockSpec((TM, D_MODEL), lambda i: (i, 0))
    return pl.pallas_call(
        _outproj_kernel, grid=(nt,),
        in_specs=[yspec, yspec, yspec, yspec, xspec, full((1, D_MODEL)), full((D_MODEL, D_MODEL)),
                  full((1, D_MODEL)), full((N_EXPERTS, D_MODEL))],
        out_specs=[xspec, xspec, pl.BlockSpec((N_EXPERTS, TM), lambda i: (0, i))],
        out_shape=[jax.ShapeDtypeStruct((T_ALL, D_MODEL), f32),
                   jax.ShapeDtypeStruct((T_ALL, D_MODEL), bf16),
                   jax.ShapeDtypeStruct((N_EXPERTS, T_ALL), f32)],
        name="outproj",
        compiler_params=pltpu.CompilerParams(dimension_semantics=("parallel",),
                                             vmem_limit_bytes=VMEM_LIMIT),
    )(ya, yb, yc, yd, x, lw["og"], lw["w_o"], lw["g2"], lw["rwt"])


def _route_kernel(aff_ref, tri_ref, posm_ref, base_ref, thr_scr, need_scr, runeq_scr, runsel_scr, *, cap):
    j = pl.program_id(0)

    @pl.when(j == 0)
    def _():
        def refine(i, thr):
            cand = thr | jnp.left_shift(jnp.int32(1), 30 - i)
            bits = pltpu.bitcast(aff_ref[...], jnp.int32)
            cnt = jnp.sum(jnp.where(bits >= cand, 1.0, 0.0), axis=1, keepdims=True)
            return jnp.where(cnt >= cap, cand, thr)

        thr = lax.fori_loop(0, 31, refine, jnp.zeros((N_EXPERTS, 1), jnp.int32))
        bits = pltpu.bitcast(aff_ref[...], jnp.int32)
        above = jnp.sum(jnp.where(bits > thr, 1.0, 0.0), axis=1, keepdims=True)
        thr_scr[...] = jnp.broadcast_to(thr, (N_EXPERTS, LANES))
        need_scr[...] = jnp.broadcast_to(cap - above, (N_EXPERTS, LANES))
        runeq_scr[...] = jnp.zeros((N_EXPERTS, LANES), f32)
        runsel_scr[...] = jnp.zeros((N_EXPERTS, LANES), f32)

    thr = thr_scr[:, 0:1]
    need = need_scr[:, 0:1]
    tiles = []
    for u in range(ROUTE_TILES):
        t0 = pl.multiple_of((j * ROUTE_TILES + u) * TT, TT)
        bits = pltpu.bitcast(aff_ref[:, pl.ds(t0, TT)], jnp.int32)
        eq = bits == thr
        eq_f = jnp.where(eq, 1.0, 0.0)
        tiles.append((bits, eq, eq_f, _dot(eq_f.astype(bf16), tri_ref[...])))
    sels = []
    runeq = runeq_scr[:, 0:1]
    for bits, eq, eq_f, eq_before in tiles:
        sel = (bits > thr) | (eq & (runeq + eq_before < need))
        sel_f = jnp.where(sel, 1.0, 0.0)
        sels.append((sel, sel_f, _dot(sel_f.astype(bf16), tri_ref[...])))
        runeq = runeq + jnp.sum(eq_f, axis=1, keepdims=True)
    runeq_scr[...] = jnp.broadcast_to(runeq, (N_EXPERTS, LANES))
    base = runsel_scr[...]
    for u, (sel, sel_f, sel_before) in enumerate(sels):
        posm_ref[:, u * TT:(u + 1) * TT] = jnp.where(sel, base[:, 0:1] + sel_before, -1.0)
        base_ref[u] = base
        base = base + jnp.sum(sel_f, axis=1, keepdims=True)
    runsel_scr[...] = base


def _route(aff, row0, nrows, tri):
    cap = CAPACITY_FACTOR * nrows // N_EXPERTS
    nt = nrows // TT
    blk0 = row0 // nrows
    vec = pltpu.VMEM((N_EXPERTS, LANES), f32)
    return pl.pallas_call(
        functools.partial(_route_kernel, cap=cap), grid=(nt // ROUTE_TILES,),
        in_specs=[pl.BlockSpec((N_EXPERTS, nrows), lambda j: (0, blk0)),
                  pl.BlockSpec((TT, TT), lambda j: (0, 0))],
        out_specs=[pl.BlockSpec((N_EXPERTS, ROUTE_TILES * TT), lambda j: (0, j)),
                   pl.BlockSpec((ROUTE_TILES, N_EXPERTS, LANES), lambda j: (j, 0, 0))],
        out_shape=[jax.ShapeDtypeStruct((N_EXPERTS, nrows), f32),
                   jax.ShapeDtypeStruct((nt, N_EXPERTS, LANES), f32)],
        scratch_shapes=[pltpu.VMEM((N_EXPERTS, LANES), jnp.int32), vec, vec, vec],
        name="route",
        compiler_params=pltpu.CompilerParams(dimension_semantics=("arbitrary",)),
    )(aff, tri)


def _window_class(count, windows, align):
    longest = jnp.max(count, axis=1) + (align - 1)
    return sum((longest > w).astype(jnp.int32) for w in windows[:-1])


def _dispatch_kernel(big_ref, first_ref, posm_ref, h2_ref, xe_hbm, stage, sem, carry, *, cap, ntiles):
    j = pl.program_id(0)
    slot = j % 2

    @pl.when(j == 0)
    def _():
        carry[...] = jnp.zeros_like(carry)
        pad = DISP_WINDOWS[-1]
        stage[1, 0:pad, :] = jnp.zeros((pad, D_MODEL), bf16)
        pads = [pltpu.make_async_copy(stage.at[1, pl.ds(0, pad)], xe_hbm.at[e, pl.ds(cap, pad)], sem.at[1])
                for e in range(N_EXPERTS)]
        for p in pads:
            p.start()
        for p in pads:
            p.wait()

    def run_start(jj, e):
        return first_ref[jj * N_EXPERTS + e]

    def window_copy(jj, e, sl, win):
        a = pl.multiple_of(run_start(jj, e) // ROW_ALIGN * ROW_ALIGN, ROW_ALIGN)
        return pltpu.make_async_copy(stage.at[sl, pl.ds(e * win, win)], xe_hbm.at[e, pl.ds(a, win)], sem.at[sl])

    def for_tile_window(jj, fn):
        for flag, win in enumerate(DISP_WINDOWS):
            @pl.when(big_ref[jj] == flag)
            def _():
                fn(win)

    def wait_all(jj, sl):
        def go(win):
            for e in range(N_EXPERTS):
                window_copy(jj, e, sl, win).wait()
        for_tile_window(jj, go)

    def compact(win):
        row = lax.broadcasted_iota(jnp.int32, (win, TT), 0)
        starts = [run_start(j, e) // ROW_ALIGN * ROW_ALIGN for e in range(N_EXPERTS)]
        onehot = jnp.concatenate(
            [jnp.where((row + starts[e]).astype(f32) == posm_ref[e:e + 1, :], 1.0, 0.0)
             for e in range(N_EXPERTS)], axis=0).astype(bf16)
        rows = _dot(onehot, h2_ref[...])
        head = lax.broadcasted_iota(jnp.int32, (ROW_ALIGN, D_MODEL), 0)
        for e in range(N_EXPERTS):
            lead = run_start(j, e) - starts[e]
            w = rows[e * win:(e + 1) * win]
            top = w[:ROW_ALIGN] + jnp.where(head < lead, carry[e], 0.0)
            stage[slot, e * win:e * win + ROW_ALIGN, :] = top.astype(bf16)
            stage[slot, e * win + ROW_ALIGN:(e + 1) * win, :] = w[ROW_ALIGN:].astype(bf16)
        for e in range(N_EXPERTS):
            end = jnp.where(j + 1 < ntiles, run_start(jnp.minimum(j + 1, ntiles - 1), e), cap)
            off = jnp.minimum(end // ROW_ALIGN * ROW_ALIGN - starts[e], win - ROW_ALIGN)
            off = pl.multiple_of(off, ROW_ALIGN)
            carry[e] = stage[slot, pl.ds(e * win + off, ROW_ALIGN), :].astype(f32)

    for_tile_window(j, compact)

    @pl.when(j > 0)
    def _():
        wait_all(j - 1, 1 - slot)

    def start_all(win):
        for e in range(N_EXPERTS):
            window_copy(j, e, slot, win).start()
    for_tile_window(j, start_all)

    @pl.when(j == ntiles - 1)
    def _():
        wait_all(j, slot)


def _dispatch(h2, posm, first, count, row0, nrows):
    cap = CAPACITY_FACTOR * nrows // N_EXPERTS
    nt = nrows // TT
    blk0 = row0 // TT
    big = _window_class(count, DISP_WINDOWS, ROW_ALIGN)
    return pl.pallas_call(
        functools.partial(_dispatch_kernel, cap=cap, ntiles=nt),
        grid_spec=pltpu.PrefetchScalarGridSpec(
            num_scalar_prefetch=2, grid=(nt,),
            in_specs=[pl.BlockSpec((N_EXPERTS, TT), lambda j, *_: (0, j)),
                      pl.BlockSpec((TT, D_MODEL), lambda j, *_: (blk0 + j, 0))],
            out_specs=pl.BlockSpec(memory_space=pl.ANY),
            scratch_shapes=[pltpu.VMEM((2, N_EXPERTS * DISP_WINDOWS[-1], D_MODEL), bf16),
                            pltpu.SemaphoreType.DMA((2,)),
                            pltpu.VMEM((N_EXPERTS, ROW_ALIGN, D_MODEL), f32)],
        ),
        out_shape=jax.ShapeDtypeStruct((N_EXPERTS, cap + DISP_WINDOWS[-1], D_MODEL), bf16),
        name="dispatch",
        compiler_params=pltpu.CompilerParams(dimension_semantics=("arbitrary",), vmem_limit_bytes=VMEM_LIMIT),
    )(big, first.reshape(-1), posm, h2)


def _ffn_kernel(xe_ref, rw_ref, wg_ref, wu_ref, wd_ref, ye_ref):
    e = pl.program_id(0)
    x = xe_ref[...]
    logits = _dot(x, rw_ref[...])
    lane = lax.broadcasted_iota(jnp.int32, logits.shape, 1)
    logits = jnp.where(lane < N_EXPERTS, logits, NEG)
    ex = jnp.exp(logits - jnp.max(logits, axis=-1, keepdims=True))
    gate = jnp.sum(jnp.where(lane == e, ex, 0.0), axis=-1, keepdims=True) / jnp.sum(ex, axis=-1, keepdims=True)
    hg = _dot(x, wg_ref[...])
    hu = _dot(x, wu_ref[...])
    h = (hg * (1.0 / (1.0 + jnp.exp(-hg))) * hu).astype(bf16)
    ye_ref[...] = (_dot(h, wd_ref[...]) * gate).astype(bf16)


def _ffn(xe, lw, nrows):
    cap = CAPACITY_FACTOR * nrows // N_EXPERTS
    wspec = pl.BlockSpec((None, D_MODEL, D_EXPERT), lambda e, t: (e, 0, 0))
    return pl.pallas_call(
        _ffn_kernel, grid=(N_EXPERTS, cap // TME),
        in_specs=[
            pl.BlockSpec((None, TME, D_MODEL), lambda e, t: (e, t, 0)),
            pl.BlockSpec((D_MODEL, LANES), lambda e, t: (0, 0)),
            wspec, wspec,
            pl.BlockSpec((None, D_EXPERT, D_MODEL), lambda e, t: (e, 0, 0)),
        ],
        out_specs=pl.BlockSpec((None, TME, D_MODEL), lambda e, t: (e, t, 0)),
        out_shape=jax.ShapeDtypeStruct((N_EXPERTS, cap, D_MODEL), bf16),
        name="expert_ffn",
        compiler_params=pltpu.CompilerParams(dimension_semantics=("parallel", "arbitrary"),
                                             vmem_limit_bytes=VMEM_LIMIT),
    )(xe, lw["rw"], lw["e_wg"], lw["e_wu"], lw["e_wd"])


def _combine_kernel(big_ref, first_ref, posm_ref, x_ref, ye_hbm, o_ref, buf, sem, *, cap, ntiles):
    j = pl.program_id(0)

    def window_start(jj, e, win):
        return jnp.minimum(first_ref[jj * N_EXPERTS + e] // COMB_ALIGN * COMB_ALIGN, cap - win)

    def window_copy(jj, e, slot, win):
        a = pl.multiple_of(window_start(jj, e, win), COMB_ALIGN)
        return pltpu.make_async_copy(ye_hbm.at[e, pl.ds(a, win)], buf.at[slot, pl.ds(e * win, win)],
                                     sem.at[slot])

    def for_tile_window(jj, fn):
        for flag, win in enumerate(COMB_WINDOWS):
            @pl.when(big_ref[jj] == flag)
            def _():
                fn(win)

    def start_all(jj, slot):
        def go(win):
            for e in range(N_EXPERTS):
                window_copy(jj, e, slot, win).start()
        for_tile_window(jj, go)

    @pl.when(j == 0)
    def _():
        start_all(0, 0)

    slot = j % 2

    def wait_all(win):
        for e in range(N_EXPERTS):
            window_copy(j, e, slot, win).wait()
    for_tile_window(j, wait_all)

    @pl.when(j + 1 < ntiles)
    def _():
        start_all(j + 1, 1 - slot)

    def add_rows(win):
        row = lax.broadcasted_iota(jnp.int32, (win, TT), 0)
        onehot = jnp.concatenate(
            [jnp.where((row + window_start(j, e, win)).astype(f32) == posm_ref[e:e + 1, :], 1.0, 0.0)
             for e in range(N_EXPERTS)], axis=0).astype(bf16)
        o_ref[...] = x_ref[...] + _dot_tn(onehot, buf[slot, 0:N_EXPERTS * win, :])
    for_tile_window(j, add_rows)


def _combine(x, ye, posm, first, count, row0, nrows):
    cap = CAPACITY_FACTOR * nrows // N_EXPERTS
    nt = nrows // TT
    blk0 = row0 // TT
    big = _window_class(count, COMB_WINDOWS, COMB_ALIGN)
    return pl.pallas_call(
        functools.partial(_combine_kernel, cap=cap, ntiles=nt),
        grid_spec=pltpu.PrefetchScalarGridSpec(
            num_scalar_prefetch=2, grid=(nt,),
            in_specs=[pl.BlockSpec((N_EXPERTS, TT), lambda j, *_: (0, j)),
                      pl.BlockSpec((TT, D_MODEL), lambda j, *_: (blk0 + j, 0)),
                      pl.BlockSpec(memory_space=pl.ANY)],
            out_specs=pl.BlockSpec((TT, D_MODEL), lambda j, *_: (blk0 + j, 0)),
            scratch_shapes=[pltpu.VMEM((2, N_EXPERTS * COMB_WINDOWS[-1], D_MODEL), bf16),
                            pltpu.SemaphoreType.DMA((2,))],
        ),
        out_shape=jax.ShapeDtypeStruct(x.shape, f32),
        input_output_aliases={3: 0},
        name="combine",
        compiler_params=pltpu.CompilerParams(dimension_semantics=("arbitrary",), vmem_limit_bytes=VMEM_LIMIT),
    )(big, first.reshape(-1), posm, x, ye)


def _stack_kernel(a_ref, b_ref, o_ref, *, na_blocks):
    o_ref[...] = jnp.where(pl.program_id(0) < na_blocks, a_ref[...], b_ref[...])


def _stack_rows(a, b):
    na, nb = a.shape[0] // TM, b.shape[0] // TM
    return pl.pallas_call(
        functools.partial(_stack_kernel, na_blocks=na), grid=(na + nb,),
        in_specs=[pl.BlockSpec((TM, D_MODEL), lambda i: (jnp.minimum(i, na - 1), 0)),
                  pl.BlockSpec((TM, D_MODEL), lambda i: (jnp.maximum(i - na, 0), 0))],
        out_specs=pl.BlockSpec((TM, D_MODEL), lambda i: (i, 0)),
        out_shape=jax.ShapeDtypeStruct((a.shape[0] + b.shape[0], D_MODEL), a.dtype),
        name="stack_rows",
        compiler_params=pltpu.CompilerParams(dimension_semantics=("arbitrary",)),
    )(a, b)


def _final_kernel(x_ref, g_ref, o_ref):
    o_ref[...] = _rms(x_ref[...], g_ref[...])


def _final_norm(x, g, row0, nrows):
    off = row0 // TM
    return pl.pallas_call(
        _final_kernel, grid=(nrows // TM,),
        in_specs=[pl.BlockSpec((TM, D_MODEL), lambda i: (i + off, 0)),
                  pl.BlockSpec((1, D_MODEL), lambda i: (0, 0))],
        out_specs=pl.BlockSpec((TM, D_MODEL), lambda i: (i, 0)),
        out_shape=jax.ShapeDtypeStruct((nrows, D_MODEL), f32),
        name="final_norm",
        compiler_params=pltpu.CompilerParams(dimension_semantics=("parallel",)),
    )(x, g)


def _natten_bias_table(rpb):
    c = np.arange(GRID_W)
    cs = np.clip(c - WIN_C // 2, 0, GRID_W - WIN_C)
    kc = np.arange(GRID_W)
    col_mask = (kc[None, :] >= cs[:, None]) & (kc[None, :] < cs[:, None] + WIN_C)
    dc = np.clip(kc[None, :] - c[:, None], -(WIN_C - 1), WIN_C - 1) + (WIN_C - 1)
    tb = jnp.where(col_mask[None, None], rpb[:, :, dc].astype(f32), NEG)
    dw = np.arange(WIN_R)[:, None] + np.arange(WIN_R)[None, :]
    t = tb[:, dw]
    t = t.transpose(1, 0, 3, 2, 4).reshape(WIN_R, C_HEADS * GRID_W, WIN_R * GRID_W)
    return t


def _rope_slabs():
    inv = 1.0 / (ROPE_THETA ** (jnp.arange(0, ROPE_DIM, 2, dtype=f32) / ROPE_DIM))
    ang = jnp.arange(SEQ, dtype=f32)[:, None] * inv[None, :]
    cos, sin = jnp.cos(ang), jnp.sin(ang)
    ones = jnp.ones((SEQ, NOPE_DIM), f32)
    zpad = jnp.zeros((SEQ, LANES - NOPE_DIM - ROPE_DIM), f32)
    cosq = jnp.concatenate([ones, cos, cos, zpad], axis=1)
    sinq = jnp.concatenate([0.0 * ones, sin, sin, zpad], axis=1)
    return cosq, sinq


def _pack_layer(i, norm1_g, w_in, a_vnorm_g, a_ws, a_bs, b_conv, c_rpb, d_qnorm_g, d_kvnorm_g,
                d_w_uq, d_w_ukv, out_norm_g, w_o, norm2_g, router_w, e_w_gate, e_w_up, e_w_down):
    half = ROPE_DIM // 2
    w = w_in[i]
    kr0 = OFF_D + Q_RANK + KV_RANK
    kr = w[:, kr0:kr0 + ROPE_DIM]
    z64 = jnp.zeros((D_MODEL, NOPE_DIM), f32)
    z32 = jnp.zeros((D_MODEL, LANES - NOPE_DIM - ROPE_DIM), f32)
    kr_slab = jnp.concatenate([z64, kr, z32], axis=1)
    kr_rot = jnp.concatenate([z64, -kr[:, half:], kr[:, :half], z32], axis=1)
    w_pack = jnp.concatenate([w[:, :kr0], kr_slab, kr_rot], axis=1).astype(bf16)

    uq = d_w_uq[i].reshape(Q_RANK, D_HEADS, NOPE_DIM + ROPE_DIM)
    nope, rope = uq[..., :NOPE_DIM], uq[..., NOPE_DIM:]
    zq = jnp.zeros((Q_RANK, D_HEADS, LANES - NOPE_DIM - ROPE_DIM), f32)
    w_uq = jnp.concatenate([nope, rope, zq], axis=-1).reshape(Q_RANK, D_HEADS * LANES)
    w_uq_rot = jnp.concatenate([0.0 * nope, -rope[..., half:], rope[..., :half], zq],
                               axis=-1).reshape(Q_RANK, D_HEADS * LANES)
    ukv = d_w_ukv[i].reshape(KV_RANK, D_HEADS, NOPE_DIM + V_DIM)
    w_k = jnp.concatenate([ukv[..., :NOPE_DIM], jnp.zeros((KV_RANK, D_HEADS, LANES - NOPE_DIM), f32)],
                          axis=-1).reshape(KV_RANK, D_HEADS * LANES)
    w_v = ukv[..., NOPE_DIM:].reshape(KV_RANK, D_HEADS * V_DIM)
    rw = jnp.concatenate([router_w[i], jnp.zeros((D_MODEL, LANES - N_EXPERTS), f32)], axis=1)
    return {
        "g1": norm1_g[i].reshape(1, D_MODEL),
        "w_in": w_pack,
        "a_vg": a_vnorm_g[i].reshape(1, GROUP_W),
        "a_ws": a_ws[i].astype(bf16),
        "a_bias": jnp.repeat(a_bs[i].T, HEAD_DIM, axis=1),
        "b_conv": b_conv[i],
        "c_bias": _natten_bias_table(c_rpb[i]),
        "d_qg": d_qnorm_g[i].reshape(1, Q_RANK),
        "d_kvg": d_kvnorm_g[i].reshape(1, KV_RANK),
        "w_uq": w_uq.astype(bf16),
        "w_uq_rot": w_uq_rot.astype(bf16),
        "w_k": w_k.astype(bf16),
        "w_v": w_v.astype(bf16),
        "og": out_norm_g[i].reshape(1, D_MODEL),
        "w_o": w_o[i].astype(bf16),
        "g2": norm2_g[i].reshape(1, D_MODEL),
        "rw": rw.astype(bf16),
        "rwt": router_w[i].T.astype(bf16),
        "e_wg": e_w_gate[i].astype(bf16),
        "e_wu": e_w_up[i].astype(bf16),
        "e_wd": e_w_down[i].astype(bf16),
    }


def _moe(xn, h2, aff, lw, tri):
    x = xn
    for row0, nrows in ((0, BATCH * SEQ), (BATCH * SEQ, DEC_BATCH * SEQ)):
        cap = CAPACITY_FACTOR * nrows // N_EXPERTS
        posm, base = _route(aff, row0, nrows, tri)
        first = base[:, :, 0].astype(jnp.int32)
        count = jnp.diff(first, axis=0, append=jnp.full((1, N_EXPERTS), cap, jnp.int32))
        xe = _dispatch(h2, posm, first, count, row0, nrows)
        ye = _ffn(xe, lw, nrows)
        x = _combine(x, ye, posm, first, count, row0, nrows)
    return x


def kernel(x_prompt, x_sample, norm1_g, w_in, a_vnorm_g, a_ws, a_bs, b_conv, c_rpb, d_qnorm_g, d_kvnorm_g, d_w_uq, d_w_ukv, out_norm_g, w_o, norm2_g, router_w, e_w_gate, e_w_up, e_w_down, final_norm_g):
    n_p = BATCH * SEQ
    n_s = DEC_BATCH * SEQ
    x = _stack_rows(x_prompt.reshape(n_p, D_MODEL), x_sample.reshape(n_s, D_MODEL))
    cosq, sinq = _rope_slabs()
    tri = jnp.asarray(np.triu(np.ones((TT, TT), np.float32), 1), bf16)
    for i in range(DEPTH):
        lw = _pack_layer(i, norm1_g, w_in, a_vnorm_g, a_ws, a_bs, b_conv, c_rpb, d_qnorm_g, d_kvnorm_g,
                         d_w_uq, d_w_ukv, out_norm_g, w_o, norm2_g, router_w, e_w_gate, e_w_up, e_w_down)
        ya, yb, zc, qt, k4, vt = _inproj(x, lw, cosq, sinq)
        yc = _natten(zc, lw["c_bias"])
        yd = _mla(qt, k4, vt)
        xn, h2, aff = _outproj(ya, yb, yc, yd, x, lw)
        x = _moe(xn, h2, aff, lw, tri)
    g = final_norm_g.reshape(1, D_MODEL)
    y_p = _final_norm(x, g, 0, n_p).reshape(BATCH, SEQ, D_MODEL)
    y_s = _final_norm(x, g, n_p, n_s).reshape(DEC_BATCH, SEQ, D_MODEL)
    return (y_p, y_s)
```

```python
import functools
import math

import numpy as np
import jax
import jax.numpy as jnp
from jax import lax
from jax.experimental import pallas as pl
from jax.experimental.pallas import tpu as pltpu

D_MODEL = 1024
BATCH = 4
SEQ = 8192
DEPTH = 4
DEC_BATCH = 2
GROUP_W = 256
HEAD_DIM = 64
A_HEADS = 4
CHUNK = 128
C_HEADS = 4
GRID_W = 64
WIN_R = 8
WIN_C = 16
D_HEADS = 4
Q_RANK = 256
KV_RANK = 128
NOPE_DIM = 64
ROPE_DIM = 32
V_DIM = 64
ROPE_THETA = 10000.0
N_EXPERTS = 16
D_EXPERT = 1024
CAPACITY_FACTOR = 2
EPS = 1e-6

NSEQ = BATCH + DEC_BATCH
T_ALL = NSEQ * SEQ
ROWS = SEQ // GRID_W
LANES = 128
NEG = -1e30

A_COLS = 2 * GROUP_W
B_COLS = 3 * GROUP_W
C_COLS = 3 * GROUP_W
OFF_B = A_COLS
OFF_C = OFF_B + B_COLS
OFF_D = OFF_C + C_COLS
D_PACK = Q_RANK + KV_RANK + 2 * LANES
N_IN_PACK = OFF_D + D_PACK

TM = 1024
TQ = 256
TK = 512
L_ROWS = 16
MLA_UNROLL = 4
MLA_LAG = 1
MLA_QSCALE = (NOPE_DIM + ROPE_DIM) ** -0.5 * math.log2(math.e)
ROWS_PER_STEP = 8
TME = 1024
TT = 256
ROUTE_TILES = 4
ROW_ALIGN = 16
DISP_WINDOWS = (80, 144, TT + ROW_ALIGN)
COMB_ALIGN = ROW_ALIGN
COMB_WINDOWS = (80, 144, TT + COMB_ALIGN)
VMEM_LIMIT = 56 * 1024 * 1024

f32 = jnp.float32
bf16 = jnp.bfloat16


def _rms(x, g):
    return x * lax.rsqrt(jnp.mean(x * x, axis=-1, keepdims=True) + EPS) * g


def _dot(a, b):
    return jnp.dot(a, b, preferred_element_type=f32)


def _dot_nt(a, b):
    return lax.dot_general(a, b, (((1,), (1,)), ((), ())), preferred_element_type=f32)


def _dot_tn(a, b):
    return lax.dot_general(a, b, (((0,), (0,)), ((), ())), preferred_element_type=f32)


def _inproj_kernel(x_ref, xp_ref, xn_ref, g1_ref, win_ref, avg_ref, aws_ref, abias_ref,
                   bconv_ref, qg_ref, kvg_ref, wuq_ref, wuqr_ref, wk_ref, wv_ref,
                   cos_ref, sin_ref,
                   ya_ref, yb_ref, zc_ref, qt_ref, k4_ref, vt_ref):
    i = pl.program_id(0)
    j = i % (SEQ // TM)
    g1 = g1_ref[...]
    h = _rms(x_ref[...], g1).astype(bf16)

    za = jax.nn.gelu(_dot(h, win_ref[:, 0:A_COLS]))
    u = za[:, :GROUP_W]
    v = _rms(za[:, GROUP_W:], avg_ref[...]).astype(bf16)
    lane = lax.broadcasted_iota(jnp.int32, (CHUNK, GROUP_W), 1)
    for c in range(TM // CHUNK):
        vc = v[c * CHUNK:(c + 1) * CHUNK, :]
        sv = abias_ref[...]
        for hd in range(A_HEADS):
            r = _dot(aws_ref[hd], vc)
            sv = sv + jnp.where((lane >= hd * HEAD_DIM) & (lane < (hd + 1) * HEAD_DIM), r, 0.0)
        ya_ref[c * CHUNK:(c + 1) * CHUNK, :] = u[c * CHUNK:(c + 1) * CHUNK, :] * sv

    zb = _dot(h, win_ref[:, OFF_B:OFF_B + B_COLS])
    bg = zb[:, :GROUP_W]
    y = zb[:, GROUP_W:2 * GROUP_W] * zb[:, 2 * GROUP_W:]
    xh = jnp.concatenate([xp_ref[...], xn_ref[...]], axis=0)
    hh = _rms(xh, g1).astype(bf16)
    zh = _dot(hh, win_ref[:, OFF_B + GROUP_W:OFF_B + B_COLS])
    yh = zh[:, :GROUP_W] * zh[:, GROUP_W:]
    y_before = jnp.where(j == 0, 0.0, yh[7:8, :])
    y_after = jnp.where(j == SEQ // TM - 1, 0.0, yh[8:9, :])
    row = lax.broadcasted_iota(jnp.int32, (TM, GROUP_W), 0)
    y_m1 = jnp.where(row == 0, y_before, pltpu.roll(y, 1, axis=0))
    y_p1 = jnp.where(row == TM - 1, y_after, pltpu.roll(y, TM - 1, axis=0))
    wc = bconv_ref[...]
    yb_ref[...] = bg * (wc[0:1, :] * y_m1 + wc[1:2, :] * y + wc[2:3, :] * y_p1)

    zc = _dot(h, win_ref[:, OFF_C:OFF_C + C_COLS])
    zc_ref[:, 0:GROUP_W] = (zc[:, 0:GROUP_W] * (HEAD_DIM ** -0.5)).astype(bf16)
    zc_ref[:, GROUP_W:] = zc[:, GROUP_W:].astype(bf16)

    zd = _dot(h, win_ref[:, OFF_D:OFF_D + D_PACK])
    cqn = _rms(zd[:, 0:Q_RANK], qg_ref[...]).astype(bf16)
    ckvn = _rms(zd[:, Q_RANK:Q_RANK + KV_RANK], kvg_ref[...]).astype(bf16)
    cos = cos_ref[...]
    sin = sin_ref[...]
    o = Q_RANK + KV_RANK
    kr = zd[:, o:o + LANES] * cos + zd[:, o + LANES:o + 2 * LANES] * sin
    q_pre = _dot(cqn, wuq_ref[...])
    q_rot = _dot(cqn, wuqr_ref[...])
    kn = _dot(ckvn, wk_ref[...])
    for hd in range(D_HEADS):
        sl = slice(hd * LANES, (hd + 1) * LANES)
        qt_ref[hd] = ((q_pre[:, sl] * cos + q_rot[:, sl] * sin) * MLA_QSCALE).T.astype(bf16)
        k4_ref[hd] = (kn[:, sl] + kr).astype(bf16)
    vt = _dot(ckvn, wv_ref[...]).T.astype(bf16)
    for hd in range(D_HEADS):
        vt_ref[hd] = vt[hd * V_DIM:(hd + 1) * V_DIM, :]


def _inproj(x, lw, cosq, sinq):
    nt = T_ALL // TM
    tps = SEQ // TM
    full = lambda shape: pl.BlockSpec(shape, lambda i: (0,) * len(shape))
    in_specs = [
        pl.BlockSpec((TM, D_MODEL), lambda i: (i, 0)),
        pl.BlockSpec((8, D_MODEL), lambda i: (jnp.maximum(i * (TM // 8) - 1, 0), 0)),
        pl.BlockSpec((8, D_MODEL), lambda i: (jnp.minimum((i + 1) * (TM // 8), T_ALL // 8 - 1), 0)),
        full((1, D_MODEL)),
        full((D_MODEL, N_IN_PACK)),
        full((1, GROUP_W)),
        full((A_HEADS, CHUNK, CHUNK)),
        full((CHUNK, GROUP_W)),
        full((3, GROUP_W)),
        full((1, Q_RANK)),
        full((1, KV_RANK)),
        full((Q_RANK, D_HEADS * LANES)),
        full((Q_RANK, D_HEADS * LANES)),
        full((KV_RANK, D_HEADS * LANES)),
        full((KV_RANK, D_HEADS * V_DIM)),
        pl.BlockSpec((TM, LANES), lambda i: (i % tps, 0)),
        pl.BlockSpec((TM, LANES), lambda i: (i % tps, 0)),
    ]
    out_specs = [
        pl.BlockSpec((TM, GROUP_W), lambda i: (i, 0)),
        pl.BlockSpec((TM, GROUP_W), lambda i: (i, 0)),
        pl.BlockSpec((TM, C_COLS), lambda i: (i, 0)),
        pl.BlockSpec((None, D_HEADS, LANES, TM), lambda i: (i // tps, 0, 0, i % tps)),
        pl.BlockSpec((None, D_HEADS, TM, LANES), lambda i: (i // tps, 0, i % tps, 0)),
        pl.BlockSpec((None, D_HEADS, V_DIM, TM), lambda i: (i // tps, 0, 0, i % tps)),
    ]
    out_shape = [
        jax.ShapeDtypeStruct((T_ALL, GROUP_W), f32),
        jax.ShapeDtypeStruct((T_ALL, GROUP_W), f32),
        jax.ShapeDtypeStruct((T_ALL, C_COLS), bf16),
        jax.ShapeDtypeStruct((NSEQ, D_HEADS, LANES, SEQ), bf16),
        jax.ShapeDtypeStruct((NSEQ, D_HEADS, SEQ, LANES), bf16),
        jax.ShapeDtypeStruct((NSEQ, D_HEADS, V_DIM, SEQ), bf16),
    ]
    return pl.pallas_call(
        _inproj_kernel, grid=(nt,), in_specs=in_specs, out_specs=out_specs, out_shape=out_shape,
        name="inproj",
        compiler_params=pltpu.CompilerParams(dimension_semantics=("parallel",),
                                             vmem_limit_bytes=VMEM_LIMIT),
    )(x, x, x, lw["g1"], lw["w_in"], lw["a_vg"], lw["a_ws"], lw["a_bias"], lw["b_conv"],
      lw["d_qg"], lw["d_kvg"], lw["w_uq"], lw["w_uq_rot"], lw["w_k"], lw["w_v"], cosq, sinq)


def _natten_kernel(q_ref, k_ref, v_ref, bias_ref, o_ref):
    jb = pl.program_id(1)
    lane = lax.broadcasted_iota(jnp.int32, (GRID_W, GROUP_W), 1)
    head_of_lane = lane // HEAD_DIM
    nkeys = WIN_R * GRID_W

    jobs = []
    for i in range(ROWS_PER_STEP):
        r = jb * ROWS_PER_STEP + i
        rs = jnp.clip(r - WIN_R // 2, 0, ROWS - WIN_R)
        d = rs - r + (WIN_R - 1)
        kstart = pl.multiple_of(rs * GRID_W, GRID_W)
        q = q_ref[i * GRID_W:(i + 1) * GRID_W, :]
        zero = jnp.zeros_like(q)
        qq = jnp.concatenate([jnp.where(head_of_lane == h, q, zero) for h in range(C_HEADS)], axis=0)
        jobs.append((_dot_nt(qq, k_ref[pl.ds(kstart, nkeys), :]), d, kstart))
    for i, (sc, d, kstart) in enumerate(jobs):
        sc = sc + bias_ref[d]
        m = jnp.max(sc, axis=-1, keepdims=True)
        p = jnp.exp(sc - m)
        l = jnp.sum(p, axis=-1, keepdims=True)
        o = _dot(p.astype(bf16), v_ref[pl.ds(kstart, nkeys), :]) / l
        out = jnp.zeros((GRID_W, GROUP_W), f32)
        for h in range(C_HEADS):
            out = jnp.where(head_of_lane == h, o[h * GRID_W:(h + 1) * GRID_W], out)
        o_ref[i * GRID_W:(i + 1) * GRID_W, :] = out


def _natten(zc, bias):
    nb = ROWS // ROWS_PER_STEP
    tq = ROWS_PER_STEP * GRID_W
    return pl.pallas_call(
        _natten_kernel, grid=(NSEQ, nb),
        in_specs=[
            pl.BlockSpec((tq, GROUP_W), lambda b, j: (b * nb + j, 0)),
            pl.BlockSpec((SEQ, GROUP_W), lambda b, j: (b, 1)),
            pl.BlockSpec((SEQ, GROUP_W), lambda b, j: (b, 2)),
            pl.BlockSpec((WIN_R, C_HEADS * GRID_W, WIN_R * GRID_W), lambda b, j: (0, 0, 0)),
        ],
        out_specs=pl.BlockSpec((tq, GROUP_W), lambda b, j: (b * nb + j, 0)),
        out_shape=jax.ShapeDtypeStruct((T_ALL, GROUP_W), f32),
        name="natten",
        compiler_params=pltpu.CompilerParams(dimension_semantics=("parallel", "arbitrary"),
                                             vmem_limit_bytes=VMEM_LIMIT),
    )(zc, zc, zc, bias)


def _mla_kernel(qt_ref, k_ref, vt_ref, o_ref, s_scr):
    nchunk = SEQ // TK
    heads = list(range(D_HEADS))
    qts = [qt_ref[h] for h in heads]

    def qk(slot, h, ci):
        ks = pl.multiple_of(ci * TK, TK)
        s_scr[slot, h] = _dot(k_ref[h, pl.ds(ks, TK), :], qts[h])

    ones = jnp.ones((L_ROWS, TK), bf16)

    def softmax_pv(slot, h, ci, stat):
        m, acc = stat
        ks = pl.multiple_of(ci * TK, TK)
        vt = jnp.concatenate([vt_ref[h, :, pl.ds(ks, TK)], ones], axis=0)
        s = s_scr[slot, h]
        m_new = jnp.maximum(m, jnp.max(s, axis=0, keepdims=True))
        alpha = jnp.exp2(m - m_new)
        p = jnp.exp2(s - m_new)
        acc = acc * alpha + _dot(vt, p.astype(bf16))
        return m_new, acc

    def half(cur, ci, stats, prefetch=True):
        out = []
        for h in heads:
            if prefetch:
                qk(1 - cur, h, ci + 1)
            if h >= MLA_LAG:
                out.append(softmax_pv(cur, h - MLA_LAG, ci, stats[h - MLA_LAG]))
        for h in range(D_HEADS - MLA_LAG, D_HEADS):
            out.append(softmax_pv(cur, h, ci, stats[h]))
        return tuple(out)

    def body(cp, stats):
        for u in range(MLA_UNROLL):
            stats = half(u % 2, MLA_UNROLL * cp + u, stats)
        return stats

    for h in heads:
        qk(0, h, 0)
    init = tuple((jnp.full((1, TQ), NEG, f32), jnp.zeros((V_DIM + L_ROWS, TQ), f32)) for _ in heads)
    res = lax.fori_loop(0, nchunk // MLA_UNROLL - 1, body, init)
    for u in range(MLA_UNROLL):
        res = half(u % 2, nchunk - MLA_UNROLL + u, res, prefetch=u < MLA_UNROLL - 1)
    o_ref[...] = jnp.concatenate([acc[:V_DIM] / acc[V_DIM:V_DIM + 1] for (_, acc) in res], axis=0).T


def _mla(qt, k4, vt):
    nq = SEQ // TQ
    return pl.pallas_call(
        _mla_kernel, grid=(NSEQ, nq),
        in_specs=[
            pl.BlockSpec((None, D_HEADS, LANES, TQ), lambda b, i: (b, 0, 0, i)),
            pl.BlockSpec((None, D_HEADS, SEQ, LANES), lambda b, i: (b, 0, 0, 0)),
            pl.BlockSpec((None, D_HEADS, V_DIM, SEQ), lambda b, i: (b, 0, 0, 0)),
        ],
        out_specs=pl.BlockSpec((TQ, GROUP_W), lambda b, i: (b * nq + i, 0)),
        out_shape=jax.ShapeDtypeStruct((T_ALL, GROUP_W), f32),
        scratch_shapes=[pltpu.VMEM((2, D_HEADS, TK, TQ), f32)],
        name="mla",
        compiler_params=pltpu.CompilerParams(
            dimension_semantics=("parallel", "arbitrary"),
            vmem_limit_bytes=VMEM_LIMIT),
    )(qt, k4, vt)


def _outproj_kernel(ya_ref, yb_ref, yc_ref, yd_ref, x_ref, og_ref, wo_ref, g2_ref, rwt_ref,
                    xn_ref, h2_ref, aff_ref):
    y = jnp.concatenate(
        [_rms(y_ref[...], og_ref[:, g * GROUP_W:(g + 1) * GROUP_W]).astype(bf16)
         for g, y_ref in enumerate((ya_ref, yb_ref, yc_ref, yd_ref))], axis=1)
    acc = x_ref[...] + _dot(y, wo_ref[...])
    xn_ref[...] = acc
    h2 = _rms(acc, g2_ref[...]).astype(bf16)
    h2_ref[...] = h2
    logits = _dot_nt(rwt_ref[...], h2)
    e = jnp.exp(logits - jnp.max(logits, axis=0, keepdims=True))
    aff_ref[...] = e / jnp.sum(e, axis=0, keepdims=True)


def _outproj(ya, yb, yc, yd, x, lw):
    nt = T_ALL // TM
    full = lambda shape: pl.BlockSpec(shape, lambda i: (0,) * len(shape))
    yspec = pl.BlockSpec((TM, GROUP_W), lambda i: (i, 0))
    xspec = pl.BlockSpec((TM, D_MODEL), lambda i: (i, 0))
    return pl.pallas_call(
        _outproj_kernel, grid=(nt,),
        in_specs=[yspec, yspec, yspec, yspec, xspec, full((1, D_MODEL)), full((D_MODEL, D_MODEL)),
                  full((1, D_MODEL)), full((N_EXPERTS, D_MODEL))],
        out_specs=[xspec, xspec, pl.BlockSpec((N_EXPERTS, TM), lambda i: (0, i))],
        out_shape=[jax.ShapeDtypeStruct((T_ALL, D_MODEL), f32),
                   jax.ShapeDtypeStruct((T_ALL, D_MODEL), bf16),
                   jax.ShapeDtypeStruct((N_EXPERTS, T_ALL), f32)],
        name="outproj",
        compiler_params=pltpu.CompilerParams(dimension_semantics=("parallel",),
                                             vmem_limit_bytes=VMEM_LIMIT),
    )(ya, yb, yc, yd, x, lw["og"], lw["w_o"], lw["g2"], lw["rwt"])


def _route_kernel(aff_ref, tri_ref, posm_ref, base_ref, thr_scr, need_scr, runeq_scr, runsel_scr, *, cap):
    j = pl.program_id(0)

    @pl.when(j == 0)
    def _():
        def refine(i, thr):
            cand = thr | jnp.left_shift(jnp.int32(1), 30 - i)
            bits = pltpu.bitcast(aff_ref[...], jnp.int32)
            cnt = jnp.sum(jnp.where(bits >= cand, 1.0, 0.0), axis=1, keepdims=True)
            return jnp.where(cnt >= cap, cand, thr)

        thr = lax.fori_loop(0, 31, refine, jnp.zeros((N_EXPERTS, 1), jnp.int32))
        bits = pltpu.bitcast(aff_ref[...], jnp.int32)
        above = jnp.sum(jnp.where(bits > thr, 1.0, 0.0), axis=1, keepdims=True)
        thr_scr[...] = jnp.broadcast_to(thr, (N_EXPERTS, LANES))
        need_scr[...] = jnp.broadcast_to(cap - above, (N_EXPERTS, LANES))
        runeq_scr[...] = jnp.zeros((N_EXPERTS, LANES), f32)
        runsel_scr[...] = jnp.zeros((N_EXPERTS, LANES), f32)

    thr = thr_scr[:, 0:1]
    need = need_scr[:, 0:1]
    tiles = []
    for u in range(ROUTE_TILES):
        t0 = pl.multiple_of((j * ROUTE_TILES + u) * TT, TT)
        bits = pltpu.bitcast(aff_ref[:, pl.ds(t0, TT)], jnp.int32)
        eq = bits == thr
        eq_f = jnp.where(eq, 1.0, 0.0)
        tiles.append((bits, eq, eq_f, _dot(eq_f.astype(bf16), tri_ref[...])))
    sels = []
    runeq = runeq_scr[:, 0:1]
    for bits, eq, eq_f, eq_before in tiles:
        sel = (bits > thr) | (eq & (runeq + eq_before < need))
        sel_f = jnp.where(sel, 1.0, 0.0)
        sels.append((sel, sel_f, _dot(sel_f.astype(bf16), tri_ref[...])))
        runeq = runeq + jnp.sum(eq_f, axis=1, keepdims=True)
    runeq_scr[...] = jnp.broadcast_to(runeq, (N_EXPERTS, LANES))
    base = runsel_scr[...]
    for u, (sel, sel_f, sel_before) in enumerate(sels):
        posm_ref[:, u * TT:(u + 1) * TT] = jnp.where(sel, base[:, 0:1] + sel_before, -1.0)
        base_ref[u] = base
        base = base + jnp.sum(sel_f, axis=1, keepdims=True)
    runsel_scr[...] = base


def _route(aff, row0, nrows, tri):
    cap = CAPACITY_FACTOR * nrows // N_EXPERTS
    nt = nrows // TT
    blk0 = row0 // nrows
    vec = pltpu.VMEM((N_EXPERTS, LANES), f32)
    return pl.pallas_call(
        functools.partial(_route_kernel, cap=cap), grid=(nt // ROUTE_TILES,),
        in_specs=[pl.BlockSpec((N_EXPERTS, nrows), lambda j: (0, blk0)),
                  pl.BlockSpec((TT, TT), lambda j: (0, 0))],
        out_specs=[pl.BlockSpec((N_EXPERTS, ROUTE_TILES * TT), lambda j: (0, j)),
                   pl.BlockSpec((ROUTE_TILES, N_EXPERTS, LANES), lambda j: (j, 0, 0))],
        out_shape=[jax.ShapeDtypeStruct((N_EXPERTS, nrows), f32),
                   jax.ShapeDtypeStruct((nt, N_EXPERTS, LANES), f32)],
        scratch_shapes=[pltpu.VMEM((N_EXPERTS, LANES), jnp.int32), vec, vec, vec],
        name="route",
        compiler_params=pltpu.CompilerParams(dimension_semantics=("arbitrary",)),
    )(aff, tri)


def _window_class(count, windows, align):
    longest = jnp.max(count, axis=1) + (align - 1)
    return sum((longest > w).astype(jnp.int32) for w in windows[:-1])


def _dispatch_kernel(big_ref, first_ref, posm_ref, h2_ref, xe_hbm, stage, sem, carry, *, cap, ntiles):
    j = pl.program_id(0)
    slot = j % 2

    @pl.when(j == 0)
    def _():
        carry[...] = jnp.zeros_like(carry)
        pad = DISP_WINDOWS[-1]
        stage[1, 0:pad, :] = jnp.zeros((pad, D_MODEL), bf16)
        pads = [pltpu.make_async_copy(stage.at[1, pl.ds(0, pad)], xe_hbm.at[e, pl.ds(cap, pad)], sem.at[1])
                for e in range(N_EXPERTS)]
        for p in pads:
            p.start()
        for p in pads:
            p.wait()

    def run_start(jj, e):
        return first_ref[jj * N_EXPERTS + e]

    def window_copy(jj, e, sl, win):
        a = pl.multiple_of(run_start(jj, e) // ROW_ALIGN * ROW_ALIGN, ROW_ALIGN)
        return pltpu.make_async_copy(stage.at[sl, pl.ds(e * win, win)], xe_hbm.at[e, pl.ds(a, win)], sem.at[sl])

    def for_tile_window(jj, fn):
        for flag, win in enumerate(DISP_WINDOWS):
            @pl.when(big_ref[jj] == flag)
            def _():
                fn(win)

    def wait_all(jj, sl):
        def go(win):
            for e in range(N_EXPERTS):
                window_copy(jj, e, sl, win).wait()
        for_tile_window(jj, go)

    def compact(win):
        row = lax.broadcasted_iota(jnp.int32, (win, TT), 0)
        starts = [run_start(j, e) // ROW_ALIGN * ROW_ALIGN for e in range(N_EXPERTS)]
        onehot = jnp.concatenate(
            [jnp.where((row + starts[e]).astype(f32) == posm_ref[e:e + 1, :], 1.0, 0.0)
             for e in range(N_EXPERTS)], axis=0).astype(bf16)
        rows = _dot(onehot, h2_ref[...])
        head = lax.broadcasted_iota(jnp.int32, (ROW_ALIGN, D_MODEL), 0)
        for e in range(N_EXPERTS):
            lead = run_start(j, e) - starts[e]
            w = rows[e * win:(e + 1) * win]
            top = w[:ROW_ALIGN] + jnp.where(head < lead, carry[e], 0.0)
            stage[slot, e * win:e * win + ROW_ALIGN, :] = top.astype(bf16)
            stage[slot, e * win + ROW_ALIGN:(e + 1) * win, :] = w[ROW_ALIGN:].astype(bf16)
        for e in range(N_EXPERTS):
            end = jnp.where(j + 1 < ntiles, run_start(jnp.minimum(j + 1, ntiles - 1), e), cap)
            off = jnp.minimum(end // ROW_ALIGN * ROW_ALIGN - starts[e], win - ROW_ALIGN)
            off = pl.multiple_of(off, ROW_ALIGN)
            carry[e] = stage[slot, pl.ds(e * win + off, ROW_ALIGN), :].astype(f32)

    for_tile_window(j, compact)

    @pl.when(j > 0)
    def _():
        wait_all(j - 1, 1 - slot)

    def start_all(win):
        for e in range(N_EXPERTS):
            window_copy(j, e, slot, win).start()
    for_tile_window(j, start_all)

    @pl.when(j == ntiles - 1)
    def _():
        wait_all(j, slot)


def _dispatch(h2, posm, first, count, row0, nrows):
    cap = CAPACITY_FACTOR * nrows // N_EXPERTS
    nt = nrows // TT
    blk0 = row0 // TT
    big = _window_class(count, DISP_WINDOWS, ROW_ALIGN)
    return pl.pallas_call(
        functools.partial(_dispatch_kernel, cap=cap, ntiles=nt),
        grid_spec=pltpu.PrefetchScalarGridSpec(
            num_scalar_prefetch=2, grid=(nt,),
            in_specs=[pl.BlockSpec((N_EXPERTS, TT), lambda j, *_: (0, j)),
                      pl.BlockSpec((TT, D_MODEL), lambda j, *_: (blk0 + j, 0))],
            out_specs=pl.BlockSpec(memory_space=pl.ANY),
            scratch_shapes=[pltpu.VMEM((2, N_EXPERTS * DISP_WINDOWS[-1], D_MODEL), bf16),
                            pltpu.SemaphoreType.DMA((2,)),
                            pltpu.VMEM((N_EXPERTS, ROW_ALIGN, D_MODEL), f32)],
        ),
        out_shape=jax.ShapeDtypeStruct((N_EXPERTS, cap + DISP_WINDOWS[-1], D_MODEL), bf16),
        name="dispatch",
        compiler_params=pltpu.CompilerParams(dimension_semantics=("arbitrary",), vmem_limit_bytes=VMEM_LIMIT),
    )(big, first.reshape(-1), posm, h2)


def _ffn_kernel(xe_ref, rw_ref, wg_ref, wu_ref, wd_ref, ye_ref):
    e = pl.program_id(0)
    x = xe_ref[...]
    logits = _dot(x, rw_ref[...])
    lane = lax.broadcasted_iota(jnp.int32, logits.shape, 1)
    logits = jnp.where(lane < N_EXPERTS, logits, NEG)
    ex = jnp.exp(logits - jnp.max(logits, axis=-1, keepdims=True))
    gate = jnp.sum(jnp.where(lane == e, ex, 0.0), axis=-1, keepdims=True) / jnp.sum(ex, axis=-1, keepdims=True)
    hg = _dot(x, wg_ref[...])
    hu = _dot(x, wu_ref[...])
    h = (hg * (1.0 / (1.0 + jnp.exp(-hg))) * hu).astype(bf16)
    ye_ref[...] = (_dot(h, wd_ref[...]) * gate).astype(bf16)


def _ffn(xe, lw, nrows):
    cap = CAPACITY_FACTOR * nrows // N_EXPERTS
    layer = lw["layer"]
    wspec = pl.BlockSpec((None, None, D_MODEL, D_EXPERT), lambda e, t: (layer, e, 0, 0))
    return pl.pallas_call(
        _ffn_kernel, grid=(N_EXPERTS, cap // TME),
        in_specs=[
            pl.BlockSpec((None, TME, D_MODEL), lambda e, t: (e, t, 0)),
            pl.BlockSpec((D_MODEL, LANES), lambda e, t: (0, 0)),
            wspec, wspec,
            pl.BlockSpec((None, None, D_EXPERT, D_MODEL), lambda e, t: (layer, e, 0, 0)),
        ],
        out_specs=pl.BlockSpec((None, TME, D_MODEL), lambda e, t: (e, t, 0)),
        out_shape=jax.ShapeDtypeStruct((N_EXPERTS, cap, D_MODEL), bf16),
        name="expert_ffn",
        compiler_params=pltpu.CompilerParams(dimension_semantics=("parallel", "arbitrary"),
                                             vmem_limit_bytes=VMEM_LIMIT),
    )(xe, lw["rw"], lw["e_wg"], lw["e_wu"], lw["e_wd"])


def _combine_kernel(big_ref, first_ref, posm_ref, x_ref, ye_hbm, o_ref, buf, sem, *, cap, ntiles):
    j = pl.program_id(0)

    def window_start(jj, e, win):
        return jnp.minimum(first_ref[jj * N_EXPERTS + e] // COMB_ALIGN * COMB_ALIGN, cap - win)

    def window_copy(jj, e, slot, win):
        a = pl.multiple_of(window_start(jj, e, win), COMB_ALIGN)
        return pltpu.make_async_copy(ye_hbm.at[e, pl.ds(a, win)], buf.at[slot, pl.ds(e * win, win)],
                                     sem.at[slot])

    def for_tile_window(jj, fn):
        for flag, win in enumerate(COMB_WINDOWS):
            @pl.when(big_ref[jj] == flag)
            def _():
                fn(win)

    def start_all(jj, slot):
        def go(win):
            for e in range(N_EXPERTS):
                window_copy(jj, e, slot, win).start()
        for_tile_window(jj, go)

    @pl.when(j == 0)
    def _():
        start_all(0, 0)

    slot = j % 2

    def wait_all(win):
        for e in range(N_EXPERTS):
            window_copy(j, e, slot, win).wait()
    for_tile_window(j, wait_all)

    @pl.when(j + 1 < ntiles)
    def _():
        start_all(j + 1, 1 - slot)

    def add_rows(win):
        row = lax.broadcasted_iota(jnp.int32, (win, TT), 0)
        onehot = jnp.concatenate(
            [jnp.where((row + window_start(j, e, win)).astype(f32) == posm_ref[e:e + 1, :], 1.0, 0.0)
             for e in range(N_EXPERTS)], axis=0).astype(bf16)
        o_ref[...] = x_ref[...] + _dot_tn(onehot, buf[slot, 0:N_EXPERTS * win, :])
    for_tile_window(j, add_rows)


def _combine(x, ye, posm, first, count, row0, nrows):
    cap = CAPACITY_FACTOR * nrows // N_EXPERTS
    nt = nrows // TT
    blk0 = row0 // TT
    big = _window_class(count, COMB_WINDOWS, COMB_ALIGN)
    return pl.pallas_call(
        functools.partial(_combine_kernel, cap=cap, ntiles=nt),
        grid_spec=pltpu.PrefetchScalarGridSpec(
            num_scalar_prefetch=2, grid=(nt,),
            in_specs=[pl.BlockSpec((N_EXPERTS, TT), lambda j, *_: (0, j)),
                      pl.BlockSpec((TT, D_MODEL), lambda j, *_: (blk0 + j, 0)),
                      pl.BlockSpec(memory_space=pl.ANY)],
            out_specs=pl.BlockSpec((TT, D_MODEL), lambda j, *_: (blk0 + j, 0)),
            scratch_shapes=[pltpu.VMEM((2, N_EXPERTS * COMB_WINDOWS[-1], D_MODEL), bf16),
                            pltpu.SemaphoreType.DMA((2,))],
        ),
        out_shape=jax.ShapeDtypeStruct(x.shape, f32),
        input_output_aliases={3: 0},
        name="combine",
        compiler_params=pltpu.CompilerParams(dimension_semantics=("arbitrary",), vmem_limit_bytes=VMEM_LIMIT),
    )(big, first.reshape(-1), posm, x, ye)


def _stack_kernel(a_ref, b_ref, o_ref, *, na_blocks):
    o_ref[...] = jnp.where(pl.program_id(0) < na_blocks, a_ref[...], b_ref[...])


def _stack_rows(a, b):
    na, nb = a.shape[0] // TM, b.shape[0] // TM
    return pl.pallas_call(
        functools.partial(_stack_kernel, na_blocks=na), grid=(na + nb,),
        in_specs=[pl.BlockSpec((TM, D_MODEL), lambda i: (jnp.minimum(i, na - 1), 0)),
                  pl.BlockSpec((TM, D_MODEL), lambda i: (jnp.maximum(i - na, 0), 0))],
        out_specs=pl.BlockSpec((TM, D_MODEL), lambda i: (i, 0)),
        out_shape=jax.ShapeDtypeStruct((a.shape[0] + b.shape[0], D_MODEL), a.dtype),
        name="stack_rows",
        compiler_params=pltpu.CompilerParams(dimension_semantics=("arbitrary",)),
    )(a, b)


def _final_kernel(x_ref, g_ref, o_ref):
    o_ref[...] = _rms(x_ref[...], g_ref[...])


def _final_norm(x, g, row0, nrows):
    off = row0 // TM
    return pl.pallas_call(
        _final_kernel, grid=(nrows // TM,),
        in_specs=[pl.BlockSpec((TM, D_MODEL), lambda i: (i + off, 0)),
                  pl.BlockSpec((1, D_MODEL), lambda i: (0, 0))],
        out_specs=pl.BlockSpec((TM, D_MODEL), lambda i: (i, 0)),
        out_shape=jax.ShapeDtypeStruct((nrows, D_MODEL), f32),
        name="final_norm",
        compiler_params=pltpu.CompilerParams(dimension_semantics=("parallel",)),
    )(x, g)


def _natten_bias_table(rpb):
    c = np.arange(GRID_W)
    cs = np.clip(c - WIN_C // 2, 0, GRID_W - WIN_C)
    kc = np.arange(GRID_W)
    col_mask = (kc[None, :] >= cs[:, None]) & (kc[None, :] < cs[:, None] + WIN_C)
    dc = np.clip(kc[None, :] - c[:, None], -(WIN_C - 1), WIN_C - 1) + (WIN_C - 1)
    tb = jnp.where(col_mask[None, None], rpb[:, :, dc].astype(f32), NEG)
    dw = np.arange(WIN_R)[:, None] + np.arange(WIN_R)[None, :]
    t = tb[:, dw]
    t = t.transpose(1, 0, 3, 2, 4).reshape(WIN_R, C_HEADS * GRID_W, WIN_R * GRID_W)
    return t


def _rope_slabs():
    inv = 1.0 / (ROPE_THETA ** (jnp.arange(0, ROPE_DIM, 2, dtype=f32) / ROPE_DIM))
    ang = jnp.arange(SEQ, dtype=f32)[:, None] * inv[None, :]
    cos, sin = jnp.cos(ang), jnp.sin(ang)
    ones = jnp.ones((SEQ, NOPE_DIM), f32)
    zpad = jnp.zeros((SEQ, LANES - NOPE_DIM - ROPE_DIM), f32)
    cosq = jnp.concatenate([ones, cos, cos, zpad], axis=1)
    sinq = jnp.concatenate([0.0 * ones, sin, sin, zpad], axis=1)
    return cosq, sinq


def _pack_layer(i, norm1_g, w_in, a_vnorm_g, a_ws, a_bs, b_conv, c_rpb, d_qnorm_g, d_kvnorm_g,
                d_w_uq, d_w_ukv, out_norm_g, w_o, norm2_g, router_w, e_w_gate, e_w_up, e_w_down):
    half = ROPE_DIM // 2
    w = w_in[i]
    kr0 = OFF_D + Q_RANK + KV_RANK
    kr = w[:, kr0:kr0 + ROPE_DIM]
    z64 = jnp.zeros((D_MODEL, NOPE_DIM), f32)
    z32 = jnp.zeros((D_MODEL, LANES - NOPE_DIM - ROPE_DIM), f32)
    kr_slab = jnp.concatenate([z64, kr, z32], axis=1)
    kr_rot = jnp.concatenate([z64, -kr[:, half:], kr[:, :half], z32], axis=1)
    w_pack = jnp.concatenate([w[:, :kr0], kr_slab, kr_rot], axis=1).astype(bf16)

    uq = d_w_uq[i].reshape(Q_RANK, D_HEADS, NOPE_DIM + ROPE_DIM)
    nope, rope = uq[..., :NOPE_DIM], uq[..., NOPE_DIM:]
    zq = jnp.zeros((Q_RANK, D_HEADS, LANES - NOPE_DIM - ROPE_DIM), f32)
    w_uq = jnp.concatenate([nope, rope, zq], axis=-1).reshape(Q_RANK, D_HEADS * LANES)
    w_uq_rot = jnp.concatenate([0.0 * nope, -rope[..., half:], rope[..., :half], zq],
                               axis=-1).reshape(Q_RANK, D_HEADS * LANES)
    ukv = d_w_ukv[i].reshape(KV_RANK, D_HEADS, NOPE_DIM + V_DIM)
    w_k = jnp.concatenate([ukv[..., :NOPE_DIM], jnp.zeros((KV_RANK, D_HEADS, LANES - NOPE_DIM), f32)],
                          axis=-1).reshape(KV_RANK, D_HEADS * LANES)
    w_v = ukv[..., NOPE_DIM:].reshape(KV_RANK, D_HEADS * V_DIM)
    rw = jnp.concatenate([router_w[i], jnp.zeros((D_MODEL, LANES - N_EXPERTS), f32)], axis=1)
    return {
        "g1": norm1_g[i].reshape(1, D_MODEL),
        "w_in": w_pack,
        "a_vg": a_vnorm_g[i].reshape(1, GROUP_W),
        "a_ws": a_ws[i].astype(bf16),
        "a_bias": jnp.repeat(a_bs[i].T, HEAD_DIM, axis=1),
        "b_conv": b_conv[i],
        "c_bias": _natten_bias_table(c_rpb[i]),
        "d_qg": d_qnorm_g[i].reshape(1, Q_RANK),
        "d_kvg": d_kvnorm_g[i].reshape(1, KV_RANK),
        "w_uq": w_uq.astype(bf16),
        "w_uq_rot": w_uq_rot.astype(bf16),
        "w_k": w_k.astype(bf16),
        "w_v": w_v.astype(bf16),
        "og": out_norm_g[i].reshape(1, D_MODEL),
        "w_o": w_o[i].astype(bf16),
        "g2": norm2_g[i].reshape(1, D_MODEL),
        "rw": rw.astype(bf16),
        "rwt": router_w[i].T.astype(bf16),
        "layer": i,
        "e_wg": e_w_gate,
        "e_wu": e_w_up,
        "e_wd": e_w_down,
    }


def _moe(xn, h2, aff, lw, tri):
    x = xn
    for row0, nrows in ((0, BATCH * SEQ), (BATCH * SEQ, DEC_BATCH * SEQ)):
        cap = CAPACITY_FACTOR * nrows // N_EXPERTS
        posm, base = _route(aff, row0, nrows, tri)
        first = base[:, :, 0].astype(jnp.int32)
        count = jnp.diff(first, axis=0, append=jnp.full((1, N_EXPERTS), cap, jnp.int32))
        xe = _dispatch(h2, posm, first, count, row0, nrows)
        ye = _ffn(xe, lw, nrows)
        x = _combine(x, ye, posm, first, count, row0, nrows)
    return x


def kernel(x_prompt, x_sample, norm1_g, w_in, a_vnorm_g, a_ws, a_bs, b_conv, c_rpb, d_qnorm_g, d_kvnorm_g, d_w_uq, d_w_ukv, out_norm_g, w_o, norm2_g, router_w, e_w_gate, e_w_up, e_w_down, final_norm_g):
    n_p = BATCH * SEQ
    n_s = DEC_BATCH * SEQ
    x = _stack_rows(x_prompt.reshape(n_p, D_MODEL), x_sample.reshape(n_s, D_MODEL))
    cosq, sinq = _rope_slabs()
    tri = jnp.asarray(np.triu(np.ones((TT, TT), np.float32), 1), bf16)
    e_w_gate, e_w_up, e_w_down = (w.astype(bf16) for w in (e_w_gate, e_w_up, e_w_down))
    for i in range(DEPTH):
        lw = _pack_layer(i, norm1_g, w_in, a_vnorm_g, a_ws, a_bs, b_conv, c_rpb, d_qnorm_g, d_kvnorm_g,
                         d_w_uq, d_w_ukv, out_norm_g, w_o, norm2_g, router_w, e_w_gate, e_w_up, e_w_down)
        ya, yb, zc, qt, k4, vt = _inproj(x, lw, cosq, sinq)
        yc = _natten(zc, lw["c_bias"])
        yd = _mla(qt, k4, vt)
        xn, h2, aff = _outproj(ya, yb, yc, yd, x, lw)
        x = _moe(xn, h2, aff, lw, tri)
    g = final_norm_g.reshape(1, D_MODEL)
    y_p = _final_norm(x, g, 0, n_p).reshape(BATCH, SEQ, D_MODEL)
    y_s = _final_norm(x, g, n_p, n_s).reshape(DEC_BATCH, SEQ, D_MODEL)
    return (y_p, y_s)
```

```python
import functools
import math

import numpy as np
import jax
import jax.numpy as jnp
from jax import lax
from jax.experimental import pallas as pl
from jax.experimental.pallas import tpu as pltpu

D_MODEL = 1024
BATCH = 4
SEQ = 8192
DEPTH = 4
DEC_BATCH = 2
GROUP_W = 256
HEAD_DIM = 64
A_HEADS = 4
CHUNK = 128
C_HEADS = 4
GRID_W = 64
WIN_R = 8
WIN_C = 16
D_HEADS = 4
Q_RANK = 256
KV_RANK = 128
NOPE_DIM = 64
ROPE_DIM = 32
V_DIM = 64
ROPE_THETA = 10000.0
N_EXPERTS = 16
D_EXPERT = 1024
CAPACITY_FACTOR = 2
EPS = 1e-6

NSEQ = BATCH + DEC_BATCH
T_ALL = NSEQ * SEQ
ROWS = SEQ // GRID_W
LANES = 128
NEG = -1e30

A_COLS = 2 * GROUP_W
B_COLS = 3 * GROUP_W
C_COLS = 3 * GROUP_W
OFF_B = A_COLS
OFF_C = OFF_B + B_COLS
OFF_D = OFF_C + C_COLS
D_PACK = Q_RANK + KV_RANK + 2 * LANES
N_IN_PACK = OFF_D + D_PACK

TM = 1024
TQ = 256
TK = 512
L_ROWS = 16
MLA_UNROLL = 4
MLA_LAG = 1
MLA_QSCALE = (NOPE_DIM + ROPE_DIM) ** -0.5 * math.log2(math.e)
ROWS_PER_STEP = 8
TME = 1024
TT = 256
ROUTE_TILES = 4
ROW_ALIGN = 16
DISP_WINDOWS = (80, 144, TT + ROW_ALIGN)
COMB_ALIGN = ROW_ALIGN
COMB_WINDOWS = (80, 144, TT + COMB_ALIGN)
VMEM_LIMIT = 56 * 1024 * 1024

f32 = jnp.float32
bf16 = jnp.bfloat16


def _rms(x, g):
    return x * lax.rsqrt(jnp.mean(x * x, axis=-1, keepdims=True) + EPS) * g


def _dot(a, b):
    return jnp.dot(a, b, preferred_element_type=f32)


def _dot_nt(a, b):
    return lax.dot_general(a, b, (((1,), (1,)), ((), ())), preferred_element_type=f32)


def _dot_tn(a, b):
    return lax.dot_general(a, b, (((0,), (0,)), ((), ())), preferred_element_type=f32)


def _inproj_kernel(x_ref, xp_ref, xn_ref, g1_ref, win_ref, avg_ref, aws_ref, abias_ref,
                   bconv_ref, qg_ref, kvg_ref, wuq_ref, wuqr_ref, wk_ref, wv_ref,
                   cos_ref, sin_ref,
                   ya_ref, yb_ref, zc_ref, qt_ref, k4_ref, vt_ref):
    i = pl.program_id(0)
    j = i % (SEQ // TM)
    g1 = g1_ref[...]
    h = _rms(x_ref[...], g1).astype(bf16)

    za = jax.nn.gelu(_dot(h, win_ref[:, 0:A_COLS]))
    u = za[:, :GROUP_W]
    v = _rms(za[:, GROUP_W:], avg_ref[...]).astype(bf16)
    lane = lax.broadcasted_iota(jnp.int32, (CHUNK, GROUP_W), 1)
    for c in range(TM // CHUNK):
        vc = v[c * CHUNK:(c + 1) * CHUNK, :]
        sv = abias_ref[...]
        for hd in range(A_HEADS):
            r = _dot(aws_ref[hd], vc)
            sv = sv + jnp.where((lane >= hd * HEAD_DIM) & (lane < (hd + 1) * HEAD_DIM), r, 0.0)
        ya_ref[c * CHUNK:(c + 1) * CHUNK, :] = u[c * CHUNK:(c + 1) * CHUNK, :] * sv

    zb = _dot(h, win_ref[:, OFF_B:OFF_B + B_COLS])
    bg = zb[:, :GROUP_W]
    y = zb[:, GROUP_W:2 * GROUP_W] * zb[:, 2 * GROUP_W:]
    xh = jnp.concatenate([xp_ref[...], xn_ref[...]], axis=0)
    hh = _rms(xh, g1).astype(bf16)
    zh = _dot(hh, win_ref[:, OFF_B + GROUP_W:OFF_B + B_COLS])
    yh = zh[:, :GROUP_W] * zh[:, GROUP_W:]
    y_before = jnp.where(j == 0, 0.0, yh[7:8, :])
    y_after = jnp.where(j == SEQ // TM - 1, 0.0, yh[8:9, :])
    row = lax.broadcasted_iota(jnp.int32, (TM, GROUP_W), 0)
    y_m1 = jnp.where(row == 0, y_before, pltpu.roll(y, 1, axis=0))
    y_p1 = jnp.where(row == TM - 1, y_after, pltpu.roll(y, TM - 1, axis=0))
    wc = bconv_ref[...]
    yb_ref[...] = bg * (wc[0:1, :] * y_m1 + wc[1:2, :] * y + wc[2:3, :] * y_p1)

    zc = _dot(h, win_ref[:, OFF_C:OFF_C + C_COLS])
    zc_ref[:, 0:GROUP_W] = (zc[:, 0:GROUP_W] * (HEAD_DIM ** -0.5)).astype(bf16)
    zc_ref[:, GROUP_W:] = zc[:, GROUP_W:].astype(bf16)

    zd = _dot(h, win_ref[:, OFF_D:OFF_D + D_PACK])
    cqn = _rms(zd[:, 0:Q_RANK], qg_ref[...]).astype(bf16)
    ckvn = _rms(zd[:, Q_RANK:Q_RANK + KV_RANK], kvg_ref[...]).astype(bf16)
    cos = cos_ref[...]
    sin = sin_ref[...]
    o = Q_RANK + KV_RANK
    kr = zd[:, o:o + LANES] * cos + zd[:, o + LANES:o + 2 * LANES] * sin
    q_pre = _dot(cqn, wuq_ref[...])
    q_rot = _dot(cqn, wuqr_ref[...])
    kn = _dot(ckvn, wk_ref[...])
    for hd in range(D_HEADS):
        sl = slice(hd * LANES, (hd + 1) * LANES)
        qt_ref[hd] = ((q_pre[:, sl] * cos + q_rot[:, sl] * sin) * MLA_QSCALE).T.astype(bf16)
        k4_ref[hd] = (kn[:, sl] + kr).astype(bf16)
    vt = _dot(ckvn, wv_ref[...]).T.astype(bf16)
    for hd in range(D_HEADS):
        vt_ref[hd] = vt[hd * V_DIM:(hd + 1) * V_DIM, :]


def _inproj(x, lw, cosq, sinq):
    nt = T_ALL // TM
    tps = SEQ // TM
    full = lambda shape: pl.BlockSpec(shape, lambda i: (0,) * len(shape))
    in_specs = [
        pl.BlockSpec((TM, D_MODEL), lambda i: (i, 0)),
        pl.BlockSpec((8, D_MODEL), lambda i: (jnp.maximum(i * (TM // 8) - 1, 0), 0)),
        pl.BlockSpec((8, D_MODEL), lambda i: (jnp.minimum((i + 1) * (TM // 8), T_ALL // 8 - 1), 0)),
        full((1, D_MODEL)),
        full((D_MODEL, N_IN_PACK)),
        full((1, GROUP_W)),
        full((A_HEADS, CHUNK, CHUNK)),
        full((CHUNK, GROUP_W)),
        full((3, GROUP_W)),
        full((1, Q_RANK)),
        full((1, KV_RANK)),
        full((Q_RANK, D_HEADS * LANES)),
        full((Q_RANK, D_HEADS * LANES)),
        full((KV_RANK, D_HEADS * LANES)),
        full((KV_RANK, D_HEADS * V_DIM)),
        pl.BlockSpec((TM, LANES), lambda i: (i % tps, 0)),
        pl.BlockSpec((TM, LANES), lambda i: (i % tps, 0)),
    ]
    out_specs = [
        pl.BlockSpec((TM, GROUP_W), lambda i: (i, 0)),
        pl.BlockSpec((TM, GROUP_W), lambda i: (i, 0)),
        pl.BlockSpec((TM, C_COLS), lambda i: (i, 0)),
        pl.BlockSpec((None, D_HEADS, LANES, TM), lambda i: (i // tps, 0, 0, i % tps)),
        pl.BlockSpec((None, D_HEADS, TM, LANES), lambda i: (i // tps, 0, i % tps, 0)),
        pl.BlockSpec((None, D_HEADS, V_DIM, TM), lambda i: (i // tps, 0, 0, i % tps)),
    ]
    out_shape = [
        jax.ShapeDtypeStruct((T_ALL, GROUP_W), f32),
        jax.ShapeDtypeStruct((T_ALL, GROUP_W), f32),
        jax.ShapeDtypeStruct((T_ALL, C_COLS), bf16),
        jax.ShapeDtypeStruct((NSEQ, D_HEADS, LANES, SEQ), bf16),
        jax.ShapeDtypeStruct((NSEQ, D_HEADS, SEQ, LANES), bf16),
        jax.ShapeDtypeStruct((NSEQ, D_HEADS, V_DIM, SEQ), bf16),
    ]
    return pl.pallas_call(
        _inproj_kernel, grid=(nt,), in_specs=in_specs, out_specs=out_specs, out_shape=out_shape,
        name="inproj",
        compiler_params=pltpu.CompilerParams(dimension_semantics=("parallel",),
                                             vmem_limit_bytes=VMEM_LIMIT),
    )(x, x, x, lw["g1"], lw["w_in"], lw["a_vg"], lw["a_ws"], lw["a_bias"], lw["b_conv"],
      lw["d_qg"], lw["d_kvg"], lw["w_uq"], lw["w_uq_rot"], lw["w_k"], lw["w_v"], cosq, sinq)


def _natten_kernel(q_ref, k_ref, v_ref, bias_ref, o_ref):
    jb = pl.program_id(1)
    lane = lax.broadcasted_iota(jnp.int32, (GRID_W, GROUP_W), 1)
    head_of_lane = lane // HEAD_DIM
    nkeys = WIN_R * GRID_W

    jobs = []
    for i in range(ROWS_PER_STEP):
        r = jb * ROWS_PER_STEP + i
        rs = jnp.clip(r - WIN_R // 2, 0, ROWS - WIN_R)
        d = rs - r + (WIN_R - 1)
        kstart = pl.multiple_of(rs * GRID_W, GRID_W)
        q = q_ref[i * GRID_W:(i + 1) * GRID_W, :]
        zero = jnp.zeros_like(q)
        qq = jnp.concatenate([jnp.where(head_of_lane == h, q, zero) for h in range(C_HEADS)], axis=0)
        jobs.append((_dot_nt(qq, k_ref[pl.ds(kstart, nkeys), :]), d, kstart))
    for i, (sc, d, kstart) in enumerate(jobs):
        sc = sc + bias_ref[d]
        m = jnp.max(sc, axis=-1, keepdims=True)
        p = jnp.exp(sc - m)
        l = jnp.sum(p, axis=-1, keepdims=True)
        o = _dot(p.astype(bf16), v_ref[pl.ds(kstart, nkeys), :]) / l
        out = jnp.zeros((GRID_W, GROUP_W), f32)
        for h in range(C_HEADS):
            out = jnp.where(head_of_lane == h, o[h * GRID_W:(h + 1) * GRID_W], out)
        o_ref[i * GRID_W:(i + 1) * GRID_W, :] = out


def _natten(zc, bias):
    nb = ROWS // ROWS_PER_STEP
    tq = ROWS_PER_STEP * GRID_W
    return pl.pallas_call(
        _natten_kernel, grid=(NSEQ, nb),
        in_specs=[
            pl.BlockSpec((tq, GROUP_W), lambda b, j: (b * nb + j, 0)),
            pl.BlockSpec((SEQ, GROUP_W), lambda b, j: (b, 1)),
            pl.BlockSpec((SEQ, GROUP_W), lambda b, j: (b, 2)),
            pl.BlockSpec((WIN_R, C_HEADS * GRID_W, WIN_R * GRID_W), lambda b, j: (0, 0, 0)),
        ],
        out_specs=pl.BlockSpec((tq, GROUP_W), lambda b, j: (b * nb + j, 0)),
        out_shape=jax.ShapeDtypeStruct((T_ALL, GROUP_W), f32),
        name="natten",
        compiler_params=pltpu.CompilerParams(dimension_semantics=("parallel", "arbitrary"),
                                             vmem_limit_bytes=VMEM_LIMIT),
    )(zc, zc, zc, bias)


def _mla_kernel(qt_ref, k_ref, vt_ref, o_ref, s_scr):
    nchunk = SEQ // TK
    heads = list(range(D_HEADS))
    qts = [qt_ref[h] for h in heads]

    def qk(slot, h, ci):
        ks = pl.multiple_of(ci * TK, TK)
        s_scr[slot, h] = _dot(k_ref[h, pl.ds(ks, TK), :], qts[h])

    ones = jnp.ones((L_ROWS, TK), bf16)

    def softmax_pv(slot, h, ci, stat):
        m, acc = stat
        ks = pl.multiple_of(ci * TK, TK)
        vt = jnp.concatenate([vt_ref[h, :, pl.ds(ks, TK)], ones], axis=0)
        s = s_scr[slot, h]
        m_new = jnp.maximum(m, jnp.max(s, axis=0, keepdims=True))
        alpha = jnp.exp2(m - m_new)
        p = jnp.exp2(s - m_new)
        acc = acc * alpha + _dot(vt, p.astype(bf16))
        return m_new, acc

    def half(cur, ci, stats, prefetch=True):
        out = []
        for h in heads:
            if prefetch:
                qk(1 - cur, h, ci + 1)
            if h >= MLA_LAG:
                out.append(softmax_pv(cur, h - MLA_LAG, ci, stats[h - MLA_LAG]))
        for h in range(D_HEADS - MLA_LAG, D_HEADS):
            out.append(softmax_pv(cur, h, ci, stats[h]))
        return tuple(out)

    def body(cp, stats):
        for u in range(MLA_UNROLL):
            stats = half(u % 2, MLA_UNROLL * cp + u, stats)
        return stats

    for h in heads:
        qk(0, h, 0)
    init = tuple((jnp.full((1, TQ), NEG, f32), jnp.zeros((V_DIM + L_ROWS, TQ), f32)) for _ in heads)
    res = lax.fori_loop(0, nchunk // MLA_UNROLL - 1, body, init)
    for u in range(MLA_UNROLL):
        res = half(u % 2, nchunk - MLA_UNROLL + u, res, prefetch=u < MLA_UNROLL - 1)
    o_ref[...] = jnp.concatenate([acc[:V_DIM] / acc[V_DIM:V_DIM + 1] for (_, acc) in res], axis=0).T


def _mla(qt, k4, vt):
    nq = SEQ // TQ
    return pl.pallas_call(
        _mla_kernel, grid=(NSEQ, nq),
        in_specs=[
            pl.BlockSpec((None, D_HEADS, LANES, TQ), lambda b, i: (b, 0, 0, i)),
            pl.BlockSpec((None, D_HEADS, SEQ, LANES), lambda b, i: (b, 0, 0, 0)),
            pl.BlockSpec((None, D_HEADS, V_DIM, SEQ), lambda b, i: (b, 0, 0, 0)),
        ],
        out_specs=pl.BlockSpec((TQ, GROUP_W), lambda b, i: (b * nq + i, 0)),
        out_shape=jax.ShapeDtypeStruct((T_ALL, GROUP_W), f32),
        scratch_shapes=[pltpu.VMEM((2, D_HEADS, TK, TQ), f32)],
        name="mla",
        compiler_params=pltpu.CompilerParams(
            dimension_semantics=("parallel", "arbitrary"),
            vmem_limit_bytes=VMEM_LIMIT),
    )(qt, k4, vt)


def _outproj_kernel(ya_ref, yb_ref, yc_ref, yd_ref, x_ref, og_ref, wo_ref, g2_ref, rwt_ref,
                    xn_ref, h2_ref, aff_ref):
    y = jnp.concatenate(
        [_rms(y_ref[...], og_ref[:, g * GROUP_W:(g + 1) * GROUP_W]).astype(bf16)
         for g, y_ref in enumerate((ya_ref, yb_ref, yc_ref, yd_ref))], axis=1)
    acc = x_ref[...] + _dot(y, wo_ref[...])
    xn_ref[...] = acc
    h2 = _rms(acc, g2_ref[...]).astype(bf16)
    h2_ref[...] = h2
    logits = _dot_nt(rwt_ref[...], h2)
    e = jnp.exp(logits - jnp.max(logits, axis=0, keepdims=True))
    aff_ref[...] = e / jnp.sum(e, axis=0, keepdims=True)


def _outproj(ya, yb, yc, yd, x, lw):
    nt = T_ALL // TM
    full = lambda shape: pl.BlockSpec(shape, lambda i: (0,) * len(shape))
    yspec = pl.BlockSpec((TM, GROUP_W), lambda i: (i, 0))
    xspec = pl.BlockSpec((TM, D_MODEL), lambda i: (i, 0))
    return pl.pallas_call(
        _outproj_kernel, grid=(nt,),
        in_specs=[yspec, yspec, yspec, yspec, xspec, full((1, D_MODEL)), full((D_MODEL, D_MODEL)),
                  full((1, D_MODEL)), full((N_EXPERTS, D_MODEL))],
        out_specs=[xspec, xspec, pl.BlockSpec((N_EXPERTS, TM), lambda i: (0, i))],
        out_shape=[jax.ShapeDtypeStruct((T_ALL, D_MODEL), f32),
                   jax.ShapeDtypeStruct((T_ALL, D_MODEL), bf16),
                   jax.ShapeDtypeStruct((N_EXPERTS, T_ALL), f32)],
        name="outproj",
        compiler_params=pltpu.CompilerParams(dimension_semantics=("parallel",),
                                             vmem_limit_bytes=VMEM_LIMIT),
    )(ya, yb, yc, yd, x, lw["og"], lw["w_o"], lw["g2"], lw["rwt"])


def _route_kernel(aff_ref, tri_ref, posm_ref, base_ref, thr_scr, need_scr, runeq_scr, runsel_scr, *, cap):
    j = pl.program_id(0)

    @pl.when(j == 0)
    def _():
        def refine(i, thr):
            cand = thr | jnp.left_shift(jnp.int32(1), 30 - i)
            bits = pltpu.bitcast(aff_ref[...], jnp.int32)
            cnt = jnp.sum(jnp.where(bits >= cand, 1.0, 0.0), axis=1, keepdims=True)
            return jnp.where(cnt >= cap, cand, thr)

        thr = lax.fori_loop(0, 31, refine, jnp.zeros((N_EXPERTS, 1), jnp.int32))
        bits = pltpu.bitcast(aff_ref[...], jnp.int32)
        above = jnp.sum(jnp.where(bits > thr, 1.0, 0.0), axis=1, keepdims=True)
        thr_scr[...] = jnp.broadcast_to(thr, (N_EXPERTS, LANES))
        need_scr[...] = jnp.broadcast_to(cap - above, (N_EXPERTS, LANES))
        runeq_scr[...] = jnp.zeros((N_EXPERTS, LANES), f32)
        runsel_scr[...] = jnp.zeros((N_EXPERTS, LANES), f32)

    thr = thr_scr[:, 0:1]
    need = need_scr[:, 0:1]
    tiles = []
    for u in range(ROUTE_TILES):
        t0 = pl.multiple_of((j * ROUTE_TILES + u) * TT, TT)
        bits = pltpu.bitcast(aff_ref[:, pl.ds(t0, TT)], jnp.int32)
        eq = bits == thr
        eq_f = jnp.where(eq, 1.0, 0.0)
        tiles.append((bits, eq, eq_f, _dot(eq_f.astype(bf16), tri_ref[...])))
    sels = []
    runeq = runeq_scr[:, 0:1]
    for bits, eq, eq_f, eq_before in tiles:
        sel = (bits > thr) | (eq & (runeq + eq_before < need))
        sel_f = jnp.where(sel, 1.0, 0.0)
        sels.append((sel, sel_f, _dot(sel_f.astype(bf16), tri_ref[...])))
        runeq = runeq + jnp.sum(eq_f, axis=1, keepdims=True)
    runeq_scr[...] = jnp.broadcast_to(runeq, (N_EXPERTS, LANES))
    base = runsel_scr[...]
    for u, (sel, sel_f, sel_before) in enumerate(sels):
        posm_ref[:, u * TT:(u + 1) * TT] = jnp.where(sel, base[:, 0:1] + sel_before, -1.0)
        base_ref[u] = base
        base = base + jnp.sum(sel_f, axis=1, keepdims=True)
    runsel_scr[...] = base


def _route(aff, row0, nrows, tri):
    cap = CAPACITY_FACTOR * nrows // N_EXPERTS
    nt = nrows // TT
    blk0 = row0 // nrows
    vec = pltpu.VMEM((N_EXPERTS, LANES), f32)
    return pl.pallas_call(
        functools.partial(_route_kernel, cap=cap), grid=(nt // ROUTE_TILES,),
        in_specs=[pl.BlockSpec((N_EXPERTS, nrows), lambda j: (0, blk0)),
                  pl.BlockSpec((TT, TT), lambda j: (0, 0))],
        out_specs=[pl.BlockSpec((N_EXPERTS, ROUTE_TILES * TT), lambda j: (0, j)),
                   pl.BlockSpec((ROUTE_TILES, N_EXPERTS, LANES), lambda j: (j, 0, 0))],
        out_shape=[jax.ShapeDtypeStruct((N_EXPERTS, nrows), f32),
                   jax.ShapeDtypeStruct((nt, N_EXPERTS, LANES), f32)],
        scratch_shapes=[pltpu.VMEM((N_EXPERTS, LANES), jnp.int32), vec, vec, vec],
        name="route",
        compiler_params=pltpu.CompilerParams(dimension_semantics=("arbitrary",)),
    )(aff, tri)


def _window_class(count, windows, align):
    longest = jnp.max(count, axis=1) + (align - 1)
    return sum((longest > w).astype(jnp.int32) for w in windows[:-1])


def _dispatch_kernel(big_ref, first_ref, posm_ref, h2_ref, xe_hbm, stage, sem, carry, *, cap, ntiles):
    j = pl.program_id(0)
    slot = j % 2

    @pl.when(j == 0)
    def _():
        carry[...] = jnp.zeros_like(carry)
        pad = DISP_WINDOWS[-1]
        stage[1, 0:pad, :] = jnp.zeros((pad, D_MODEL), bf16)
        pads = [pltpu.make_async_copy(stage.at[1, pl.ds(0, pad)], xe_hbm.at[e, pl.ds(cap, pad)], sem.at[1])
                for e in range(N_EXPERTS)]
        for p in pads:
            p.start()
        for p in pads:
            p.wait()

    def run_start(jj, e):
        return first_ref[jj * N_EXPERTS + e]

    def window_copy(jj, e, sl, win):
        a = pl.multiple_of(run_start(jj, e) // ROW_ALIGN * ROW_ALIGN, ROW_ALIGN)
        return pltpu.make_async_copy(stage.at[sl, pl.ds(e * win, win)], xe_hbm.at[e, pl.ds(a, win)], sem.at[sl])

    def for_tile_window(jj, fn):
        for flag, win in enumerate(DISP_WINDOWS):
            @pl.when(big_ref[jj] == flag)
            def _():
                fn(win)

    def wait_all(jj, sl):
        def go(win):
            for e in range(N_EXPERTS):
                window_copy(jj, e, sl, win).wait()
        for_tile_window(jj, go)

    def compact(win):
        row = lax.broadcasted_iota(jnp.int32, (win, TT), 0)
        starts = [run_start(j, e) // ROW_ALIGN * ROW_ALIGN for e in range(N_EXPERTS)]
        onehot = jnp.concatenate(
            [jnp.where((row + starts[e]).astype(f32) == posm_ref[e:e + 1, :], 1.0, 0.0)
             for e in range(N_EXPERTS)], axis=0).astype(bf16)
        rows = _dot(onehot, h2_ref[...])
        head = lax.broadcasted_iota(jnp.int32, (ROW_ALIGN, D_MODEL), 0)
        for e in range(N_EXPERTS):
            lead = run_start(j, e) - starts[e]
            w = rows[e * win:(e + 1) * win]
            top = w[:ROW_ALIGN] + jnp.where(head < lead, carry[e], 0.0)
            stage[slot, e * win:e * win + ROW_ALIGN, :] = top.astype(bf16)
            stage[slot, e * win + ROW_ALIGN:(e + 1) * win, :] = w[ROW_ALIGN:].astype(bf16)
        for e in range(N_EXPERTS):
            end = jnp.where(j + 1 < ntiles, run_start(jnp.minimum(j + 1, ntiles - 1), e), cap)
            off = jnp.minimum(end // ROW_ALIGN * ROW_ALIGN - starts[e], win - ROW_ALIGN)
            off = pl.multiple_of(off, ROW_ALIGN)
            carry[e] = stage[slot, pl.ds(e * win + off, ROW_ALIGN), :].astype(f32)

    for_tile_window(j, compact)

    @pl.when(j > 0)
    def _():
        wait_all(j - 1, 1 - slot)

    def start_all(win):
        for e in range(N_EXPERTS):
            window_copy(j, e, slot, win).start()
    for_tile_window(j, start_all)

    @pl.when(j == ntiles - 1)
    def _():
        wait_all(j, slot)


def _dispatch(h2, posm, first, count, row0, nrows):
    cap = CAPACITY_FACTOR * nrows // N_EXPERTS
    nt = nrows // TT
    blk0 = row0 // TT
    big = _window_class(count, DISP_WINDOWS, ROW_ALIGN)
    return pl.pallas_call(
        functools.partial(_dispatch_kernel, cap=cap, ntiles=nt),
        grid_spec=pltpu.PrefetchScalarGridSpec(
            num_scalar_prefetch=2, grid=(nt,),
            in_specs=[pl.BlockSpec((N_EXPERTS, TT), lambda j, *_: (0, j)),
                      pl.BlockSpec((TT, D_MODEL), lambda j, *_: (blk0 + j, 0))],
            out_specs=pl.BlockSpec(memory_space=pl.ANY),
            scratch_shapes=[pltpu.VMEM((2, N_EXPERTS * DISP_WINDOWS[-1], D_MODEL), bf16),
                            pltpu.SemaphoreType.DMA((2,)),
                            pltpu.VMEM((N_EXPERTS, ROW_ALIGN, D_MODEL), f32)],
        ),
        out_shape=jax.ShapeDtypeStruct((N_EXPERTS, cap + DISP_WINDOWS[-1], D_MODEL), bf16),
        name="dispatch",
        compiler_params=pltpu.CompilerParams(dimension_semantics=("arbitrary",), vmem_limit_bytes=VMEM_LIMIT),
    )(big, first.reshape(-1), posm, h2)


def _ffn_kernel(xe_ref, rw_ref, wg_ref, wu_ref, wd_ref, ye_ref):
    e = pl.program_id(0)
    x = xe_ref[...]
    logits = _dot(x, rw_ref[...])
    lane = lax.broadcasted_iota(jnp.int32, logits.shape, 1)
    logits = jnp.where(lane < N_EXPERTS, logits, NEG)
    ex = jnp.exp(logits - jnp.max(logits, axis=-1, keepdims=True))
    gate = jnp.sum(jnp.where(lane == e, ex, 0.0), axis=-1, keepdims=True) / jnp.sum(ex, axis=-1, keepdims=True)
    hg = _dot(x, wg_ref[...].astype(bf16))
    hu = _dot(x, wu_ref[...].astype(bf16))
    h = (hg * (1.0 / (1.0 + jnp.exp(-hg))) * hu).astype(bf16)
    ye_ref[...] = (_dot(h, wd_ref[...].astype(bf16)) * gate).astype(bf16)


def _ffn(xe, lw, nrows):
    cap = CAPACITY_FACTOR * nrows // N_EXPERTS
    layer = lw["layer"]
    wspec = pl.BlockSpec((None, None, D_MODEL, D_EXPERT), lambda e, t: (layer, e, 0, 0))
    return pl.pallas_call(
        _ffn_kernel, grid=(N_EXPERTS, cap // TME),
        in_specs=[
            pl.BlockSpec((None, TME, D_MODEL), lambda e, t: (e, t, 0)),
            pl.BlockSpec((D_MODEL, LANES), lambda e, t: (0, 0)),
            wspec, wspec,
            pl.BlockSpec((None, None, D_EXPERT, D_MODEL), lambda e, t: (layer, e, 0, 0)),
        ],
        out_specs=pl.BlockSpec((None, TME, D_MODEL), lambda e, t: (e, t, 0)),
        out_shape=jax.ShapeDtypeStruct((N_EXPERTS, cap, D_MODEL), bf16),
        name="expert_ffn",
        compiler_params=pltpu.CompilerParams(dimension_semantics=("parallel", "arbitrary"),
                                             vmem_limit_bytes=VMEM_LIMIT),
    )(xe, lw["rw"], lw["e_wg"], lw["e_wu"], lw["e_wd"])


def _combine_kernel(big_ref, first_ref, posm_ref, x_ref, ye_hbm, o_ref, buf, sem, *, cap, ntiles):
    j = pl.program_id(0)

    def window_start(jj, e, win):
        return jnp.minimum(first_ref[jj * N_EXPERTS + e] // COMB_ALIGN * COMB_ALIGN, cap - win)

    def window_copy(jj, e, slot, win):
        a = pl.multiple_of(window_start(jj, e, win), COMB_ALIGN)
        return pltpu.make_async_copy(ye_hbm.at[e, pl.ds(a, win)], buf.at[slot, pl.ds(e * win, win)],
                                     sem.at[slot])

    def for_tile_window(jj, fn):
        for flag, win in enumerate(COMB_WINDOWS):
            @pl.when(big_ref[jj] == flag)
            def _():
                fn(win)

    def start_all(jj, slot):
        def go(win):
            for e in range(N_EXPERTS):
                window_copy(jj, e, slot, win).start()
        for_tile_window(jj, go)

    @pl.when(j == 0)
    def _():
        start_all(0, 0)

    slot = j % 2

    def wait_all(win):
        for e in range(N_EXPERTS):
            window_copy(j, e, slot, win).wait()
    for_tile_window(j, wait_all)

    @pl.when(j + 1 < ntiles)
    def _():
        start_all(j + 1, 1 - slot)

    def add_rows(win):
        row = lax.broadcasted_iota(jnp.int32, (win, TT), 0)
        onehot = jnp.concatenate(
            [jnp.where((row + window_start(j, e, win)).astype(f32) == posm_ref[e:e + 1, :], 1.0, 0.0)
             for e in range(N_EXPERTS)], axis=0).astype(bf16)
        o_ref[...] = x_ref[...] + _dot_tn(onehot, buf[slot, 0:N_EXPERTS * win, :])
    for_tile_window(j, add_rows)


def _combine(x, ye, posm, first, count, row0, nrows):
    cap = CAPACITY_FACTOR * nrows // N_EXPERTS
    nt = nrows // TT
    blk0 = row0 // TT
    big = _window_class(count, COMB_WINDOWS, COMB_ALIGN)
    return pl.pallas_call(
        functools.partial(_combine_kernel, cap=cap, ntiles=nt),
        grid_spec=pltpu.PrefetchScalarGridSpec(
            num_scalar_prefetch=2, grid=(nt,),
            in_specs=[pl.BlockSpec((N_EXPERTS, TT), lambda j, *_: (0, j)),
                      pl.BlockSpec((TT, D_MODEL), lambda j, *_: (blk0 + j, 0)),
                      pl.BlockSpec(memory_space=pl.ANY)],
            out_specs=pl.BlockSpec((TT, D_MODEL), lambda j, *_: (blk0 + j, 0)),
            scratch_shapes=[pltpu.VMEM((2, N_EXPERTS * COMB_WINDOWS[-1], D_MODEL), bf16),
                            pltpu.SemaphoreType.DMA((2,))],
        ),
        out_shape=jax.ShapeDtypeStruct(x.shape, f32),
        input_output_aliases={3: 0},
        name="combine",
        compiler_params=pltpu.CompilerParams(dimension_semantics=("arbitrary",), vmem_limit_bytes=VMEM_LIMIT),
    )(big, first.reshape(-1), posm, x, ye)


def _stack_kernel(a_ref, b_ref, o_ref, *, na_blocks):
    o_ref[...] = jnp.where(pl.program_id(0) < na_blocks, a_ref[...], b_ref[...])


def _stack_rows(a, b):
    na, nb = a.shape[0] // TM, b.shape[0] // TM
    return pl.pallas_call(
        functools.partial(_stack_kernel, na_blocks=na), grid=(na + nb,),
        in_specs=[pl.BlockSpec((TM, D_MODEL), lambda i: (jnp.minimum(i, na - 1), 0)),
                  pl.BlockSpec((TM, D_MODEL), lambda i: (jnp.maximum(i - na, 0), 0))],
        out_specs=pl.BlockSpec((TM, D_MODEL), lambda i: (i, 0)),
        out_shape=jax.ShapeDtypeStruct((a.shape[0] + b.shape[0], D_MODEL), a.dtype),
        name="stack_rows",
        compiler_params=pltpu.CompilerParams(dimension_semantics=("arbitrary",)),
    )(a, b)


def _final_kernel(x_ref, g_ref, o_ref):
    o_ref[...] = _rms(x_ref[...], g_ref[...])


def _final_norm(x, g, row0, nrows):
    off = row0 // TM
    return pl.pallas_call(
        _final_kernel, grid=(nrows // TM,),
        in_specs=[pl.BlockSpec((TM, D_MODEL), lambda i: (i + off, 0)),
                  pl.BlockSpec((1, D_MODEL), lambda i: (0, 0))],
        out_specs=pl.BlockSpec((TM, D_MODEL), lambda i: (i, 0)),
        out_shape=jax.ShapeDtypeStruct((nrows, D_MODEL), f32),
        name="final_norm",
        compiler_params=pltpu.CompilerParams(dimension_semantics=("parallel",)),
    )(x, g)


def _natten_bias_table(rpb):
    c = np.arange(GRID_W)
    cs = np.clip(c - WIN_C // 2, 0, GRID_W - WIN_C)
    kc = np.arange(GRID_W)
    col_mask = (kc[None, :] >= cs[:, None]) & (kc[None, :] < cs[:, None] + WIN_C)
    dc = np.clip(kc[None, :] - c[:, None], -(WIN_C - 1), WIN_C - 1) + (WIN_C - 1)
    tb = jnp.where(col_mask[None, None], rpb[:, :, dc].astype(f32), NEG)
    dw = np.arange(WIN_R)[:, None] + np.arange(WIN_R)[None, :]
    t = tb[:, dw]
    t = t.transpose(1, 0, 3, 2, 4).reshape(WIN_R, C_HEADS * GRID_W, WIN_R * GRID_W)
    return t


def _rope_slabs():
    inv = 1.0 / (ROPE_THETA ** (jnp.arange(0, ROPE_DIM, 2, dtype=f32) / ROPE_DIM))
    ang = jnp.arange(SEQ, dtype=f32)[:, None] * inv[None, :]
    cos, sin = jnp.cos(ang), jnp.sin(ang)
    ones = jnp.ones((SEQ, NOPE_DIM), f32)
    zpad = jnp.zeros((SEQ, LANES - NOPE_DIM - ROPE_DIM), f32)
    cosq = jnp.concatenate([ones, cos, cos, zpad], axis=1)
    sinq = jnp.concatenate([0.0 * ones, sin, sin, zpad], axis=1)
    return cosq, sinq


def _pack_layer(i, norm1_g, w_in, a_vnorm_g, a_ws, a_bs, b_conv, c_rpb, d_qnorm_g, d_kvnorm_g,
                d_w_uq, d_w_ukv, out_norm_g, w_o, norm2_g, router_w, e_w_gate, e_w_up, e_w_down):
    half = ROPE_DIM // 2
    w = w_in[i]
    kr0 = OFF_D + Q_RANK + KV_RANK
    kr = w[:, kr0:kr0 + ROPE_DIM]
    z64 = jnp.zeros((D_MODEL, NOPE_DIM), f32)
    z32 = jnp.zeros((D_MODEL, LANES - NOPE_DIM - ROPE_DIM), f32)
    kr_slab = jnp.concatenate([z64, kr, z32], axis=1)
    kr_rot = jnp.concatenate([z64, -kr[:, half:], kr[:, :half], z32], axis=1)
    w_pack = jnp.concatenate([w[:, :kr0], kr_slab, kr_rot], axis=1).astype(bf16)

    uq = d_w_uq[i].reshape(Q_RANK, D_HEADS, NOPE_DIM + ROPE_DIM)
    nope, rope = uq[..., :NOPE_DIM], uq[..., NOPE_DIM:]
    zq = jnp.zeros((Q_RANK, D_HEADS, LANES - NOPE_DIM - ROPE_DIM), f32)
    w_uq = jnp.concatenate([nope, rope, zq], axis=-1).reshape(Q_RANK, D_HEADS * LANES)
    w_uq_rot = jnp.concatenate([0.0 * nope, -rope[..., half:], rope[..., :half], zq],
                               axis=-1).reshape(Q_RANK, D_HEADS * LANES)
    ukv = d_w_ukv[i].reshape(KV_RANK, D_HEADS, NOPE_DIM + V_DIM)
    w_k = jnp.concatenate([ukv[..., :NOPE_DIM], jnp.zeros((KV_RANK, D_HEADS, LANES - NOPE_DIM), f32)],
                          axis=-1).reshape(KV_RANK, D_HEADS * LANES)
    w_v = ukv[..., NOPE_DIM:].reshape(KV_RANK, D_HEADS * V_DIM)
    rw = jnp.concatenate([router_w[i], jnp.zeros((D_MODEL, LANES - N_EXPERTS), f32)], axis=1)
    return {
        "g1": norm1_g[i].reshape(1, D_MODEL),
        "w_in": w_pack,
        "a_vg": a_vnorm_g[i].reshape(1, GROUP_W),
        "a_ws": a_ws[i].astype(bf16),
        "a_bias": jnp.repeat(a_bs[i].T, HEAD_DIM, axis=1),
        "b_conv": b_conv[i],
        "c_bias": _natten_bias_table(c_rpb[i]),
        "d_qg": d_qnorm_g[i].reshape(1, Q_RANK),
        "d_kvg": d_kvnorm_g[i].reshape(1, KV_RANK),
        "w_uq": w_uq.astype(bf16),
        "w_uq_rot": w_uq_rot.astype(bf16),
        "w_k": w_k.astype(bf16),
        "w_v": w_v.astype(bf16),
        "og": out_norm_g[i].reshape(1, D_MODEL),
        "w_o": w_o[i].astype(bf16),
        "g2": norm2_g[i].reshape(1, D_MODEL),
        "rw": rw.astype(bf16),
        "rwt": router_w[i].T.astype(bf16),
        "layer": i,
        "e_wg": e_w_gate,
        "e_wu": e_w_up,
        "e_wd": e_w_down,
    }


def _moe(xn, h2, aff, lw, tri):
    x = xn
    for row0, nrows in ((0, BATCH * SEQ), (BATCH * SEQ, DEC_BATCH * SEQ)):
        cap = CAPACITY_FACTOR * nrows // N_EXPERTS
        posm, base = _route(aff, row0, nrows, tri)
        first = base[:, :, 0].astype(jnp.int32)
        count = jnp.diff(first, axis=0, append=jnp.full((1, N_EXPERTS), cap, jnp.int32))
        xe = _dispatch(h2, posm, first, count, row0, nrows)
        ye = _ffn(xe, lw, nrows)
        x = _combine(x, ye, posm, first, count, row0, nrows)
    return x


def kernel(x_prompt, x_sample, norm1_g, w_in, a_vnorm_g, a_ws, a_bs, b_conv, c_rpb, d_qnorm_g, d_kvnorm_g, d_w_uq, d_w_ukv, out_norm_g, w_o, norm2_g, router_w, e_w_gate, e_w_up, e_w_down, final_norm_g):
    n_p = BATCH * SEQ
    n_s = DEC_BATCH * SEQ
    x = _stack_rows(x_prompt.reshape(n_p, D_MODEL), x_sample.reshape(n_s, D_MODEL))
    cosq, sinq = _rope_slabs()
    tri = jnp.asarray(np.triu(np.ones((TT, TT), np.float32), 1), bf16)
    for i in range(DEPTH):
        lw = _pack_layer(i, norm1_g, w_in, a_vnorm_g, a_ws, a_bs, b_conv, c_rpb, d_qnorm_g, d_kvnorm_g,
                         d_w_uq, d_w_ukv, out_norm_g, w_o, norm2_g, router_w, e_w_gate, e_w_up, e_w_down)
        ya, yb, zc, qt, k4, vt = _inproj(x, lw, cosq, sinq)
        yc = _natten(zc, lw["c_bias"])
        yd = _mla(qt, k4, vt)
        xn, h2, aff = _outproj(ya, yb, yc, yd, x, lw)
        x = _moe(xn, h2, aff, lw, tri)
    g = final_norm_g.reshape(1, D_MODEL)
    y_p = _final_norm(x, g, 0, n_p).reshape(BATCH, SEQ, D_MODEL)
    y_s = _final_norm(x, g, n_p, n_s).reshape(DEC_BATCH, SEQ, D_MODEL)
    return (y_p, y_s)
```

```python
import functools
import math

import numpy as np
import jax
import jax.numpy as jnp
from jax import lax
from jax.experimental import pallas as pl
from jax.experimental.pallas import tpu as pltpu

D_MODEL = 1024
BATCH = 4
SEQ = 8192
DEPTH = 4
DEC_BATCH = 2
GROUP_W = 256
HEAD_DIM = 64
A_HEADS = 4
CHUNK = 128
C_HEADS = 4
GRID_W = 64
WIN_R = 8
WIN_C = 16
D_HEADS = 4
Q_RANK = 256
KV_RANK = 128
NOPE_DIM = 64
ROPE_DIM = 32
V_DIM = 64
ROPE_THETA = 10000.0
N_EXPERTS = 16
D_EXPERT = 1024
CAPACITY_FACTOR = 2
EPS = 1e-6

NSEQ = BATCH + DEC_BATCH
T_ALL = NSEQ * SEQ
ROWS = SEQ // GRID_W
LANES = 128
NEG = -1e30

A_COLS = 2 * GROUP_W
B_COLS = 3 * GROUP_W
C_COLS = 3 * GROUP_W
OFF_B = A_COLS
OFF_C = OFF_B + B_COLS
OFF_D = OFF_C + C_COLS
D_PACK = Q_RANK + KV_RANK + 2 * LANES
N_IN_PACK = OFF_D + D_PACK

TM = 1024
TQ = 256
TK = 512
L_ROWS = 16
MLA_UNROLL = 4
MLA_LAG = 1
MLA_QSCALE = (NOPE_DIM + ROPE_DIM) ** -0.5 * math.log2(math.e)
ROWS_PER_STEP = 8
TME = 1024
TT = 256
ROUTE_TILES = 4
ROW_ALIGN = 16
DISP_WINDOWS = (80, 144, TT + ROW_ALIGN)
COMB_ALIGN = ROW_ALIGN
COMB_WINDOWS = (80, 144, TT + COMB_ALIGN)
COMB_BUFS = 3
VMEM_LIMIT = 56 * 1024 * 1024

f32 = jnp.float32
bf16 = jnp.bfloat16


def _rms(x, g):
    return x * lax.rsqrt(jnp.mean(x * x, axis=-1, keepdims=True) + EPS) * g


def _dot(a, b):
    return jnp.dot(a, b, preferred_element_type=f32)


def _dot_nt(a, b):
    return lax.dot_general(a, b, (((1,), (1,)), ((), ())), preferred_element_type=f32)


def _dot_tn(a, b):
    return lax.dot_general(a, b, (((0,), (0,)), ((), ())), preferred_element_type=f32)


def _inproj_kernel(x_ref, xp_ref, xn_ref, g1_ref, win_ref, avg_ref, aws_ref, abias_ref,
                   bconv_ref, qg_ref, kvg_ref, wuq_ref, wuqr_ref, wk_ref, wv_ref,
                   cos_ref, sin_ref,
                   ya_ref, yb_ref, zc_ref, qt_ref, k4_ref, vt_ref):
    i = pl.program_id(0)
    j = i % (SEQ // TM)
    g1 = g1_ref[...]
    h = _rms(x_ref[...], g1).astype(bf16)

    za = jax.nn.gelu(_dot(h, win_ref[:, 0:A_COLS]))
    u = za[:, :GROUP_W]
    v = _rms(za[:, GROUP_W:], avg_ref[...]).astype(bf16)
    lane = lax.broadcasted_iota(jnp.int32, (CHUNK, GROUP_W), 1)
    for c in range(TM // CHUNK):
        vc = v[c * CHUNK:(c + 1) * CHUNK, :]
        sv = abias_ref[...]
        for hd in range(A_HEADS):
            r = _dot(aws_ref[hd], vc)
            sv = sv + jnp.where((lane >= hd * HEAD_DIM) & (lane < (hd + 1) * HEAD_DIM), r, 0.0)
        ya_ref[c * CHUNK:(c + 1) * CHUNK, :] = u[c * CHUNK:(c + 1) * CHUNK, :] * sv

    zb = _dot(h, win_ref[:, OFF_B:OFF_B + B_COLS])
    bg = zb[:, :GROUP_W]
    y = zb[:, GROUP_W:2 * GROUP_W] * zb[:, 2 * GROUP_W:]
    xh = jnp.concatenate([xp_ref[...], xn_ref[...]], axis=0)
    hh = _rms(xh, g1).astype(bf16)
    zh = _dot(hh, win_ref[:, OFF_B + GROUP_W:OFF_B + B_COLS])
    yh = zh[:, :GROUP_W] * zh[:, GROUP_W:]
    y_before = jnp.where(j == 0, 0.0, yh[7:8, :])
    y_after = jnp.where(j == SEQ // TM - 1, 0.0, yh[8:9, :])
    row = lax.broadcasted_iota(jnp.int32, (TM, GROUP_W), 0)
    y_m1 = jnp.where(row == 0, y_before, pltpu.roll(y, 1, axis=0))
    y_p1 = jnp.where(row == TM - 1, y_after, pltpu.roll(y, TM - 1, axis=0))
    wc = bconv_ref[...]
    yb_ref[...] = bg * (wc[0:1, :] * y_m1 + wc[1:2, :] * y + wc[2:3, :] * y_p1)

    zc = _dot(h, win_ref[:, OFF_C:OFF_C + C_COLS])
    zc_ref[:, 0:GROUP_W] = (zc[:, 0:GROUP_W] * (HEAD_DIM ** -0.5)).astype(bf16)
    zc_ref[:, GROUP_W:] = zc[:, GROUP_W:].astype(bf16)

    zd = _dot(h, win_ref[:, OFF_D:OFF_D + D_PACK])
    cqn = _rms(zd[:, 0:Q_RANK], qg_ref[...]).astype(bf16)
    ckvn = _rms(zd[:, Q_RANK:Q_RANK + KV_RANK], kvg_ref[...]).astype(bf16)
    cos = cos_ref[...]
    sin = sin_ref[...]
    o = Q_RANK + KV_RANK
    kr = zd[:, o:o + LANES] * cos + zd[:, o + LANES:o + 2 * LANES] * sin
    q_pre = _dot(cqn, wuq_ref[...])
    q_rot = _dot(cqn, wuqr_ref[...])
    kn = _dot(ckvn, wk_ref[...])
    for hd in range(D_HEADS):
        sl = slice(hd * LANES, (hd + 1) * LANES)
        qt_ref[hd] = ((q_pre[:, sl] * cos + q_rot[:, sl] * sin) * MLA_QSCALE).T.astype(bf16)
        k4_ref[hd] = (kn[:, sl] + kr).astype(bf16)
    vt = _dot(ckvn, wv_ref[...]).T.astype(bf16)
    for hd in range(D_HEADS):
        vt_ref[hd] = vt[hd * V_DIM:(hd + 1) * V_DIM, :]


def _inproj(x, lw, cosq, sinq):
    nt = T_ALL // TM
    tps = SEQ // TM
    full = lambda shape: pl.BlockSpec(shape, lambda i: (0,) * len(shape))
    in_specs = [
        pl.BlockSpec((TM, D_MODEL), lambda i: (i, 0)),
        pl.BlockSpec((8, D_MODEL), lambda i: (jnp.maximum(i * (TM // 8) - 1, 0), 0)),
        pl.BlockSpec((8, D_MODEL), lambda i: (jnp.minimum((i + 1) * (TM // 8), T_ALL // 8 - 1), 0)),
        full((1, D_MODEL)),
        full((D_MODEL, N_IN_PACK)),
        full((1, GROUP_W)),
        full((A_HEADS, CHUNK, CHUNK)),
        full((CHUNK, GROUP_W)),
        full((3, GROUP_W)),
        full((1, Q_RANK)),
        full((1, KV_RANK)),
        full((Q_RANK, D_HEADS * LANES)),
        full((Q_RANK, D_HEADS * LANES)),
        full((KV_RANK, D_HEADS * LANES)),
        full((KV_RANK, D_HEADS * V_DIM)),
        pl.BlockSpec((TM, LANES), lambda i: (i % tps, 0)),
        pl.BlockSpec((TM, LANES), lambda i: (i % tps, 0)),
    ]
    out_specs = [
        pl.BlockSpec((TM, GROUP_W), lambda i: (i, 0)),
        pl.BlockSpec((TM, GROUP_W), lambda i: (i, 0)),
        pl.BlockSpec((TM, C_COLS), lambda i: (i, 0)),
        pl.BlockSpec((None, D_HEADS, LANES, TM), lambda i: (i // tps, 0, 0, i % tps)),
        pl.BlockSpec((None, D_HEADS, TM, LANES), lambda i: (i // tps, 0, i % tps, 0)),
        pl.BlockSpec((None, D_HEADS, V_DIM, TM), lambda i: (i // tps, 0, 0, i % tps)),
    ]
    out_shape = [
        jax.ShapeDtypeStruct((T_ALL, GROUP_W), f32),
        jax.ShapeDtypeStruct((T_ALL, GROUP_W), f32),
        jax.ShapeDtypeStruct((T_ALL, C_COLS), bf16),
        jax.ShapeDtypeStruct((NSEQ, D_HEADS, LANES, SEQ), bf16),
        jax.ShapeDtypeStruct((NSEQ, D_HEADS, SEQ, LANES), bf16),
        jax.ShapeDtypeStruct((NSEQ, D_HEADS, V_DIM, SEQ), bf16),
    ]
    return pl.pallas_call(
        _inproj_kernel, grid=(nt,), in_specs=in_specs, out_specs=out_specs, out_shape=out_shape,
        name="inproj",
        compiler_params=pltpu.CompilerParams(dimension_semantics=("parallel",),
                                             vmem_limit_bytes=VMEM_LIMIT),
    )(x, x, x, lw["g1"], lw["w_in"], lw["a_vg"], lw["a_ws"], lw["a_bias"], lw["b_conv"],
      lw["d_qg"], lw["d_kvg"], lw["w_uq"], lw["w_uq_rot"], lw["w_k"], lw["w_v"], cosq, sinq)


def _natten_kernel(q_ref, k_ref, v_ref, bias_ref, o_ref):
    jb = pl.program_id(1)
    lane = lax.broadcasted_iota(jnp.int32, (GRID_W, GROUP_W), 1)
    head_of_lane = lane // HEAD_DIM
    nkeys = WIN_R * GRID_W

    jobs = []
    for i in range(ROWS_PER_STEP):
        r = jb * ROWS_PER_STEP + i
        rs = jnp.clip(r - WIN_R // 2, 0, ROWS - WIN_R)
        d = rs - r + (WIN_R - 1)
        kstart = pl.multiple_of(rs * GRID_W, GRID_W)
        q = q_ref[i * GRID_W:(i + 1) * GRID_W, :]
        zero = jnp.zeros_like(q)
        qq = jnp.concatenate([jnp.where(head_of_lane == h, q, zero) for h in range(C_HEADS)], axis=0)
        jobs.append((_dot_nt(qq, k_ref[pl.ds(kstart, nkeys), :]), d, kstart))
    for i, (sc, d, kstart) in enumerate(jobs):
        sc = sc + bias_ref[d]
        m = jnp.max(sc, axis=-1, keepdims=True)
        p = jnp.exp(sc - m)
        l = jnp.sum(p, axis=-1, keepdims=True)
        o = _dot(p.astype(bf16), v_ref[pl.ds(kstart, nkeys), :]) / l
        out = jnp.zeros((GRID_W, GROUP_W), f32)
        for h in range(C_HEADS):
            out = jnp.where(head_of_lane == h, o[h * GRID_W:(h + 1) * GRID_W], out)
        o_ref[i * GRID_W:(i + 1) * GRID_W, :] = out


def _natten(zc, bias):
    nb = ROWS // ROWS_PER_STEP
    tq = ROWS_PER_STEP * GRID_W
    return pl.pallas_call(
        _natten_kernel, grid=(NSEQ, nb),
        in_specs=[
            pl.BlockSpec((tq, GROUP_W), lambda b, j: (b * nb + j, 0)),
            pl.BlockSpec((SEQ, GROUP_W), lambda b, j: (b, 1)),
            pl.BlockSpec((SEQ, GROUP_W), lambda b, j: (b, 2)),
            pl.BlockSpec((WIN_R, C_HEADS * GRID_W, WIN_R * GRID_W), lambda b, j: (0, 0, 0)),
        ],
        out_specs=pl.BlockSpec((tq, GROUP_W), lambda b, j: (b * nb + j, 0)),
        out_shape=jax.ShapeDtypeStruct((T_ALL, GROUP_W), f32),
        name="natten",
        compiler_params=pltpu.CompilerParams(dimension_semantics=("parallel", "arbitrary"),
                                             vmem_limit_bytes=VMEM_LIMIT),
    )(zc, zc, zc, bias)


def _mla_kernel(qt_ref, k_ref, vt_ref, o_ref, s_scr):
    nchunk = SEQ // TK
    heads = list(range(D_HEADS))
    qts = [qt_ref[h] for h in heads]

    def qk(slot, h, ci):
        ks = pl.multiple_of(ci * TK, TK)
        s_scr[slot, h] = _dot(k_ref[h, pl.ds(ks, TK), :], qts[h])

    ones = jnp.ones((L_ROWS, TK), bf16)

    def softmax_pv(slot, h, ci, stat):
        m, acc = stat
        ks = pl.multiple_of(ci * TK, TK)
        vt = jnp.concatenate([vt_ref[h, :, pl.ds(ks, TK)], ones], axis=0)
        s = s_scr[slot, h]
        m_new = jnp.maximum(m, jnp.max(s, axis=0, keepdims=True))
        alpha = jnp.exp2(m - m_new)
        p = jnp.exp2(s - m_new)
        acc = acc * alpha + _dot(vt, p.astype(bf16))
        return m_new, acc

    def half(cur, ci, stats, prefetch=True):
        out = []
        for h in heads:
            if prefetch:
                qk(1 - cur, h, ci + 1)
            if h >= MLA_LAG:
                out.append(softmax_pv(cur, h - MLA_LAG, ci, stats[h - MLA_LAG]))
        for h in range(D_HEADS - MLA_LAG, D_HEADS):
            out.append(softmax_pv(cur, h, ci, stats[h]))
        return tuple(out)

    def body(cp, stats):
        for u in range(MLA_UNROLL):
            stats = half(u % 2, MLA_UNROLL * cp + u, stats)
        return stats

    for h in heads:
        qk(0, h, 0)
    init = tuple((jnp.full((1, TQ), NEG, f32), jnp.zeros((V_DIM + L_ROWS, TQ), f32)) for _ in heads)
    res = lax.fori_loop(0, nchunk // MLA_UNROLL - 1, body, init)
    for u in range(MLA_UNROLL):
        res = half(u % 2, nchunk - MLA_UNROLL + u, res, prefetch=u < MLA_UNROLL - 1)
    o_ref[...] = jnp.concatenate([acc[:V_DIM] / acc[V_DIM:V_DIM + 1] for (_, acc) in res], axis=0).T


def _mla(qt, k4, vt):
    nq = SEQ // TQ
    return pl.pallas_call(
        _mla_kernel, grid=(NSEQ, nq),
        in_specs=[
            pl.BlockSpec((None, D_HEADS, LANES, TQ), lambda b, i: (b, 0, 0, i)),
            pl.BlockSpec((None, D_HEADS, SEQ, LANES), lambda b, i: (b, 0, 0, 0)),
            pl.BlockSpec((None, D_HEADS, V_DIM, SEQ), lambda b, i: (b, 0, 0, 0)),
        ],
        out_specs=pl.BlockSpec((TQ, GROUP_W), lambda b, i: (b * nq + i, 0)),
        out_shape=jax.ShapeDtypeStruct((T_ALL, GROUP_W), f32),
        scratch_shapes=[pltpu.VMEM((2, D_HEADS, TK, TQ), f32)],
        name="mla",
        compiler_params=pltpu.CompilerParams(
            dimension_semantics=("parallel", "arbitrary"),
            vmem_limit_bytes=VMEM_LIMIT),
    )(qt, k4, vt)


def _outproj_kernel(ya_ref, yb_ref, yc_ref, yd_ref, x_ref, og_ref, wo_ref, g2_ref, rwt_ref,
                    xn_ref, h2_ref, aff_ref):
    y = jnp.concatenate(
        [_rms(y_ref[...], og_ref[:, g * GROUP_W:(g + 1) * GROUP_W]).astype(bf16)
         for g, y_ref in enumerate((ya_ref, yb_ref, yc_ref, yd_ref))], axis=1)
    acc = x_ref[...] + _dot(y, wo_ref[...])
    xn_ref[...] = acc
    h2 = _rms(acc, g2_ref[...]).astype(bf16)
    h2_ref[...] = h2
    logits = _dot_nt(rwt_ref[...], h2)
    e = jnp.exp(logits - jnp.max(logits, axis=0, keepdims=True))
    aff_ref[...] = e / jnp.sum(e, axis=0, keepdims=True)


def _outproj(ya, yb, yc, yd, x, lw):
    nt = T_ALL // TM
    full = lambda shape: pl.BlockSpec(shape, lambda i: (0,) * len(shape))
    yspec = pl.BlockSpec((TM, GROUP_W), lambda i: (i, 0))
    xspec = pl.BlockSpec((TM, D_MODEL), lambda i: (i, 0))
    return pl.pallas_call(
        _outproj_kernel, grid=(nt,),
        in_specs=[yspec, yspec, yspec, yspec, xspec, full((1, D_MODEL)), full((D_MODEL, D_MODEL)),
                  full((1, D_MODEL)), full((N_EXPERTS, D_MODEL))],
        out_specs=[xspec, xspec, pl.BlockSpec((N_EXPERTS, TM), lambda i: (0, i))],
        out_shape=[jax.ShapeDtypeStruct((T_ALL, D_MODEL), f32),
                   jax.ShapeDtypeStruct((T_ALL, D_MODEL), bf16),
                   jax.ShapeDtypeStruct((N_EXPERTS, T_ALL), f32)],
        name="outproj",
        compiler_params=pltpu.CompilerParams(dimension_semantics=("parallel",),
                                             vmem_limit_bytes=VMEM_LIMIT),
    )(ya, yb, yc, yd, x, lw["og"], lw["w_o"], lw["g2"], lw["rwt"])


def _route_kernel(aff_ref, tri_ref, posm_ref, base_ref, thr_scr, need_scr, runeq_scr, runsel_scr, *, cap):
    j = pl.program_id(0)

    @pl.when(j == 0)
    def _():
        def refine(i, thr):
            cand = thr | jnp.left_shift(jnp.int32(1), 30 - i)
            bits = pltpu.bitcast(aff_ref[...], jnp.int32)
            cnt = jnp.sum(jnp.where(bits >= cand, 1.0, 0.0), axis=1, keepdims=True)
            return jnp.where(cnt >= cap, cand, thr)

        thr = lax.fori_loop(0, 31, refine, jnp.zeros((N_EXPERTS, 1), jnp.int32))
        bits = pltpu.bitcast(aff_ref[...], jnp.int32)
        above = jnp.sum(jnp.where(bits > thr, 1.0, 0.0), axis=1, keepdims=True)
        thr_scr[...] = jnp.broadcast_to(thr, (N_EXPERTS, LANES))
        need_scr[...] = jnp.broadcast_to(cap - above, (N_EXPERTS, LANES))
        runeq_scr[...] = jnp.zeros((N_EXPERTS, LANES), f32)
        runsel_scr[...] = jnp.zeros((N_EXPERTS, LANES), f32)

    thr = thr_scr[:, 0:1]
    need = need_scr[:, 0:1]
    tiles = []
    for u in range(ROUTE_TILES):
        t0 = pl.multiple_of((j * ROUTE_TILES + u) * TT, TT)
        bits = pltpu.bitcast(aff_ref[:, pl.ds(t0, TT)], jnp.int32)
        eq = bits == thr
        eq_f = jnp.where(eq, 1.0, 0.0)
        tiles.append((bits, eq, eq_f, _dot(eq_f.astype(bf16), tri_ref[...])))
    sels = []
    runeq = runeq_scr[:, 0:1]
    for bits, eq, eq_f, eq_before in tiles:
        sel = (bits > thr) | (eq & (runeq + eq_before < need))
        sel_f = jnp.where(sel, 1.0, 0.0)
        sels.append((sel, sel_f, _dot(sel_f.astype(bf16), tri_ref[...])))
        runeq = runeq + jnp.sum(eq_f, axis=1, keepdims=True)
    runeq_scr[...] = jnp.broadcast_to(runeq, (N_EXPERTS, LANES))
    base = runsel_scr[...]
    for u, (sel, sel_f, sel_before) in enumerate(sels):
        posm_ref[:, u * TT:(u + 1) * TT] = jnp.where(sel, base[:, 0:1] + sel_before, -1.0)
        base_ref[u] = base
        base = base + jnp.sum(sel_f, axis=1, keepdims=True)
    runsel_scr[...] = base


def _route(aff, row0, nrows, tri):
    cap = CAPACITY_FACTOR * nrows // N_EXPERTS
    nt = nrows // TT
    blk0 = row0 // nrows
    vec = pltpu.VMEM((N_EXPERTS, LANES), f32)
    return pl.pallas_call(
        functools.partial(_route_kernel, cap=cap), grid=(nt // ROUTE_TILES,),
        in_specs=[pl.BlockSpec((N_EXPERTS, nrows), lambda j: (0, blk0)),
                  pl.BlockSpec((TT, TT), lambda j: (0, 0))],
        out_specs=[pl.BlockSpec((N_EXPERTS, ROUTE_TILES * TT), lambda j: (0, j)),
                   pl.BlockSpec((ROUTE_TILES, N_EXPERTS, LANES), lambda j: (j, 0, 0))],
        out_shape=[jax.ShapeDtypeStruct((N_EXPERTS, nrows), f32),
                   jax.ShapeDtypeStruct((nt, N_EXPERTS, LANES), f32)],
        scratch_shapes=[pltpu.VMEM((N_EXPERTS, LANES), jnp.int32), vec, vec, vec],
        name="route",
        compiler_params=pltpu.CompilerParams(dimension_semantics=("arbitrary",)),
    )(aff, tri)


def _window_class(count, windows, align):
    longest = jnp.max(count, axis=1) + (align - 1)
    return sum((longest > w).astype(jnp.int32) for w in windows[:-1])


def _dispatch_kernel(big_ref, first_ref, posm_ref, h2_ref, xe_hbm, stage, sem, carry, *, cap, ntiles):
    j = pl.program_id(0)
    slot = j % 2

    @pl.when(j == 0)
    def _():
        carry[...] = jnp.zeros_like(carry)
        pad = DISP_WINDOWS[-1]
        stage[1, 0:pad, :] = jnp.zeros((pad, D_MODEL), bf16)
        pads = [pltpu.make_async_copy(stage.at[1, pl.ds(0, pad)], xe_hbm.at[e, pl.ds(cap, pad)], sem.at[1])
                for e in range(N_EXPERTS)]
        for p in pads:
            p.start()
        for p in pads:
            p.wait()

    def run_start(jj, e):
        return first_ref[jj * N_EXPERTS + e]

    def window_copy(jj, e, sl, win):
        a = pl.multiple_of(run_start(jj, e) // ROW_ALIGN * ROW_ALIGN, ROW_ALIGN)
        return pltpu.make_async_copy(stage.at[sl, pl.ds(e * win, win)], xe_hbm.at[e, pl.ds(a, win)], sem.at[sl])

    def for_tile_window(jj, fn):
        for flag, win in enumerate(DISP_WINDOWS):
            @pl.when(big_ref[jj] == flag)
            def _():
                fn(win)

    def wait_all(jj, sl):
        def go(win):
            for e in range(N_EXPERTS):
                window_copy(jj, e, sl, win).wait()
        for_tile_window(jj, go)

    def compact(win):
        row = lax.broadcasted_iota(jnp.int32, (win, TT), 0)
        starts = [run_start(j, e) // ROW_ALIGN * ROW_ALIGN for e in range(N_EXPERTS)]
        onehot = jnp.concatenate(
            [jnp.where((row + starts[e]).astype(f32) == posm_ref[e:e + 1, :], 1.0, 0.0)
             for e in range(N_EXPERTS)], axis=0).astype(bf16)
        rows = _dot(onehot, h2_ref[...])
        head = lax.broadcasted_iota(jnp.int32, (ROW_ALIGN, D_MODEL), 0)
        for e in range(N_EXPERTS):
            lead = run_start(j, e) - starts[e]
            w = rows[e * win:(e + 1) * win]
            top = w[:ROW_ALIGN] + jnp.where(head < lead, carry[e], 0.0)
            stage[slot, e * win:e * win + ROW_ALIGN, :] = top.astype(bf16)
            stage[slot, e * win + ROW_ALIGN:(e + 1) * win, :] = w[ROW_ALIGN:].astype(bf16)
        for e in range(N_EXPERTS):
            end = jnp.where(j + 1 < ntiles, run_start(jnp.minimum(j + 1, ntiles - 1), e), cap)
            off = jnp.minimum(end // ROW_ALIGN * ROW_ALIGN - starts[e], win - ROW_ALIGN)
            off = pl.multiple_of(off, ROW_ALIGN)
            carry[e] = stage[slot, pl.ds(e * win + off, ROW_ALIGN), :].astype(f32)

    for_tile_window(j, compact)

    @pl.when(j > 0)
    def _():
        wait_all(j - 1, 1 - slot)

    def start_all(win):
        for e in range(N_EXPERTS):
            window_copy(j, e, slot, win).start()
    for_tile_window(j, start_all)

    @pl.when(j == ntiles - 1)
    def _():
        wait_all(j, slot)


def _dispatch(h2, posm, first, count, row0, nrows):
    cap = CAPACITY_FACTOR * nrows // N_EXPERTS
    nt = nrows // TT
    blk0 = row0 // TT
    big = _window_class(count, DISP_WINDOWS, ROW_ALIGN)
    return pl.pallas_call(
        functools.partial(_dispatch_kernel, cap=cap, ntiles=nt),
        grid_spec=pltpu.PrefetchScalarGridSpec(
            num_scalar_prefetch=2, grid=(nt,),
            in_specs=[pl.BlockSpec((N_EXPERTS, TT), lambda j, *_: (0, j)),
                      pl.BlockSpec((TT, D_MODEL), lambda j, *_: (blk0 + j, 0))],
            out_specs=pl.BlockSpec(memory_space=pl.ANY),
            scratch_shapes=[pltpu.VMEM((2, N_EXPERTS * DISP_WINDOWS[-1], D_MODEL), bf16),
                            pltpu.SemaphoreType.DMA((2,)),
                            pltpu.VMEM((N_EXPERTS, ROW_ALIGN, D_MODEL), f32)],
        ),
        out_shape=jax.ShapeDtypeStruct((N_EXPERTS, cap + DISP_WINDOWS[-1], D_MODEL), bf16),
        name="dispatch",
        compiler_params=pltpu.CompilerParams(dimension_semantics=("arbitrary",), vmem_limit_bytes=VMEM_LIMIT),
    )(big, first.reshape(-1), posm, h2)


def _ffn_kernel(xe_ref, rw_ref, wg_ref, wu_ref, wd_ref, ye_ref):
    e = pl.program_id(0)
    x = xe_ref[...]
    logits = _dot(x, rw_ref[...])
    lane = lax.broadcasted_iota(jnp.int32, logits.shape, 1)
    logits = jnp.where(lane < N_EXPERTS, logits, NEG)
    ex = jnp.exp(logits - jnp.max(logits, axis=-1, keepdims=True))
    gate = jnp.sum(jnp.where(lane == e, ex, 0.0), axis=-1, keepdims=True) / jnp.sum(ex, axis=-1, keepdims=True)
    hg = _dot(x, wg_ref[...].astype(bf16))
    hu = _dot(x, wu_ref[...].astype(bf16))
    h = (hg * (1.0 / (1.0 + jnp.exp(-hg))) * hu).astype(bf16)
    ye_ref[...] = (_dot(h, wd_ref[...].astype(bf16)) * gate).astype(bf16)


def _ffn(xe, lw, nrows):
    cap = CAPACITY_FACTOR * nrows // N_EXPERTS
    layer = lw["layer"]
    wspec = pl.BlockSpec((None, None, D_MODEL, D_EXPERT), lambda e, t: (layer, e, 0, 0))
    return pl.pallas_call(
        _ffn_kernel, grid=(N_EXPERTS, cap // TME),
        in_specs=[
            pl.BlockSpec((None, TME, D_MODEL), lambda e, t: (e, t, 0)),
            pl.BlockSpec((D_MODEL, LANES), lambda e, t: (0, 0)),
            wspec, wspec,
            pl.BlockSpec((None, None, D_EXPERT, D_MODEL), lambda e, t: (layer, e, 0, 0)),
        ],
        out_specs=pl.BlockSpec((None, TME, D_MODEL), lambda e, t: (e, t, 0)),
        out_shape=jax.ShapeDtypeStruct((N_EXPERTS, cap, D_MODEL), bf16),
        name="expert_ffn",
        compiler_params=pltpu.CompilerParams(dimension_semantics=("parallel", "arbitrary"),
                                             vmem_limit_bytes=VMEM_LIMIT),
    )(xe, lw["rw"], lw["e_wg"], lw["e_wu"], lw["e_wd"])


def _combine_kernel(big_ref, first_ref, posm_ref, x_ref, ye_hbm, o_ref, buf, sem, *, cap, ntiles):
    j = pl.program_id(0)

    def window_start(jj, e, win):
        return jnp.minimum(first_ref[jj * N_EXPERTS + e] // COMB_ALIGN * COMB_ALIGN, cap - win)

    def window_copy(jj, e, slot, win):
        a = pl.multiple_of(window_start(jj, e, win), COMB_ALIGN)
        return pltpu.make_async_copy(ye_hbm.at[e, pl.ds(a, win)], buf.at[slot, pl.ds(e * win, win)],
                                     sem.at[slot])

    def for_tile_window(jj, fn):
        for flag, win in enumerate(COMB_WINDOWS):
            @pl.when(big_ref[jj] == flag)
            def _():
                fn(win)

    def start_all(jj, slot):
        def go(win):
            for e in range(N_EXPERTS):
                window_copy(jj, e, slot, win).start()
        for_tile_window(jj, go)

    @pl.when(j == 0)
    def _():
        for ahead in range(COMB_BUFS - 1):
            start_all(ahead, ahead)

    slot = j % COMB_BUFS

    def wait_all(win):
        for e in range(N_EXPERTS):
            window_copy(j, e, slot, win).wait()
    for_tile_window(j, wait_all)

    @pl.when(j + COMB_BUFS - 1 < ntiles)
    def _():
        start_all(j + COMB_BUFS - 1, (j + COMB_BUFS - 1) % COMB_BUFS)

    def add_rows(win):
        row = lax.broadcasted_iota(jnp.int32, (win, TT), 0)
        onehot = jnp.concatenate(
            [jnp.where((row + window_start(j, e, win)).astype(f32) == posm_ref[e:e + 1, :], 1.0, 0.0)
             for e in range(N_EXPERTS)], axis=0).astype(bf16)
        o_ref[...] = x_ref[...] + _dot_tn(onehot, buf[slot, 0:N_EXPERTS * win, :])
    for_tile_window(j, add_rows)


def _combine(x, ye, posm, first, count, row0, nrows):
    cap = CAPACITY_FACTOR * nrows // N_EXPERTS
    nt = nrows // TT
    blk0 = row0 // TT
    big = _window_class(count, COMB_WINDOWS, COMB_ALIGN)
    return pl.pallas_call(
        functools.partial(_combine_kernel, cap=cap, ntiles=nt),
        grid_spec=pltpu.PrefetchScalarGridSpec(
            num_scalar_prefetch=2, grid=(nt,),
            in_specs=[pl.BlockSpec((N_EXPERTS, TT), lambda j, *_: (0, j)),
                      pl.BlockSpec((TT, D_MODEL), lambda j, *_: (blk0 + j, 0)),
                      pl.BlockSpec(memory_space=pl.ANY)],
            out_specs=pl.BlockSpec((TT, D_MODEL), lambda j, *_: (blk0 + j, 0)),
            scratch_shapes=[pltpu.VMEM((COMB_BUFS, N_EXPERTS * COMB_WINDOWS[-1], D_MODEL), bf16),
                            pltpu.SemaphoreType.DMA((COMB_BUFS,))],
        ),
        out_shape=jax.ShapeDtypeStruct(x.shape, f32),
        input_output_aliases={3: 0},
        name="combine",
        compiler_params=pltpu.CompilerParams(dimension_semantics=("arbitrary",), vmem_limit_bytes=VMEM_LIMIT),
    )(big, first.reshape(-1), posm, x, ye)


def _stack_kernel(a_ref, b_ref, o_ref, *, na_blocks):
    o_ref[...] = jnp.where(pl.program_id(0) < na_blocks, a_ref[...], b_ref[...])


def _stack_rows(a, b):
    na, nb = a.shape[0] // TM, b.shape[0] // TM
    return pl.pallas_call(
        functools.partial(_stack_kernel, na_blocks=na), grid=(na + nb,),
        in_specs=[pl.BlockSpec((TM, D_MODEL), lambda i: (jnp.minimum(i, na - 1), 0)),
                  pl.BlockSpec((TM, D_MODEL), lambda i: (jnp.maximum(i - na, 0), 0))],
        out_specs=pl.BlockSpec((TM, D_MODEL), lambda i: (i, 0)),
        out_shape=jax.ShapeDtypeStruct((a.shape[0] + b.shape[0], D_MODEL), a.dtype),
        name="stack_rows",
        compiler_params=pltpu.CompilerParams(dimension_semantics=("arbitrary",)),
    )(a, b)


def _final_kernel(x_ref, g_ref, o_ref):
    o_ref[...] = _rms(x_ref[...], g_ref[...])


def _final_norm(x, g, row0, nrows):
    off = row0 // TM
    return pl.pallas_call(
        _final_kernel, grid=(nrows // TM,),
        in_specs=[pl.BlockSpec((TM, D_MODEL), lambda i: (i + off, 0)),
                  pl.BlockSpec((1, D_MODEL), lambda i: (0, 0))],
        out_specs=pl.BlockSpec((TM, D_MODEL), lambda i: (i, 0)),
        out_shape=jax.ShapeDtypeStruct((nrows, D_MODEL), f32),
        name="final_norm",
        compiler_params=pltpu.CompilerParams(dimension_semantics=("parallel",)),
    )(x, g)


def _natten_bias_table(rpb):
    c = np.arange(GRID_W)
    cs = np.clip(c - WIN_C // 2, 0, GRID_W - WIN_C)
    kc = np.arange(GRID_W)
    col_mask = (kc[None, :] >= cs[:, None]) & (kc[None, :] < cs[:, None] + WIN_C)
    dc = np.clip(kc[None, :] - c[:, None], -(WIN_C - 1), WIN_C - 1) + (WIN_C - 1)
    tb = jnp.where(col_mask[None, None], rpb[:, :, dc].astype(f32), NEG)
    dw = np.arange(WIN_R)[:, None] + np.arange(WIN_R)[None, :]
    t = tb[:, dw]
    t = t.transpose(1, 0, 3, 2, 4).reshape(WIN_R, C_HEADS * GRID_W, WIN_R * GRID_W)
    return t


def _rope_slabs():
    inv = 1.0 / (ROPE_THETA ** (jnp.arange(0, ROPE_DIM, 2, dtype=f32) / ROPE_DIM))
    ang = jnp.arange(SEQ, dtype=f32)[:, None] * inv[None, :]
    cos, sin = jnp.cos(ang), jnp.sin(ang)
    ones = jnp.ones((SEQ, NOPE_DIM), f32)
    zpad = jnp.zeros((SEQ, LANES - NOPE_DIM - ROPE_DIM), f32)
    cosq = jnp.concatenate([ones, cos, cos, zpad], axis=1)
    sinq = jnp.concatenate([0.0 * ones, sin, sin, zpad], axis=1)
    return cosq, sinq


def _pack_layer(i, norm1_g, w_in, a_vnorm_g, a_ws, a_bs, b_conv, c_rpb, d_qnorm_g, d_kvnorm_g,
                d_w_uq, d_w_ukv, out_norm_g, w_o, norm2_g, router_w, e_w_gate, e_w_up, e_w_down):
    half = ROPE_DIM // 2
    w = w_in[i]
    kr0 = OFF_D + Q_RANK + KV_RANK
    kr = w[:, kr0:kr0 + ROPE_DIM]
    z64 = jnp.zeros((D_MODEL, NOPE_DIM), f32)
    z32 = jnp.zeros((D_MODEL, LANES - NOPE_DIM - ROPE_DIM), f32)
    kr_slab = jnp.concatenate([z64, kr, z32], axis=1)
    kr_rot = jnp.concatenate([z64, -kr[:, half:], kr[:, :half], z32], axis=1)
    w_pack = jnp.concatenate([w[:, :kr0], kr_slab, kr_rot], axis=1).astype(bf16)

    uq = d_w_uq[i].reshape(Q_RANK, D_HEADS, NOPE_DIM + ROPE_DIM)
    nope, rope = uq[..., :NOPE_DIM], uq[..., NOPE_DIM:]
    zq = jnp.zeros((Q_RANK, D_HEADS, LANES - NOPE_DIM - ROPE_DIM), f32)
    w_uq = jnp.concatenate([nope, rope, zq], axis=-1).reshape(Q_RANK, D_HEADS * LANES)
    w_uq_rot = jnp.concatenate([0.0 * nope, -rope[..., half:], rope[..., :half], zq],
                               axis=-1).reshape(Q_RANK, D_HEADS * LANES)
    ukv = d_w_ukv[i].reshape(KV_RANK, D_HEADS, NOPE_DIM + V_DIM)
    w_k = jnp.concatenate([ukv[..., :NOPE_DIM], jnp.zeros((KV_RANK, D_HEADS, LANES - NOPE_DIM), f32)],
                          axis=-1).reshape(KV_RANK, D_HEADS * LANES)
    w_v = ukv[..., NOPE_DIM:].reshape(KV_RANK, D_HEADS * V_DIM)
    rw = jnp.concatenate([router_w[i], jnp.zeros((D_MODEL, LANES - N_EXPERTS), f32)], axis=1)
    return {
        "g1": norm1_g[i].reshape(1, D_MODEL),
        "w_in": w_pack,
        "a_vg": a_vnorm_g[i].reshape(1, GROUP_W),
        "a_ws": a_ws[i].astype(bf16),
        "a_bias": jnp.repeat(a_bs[i].T, HEAD_DIM, axis=1),
        "b_conv": b_conv[i],
        "c_bias": _natten_bias_table(c_rpb[i]),
        "d_qg": d_qnorm_g[i].reshape(1, Q_RANK),
        "d_kvg": d_kvnorm_g[i].reshape(1, KV_RANK),
        "w_uq": w_uq.astype(bf16),
        "w_uq_rot": w_uq_rot.astype(bf16),
        "w_k": w_k.astype(bf16),
        "w_v": w_v.astype(bf16),
        "og": out_norm_g[i].reshape(1, D_MODEL),
        "w_o": w_o[i].astype(bf16),
        "g2": norm2_g[i].reshape(1, D_MODEL),
        "rw": rw.astype(bf16),
        "rwt": router_w[i].T.astype(bf16),
        "layer": i,
        "e_wg": e_w_gate,
        "e_wu": e_w_up,
        "e_wd": e_w_down,
    }


def _moe(xn, h2, aff, lw, tri):
    x = xn
    for row0, nrows in ((0, BATCH * SEQ), (BATCH * SEQ, DEC_BATCH * SEQ)):
        cap = CAPACITY_FACTOR * nrows // N_EXPERTS
        posm, base = _route(aff, row0, nrows, tri)
        first = base[:, :, 0].astype(jnp.int32)
        count = jnp.diff(first, axis=0, append=jnp.full((1, N_EXPERTS), cap, jnp.int32))
        xe = _dispatch(h2, posm, first, count, row0, nrows)
        ye = _ffn(xe, lw, nrows)
        x = _combine(x, ye, posm, first, count, row0, nrows)
    return x


def kernel(x_prompt, x_sample, norm1_g, w_in, a_vnorm_g, a_ws, a_bs, b_conv, c_rpb, d_qnorm_g, d_kvnorm_g, d_w_uq, d_w_ukv, out_norm_g, w_o, norm2_g, router_w, e_w_gate, e_w_up, e_w_down, final_norm_g):
    n_p = BATCH * SEQ
    n_s = DEC_BATCH * SEQ
    x = _stack_rows(x_prompt.reshape(n_p, D_MODEL), x_sample.reshape(n_s, D_MODEL))
    cosq, sinq = _rope_slabs()
    tri = jnp.asarray(np.triu(np.ones((TT, TT), np.float32), 1), bf16)
    for i in range(DEPTH):
        lw = _pack_layer(i, norm1_g, w_in, a_vnorm_g, a_ws, a_bs, b_conv, c_rpb, d_qnorm_g, d_kvnorm_g,
                         d_w_uq, d_w_ukv, out_norm_g, w_o, norm2_g, router_w, e_w_gate, e_w_up, e_w_down)
        ya, yb, zc, qt, k4, vt = _inproj(x, lw, cosq, sinq)
        yc = _natten(zc, lw["c_bias"])
        yd = _mla(qt, k4, vt)
        xn, h2, aff = _outproj(ya, yb, yc, yd, x, lw)
        x = _moe(xn, h2, aff, lw, tri)
    g = final_norm_g.reshape(1, D_MODEL)
    y_p = _final_norm(x, g, 0, n_p).reshape(BATCH, SEQ, D_MODEL)
    y_s = _final_norm(x, g, n_p, n_s).reshape(DEC_BATCH, SEQ, D_MODEL)
    return (y_p, y_s)
```

```python
import functools
import math

import numpy as np
import jax
import jax.numpy as jnp
from jax import lax
from jax.experimental import pallas as pl
from jax.experimental.pallas import tpu as pltpu

D_MODEL = 1024
BATCH = 4
SEQ = 8192
DEPTH = 4
DEC_BATCH = 2
GROUP_W = 256
HEAD_DIM = 64
A_HEADS = 4
CHUNK = 128
C_HEADS = 4
GRID_W = 64
WIN_R = 8
WIN_C = 16
D_HEADS = 4
Q_RANK = 256
KV_RANK = 128
NOPE_DIM = 64
ROPE_DIM = 32
V_DIM = 64
ROPE_THETA = 10000.0
N_EXPERTS = 16
D_EXPERT = 1024
CAPACITY_FACTOR = 2
EPS = 1e-6

NSEQ = BATCH + DEC_BATCH
T_ALL = NSEQ * SEQ
ROWS = SEQ // GRID_W
LANES = 128
NEG = -1e30

A_COLS = 2 * GROUP_W
B_COLS = 3 * GROUP_W
C_COLS = 3 * GROUP_W
OFF_B = A_COLS
OFF_C = OFF_B + B_COLS
OFF_D = OFF_C + C_COLS
D_PACK = Q_RANK + KV_RANK + 2 * LANES
N_IN_PACK = OFF_D + D_PACK

TM = 1024
TQ = 256
TK = 512
L_ROWS = 16
MLA_UNROLL = 4
MLA_LAG = 1
MLA_QSCALE = (NOPE_DIM + ROPE_DIM) ** -0.5 * math.log2(math.e)
ROWS_PER_STEP = 16
TME = 1024
TT = 256
ROUTE_TILES = 4
ROW_ALIGN = 16
DISP_WINDOWS = (80, 144, TT + ROW_ALIGN)
COMB_ALIGN = ROW_ALIGN
COMB_WINDOWS = (80, 144, TT + COMB_ALIGN)
COMB_BUFS = 3
VMEM_LIMIT = 56 * 1024 * 1024

f32 = jnp.float32
bf16 = jnp.bfloat16


def _rms(x, g):
    return x * lax.rsqrt(jnp.mean(x * x, axis=-1, keepdims=True) + EPS) * g


def _dot(a, b):
    return jnp.dot(a, b, preferred_element_type=f32)


def _dot_nt(a, b):
    return lax.dot_general(a, b, (((1,), (1,)), ((), ())), preferred_element_type=f32)


def _dot_tn(a, b):
    return lax.dot_general(a, b, (((0,), (0,)), ((), ())), preferred_element_type=f32)


def _inproj_kernel(x_ref, xp_ref, xn_ref, g1_ref, win_ref, avg_ref, aws_ref, abias_ref,
                   bconv_ref, qg_ref, kvg_ref, wuq_ref, wuqr_ref, wk_ref, wv_ref,
                   cos_ref, sin_ref,
                   ya_ref, yb_ref, zc_ref, qt_ref, k4_ref, vt_ref):
    i = pl.program_id(0)
    j = i % (SEQ // TM)
    g1 = g1_ref[...]
    h = _rms(x_ref[...], g1).astype(bf16)

    za = jax.nn.gelu(_dot(h, win_ref[:, 0:A_COLS]))
    u = za[:, :GROUP_W]
    v = _rms(za[:, GROUP_W:], avg_ref[...]).astype(bf16)
    lane = lax.broadcasted_iota(jnp.int32, (CHUNK, GROUP_W), 1)
    for c in range(TM // CHUNK):
        vc = v[c * CHUNK:(c + 1) * CHUNK, :]
        sv = abias_ref[...]
        for hd in range(A_HEADS):
            r = _dot(aws_ref[hd], vc)
            sv = sv + jnp.where((lane >= hd * HEAD_DIM) & (lane < (hd + 1) * HEAD_DIM), r, 0.0)
        ya_ref[c * CHUNK:(c + 1) * CHUNK, :] = u[c * CHUNK:(c + 1) * CHUNK, :] * sv

    zb = _dot(h, win_ref[:, OFF_B:OFF_B + B_COLS])
    bg = zb[:, :GROUP_W]
    y = zb[:, GROUP_W:2 * GROUP_W] * zb[:, 2 * GROUP_W:]
    xh = jnp.concatenate([xp_ref[...], xn_ref[...]], axis=0)
    hh = _rms(xh, g1).astype(bf16)
    zh = _dot(hh, win_ref[:, OFF_B + GROUP_W:OFF_B + B_COLS])
    yh = zh[:, :GROUP_W] * zh[:, GROUP_W:]
    y_before = jnp.where(j == 0, 0.0, yh[7:8, :])
    y_after = jnp.where(j == SEQ // TM - 1, 0.0, yh[8:9, :])
    row = lax.broadcasted_iota(jnp.int32, (TM, GROUP_W), 0)
    y_m1 = jnp.where(row == 0, y_before, pltpu.roll(y, 1, axis=0))
    y_p1 = jnp.where(row == TM - 1, y_after, pltpu.roll(y, TM - 1, axis=0))
    wc = bconv_ref[...]
    yb_ref[...] = bg * (wc[0:1, :] * y_m1 + wc[1:2, :] * y + wc[2:3, :] * y_p1)

    zc = _dot(h, win_ref[:, OFF_C:OFF_C + C_COLS])
    zc_ref[:, 0:GROUP_W] = (zc[:, 0:GROUP_W] * (HEAD_DIM ** -0.5)).astype(bf16)
    zc_ref[:, GROUP_W:] = zc[:, GROUP_W:].astype(bf16)

    zd = _dot(h, win_ref[:, OFF_D:OFF_D + D_PACK])
    cqn = _rms(zd[:, 0:Q_RANK], qg_ref[...]).astype(bf16)
    ckvn = _rms(zd[:, Q_RANK:Q_RANK + KV_RANK], kvg_ref[...]).astype(bf16)
    cos = cos_ref[...]
    sin = sin_ref[...]
    o = Q_RANK + KV_RANK
    kr = zd[:, o:o + LANES] * cos + zd[:, o + LANES:o + 2 * LANES] * sin
    q_pre = _dot(cqn, wuq_ref[...])
    q_rot = _dot(cqn, wuqr_ref[...])
    kn = _dot(ckvn, wk_ref[...])
    for hd in range(D_HEADS):
        sl = slice(hd * LANES, (hd + 1) * LANES)
        qt_ref[hd] = ((q_pre[:, sl] * cos + q_rot[:, sl] * sin) * MLA_QSCALE).T.astype(bf16)
        k4_ref[hd] = (kn[:, sl] + kr).astype(bf16)
    vt = _dot(ckvn, wv_ref[...]).T.astype(bf16)
    for hd in range(D_HEADS):
        vt_ref[hd] = vt[hd * V_DIM:(hd + 1) * V_DIM, :]


def _inproj(x, lw, cosq, sinq):
    nt = T_ALL // TM
    tps = SEQ // TM
    full = lambda shape: pl.BlockSpec(shape, lambda i: (0,) * len(shape))
    in_specs = [
        pl.BlockSpec((TM, D_MODEL), lambda i: (i, 0)),
        pl.BlockSpec((8, D_MODEL), lambda i: (jnp.maximum(i * (TM // 8) - 1, 0), 0)),
        pl.BlockSpec((8, D_MODEL), lambda i: (jnp.minimum((i + 1) * (TM // 8), T_ALL // 8 - 1), 0)),
        full((1, D_MODEL)),
        full((D_MODEL, N_IN_PACK)),
        full((1, GROUP_W)),
        full((A_HEADS, CHUNK, CHUNK)),
        full((CHUNK, GROUP_W)),
        full((3, GROUP_W)),
        full((1, Q_RANK)),
        full((1, KV_RANK)),
        full((Q_RANK, D_HEADS * LANES)),
        full((Q_RANK, D_HEADS * LANES)),
        full((KV_RANK, D_HEADS * LANES)),
        full((KV_RANK, D_HEADS * V_DIM)),
        pl.BlockSpec((TM, LANES), lambda i: (i % tps, 0)),
        pl.BlockSpec((TM, LANES), lambda i: (i % tps, 0)),
    ]
    out_specs = [
        pl.BlockSpec((TM, GROUP_W), lambda i: (i, 0)),
        pl.BlockSpec((TM, GROUP_W), lambda i: (i, 0)),
        pl.BlockSpec((TM, C_COLS), lambda i: (i, 0)),
        pl.BlockSpec((None, D_HEADS, LANES, TM), lambda i: (i // tps, 0, 0, i % tps)),
        pl.BlockSpec((None, D_HEADS, TM, LANES), lambda i: (i // tps, 0, i % tps, 0)),
        pl.BlockSpec((None, D_HEADS, V_DIM, TM), lambda i: (i // tps, 0, 0, i % tps)),
    ]
    out_shape = [
        jax.ShapeDtypeStruct((T_ALL, GROUP_W), f32),
        jax.ShapeDtypeStruct((T_ALL, GROUP_W), f32),
        jax.ShapeDtypeStruct((T_ALL, C_COLS), bf16),
        jax.ShapeDtypeStruct((NSEQ, D_HEADS, LANES, SEQ), bf16),
        jax.ShapeDtypeStruct((NSEQ, D_HEADS, SEQ, LANES), bf16),
        jax.ShapeDtypeStruct((NSEQ, D_HEADS, V_DIM, SEQ), bf16),
    ]
    return pl.pallas_call(
        _inproj_kernel, grid=(nt,), in_specs=in_specs, out_specs=out_specs, out_shape=out_shape,
        name="inproj",
        compiler_params=pltpu.CompilerParams(dimension_semantics=("parallel",),
                                             vmem_limit_bytes=VMEM_LIMIT),
    )(x, x, x, lw["g1"], lw["w_in"], lw["a_vg"], lw["a_ws"], lw["a_bias"], lw["b_conv"],
      lw["d_qg"], lw["d_kvg"], lw["w_uq"], lw["w_uq_rot"], lw["w_k"], lw["w_v"], cosq, sinq)


def _natten_kernel(q_ref, k_ref, v_ref, bias_ref, o_ref):
    jb = pl.program_id(1)
    lane = lax.broadcasted_iota(jnp.int32, (GRID_W, GROUP_W), 1)
    head_of_lane = lane // HEAD_DIM
    nkeys = WIN_R * GRID_W

    jobs = []
    for i in range(ROWS_PER_STEP):
        r = jb * ROWS_PER_STEP + i
        rs = jnp.clip(r - WIN_R // 2, 0, ROWS - WIN_R)
        d = rs - r + (WIN_R - 1)
        kstart = pl.multiple_of(rs * GRID_W, GRID_W)
        q = q_ref[i * GRID_W:(i + 1) * GRID_W, :]
        zero = jnp.zeros_like(q)
        qq = jnp.concatenate([jnp.where(head_of_lane == h, q, zero) for h in range(C_HEADS)], axis=0)
        jobs.append((_dot_nt(qq, k_ref[pl.ds(kstart, nkeys), :]), d, kstart))
    for i, (sc, d, kstart) in enumerate(jobs):
        sc = sc + bias_ref[d]
        m = jnp.max(sc, axis=-1, keepdims=True)
        p = jnp.exp(sc - m)
        l = jnp.sum(p, axis=-1, keepdims=True)
        o = _dot(p.astype(bf16), v_ref[pl.ds(kstart, nkeys), :]) / l
        out = jnp.zeros((GRID_W, GROUP_W), f32)
        for h in range(C_HEADS):
            out = jnp.where(head_of_lane == h, o[h * GRID_W:(h + 1) * GRID_W], out)
        o_ref[i * GRID_W:(i + 1) * GRID_W, :] = out


def _natten(zc, bias):
    nb = ROWS // ROWS_PER_STEP
    tq = ROWS_PER_STEP * GRID_W
    return pl.pallas_call(
        _natten_kernel, grid=(NSEQ, nb),
        in_specs=[
            pl.BlockSpec((tq, GROUP_W), lambda b, j: (b * nb + j, 0)),
            pl.BlockSpec((SEQ, GROUP_W), lambda b, j: (b, 1)),
            pl.BlockSpec((SEQ, GROUP_W), lambda b, j: (b, 2)),
            pl.BlockSpec((WIN_R, C_HEADS * GRID_W, WIN_R * GRID_W), lambda b, j: (0, 0, 0)),
        ],
        out_specs=pl.BlockSpec((tq, GROUP_W), lambda b, j: (b * nb + j, 0)),
        out_shape=jax.ShapeDtypeStruct((T_ALL, GROUP_W), f32),
        name="natten",
        compiler_params=pltpu.CompilerParams(dimension_semantics=("parallel", "arbitrary"),
                                             vmem_limit_bytes=VMEM_LIMIT),
    )(zc, zc, zc, bias)


def _mla_kernel(qt_ref, k_ref, vt_ref, o_ref, s_scr):
    nchunk = SEQ // TK
    heads = list(range(D_HEADS))
    qts = [qt_ref[h] for h in heads]

    def qk(slot, h, ci):
        ks = pl.multiple_of(ci * TK, TK)
        s_scr[slot, h] = _dot(k_ref[h, pl.ds(ks, TK), :], qts[h])

    ones = jnp.ones((L_ROWS, TK), bf16)

    def softmax_pv(slot, h, ci, stat):
        m, acc = stat
        ks = pl.multiple_of(ci * TK, TK)
        vt = jnp.concatenate([vt_ref[h, :, pl.ds(ks, TK)], ones], axis=0)
        s = s_scr[slot, h]
        m_new = jnp.maximum(m, jnp.max(s, axis=0, keepdims=True))
        alpha = jnp.exp2(m - m_new)
        p = jnp.exp2(s - m_new)
        acc = acc * alpha + _dot(vt, p.astype(bf16))
        return m_new, acc

    def half(cur, ci, stats, prefetch=True):
        out = []
        for h in heads:
            if prefetch:
                qk(1 - cur, h, ci + 1)
            if h >= MLA_LAG:
                out.append(softmax_pv(cur, h - MLA_LAG, ci, stats[h - MLA_LAG]))
        for h in range(D_HEADS - MLA_LAG, D_HEADS):
            out.append(softmax_pv(cur, h, ci, stats[h]))
        return tuple(out)

    def body(cp, stats):
        for u in range(MLA_UNROLL):
            stats = half(u % 2, MLA_UNROLL * cp + u, stats)
        return stats

    for h in heads:
        qk(0, h, 0)
    init = tuple((jnp.full((1, TQ), NEG, f32), jnp.zeros((V_DIM + L_ROWS, TQ), f32)) for _ in heads)
    res = lax.fori_loop(0, nchunk // MLA_UNROLL - 1, body, init)
    for u in range(MLA_UNROLL):
        res = half(u % 2, nchunk - MLA_UNROLL + u, res, prefetch=u < MLA_UNROLL - 1)
    o_ref[...] = jnp.concatenate([acc[:V_DIM] / acc[V_DIM:V_DIM + 1] for (_, acc) in res], axis=0).T


def _mla(qt, k4, vt):
    nq = SEQ // TQ
    return pl.pallas_call(
        _mla_kernel, grid=(NSEQ, nq),
        in_specs=[
            pl.BlockSpec((None, D_HEADS, LANES, TQ), lambda b, i: (b, 0, 0, i)),
            pl.BlockSpec((None, D_HEADS, SEQ, LANES), lambda b, i: (b, 0, 0, 0)),
            pl.BlockSpec((None, D_HEADS, V_DIM, SEQ), lambda b, i: (b, 0, 0, 0)),
        ],
        out_specs=pl.BlockSpec((TQ, GROUP_W), lambda b, i: (b * nq + i, 0)),
        out_shape=jax.ShapeDtypeStruct((T_ALL, GROUP_W), f32),
        scratch_shapes=[pltpu.VMEM((2, D_HEADS, TK, TQ), f32)],
        name="mla",
        compiler_params=pltpu.CompilerParams(
            dimension_semantics=("parallel", "arbitrary"),
            vmem_limit_bytes=VMEM_LIMIT),
    )(qt, k4, vt)


def _outproj_kernel(ya_ref, yb_ref, yc_ref, yd_ref, x_ref, og_ref, wo_ref, g2_ref, rwt_ref,
                    xn_ref, h2_ref, aff_ref):
    y = jnp.concatenate(
        [_rms(y_ref[...], og_ref[:, g * GROUP_W:(g + 1) * GROUP_W]).astype(bf16)
         for g, y_ref in enumerate((ya_ref, yb_ref, yc_ref, yd_ref))], axis=1)
    acc = x_ref[...] + _dot(y, wo_ref[...])
    xn_ref[...] = acc
    h2 = _rms(acc, g2_ref[...]).astype(bf16)
    h2_ref[...] = h2
    logits = _dot_nt(rwt_ref[...], h2)
    e = jnp.exp(logits - jnp.max(logits, axis=0, keepdims=True))
    aff_ref[...] = e / jnp.sum(e, axis=0, keepdims=True)


def _outproj(ya, yb, yc, yd, x, lw):
    nt = T_ALL // TM
    full = lambda shape: pl.BlockSpec(shape, lambda i: (0,) * len(shape))
    yspec = pl.BlockSpec((TM, GROUP_W), lambda i: (i, 0))
    xspec = pl.BlockSpec((TM, D_MODEL), lambda i: (i, 0))
    return pl.pallas_call(
        _outproj_kernel, grid=(nt,),
        in_specs=[yspec, yspec, yspec, yspec, xspec, full((1, D_MODEL)), full((D_MODEL, D_MODEL)),
                  full((1, D_MODEL)), full((N_EXPERTS, D_MODEL))],
        out_specs=[xspec, xspec, pl.BlockSpec((N_EXPERTS, TM), lambda i: (0, i))],
        out_shape=[jax.ShapeDtypeStruct((T_ALL, D_MODEL), f32),
                   jax.ShapeDtypeStruct((T_ALL, D_MODEL), bf16),
                   jax.ShapeDtypeStruct((N_EXPERTS, T_ALL), f32)],
        name="outproj",
        compiler_params=pltpu.CompilerParams(dimension_semantics=("parallel",),
                                             vmem_limit_bytes=VMEM_LIMIT),
    )(ya, yb, yc, yd, x, lw["og"], lw["w_o"], lw["g2"], lw["rwt"])


def _route_kernel(aff_ref, tri_ref, posm_ref, base_ref, thr_scr, need_scr, runeq_scr, runsel_scr, *, cap):
    j = pl.program_id(0)

    @pl.when(j == 0)
    def _():
        def refine(i, thr):
            cand = thr | jnp.left_shift(jnp.int32(1), 30 - i)
            bits = pltpu.bitcast(aff_ref[...], jnp.int32)
            cnt = jnp.sum(jnp.where(bits >= cand, 1.0, 0.0), axis=1, keepdims=True)
            return jnp.where(cnt >= cap, cand, thr)

        thr = lax.fori_loop(0, 31, refine, jnp.zeros((N_EXPERTS, 1), jnp.int32))
        bits = pltpu.bitcast(aff_ref[...], jnp.int32)
        above = jnp.sum(jnp.where(bits > thr, 1.0, 0.0), axis=1, keepdims=True)
        thr_scr[...] = jnp.broadcast_to(thr, (N_EXPERTS, LANES))
        need_scr[...] = jnp.broadcast_to(cap - above, (N_EXPERTS, LANES))
        runeq_scr[...] = jnp.zeros((N_EXPERTS, LANES), f32)
        runsel_scr[...] = jnp.zeros((N_EXPERTS, LANES), f32)

    thr = thr_scr[:, 0:1]
    need = need_scr[:, 0:1]
    tiles = []
    for u in range(ROUTE_TILES):
        t0 = pl.multiple_of((j * ROUTE_TILES + u) * TT, TT)
        bits = pltpu.bitcast(aff_ref[:, pl.ds(t0, TT)], jnp.int32)
        eq = bits == thr
        eq_f = jnp.where(eq, 1.0, 0.0)
        tiles.append((bits, eq, eq_f, _dot(eq_f.astype(bf16), tri_ref[...])))
    sels = []
    runeq = runeq_scr[:, 0:1]
    for bits, eq, eq_f, eq_before in tiles:
        sel = (bits > thr) | (eq & (runeq + eq_before < need))
        sel_f = jnp.where(sel, 1.0, 0.0)
        sels.append((sel, sel_f, _dot(sel_f.astype(bf16), tri_ref[...])))
        runeq = runeq + jnp.sum(eq_f, axis=1, keepdims=True)
    runeq_scr[...] = jnp.broadcast_to(runeq, (N_EXPERTS, LANES))
    base = runsel_scr[...]
    for u, (sel, sel_f, sel_before) in enumerate(sels):
        posm_ref[:, u * TT:(u + 1) * TT] = jnp.where(sel, base[:, 0:1] + sel_before, -1.0)
        base_ref[u] = base
        base = base + jnp.sum(sel_f, axis=1, keepdims=True)
    runsel_scr[...] = base


def _route(aff, row0, nrows, tri):
    cap = CAPACITY_FACTOR * nrows // N_EXPERTS
    nt = nrows // TT
    blk0 = row0 // nrows
    vec = pltpu.VMEM((N_EXPERTS, LANES), f32)
    return pl.pallas_call(
        functools.partial(_route_kernel, cap=cap), grid=(nt // ROUTE_TILES,),
        in_specs=[pl.BlockSpec((N_EXPERTS, nrows), lambda j: (0, blk0)),
                  pl.BlockSpec((TT, TT), lambda j: (0, 0))],
        out_specs=[pl.BlockSpec((N_EXPERTS, ROUTE_TILES * TT), lambda j: (0, j)),
                   pl.BlockSpec((ROUTE_TILES, N_EXPERTS, LANES), lambda j: (j, 0, 0))],
        out_shape=[jax.ShapeDtypeStruct((N_EXPERTS, nrows), f32),
                   jax.ShapeDtypeStruct((nt, N_EXPERTS, LANES), f32)],
        scratch_shapes=[pltpu.VMEM((N_EXPERTS, LANES), jnp.int32), vec, vec, vec],
        name="route",
        compiler_params=pltpu.CompilerParams(dimension_semantics=("arbitrary",)),
    )(aff, tri)


def _window_class(count, windows, align):
    longest = jnp.max(count, axis=1) + (align - 1)
    return sum((longest > w).astype(jnp.int32) for w in windows[:-1])


def _dispatch_kernel(big_ref, first_ref, posm_ref, h2_ref, xe_hbm, stage, sem, carry, *, cap, ntiles):
    j = pl.program_id(0)
    slot = j % 2

    @pl.when(j == 0)
    def _():
        carry[...] = jnp.zeros_like(carry)
        pad = DISP_WINDOWS[-1]
        stage[1, 0:pad, :] = jnp.zeros((pad, D_MODEL), bf16)
        pads = [pltpu.make_async_copy(stage.at[1, pl.ds(0, pad)], xe_hbm.at[e, pl.ds(cap, pad)], sem.at[1])
                for e in range(N_EXPERTS)]
        for p in pads:
            p.start()
        for p in pads:
            p.wait()

    def run_start(jj, e):
        return first_ref[jj * N_EXPERTS + e]

    def window_copy(jj, e, sl, win):
        a = pl.multiple_of(run_start(jj, e) // ROW_ALIGN * ROW_ALIGN, ROW_ALIGN)
        return pltpu.make_async_copy(stage.at[sl, pl.ds(e * win, win)], xe_hbm.at[e, pl.ds(a, win)], sem.at[sl])

    def for_tile_window(jj, fn):
        for flag, win in enumerate(DISP_WINDOWS):
            @pl.when(big_ref[jj] == flag)
            def _():
                fn(win)

    def wait_all(jj, sl):
        def go(win):
            for e in range(N_EXPERTS):
                window_copy(jj, e, sl, win).wait()
        for_tile_window(jj, go)

    def compact(win):
        row = lax.broadcasted_iota(jnp.int32, (win, TT), 0)
        starts = [run_start(j, e) // ROW_ALIGN * ROW_ALIGN for e in range(N_EXPERTS)]
        onehot = jnp.concatenate(
            [jnp.where((row + starts[e]).astype(f32) == posm_ref[e:e + 1, :], 1.0, 0.0)
             for e in range(N_EXPERTS)], axis=0).astype(bf16)
        rows = _dot(onehot, h2_ref[...])
        head = lax.broadcasted_iota(jnp.int32, (ROW_ALIGN, D_MODEL), 0)
        for e in range(N_EXPERTS):
            lead = run_start(j, e) - starts[e]
            w = rows[e * win:(e + 1) * win]
            top = w[:ROW_ALIGN] + jnp.where(head < lead, carry[e], 0.0)
            stage[slot, e * win:e * win + ROW_ALIGN, :] = top.astype(bf16)
            stage[slot, e * win + ROW_ALIGN:(e + 1) * win, :] = w[ROW_ALIGN:].astype(bf16)
        for e in range(N_EXPERTS):
            end = jnp.where(j + 1 < ntiles, run_start(jnp.minimum(j + 1, ntiles - 1), e), cap)
            off = jnp.minimum(end // ROW_ALIGN * ROW_ALIGN - starts[e], win - ROW_ALIGN)
            off = pl.multiple_of(off, ROW_ALIGN)
            carry[e] = stage[slot, pl.ds(e * win + off, ROW_ALIGN), :].astype(f32)

    for_tile_window(j, compact)

    @pl.when(j > 0)
    def _():
        wait_all(j - 1, 1 - slot)

    def start_all(win):
        for e in range(N_EXPERTS):
            window_copy(j, e, slot, win).start()
    for_tile_window(j, start_all)

    @pl.when(j == ntiles - 1)
    def _():
        wait_all(j, slot)


def _dispatch(h2, posm, first, count, row0, nrows):
    cap = CAPACITY_FACTOR * nrows // N_EXPERTS
    nt = nrows // TT
    blk0 = row0 // TT
    big = _window_class(count, DISP_WINDOWS, ROW_ALIGN)
    return pl.pallas_call(
        functools.partial(_dispatch_kernel, cap=cap, ntiles=nt),
        grid_spec=pltpu.PrefetchScalarGridSpec(
            num_scalar_prefetch=2, grid=(nt,),
            in_specs=[pl.BlockSpec((N_EXPERTS, TT), lambda j, *_: (0, j)),
                      pl.BlockSpec((TT, D_MODEL), lambda j, *_: (blk0 + j, 0))],
            out_specs=pl.BlockSpec(memory_space=pl.ANY),
            scratch_shapes=[pltpu.VMEM((2, N_EXPERTS * DISP_WINDOWS[-1], D_MODEL), bf16),
                            pltpu.SemaphoreType.DMA((2,)),
                            pltpu.VMEM((N_EXPERTS, ROW_ALIGN, D_MODEL), f32)],
        ),
        out_shape=jax.ShapeDtypeStruct((N_EXPERTS, cap + DISP_WINDOWS[-1], D_MODEL), bf16),
        name="dispatch",
        compiler_params=pltpu.CompilerParams(dimension_semantics=("arbitrary",), vmem_limit_bytes=VMEM_LIMIT),
    )(big, first.reshape(-1), posm, h2)


def _ffn_kernel(xe_ref, rw_ref, wg_ref, wu_ref, wd_ref, ye_ref):
    e = pl.program_id(0)
    x = xe_ref[...]
    logits = _dot(x, rw_ref[...])
    lane = lax.broadcasted_iota(jnp.int32, logits.shape, 1)
    logits = jnp.where(lane < N_EXPERTS, logits, NEG)
    ex = jnp.exp(logits - jnp.max(logits, axis=-1, keepdims=True))
    gate = jnp.sum(jnp.where(lane == e, ex, 0.0), axis=-1, keepdims=True) / jnp.sum(ex, axis=-1, keepdims=True)
    hg = _dot(x, wg_ref[...].astype(bf16))
    hu = _dot(x, wu_ref[...].astype(bf16))
    h = (hg * (1.0 / (1.0 + jnp.exp(-hg))) * hu).astype(bf16)
    ye_ref[...] = (_dot(h, wd_ref[...].astype(bf16)) * gate).astype(bf16)


def _ffn(xe, lw, nrows):
    cap = CAPACITY_FACTOR * nrows // N_EXPERTS
    layer = lw["layer"]
    wspec = pl.BlockSpec((None, None, D_MODEL, D_EXPERT), lambda e, t: (layer, e, 0, 0))
    return pl.pallas_call(
        _ffn_kernel, grid=(N_EXPERTS, cap // TME),
        in_specs=[
            pl.BlockSpec((None, TME, D_MODEL), lambda e, t: (e, t, 0)),
            pl.BlockSpec((D_MODEL, LANES), lambda e, t: (0, 0)),
            wspec, wspec,
            pl.BlockSpec((None, None, D_EXPERT, D_MODEL), lambda e, t: (layer, e, 0, 0)),
        ],
        out_specs=pl.BlockSpec((None, TME, D_MODEL), lambda e, t: (e, t, 0)),
        out_shape=jax.ShapeDtypeStruct((N_EXPERTS, cap, D_MODEL), bf16),
        name="expert_ffn",
        compiler_params=pltpu.CompilerParams(dimension_semantics=("parallel", "arbitrary"),
                                             vmem_limit_bytes=VMEM_LIMIT),
    )(xe, lw["rw"], lw["e_wg"], lw["e_wu"], lw["e_wd"])


def _combine_kernel(big_ref, first_ref, posm_ref, x_ref, ye_hbm, g_ref, o_ref, buf, sem, *, cap, ntiles, final):
    j = pl.program_id(0)

    def window_start(jj, e, win):
        return jnp.minimum(first_ref[jj * N_EXPERTS + e] // COMB_ALIGN * COMB_ALIGN, cap - win)

    def window_copy(jj, e, slot, win):
        a = pl.multiple_of(window_start(jj, e, win), COMB_ALIGN)
        return pltpu.make_async_copy(ye_hbm.at[e, pl.ds(a, win)], buf.at[slot, pl.ds(e * win, win)],
                                     sem.at[slot])

    def for_tile_window(jj, fn):
        for flag, win in enumerate(COMB_WINDOWS):
            @pl.when(big_ref[jj] == flag)
            def _():
                fn(win)

    def start_all(jj, slot):
        def go(win):
            for e in range(N_EXPERTS):
                window_copy(jj, e, slot, win).start()
        for_tile_window(jj, go)

    @pl.when(j == 0)
    def _():
        for ahead in range(COMB_BUFS - 1):
            start_all(ahead, ahead)

    slot = j % COMB_BUFS

    def wait_all(win):
        for e in range(N_EXPERTS):
            window_copy(j, e, slot, win).wait()
    for_tile_window(j, wait_all)

    @pl.when(j + COMB_BUFS - 1 < ntiles)
    def _():
        start_all(j + COMB_BUFS - 1, (j + COMB_BUFS - 1) % COMB_BUFS)

    def add_rows(win):
        row = lax.broadcasted_iota(jnp.int32, (win, TT), 0)
        onehot = jnp.concatenate(
            [jnp.where((row + window_start(j, e, win)).astype(f32) == posm_ref[e:e + 1, :], 1.0, 0.0)
             for e in range(N_EXPERTS)], axis=0).astype(bf16)
        out = x_ref[...] + _dot_tn(onehot, buf[slot, 0:N_EXPERTS * win, :])
        o_ref[...] = _rms(out, g_ref[...]) if final else out
    for_tile_window(j, add_rows)


def _combine(x, ye, posm, first, count, row0, nrows, final_g=None):
    cap = CAPACITY_FACTOR * nrows // N_EXPERTS
    nt = nrows // TT
    blk0 = row0 // TT
    final = final_g is not None
    big = _window_class(count, COMB_WINDOWS, COMB_ALIGN)
    gain = final_g if final else jnp.ones((1, D_MODEL), f32)
    return pl.pallas_call(
        functools.partial(_combine_kernel, cap=cap, ntiles=nt, final=final),
        grid_spec=pltpu.PrefetchScalarGridSpec(
            num_scalar_prefetch=2, grid=(nt,),
            in_specs=[pl.BlockSpec((N_EXPERTS, TT), lambda j, *_: (0, j)),
                      pl.BlockSpec((TT, D_MODEL), lambda j, *_: (blk0 + j, 0)),
                      pl.BlockSpec(memory_space=pl.ANY),
                      pl.BlockSpec((1, D_MODEL), lambda j, *_: (0, 0))],
            out_specs=pl.BlockSpec((TT, D_MODEL), lambda j, *_: ((0 if final else blk0) + j, 0)),
            scratch_shapes=[pltpu.VMEM((COMB_BUFS, N_EXPERTS * COMB_WINDOWS[-1], D_MODEL), bf16),
                            pltpu.SemaphoreType.DMA((COMB_BUFS,))],
        ),
        out_shape=jax.ShapeDtypeStruct((nrows, D_MODEL) if final else x.shape, f32),
        input_output_aliases={} if final else {3: 0},
        name="combine",
        compiler_params=pltpu.CompilerParams(dimension_semantics=("arbitrary",), vmem_limit_bytes=VMEM_LIMIT),
    )(big, first.reshape(-1), posm, x, ye, gain)


def _stack_kernel(a_ref, b_ref, o_ref, *, na_blocks):
    o_ref[...] = jnp.where(pl.program_id(0) < na_blocks, a_ref[...], b_ref[...])


def _stack_rows(a, b):
    na, nb = a.shape[0] // TM, b.shape[0] // TM
    return pl.pallas_call(
        functools.partial(_stack_kernel, na_blocks=na), grid=(na + nb,),
        in_specs=[pl.BlockSpec((TM, D_MODEL), lambda i: (jnp.minimum(i, na - 1), 0)),
                  pl.BlockSpec((TM, D_MODEL), lambda i: (jnp.maximum(i - na, 0), 0))],
        out_specs=pl.BlockSpec((TM, D_MODEL), lambda i: (i, 0)),
        out_shape=jax.ShapeDtypeStruct((a.shape[0] + b.shape[0], D_MODEL), a.dtype),
        name="stack_rows",
        compiler_params=pltpu.CompilerParams(dimension_semantics=("arbitrary",)),
    )(a, b)


def _natten_bias_table(rpb):
    c = np.arange(GRID_W)
    cs = np.clip(c - WIN_C // 2, 0, GRID_W - WIN_C)
    kc = np.arange(GRID_W)
    col_mask = (kc[None, :] >= cs[:, None]) & (kc[None, :] < cs[:, None] + WIN_C)
    dc = np.clip(kc[None, :] - c[:, None], -(WIN_C - 1), WIN_C - 1) + (WIN_C - 1)
    tb = jnp.where(col_mask[None, None], rpb[:, :, dc].astype(f32), NEG)
    dw = np.arange(WIN_R)[:, None] + np.arange(WIN_R)[None, :]
    t = tb[:, dw]
    t = t.transpose(1, 0, 3, 2, 4).reshape(WIN_R, C_HEADS * GRID_W, WIN_R * GRID_W)
    return t


def _rope_slabs():
    inv = 1.0 / (ROPE_THETA ** (jnp.arange(0, ROPE_DIM, 2, dtype=f32) / ROPE_DIM))
    ang = jnp.arange(SEQ, dtype=f32)[:, None] * inv[None, :]
    cos, sin = jnp.cos(ang), jnp.sin(ang)
    ones = jnp.ones((SEQ, NOPE_DIM), f32)
    zpad = jnp.zeros((SEQ, LANES - NOPE_DIM - ROPE_DIM), f32)
    cosq = jnp.concatenate([ones, cos, cos, zpad], axis=1)
    sinq = jnp.concatenate([0.0 * ones, sin, sin, zpad], axis=1)
    return cosq, sinq


def _pack_layer(i, norm1_g, w_in, a_vnorm_g, a_ws, a_bs, b_conv, c_rpb, d_qnorm_g, d_kvnorm_g,
                d_w_uq, d_w_ukv, out_norm_g, w_o, norm2_g, router_w, e_w_gate, e_w_up, e_w_down):
    half = ROPE_DIM // 2
    w = w_in[i]
    kr0 = OFF_D + Q_RANK + KV_RANK
    kr = w[:, kr0:kr0 + ROPE_DIM]
    z64 = jnp.zeros((D_MODEL, NOPE_DIM), f32)
    z32 = jnp.zeros((D_MODEL, LANES - NOPE_DIM - ROPE_DIM), f32)
    kr_slab = jnp.concatenate([z64, kr, z32], axis=1)
    kr_rot = jnp.concatenate([z64, -kr[:, half:], kr[:, :half], z32], axis=1)
    w_pack = jnp.concatenate([w[:, :kr0], kr_slab, kr_rot], axis=1).astype(bf16)

    uq = d_w_uq[i].reshape(Q_RANK, D_HEADS, NOPE_DIM + ROPE_DIM)
    nope, rope = uq[..., :NOPE_DIM], uq[..., NOPE_DIM:]
    zq = jnp.zeros((Q_RANK, D_HEADS, LANES - NOPE_DIM - ROPE_DIM), f32)
    w_uq = jnp.concatenate([nope, rope, zq], axis=-1).reshape(Q_RANK, D_HEADS * LANES)
    w_uq_rot = jnp.concatenate([0.0 * nope, -rope[..., half:], rope[..., :half], zq],
                               axis=-1).reshape(Q_RANK, D_HEADS * LANES)
    ukv = d_w_ukv[i].reshape(KV_RANK, D_HEADS, NOPE_DIM + V_DIM)
    w_k = jnp.concatenate([ukv[..., :NOPE_DIM], jnp.zeros((KV_RANK, D_HEADS, LANES - NOPE_DIM), f32)],
                          axis=-1).reshape(KV_RANK, D_HEADS * LANES)
    w_v = ukv[..., NOPE_DIM:].reshape(KV_RANK, D_HEADS * V_DIM)
    rw = jnp.concatenate([router_w[i], jnp.zeros((D_MODEL, LANES - N_EXPERTS), f32)], axis=1)
    return {
        "g1": norm1_g[i].reshape(1, D_MODEL),
        "w_in": w_pack,
        "a_vg": a_vnorm_g[i].reshape(1, GROUP_W),
        "a_ws": a_ws[i].astype(bf16),
        "a_bias": jnp.repeat(a_bs[i].T, HEAD_DIM, axis=1),
        "b_conv": b_conv[i],
        "c_bias": _natten_bias_table(c_rpb[i]),
        "d_qg": d_qnorm_g[i].reshape(1, Q_RANK),
        "d_kvg": d_kvnorm_g[i].reshape(1, KV_RANK),
        "w_uq": w_uq.astype(bf16),
        "w_uq_rot": w_uq_rot.astype(bf16),
        "w_k": w_k.astype(bf16),
        "w_v": w_v.astype(bf16),
        "og": out_norm_g[i].reshape(1, D_MODEL),
        "w_o": w_o[i].astype(bf16),
        "g2": norm2_g[i].reshape(1, D_MODEL),
        "rw": rw.astype(bf16),
        "rwt": router_w[i].T.astype(bf16),
        "layer": i,
        "e_wg": e_w_gate,
        "e_wu": e_w_up,
        "e_wd": e_w_down,
    }


def _moe(xn, h2, aff, lw, tri, final_g=None):
    x = xn
    outs = []
    for row0, nrows in ((0, BATCH * SEQ), (BATCH * SEQ, DEC_BATCH * SEQ)):
        cap = CAPACITY_FACTOR * nrows // N_EXPERTS
        posm, base = _route(aff, row0, nrows, tri)
        first = base[:, :, 0].astype(jnp.int32)
        count = jnp.diff(first, axis=0, append=jnp.full((1, N_EXPERTS), cap, jnp.int32))
        xe = _dispatch(h2, posm, first, count, row0, nrows)
        ye = _ffn(xe, lw, nrows)
        if final_g is None:
            x = _combine(x, ye, posm, first, count, row0, nrows)
        else:
            outs.append(_combine(xn, ye, posm, first, count, row0, nrows, final_g))
    return x if final_g is None else outs


def kernel(x_prompt, x_sample, norm1_g, w_in, a_vnorm_g, a_ws, a_bs, b_conv, c_rpb, d_qnorm_g, d_kvnorm_g, d_w_uq, d_w_ukv, out_norm_g, w_o, norm2_g, router_w, e_w_gate, e_w_up, e_w_down, final_norm_g):
    n_p = BATCH * SEQ
    n_s = DEC_BATCH * SEQ
    x = _stack_rows(x_prompt.reshape(n_p, D_MODEL), x_sample.reshape(n_s, D_MODEL))
    cosq, sinq = _rope_slabs()
    tri = jnp.asarray(np.triu(np.ones((TT, TT), np.float32), 1), bf16)
    for i in range(DEPTH):
        lw = _pack_layer(i, norm1_g, w_in, a_vnorm_g, a_ws, a_bs, b_conv, c_rpb, d_qnorm_g, d_kvnorm_g,
                         d_w_uq, d_w_ukv, out_norm_g, w_o, norm2_g, router_w, e_w_gate, e_w_up, e_w_down)
        ya, yb, zc, qt, k4, vt = _inproj(x, lw, cosq, sinq)
        yc = _natten(zc, lw["c_bias"])
        yd = _mla(qt, k4, vt)
        xn, h2, aff = _outproj(ya, yb, yc, yd, x, lw)
        x = _moe(xn, h2, aff, lw, tri, final_norm_g.reshape(1, D_MODEL) if i == DEPTH - 1 else None)
    y_p, y_s = x
    return (y_p.reshape(BATCH, SEQ, D_MODEL), y_s.reshape(DEC_BATCH, SEQ, D_MODEL))
```

```python
import functools
import math

import numpy as np
import jax
import jax.numpy as jnp
from jax import lax
from jax.experimental import pallas as pl
from jax.experimental.pallas import tpu as pltpu

D_MODEL = 1024
BATCH = 4
SEQ = 8192
DEPTH = 4
DEC_BATCH = 2
GROUP_W = 256
HEAD_DIM = 64
A_HEADS = 4
CHUNK = 128
C_HEADS = 4
GRID_W = 64
WIN_R = 8
WIN_C = 16
D_HEADS = 4
Q_RANK = 256
KV_RANK = 128
NOPE_DIM = 64
ROPE_DIM = 32
V_DIM = 64
ROPE_THETA = 10000.0
N_EXPERTS = 16
D_EXPERT = 1024
CAPACITY_FACTOR = 2
EPS = 1e-6

NSEQ = BATCH + DEC_BATCH
T_ALL = NSEQ * SEQ
ROWS = SEQ // GRID_W
LANES = 128
NEG = -1e30

A_COLS = 2 * GROUP_W
B_COLS = 3 * GROUP_W
C_COLS = 3 * GROUP_W
OFF_B = A_COLS
OFF_C = OFF_B + B_COLS
OFF_D = OFF_C + C_COLS
D_PACK = Q_RANK + KV_RANK + 2 * LANES
N_IN_PACK = OFF_D + D_PACK

TM = 1024
TQ = 256
TK = 512
L_ROWS = 16
MLA_UNROLL = 4
MLA_LAG = 1
MLA_QSCALE = (NOPE_DIM + ROPE_DIM) ** -0.5 * math.log2(math.e)
ROWS_PER_STEP = 16
TME = 1024
TT = 256
ROUTE_TILES = 4
ROW_ALIGN = 16
DISP_WINDOWS = (80, 144, TT + ROW_ALIGN)
COMB_ALIGN = ROW_ALIGN
COMB_WINDOWS = (80, 144, TT + COMB_ALIGN)
COMB_BUFS = 3
VMEM_LIMIT = 56 * 1024 * 1024

f32 = jnp.float32
bf16 = jnp.bfloat16


def _rms(x, g):
    return x * lax.rsqrt(jnp.mean(x * x, axis=-1, keepdims=True) + EPS) * g


def _dot(a, b):
    return jnp.dot(a, b, preferred_element_type=f32)


def _dot_nt(a, b):
    return lax.dot_general(a, b, (((1,), (1,)), ((), ())), preferred_element_type=f32)


def _dot_tn(a, b):
    return lax.dot_general(a, b, (((0,), (0,)), ((), ())), preferred_element_type=f32)


def _inproj_kernel(x_ref, xp_ref, xn_ref, g1_ref, win_ref, avg_ref, aws_ref, abias_ref,
                   bconv_ref, qg_ref, kvg_ref, wuq_ref, wuqr_ref, wk_ref, wv_ref,
                   cos_ref, sin_ref,
                   ya_ref, yb_ref, zc_ref, qt_ref, k4_ref, vt_ref):
    i = pl.program_id(0)
    j = i % (SEQ // TM)
    g1 = g1_ref[...]
    h = _rms(x_ref[...], g1).astype(bf16)

    za = _dot(h, win_ref[:, 0:A_COLS])
    zb = _dot(h, win_ref[:, OFF_B:OFF_B + B_COLS])
    zc = _dot(h, win_ref[:, OFF_C:OFF_C + C_COLS])
    zd = _dot(h, win_ref[:, OFF_D:OFF_D + D_PACK])
    xh = jnp.concatenate([xp_ref[...], xn_ref[...]], axis=0)
    hh = _rms(xh, g1).astype(bf16)
    zh = _dot(hh, win_ref[:, OFF_B + GROUP_W:OFF_B + B_COLS])

    zc_ref[:, 0:GROUP_W] = (zc[:, 0:GROUP_W] * (HEAD_DIM ** -0.5)).astype(bf16)
    zc_ref[:, GROUP_W:] = zc[:, GROUP_W:].astype(bf16)

    cqn = _rms(zd[:, 0:Q_RANK], qg_ref[...]).astype(bf16)
    ckvn = _rms(zd[:, Q_RANK:Q_RANK + KV_RANK], kvg_ref[...]).astype(bf16)
    cos = cos_ref[...]
    sin = sin_ref[...]
    o = Q_RANK + KV_RANK
    kr = zd[:, o:o + LANES] * cos + zd[:, o + LANES:o + 2 * LANES] * sin
    q_pre = _dot(cqn, wuq_ref[...])
    q_rot = _dot(cqn, wuqr_ref[...])
    kn = _dot(ckvn, wk_ref[...])
    vt = _dot(ckvn, wv_ref[...]).T.astype(bf16)

    za = jax.nn.gelu(za)
    u = za[:, :GROUP_W]
    v = _rms(za[:, GROUP_W:], avg_ref[...]).astype(bf16)
    lane = lax.broadcasted_iota(jnp.int32, (CHUNK, GROUP_W), 1)
    for c in range(TM // CHUNK):
        vc = v[c * CHUNK:(c + 1) * CHUNK, :]
        sv = abias_ref[...]
        for hd in range(A_HEADS):
            r = _dot(aws_ref[hd], vc)
            sv = sv + jnp.where((lane >= hd * HEAD_DIM) & (lane < (hd + 1) * HEAD_DIM), r, 0.0)
        ya_ref[c * CHUNK:(c + 1) * CHUNK, :] = u[c * CHUNK:(c + 1) * CHUNK, :] * sv

    bg = zb[:, :GROUP_W]
    y = zb[:, GROUP_W:2 * GROUP_W] * zb[:, 2 * GROUP_W:]
    yh = zh[:, :GROUP_W] * zh[:, GROUP_W:]
    y_before = jnp.where(j == 0, 0.0, yh[7:8, :])
    y_after = jnp.where(j == SEQ // TM - 1, 0.0, yh[8:9, :])
    row = lax.broadcasted_iota(jnp.int32, (TM, GROUP_W), 0)
    y_m1 = jnp.where(row == 0, y_before, pltpu.roll(y, 1, axis=0))
    y_p1 = jnp.where(row == TM - 1, y_after, pltpu.roll(y, TM - 1, axis=0))
    wc = bconv_ref[...]
    yb_ref[...] = bg * (wc[0:1, :] * y_m1 + wc[1:2, :] * y + wc[2:3, :] * y_p1)

    for hd in range(D_HEADS):
        sl = slice(hd * LANES, (hd + 1) * LANES)
        qt_ref[hd] = ((q_pre[:, sl] * cos + q_rot[:, sl] * sin) * MLA_QSCALE).T.astype(bf16)
        k4_ref[hd] = (kn[:, sl] + kr).astype(bf16)
        vt_ref[hd] = vt[hd * V_DIM:(hd + 1) * V_DIM, :]


def _inproj(x, lw, cosq, sinq):
    nt = T_ALL // TM
    tps = SEQ // TM
    full = lambda shape: pl.BlockSpec(shape, lambda i: (0,) * len(shape))
    in_specs = [
        pl.BlockSpec((TM, D_MODEL), lambda i: (i, 0)),
        pl.BlockSpec((8, D_MODEL), lambda i: (jnp.maximum(i * (TM // 8) - 1, 0), 0)),
        pl.BlockSpec((8, D_MODEL), lambda i: (jnp.minimum((i + 1) * (TM // 8), T_ALL // 8 - 1), 0)),
        full((1, D_MODEL)),
        full((D_MODEL, N_IN_PACK)),
        full((1, GROUP_W)),
        full((A_HEADS, CHUNK, CHUNK)),
        full((CHUNK, GROUP_W)),
        full((3, GROUP_W)),
        full((1, Q_RANK)),
        full((1, KV_RANK)),
        full((Q_RANK, D_HEADS * LANES)),
        full((Q_RANK, D_HEADS * LANES)),
        full((KV_RANK, D_HEADS * LANES)),
        full((KV_RANK, D_HEADS * V_DIM)),
        pl.BlockSpec((TM, LANES), lambda i: (i % tps, 0)),
        pl.BlockSpec((TM, LANES), lambda i: (i % tps, 0)),
    ]
    out_specs = [
        pl.BlockSpec((TM, GROUP_W), lambda i: (i, 0)),
        pl.BlockSpec((TM, GROUP_W), lambda i: (i, 0)),
        pl.BlockSpec((TM, C_COLS), lambda i: (i, 0)),
        pl.BlockSpec((None, D_HEADS, LANES, TM), lambda i: (i // tps, 0, 0, i % tps)),
        pl.BlockSpec((None, D_HEADS, TM, LANES), lambda i: (i // tps, 0, i % tps, 0)),
        pl.BlockSpec((None, D_HEADS, V_DIM, TM), lambda i: (i // tps, 0, 0, i % tps)),
    ]
    out_shape = [
        jax.ShapeDtypeStruct((T_ALL, GROUP_W), f32),
        jax.ShapeDtypeStruct((T_ALL, GROUP_W), f32),
        jax.ShapeDtypeStruct((T_ALL, C_COLS), bf16),
        jax.ShapeDtypeStruct((NSEQ, D_HEADS, LANES, SEQ), bf16),
        jax.ShapeDtypeStruct((NSEQ, D_HEADS, SEQ, LANES), bf16),
        jax.ShapeDtypeStruct((NSEQ, D_HEADS, V_DIM, SEQ), bf16),
    ]
    return pl.pallas_call(
        _inproj_kernel, grid=(nt,), in_specs=in_specs, out_specs=out_specs, out_shape=out_shape,
        name="inproj",
        compiler_params=pltpu.CompilerParams(dimension_semantics=("parallel",),
                                             vmem_limit_bytes=VMEM_LIMIT),
    )(x, x, x, lw["g1"], lw["w_in"], lw["a_vg"], lw["a_ws"], lw["a_bias"], lw["b_conv"],
      lw["d_qg"], lw["d_kvg"], lw["w_uq"], lw["w_uq_rot"], lw["w_k"], lw["w_v"], cosq, sinq)


def _natten_kernel(q_ref, k_ref, v_ref, bias_ref, o_ref):
    jb = pl.program_id(1)
    lane = lax.broadcasted_iota(jnp.int32, (GRID_W, GROUP_W), 1)
    head_of_lane = lane // HEAD_DIM
    nkeys = WIN_R * GRID_W

    jobs = []
    for i in range(ROWS_PER_STEP):
        r = jb * ROWS_PER_STEP + i
        rs = jnp.clip(r - WIN_R // 2, 0, ROWS - WIN_R)
        d = rs - r + (WIN_R - 1)
        kstart = pl.multiple_of(rs * GRID_W, GRID_W)
        q = q_ref[i * GRID_W:(i + 1) * GRID_W, :]
        zero = jnp.zeros_like(q)
        qq = jnp.concatenate([jnp.where(head_of_lane == h, q, zero) for h in range(C_HEADS)], axis=0)
        jobs.append((_dot_nt(qq, k_ref[pl.ds(kstart, nkeys), :]), d, kstart))
    for i, (sc, d, kstart) in enumerate(jobs):
        sc = sc + bias_ref[d]
        m = jnp.max(sc, axis=-1, keepdims=True)
        p = jnp.exp(sc - m)
        l = jnp.sum(p, axis=-1, keepdims=True)
        o = _dot(p.astype(bf16), v_ref[pl.ds(kstart, nkeys), :]) / l
        out = jnp.zeros((GRID_W, GROUP_W), f32)
        for h in range(C_HEADS):
            out = jnp.where(head_of_lane == h, o[h * GRID_W:(h + 1) * GRID_W], out)
        o_ref[i * GRID_W:(i + 1) * GRID_W, :] = out


def _natten(zc, bias):
    nb = ROWS // ROWS_PER_STEP
    tq = ROWS_PER_STEP * GRID_W
    return pl.pallas_call(
        _natten_kernel, grid=(NSEQ, nb),
        in_specs=[
            pl.BlockSpec((tq, GROUP_W), lambda b, j: (b * nb + j, 0)),
            pl.BlockSpec((SEQ, GROUP_W), lambda b, j: (b, 1)),
            pl.BlockSpec((SEQ, GROUP_W), lambda b, j: (b, 2)),
            pl.BlockSpec((WIN_R, C_HEADS * GRID_W, WIN_R * GRID_W), lambda b, j: (0, 0, 0)),
        ],
        out_specs=pl.BlockSpec((tq, GROUP_W), lambda b, j: (b * nb + j, 0)),
        out_shape=jax.ShapeDtypeStruct((T_ALL, GROUP_W), f32),
        name="natten",
        compiler_params=pltpu.CompilerParams(dimension_semantics=("parallel", "arbitrary"),
                                             vmem_limit_bytes=VMEM_LIMIT),
    )(zc, zc, zc, bias)


def _mla_kernel(qt_ref, k_ref, vt_ref, o_ref, s_scr):
    nchunk = SEQ // TK
    heads = list(range(D_HEADS))
    qts = [qt_ref[h] for h in heads]

    def qk(slot, h, ci):
        ks = pl.multiple_of(ci * TK, TK)
        s_scr[slot, h] = _dot(k_ref[h, pl.ds(ks, TK), :], qts[h])

    ones = jnp.ones((L_ROWS, TK), bf16)

    def softmax_pv(slot, h, ci, stat):
        m, acc = stat
        ks = pl.multiple_of(ci * TK, TK)
        vt = jnp.concatenate([vt_ref[h, :, pl.ds(ks, TK)], ones], axis=0)
        s = s_scr[slot, h]
        m_new = jnp.maximum(m, jnp.max(s, axis=0, keepdims=True))
        alpha = jnp.exp2(m - m_new)
        p = jnp.exp2(s - m_new)
        acc = acc * alpha + _dot(vt, p.astype(bf16))
        return m_new, acc

    def half(cur, ci, stats, prefetch=True):
        out = []
        for h in heads:
            if prefetch:
                qk(1 - cur, h, ci + 1)
            if h >= MLA_LAG:
                out.append(softmax_pv(cur, h - MLA_LAG, ci, stats[h - MLA_LAG]))
        for h in range(D_HEADS - MLA_LAG, D_HEADS):
            out.append(softmax_pv(cur, h, ci, stats[h]))
        return tuple(out)

    def body(cp, stats):
        for u in range(MLA_UNROLL):
            stats = half(u % 2, MLA_UNROLL * cp + u, stats)
        return stats

    for h in heads:
        qk(0, h, 0)
    init = tuple((jnp.full((1, TQ), NEG, f32), jnp.zeros((V_DIM + L_ROWS, TQ), f32)) for _ in heads)
    res = lax.fori_loop(0, nchunk // MLA_UNROLL - 1, body, init)
    for u in range(MLA_UNROLL):
        res = half(u % 2, nchunk - MLA_UNROLL + u, res, prefetch=u < MLA_UNROLL - 1)
    o_ref[...] = jnp.concatenate([acc[:V_DIM] / acc[V_DIM:V_DIM + 1] for (_, acc) in res], axis=0).T


def _mla(qt, k4, vt):
    nq = SEQ // TQ
    return pl.pallas_call(
        _mla_kernel, grid=(NSEQ, nq),
        in_specs=[
            pl.BlockSpec((None, D_HEADS, LANES, TQ), lambda b, i: (b, 0, 0, i)),
            pl.BlockSpec((None, D_HEADS, SEQ, LANES), lambda b, i: (b, 0, 0, 0)),
            pl.BlockSpec((None, D_HEADS, V_DIM, SEQ), lambda b, i: (b, 0, 0, 0)),
        ],
        out_specs=pl.BlockSpec((TQ, GROUP_W), lambda b, i: (b * nq + i, 0)),
        out_shape=jax.ShapeDtypeStruct((T_ALL, GROUP_W), f32),
        scratch_shapes=[pltpu.VMEM((2, D_HEADS, TK, TQ), f32)],
        name="mla",
        compiler_params=pltpu.CompilerParams(
            dimension_semantics=("parallel", "arbitrary"),
            vmem_limit_bytes=VMEM_LIMIT),
    )(qt, k4, vt)


def _outproj_kernel(ya_ref, yb_ref, yc_ref, yd_ref, x_ref, og_ref, wo_ref, g2_ref, rwt_ref,
                    xn_ref, h2_ref, aff_ref):
    y = jnp.concatenate(
        [_rms(y_ref[...], og_ref[:, g * GROUP_W:(g + 1) * GROUP_W]).astype(bf16)
         for g, y_ref in enumerate((ya_ref, yb_ref, yc_ref, yd_ref))], axis=1)
    acc = x_ref[...] + _dot(y, wo_ref[...])
    xn_ref[...] = acc
    h2 = _rms(acc, g2_ref[...]).astype(bf16)
    h2_ref[...] = h2
    logits = _dot_nt(rwt_ref[...], h2)
    e = jnp.exp(logits - jnp.max(logits, axis=0, keepdims=True))
    aff_ref[...] = e / jnp.sum(e, axis=0, keepdims=True)


def _outproj(ya, yb, yc, yd, x, lw):
    nt = T_ALL // TM
    full = lambda shape: pl.BlockSpec(shape, lambda i: (0,) * len(shape))
    yspec = pl.BlockSpec((TM, GROUP_W), lambda i: (i, 0))
    xspec = pl.BlockSpec((TM, D_MODEL), lambda i: (i, 0))
    return pl.pallas_call(
        _outproj_kernel, grid=(nt,),
        in_specs=[yspec, yspec, yspec, yspec, xspec, full((1, D_MODEL)), full((D_MODEL, D_MODEL)),
                  full((1, D_MODEL)), full((N_EXPERTS, D_MODEL))],
        out_specs=[xspec, xspec, pl.BlockSpec((N_EXPERTS, TM), lambda i: (0, i))],
        out_shape=[jax.ShapeDtypeStruct((T_ALL, D_MODEL), f32),
                   jax.ShapeDtypeStruct((T_ALL, D_MODEL), bf16),
                   jax.ShapeDtypeStruct((N_EXPERTS, T_ALL), f32)],
        name="outproj",
        compiler_params=pltpu.CompilerParams(dimension_semantics=("parallel",),
                                             vmem_limit_bytes=VMEM_LIMIT),
    )(ya, yb, yc, yd, x, lw["og"], lw["w_o"], lw["g2"], lw["rwt"])


def _route_kernel(aff_ref, tri_ref, posm_ref, base_ref, thr_scr, need_scr, runeq_scr, runsel_scr, *, cap):
    j = pl.program_id(0)

    @pl.when(j == 0)
    def _():
        def refine(i, thr):
            cand = thr | jnp.left_shift(jnp.int32(1), 30 - i)
            bits = pltpu.bitcast(aff_ref[...], jnp.int32)
            cnt = jnp.sum(jnp.where(bits >= cand, 1.0, 0.0), axis=1, keepdims=True)
            return jnp.where(cnt >= cap, cand, thr)

        thr = lax.fori_loop(0, 31, refine, jnp.zeros((N_EXPERTS, 1), jnp.int32))
        bits = pltpu.bitcast(aff_ref[...], jnp.int32)
        above = jnp.sum(jnp.where(bits > thr, 1.0, 0.0), axis=1, keepdims=True)
        thr_scr[...] = jnp.broadcast_to(thr, (N_EXPERTS, LANES))
        need_scr[...] = jnp.broadcast_to(cap - above, (N_EXPERTS, LANES))
        runeq_scr[...] = jnp.zeros((N_EXPERTS, LANES), f32)
        runsel_scr[...] = jnp.zeros((N_EXPERTS, LANES), f32)

    thr = thr_scr[:, 0:1]
    need = need_scr[:, 0:1]
    tiles = []
    for u in range(ROUTE_TILES):
        t0 = pl.multiple_of((j * ROUTE_TILES + u) * TT, TT)
        bits = pltpu.bitcast(aff_ref[:, pl.ds(t0, TT)], jnp.int32)
        eq = bits == thr
        eq_f = jnp.where(eq, 1.0, 0.0)
        tiles.append((bits, eq, eq_f, _dot(eq_f.astype(bf16), tri_ref[...])))
    sels = []
    runeq = runeq_scr[:, 0:1]
    for bits, eq, eq_f, eq_before in tiles:
        sel = (bits > thr) | (eq & (runeq + eq_before < need))
        sel_f = jnp.where(sel, 1.0, 0.0)
        sels.append((sel, sel_f, _dot(sel_f.astype(bf16), tri_ref[...])))
        runeq = runeq + jnp.sum(eq_f, axis=1, keepdims=True)
    runeq_scr[...] = jnp.broadcast_to(runeq, (N_EXPERTS, LANES))
    base = runsel_scr[...]
    for u, (sel, sel_f, sel_before) in enumerate(sels):
        posm_ref[:, u * TT:(u + 1) * TT] = jnp.where(sel, base[:, 0:1] + sel_before, -1.0)
        base_ref[u] = base
        base = base + jnp.sum(sel_f, axis=1, keepdims=True)
    runsel_scr[...] = base


def _route(aff, row0, nrows, tri):
    cap = CAPACITY_FACTOR * nrows // N_EXPERTS
    nt = nrows // TT
    blk0 = row0 // nrows
    vec = pltpu.VMEM((N_EXPERTS, LANES), f32)
    return pl.pallas_call(
        functools.partial(_route_kernel, cap=cap), grid=(nt // ROUTE_TILES,),
        in_specs=[pl.BlockSpec((N_EXPERTS, nrows), lambda j: (0, blk0)),
                  pl.BlockSpec((TT, TT), lambda j: (0, 0))],
        out_specs=[pl.BlockSpec((N_EXPERTS, ROUTE_TILES * TT), lambda j: (0, j)),
                   pl.BlockSpec((ROUTE_TILES, N_EXPERTS, LANES), lambda j: (j, 0, 0))],
        out_shape=[jax.ShapeDtypeStruct((N_EXPERTS, nrows), f32),
                   jax.ShapeDtypeStruct((nt, N_EXPERTS, LANES), f32)],
        scratch_shapes=[pltpu.VMEM((N_EXPERTS, LANES), jnp.int32), vec, vec, vec],
        name="route",
        compiler_params=pltpu.CompilerParams(dimension_semantics=("arbitrary",)),
    )(aff, tri)


def _window_class(count, windows, align):
    longest = jnp.max(count, axis=1) + (align - 1)
    return sum((longest > w).astype(jnp.int32) for w in windows[:-1])


def _dispatch_kernel(big_ref, first_ref, posm_ref, h2_ref, xe_hbm, stage, sem, carry, *, cap, ntiles):
    j = pl.program_id(0)
    slot = j % 2

    @pl.when(j == 0)
    def _():
        carry[...] = jnp.zeros_like(carry)
        pad = DISP_WINDOWS[-1]
        stage[1, 0:pad, :] = jnp.zeros((pad, D_MODEL), bf16)
        pads = [pltpu.make_async_copy(stage.at[1, pl.ds(0, pad)], xe_hbm.at[e, pl.ds(cap, pad)], sem.at[1])
                for e in range(N_EXPERTS)]
        for p in pads:
            p.start()
        for p in pads:
            p.wait()

    def run_start(jj, e):
        return first_ref[jj * N_EXPERTS + e]

    def window_copy(jj, e, sl, win):
        a = pl.multiple_of(run_start(jj, e) // ROW_ALIGN * ROW_ALIGN, ROW_ALIGN)
        return pltpu.make_async_copy(stage.at[sl, pl.ds(e * win, win)], xe_hbm.at[e, pl.ds(a, win)], sem.at[sl])

    def for_tile_window(jj, fn):
        for flag, win in enumerate(DISP_WINDOWS):
            @pl.when(big_ref[jj] == flag)
            def _():
                fn(win)

    def wait_all(jj, sl):
        def go(win):
            for e in range(N_EXPERTS):
                window_copy(jj, e, sl, win).wait()
        for_tile_window(jj, go)

    def compact(win):
        row = lax.broadcasted_iota(jnp.int32, (win, TT), 0)
        starts = [run_start(j, e) // ROW_ALIGN * ROW_ALIGN for e in range(N_EXPERTS)]
        onehot = jnp.concatenate(
            [jnp.where((row + starts[e]).astype(f32) == posm_ref[e:e + 1, :], 1.0, 0.0)
             for e in range(N_EXPERTS)], axis=0).astype(bf16)
        rows = _dot(onehot, h2_ref[...])
        head = lax.broadcasted_iota(jnp.int32, (ROW_ALIGN, D_MODEL), 0)
        for e in range(N_EXPERTS):
            lead = run_start(j, e) - starts[e]
            w = rows[e * win:(e + 1) * win]
            top = w[:ROW_ALIGN] + jnp.where(head < lead, carry[e], 0.0)
            stage[slot, e * win:e * win + ROW_ALIGN, :] = top.astype(bf16)
            stage[slot, e * win + ROW_ALIGN:(e + 1) * win, :] = w[ROW_ALIGN:].astype(bf16)
        for e in range(N_EXPERTS):
            end = jnp.where(j + 1 < ntiles, run_start(jnp.minimum(j + 1, ntiles - 1), e), cap)
            off = jnp.minimum(end // ROW_ALIGN * ROW_ALIGN - starts[e], win - ROW_ALIGN)
            off = pl.multiple_of(off, ROW_ALIGN)
            carry[e] = stage[slot, pl.ds(e * win + off, ROW_ALIGN), :].astype(f32)

    for_tile_window(j, compact)

    @pl.when(j > 0)
    def _():
        wait_all(j - 1, 1 - slot)

    def start_all(win):
        for e in range(N_EXPERTS):
            window_copy(j, e, slot, win).start()
    for_tile_window(j, start_all)

    @pl.when(j == ntiles - 1)
    def _():
        wait_all(j, slot)


def _dispatch(h2, posm, first, count, row0, nrows):
    cap = CAPACITY_FACTOR * nrows // N_EXPERTS
    nt = nrows // TT
    blk0 = row0 // TT
    big = _window_class(count, DISP_WINDOWS, ROW_ALIGN)
    return pl.pallas_call(
        functools.partial(_dispatch_kernel, cap=cap, ntiles=nt),
        grid_spec=pltpu.PrefetchScalarGridSpec(
            num_scalar_prefetch=2, grid=(nt,),
            in_specs=[pl.BlockSpec((N_EXPERTS, TT), lambda j, *_: (0, j)),
                      pl.BlockSpec((TT, D_MODEL), lambda j, *_: (blk0 + j, 0))],
            out_specs=pl.BlockSpec(memory_space=pl.ANY),
            scratch_shapes=[pltpu.VMEM((2, N_EXPERTS * DISP_WINDOWS[-1], D_MODEL), bf16),
                            pltpu.SemaphoreType.DMA((2,)),
                            pltpu.VMEM((N_EXPERTS, ROW_ALIGN, D_MODEL), f32)],
        ),
        out_shape=jax.ShapeDtypeStruct((N_EXPERTS, cap + DISP_WINDOWS[-1], D_MODEL), bf16),
        name="dispatch",
        compiler_params=pltpu.CompilerParams(dimension_semantics=("arbitrary",), vmem_limit_bytes=VMEM_LIMIT),
    )(big, first.reshape(-1), posm, h2)


def _ffn_kernel(xe_ref, rw_ref, wg_ref, wu_ref, wd_ref, ye_ref):
    e = pl.program_id(0)
    x = xe_ref[...]
    logits = _dot(x, rw_ref[...])
    lane = lax.broadcasted_iota(jnp.int32, logits.shape, 1)
    logits = jnp.where(lane < N_EXPERTS, logits, NEG)
    ex = jnp.exp(logits - jnp.max(logits, axis=-1, keepdims=True))
    gate = jnp.sum(jnp.where(lane == e, ex, 0.0), axis=-1, keepdims=True) / jnp.sum(ex, axis=-1, keepdims=True)
    hg = _dot(x, wg_ref[...].astype(bf16))
    hu = _dot(x, wu_ref[...].astype(bf16))
    h = (hg * (1.0 / (1.0 + jnp.exp(-hg))) * hu).astype(bf16)
    ye_ref[...] = (_dot(h, wd_ref[...].astype(bf16)) * gate).astype(bf16)


def _ffn(xe, lw, nrows):
    cap = CAPACITY_FACTOR * nrows // N_EXPERTS
    layer = lw["layer"]
    wspec = pl.BlockSpec((None, None, D_MODEL, D_EXPERT), lambda e, t: (layer, e, 0, 0))
    return pl.pallas_call(
        _ffn_kernel, grid=(N_EXPERTS, cap // TME),
        in_specs=[
            pl.BlockSpec((None, TME, D_MODEL), lambda e, t: (e, t, 0)),
            pl.BlockSpec((D_MODEL, LANES), lambda e, t: (0, 0)),
            wspec, wspec,
            pl.BlockSpec((None, None, D_EXPERT, D_MODEL), lambda e, t: (layer, e, 0, 0)),
        ],
        out_specs=pl.BlockSpec((None, TME, D_MODEL), lambda e, t: (e, t, 0)),
        out_shape=jax.ShapeDtypeStruct((N_EXPERTS, cap, D_MODEL), bf16),
        name="expert_ffn",
        compiler_params=pltpu.CompilerParams(dimension_semantics=("parallel", "arbitrary"),
                                             vmem_limit_bytes=VMEM_LIMIT),
    )(xe, lw["rw"], lw["e_wg"], lw["e_wu"], lw["e_wd"])


def _combine_kernel(big_ref, first_ref, posm_ref, x_ref, ye_hbm, g_ref, o_ref, buf, sem, *, cap, ntiles, final):
    j = pl.program_id(0)

    def window_start(jj, e, win):
        return jnp.minimum(first_ref[jj * N_EXPERTS + e] // COMB_ALIGN * COMB_ALIGN, cap - win)

    def window_copy(jj, e, slot, win):
        a = pl.multiple_of(window_start(jj, e, win), COMB_ALIGN)
        return pltpu.make_async_copy(ye_hbm.at[e, pl.ds(a, win)], buf.at[slot, pl.ds(e * win, win)],
                                     sem.at[slot])

    def for_tile_window(jj, fn):
        for flag, win in enumerate(COMB_WINDOWS):
            @pl.when(big_ref[jj] == flag)
            def _():
                fn(win)

    def start_all(jj, slot):
        def go(win):
            for e in range(N_EXPERTS):
                window_copy(jj, e, slot, win).start()
        for_tile_window(jj, go)

    @pl.when(j == 0)
    def _():
        for ahead in range(COMB_BUFS - 1):
            start_all(ahead, ahead)

    slot = j % COMB_BUFS

    def wait_all(win):
        for e in range(N_EXPERTS):
            window_copy(j, e, slot, win).wait()
    for_tile_window(j, wait_all)

    @pl.when(j + COMB_BUFS - 1 < ntiles)
    def _():
        start_all(j + COMB_BUFS - 1, (j + COMB_BUFS - 1) % COMB_BUFS)

    def add_rows(win):
        row = lax.broadcasted_iota(jnp.int32, (win, TT), 0)
        onehot = jnp.concatenate(
            [jnp.where((row + window_start(j, e, win)).astype(f32) == posm_ref[e:e + 1, :], 1.0, 0.0)
             for e in range(N_EXPERTS)], axis=0).astype(bf16)
        out = x_ref[...] + _dot_tn(onehot, buf[slot, 0:N_EXPERTS * win, :])
        o_ref[...] = _rms(out, g_ref[...]) if final else out
    for_tile_window(j, add_rows)


def _combine(x, ye, posm, first, count, row0, nrows, final_g=None):
    cap = CAPACITY_FACTOR * nrows // N_EXPERTS
    nt = nrows // TT
    blk0 = row0 // TT
    final = final_g is not None
    big = _window_class(count, COMB_WINDOWS, COMB_ALIGN)
    gain = final_g if final else jnp.ones((1, D_MODEL), f32)
    return pl.pallas_call(
        functools.partial(_combine_kernel, cap=cap, ntiles=nt, final=final),
        grid_spec=pltpu.PrefetchScalarGridSpec(
            num_scalar_prefetch=2, grid=(nt,),
            in_specs=[pl.BlockSpec((N_EXPERTS, TT), lambda j, *_: (0, j)),
                      pl.BlockSpec((TT, D_MODEL), lambda j, *_: (blk0 + j, 0)),
                      pl.BlockSpec(memory_space=pl.ANY),
                      pl.BlockSpec((1, D_MODEL), lambda j, *_: (0, 0))],
            out_specs=pl.BlockSpec((TT, D_MODEL), lambda j, *_: ((0 if final else blk0) + j, 0)),
            scratch_shapes=[pltpu.VMEM((COMB_BUFS, N_EXPERTS * COMB_WINDOWS[-1], D_MODEL), bf16),
                            pltpu.SemaphoreType.DMA((COMB_BUFS,))],
        ),
        out_shape=jax.ShapeDtypeStruct((nrows, D_MODEL) if final else x.shape, f32),
        input_output_aliases={} if final else {3: 0},
        name="combine",
        compiler_params=pltpu.CompilerParams(dimension_semantics=("arbitrary",), vmem_limit_bytes=VMEM_LIMIT),
    )(big, first.reshape(-1), posm, x, ye, gain)


def _stack_kernel(a_ref, b_ref, o_ref, *, na_blocks):
    o_ref[...] = jnp.where(pl.program_id(0) < na_blocks, a_ref[...], b_ref[...])


def _stack_rows(a, b):
    na, nb = a.shape[0] // TM, b.shape[0] // TM
    return pl.pallas_call(
        functools.partial(_stack_kernel, na_blocks=na), grid=(na + nb,),
        in_specs=[pl.BlockSpec((TM, D_MODEL), lambda i: (jnp.minimum(i, na - 1), 0)),
                  pl.BlockSpec((TM, D_MODEL), lambda i: (jnp.maximum(i - na, 0), 0))],
        out_specs=pl.BlockSpec((TM, D_MODEL), lambda i: (i, 0)),
        out_shape=jax.ShapeDtypeStruct((a.shape[0] + b.shape[0], D_MODEL), a.dtype),
        name="stack_rows",
        compiler_params=pltpu.CompilerParams(dimension_semantics=("arbitrary",)),
    )(a, b)


def _natten_bias_table(rpb):
    c = np.arange(GRID_W)
    cs = np.clip(c - WIN_C // 2, 0, GRID_W - WIN_C)
    kc = np.arange(GRID_W)
    col_mask = (kc[None, :] >= cs[:, None]) & (kc[None, :] < cs[:, None] + WIN_C)
    dc = np.clip(kc[None, :] - c[:, None], -(WIN_C - 1), WIN_C - 1) + (WIN_C - 1)
    tb = jnp.where(col_mask[None, None], rpb[:, :, dc].astype(f32), NEG)
    dw = np.arange(WIN_R)[:, None] + np.arange(WIN_R)[None, :]
    t = tb[:, dw]
    t = t.transpose(1, 0, 3, 2, 4).reshape(WIN_R, C_HEADS * GRID_W, WIN_R * GRID_W)
    return t


def _rope_slabs():
    inv = 1.0 / (ROPE_THETA ** (jnp.arange(0, ROPE_DIM, 2, dtype=f32) / ROPE_DIM))
    ang = jnp.arange(SEQ, dtype=f32)[:, None] * inv[None, :]
    cos, sin = jnp.cos(ang), jnp.sin(ang)
    ones = jnp.ones((SEQ, NOPE_DIM), f32)
    zpad = jnp.zeros((SEQ, LANES - NOPE_DIM - ROPE_DIM), f32)
    cosq = jnp.concatenate([ones, cos, cos, zpad], axis=1)
    sinq = jnp.concatenate([0.0 * ones, sin, sin, zpad], axis=1)
    return cosq, sinq


def _pack_layer(i, norm1_g, w_in, a_vnorm_g, a_ws, a_bs, b_conv, c_rpb, d_qnorm_g, d_kvnorm_g,
                d_w_uq, d_w_ukv, out_norm_g, w_o, norm2_g, router_w, e_w_gate, e_w_up, e_w_down):
    half = ROPE_DIM // 2
    w = w_in[i]
    kr0 = OFF_D + Q_RANK + KV_RANK
    kr = w[:, kr0:kr0 + ROPE_DIM]
    z64 = jnp.zeros((D_MODEL, NOPE_DIM), f32)
    z32 = jnp.zeros((D_MODEL, LANES - NOPE_DIM - ROPE_DIM), f32)
    kr_slab = jnp.concatenate([z64, kr, z32], axis=1)
    kr_rot = jnp.concatenate([z64, -kr[:, half:], kr[:, :half], z32], axis=1)
    w_pack = jnp.concatenate([w[:, :kr0], kr_slab, kr_rot], axis=1).astype(bf16)

    uq = d_w_uq[i].reshape(Q_RANK, D_HEADS, NOPE_DIM + ROPE_DIM)
    nope, rope = uq[..., :NOPE_DIM], uq[..., NOPE_DIM:]
    zq = jnp.zeros((Q_RANK, D_HEADS, LANES - NOPE_DIM - ROPE_DIM), f32)
    w_uq = jnp.concatenate([nope, rope, zq], axis=-1).reshape(Q_RANK, D_HEADS * LANES)
    w_uq_rot = jnp.concatenate([0.0 * nope, -rope[..., half:], rope[..., :half], zq],
                               axis=-1).reshape(Q_RANK, D_HEADS * LANES)
    ukv = d_w_ukv[i].reshape(KV_RANK, D_HEADS, NOPE_DIM + V_DIM)
    w_k = jnp.concatenate([ukv[..., :NOPE_DIM], jnp.zeros((KV_RANK, D_HEADS, LANES - NOPE_DIM), f32)],
                          axis=-1).reshape(KV_RANK, D_HEADS * LANES)
    w_v = ukv[..., NOPE_DIM:].reshape(KV_RANK, D_HEADS * V_DIM)
    rw = jnp.concatenate([router_w[i], jnp.zeros((D_MODEL, LANES - N_EXPERTS), f32)], axis=1)
    return {
        "g1": norm1_g[i].reshape(1, D_MODEL),
        "w_in": w_pack,
        "a_vg": a_vnorm_g[i].reshape(1, GROUP_W),
        "a_ws": a_ws[i].astype(bf16),
        "a_bias": jnp.repeat(a_bs[i].T, HEAD_DIM, axis=1),
        "b_conv": b_conv[i],
        "c_bias": _natten_bias_table(c_rpb[i]),
        "d_qg": d_qnorm_g[i].reshape(1, Q_RANK),
        "d_kvg": d_kvnorm_g[i].reshape(1, KV_RANK),
        "w_uq": w_uq.astype(bf16),
        "w_uq_rot": w_uq_rot.astype(bf16),
        "w_k": w_k.astype(bf16),
        "w_v": w_v.astype(bf16),
        "og": out_norm_g[i].reshape(1, D_MODEL),
        "w_o": w_o[i].astype(bf16),
        "g2": norm2_g[i].reshape(1, D_MODEL),
        "rw": rw.astype(bf16),
        "rwt": router_w[i].T.astype(bf16),
        "layer": i,
        "e_wg": e_w_gate,
        "e_wu": e_w_up,
        "e_wd": e_w_down,
    }


def _moe(xn, h2, aff, lw, tri, final_g=None):
    x = xn
    outs = []
    for row0, nrows in ((0, BATCH * SEQ), (BATCH * SEQ, DEC_BATCH * SEQ)):
        cap = CAPACITY_FACTOR * nrows // N_EXPERTS
        posm, base = _route(aff, row0, nrows, tri)
        first = base[:, :, 0].astype(jnp.int32)
        count = jnp.diff(first, axis=0, append=jnp.full((1, N_EXPERTS), cap, jnp.int32))
        xe = _dispatch(h2, posm, first, count, row0, nrows)
        ye = _ffn(xe, lw, nrows)
        if final_g is None:
            x = _combine(x, ye, posm, first, count, row0, nrows)
        else:
            outs.append(_combine(xn, ye, posm, first, count, row0, nrows, final_g))
    return x if final_g is None else outs


def kernel(x_prompt, x_sample, norm1_g, w_in, a_vnorm_g, a_ws, a_bs, b_conv, c_rpb, d_qnorm_g, d_kvnorm_g, d_w_uq, d_w_ukv, out_norm_g, w_o, norm2_g, router_w, e_w_gate, e_w_up, e_w_down, final_norm_g):
    n_p = BATCH * SEQ
    n_s = DEC_BATCH * SEQ
    x = _stack_rows(x_prompt.reshape(n_p, D_MODEL), x_sample.reshape(n_s, D_MODEL))
    cosq, sinq = _rope_slabs()
    tri = jnp.asarray(np.triu(np.ones((TT, TT), np.float32), 1), bf16)
    for i in range(DEPTH):
        lw = _pack_layer(i, norm1_g, w_in, a_vnorm_g, a_ws, a_bs, b_conv, c_rpb, d_qnorm_g, d_kvnorm_g,
                         d_w_uq, d_w_ukv, out_norm_g, w_o, norm2_g, router_w, e_w_gate, e_w_up, e_w_down)
        ya, yb, zc, qt, k4, vt = _inproj(x, lw, cosq, sinq)
        yc = _natten(zc, lw["c_bias"])
        yd = _mla(qt, k4, vt)
        xn, h2, aff = _outproj(ya, yb, yc, yd, x, lw)
        x = _moe(xn, h2, aff, lw, tri, final_norm_g.reshape(1, D_MODEL) if i == DEPTH - 1 else None)
    y_p, y_s = x
    return (y_p.reshape(BATCH, SEQ, D_MODEL), y_s.reshape(DEC_BATCH, SEQ, D_MODEL))
```

```python
import functools
import math

import numpy as np
import jax
import jax.numpy as jnp
from jax import lax
from jax.experimental import pallas as pl
from jax.experimental.pallas import tpu as pltpu

D_MODEL = 1024
BATCH = 4
SEQ = 8192
DEPTH = 4
DEC_BATCH = 2
GROUP_W = 256
HEAD_DIM = 64
A_HEADS = 4
CHUNK = 128
C_HEADS = 4
GRID_W = 64
WIN_R = 8
WIN_C = 16
D_HEADS = 4
Q_RANK = 256
KV_RANK = 128
NOPE_DIM = 64
ROPE_DIM = 32
V_DIM = 64
ROPE_THETA = 10000.0
N_EXPERTS = 16
D_EXPERT = 1024
CAPACITY_FACTOR = 2
EPS = 1e-6

NSEQ = BATCH + DEC_BATCH
T_ALL = NSEQ * SEQ
ROWS = SEQ // GRID_W
LANES = 128
NEG = -1e30

A_COLS = 2 * GROUP_W
B_COLS = 3 * GROUP_W
C_COLS = 3 * GROUP_W
OFF_B = A_COLS
OFF_C = OFF_B + B_COLS
OFF_D = OFF_C + C_COLS
D_PACK = Q_RANK + KV_RANK + 2 * LANES
N_IN_PACK = OFF_D + D_PACK

TM = 1024
TQ = 256
TK = 512
L_ROWS = 16
MLA_UNROLL = 4
MLA_QTILES = 2
MLA_LAG = 1
MLA_QSCALE = (NOPE_DIM + ROPE_DIM) ** -0.5 * math.log2(math.e)
ROWS_PER_STEP = 16
TME = 1024
TT = 256
ROUTE_TILES = 4
ROW_ALIGN = 16
DISP_WINDOWS = (80, 144, TT + ROW_ALIGN)
COMB_ALIGN = ROW_ALIGN
COMB_WINDOWS = (80, 144, TT + COMB_ALIGN)
COMB_BUFS = 3
VMEM_LIMIT = 56 * 1024 * 1024

f32 = jnp.float32
bf16 = jnp.bfloat16


def _rms(x, g):
    return x * lax.rsqrt(jnp.mean(x * x, axis=-1, keepdims=True) + EPS) * g


def _dot(a, b):
    return jnp.dot(a, b, preferred_element_type=f32)


def _dot_nt(a, b):
    return lax.dot_general(a, b, (((1,), (1,)), ((), ())), preferred_element_type=f32)


def _dot_tn(a, b):
    return lax.dot_general(a, b, (((0,), (0,)), ((), ())), preferred_element_type=f32)


def _inproj_kernel(x_ref, xp_ref, xn_ref, g1_ref, win_ref, avg_ref, aws_ref, abias_ref,
                   bconv_ref, qg_ref, kvg_ref, wuq_ref, wuqr_ref, wk_ref, wv_ref,
                   cos_ref, sin_ref,
                   ya_ref, yb_ref, zc_ref, qt_ref, k4_ref, vt_ref):
    i = pl.program_id(0)
    j = i % (SEQ // TM)
    g1 = g1_ref[...]
    h = _rms(x_ref[...], g1).astype(bf16)

    za = _dot(h, win_ref[:, 0:A_COLS])
    zb = _dot(h, win_ref[:, OFF_B:OFF_B + B_COLS])
    zc = _dot(h, win_ref[:, OFF_C:OFF_C + C_COLS])
    zd = _dot(h, win_ref[:, OFF_D:OFF_D + D_PACK])
    xh = jnp.concatenate([xp_ref[...], xn_ref[...]], axis=0)
    hh = _rms(xh, g1).astype(bf16)
    zh = _dot(hh, win_ref[:, OFF_B + GROUP_W:OFF_B + B_COLS])

    zc_ref[:, 0:GROUP_W] = (zc[:, 0:GROUP_W] * (HEAD_DIM ** -0.5)).astype(bf16)
    zc_ref[:, GROUP_W:] = zc[:, GROUP_W:].astype(bf16)

    cqn = _rms(zd[:, 0:Q_RANK], qg_ref[...]).astype(bf16)
    ckvn = _rms(zd[:, Q_RANK:Q_RANK + KV_RANK], kvg_ref[...]).astype(bf16)
    cos = cos_ref[...]
    sin = sin_ref[...]
    o = Q_RANK + KV_RANK
    kr = zd[:, o:o + LANES] * cos + zd[:, o + LANES:o + 2 * LANES] * sin
    q_pre = _dot(cqn, wuq_ref[...])
    q_rot = _dot(cqn, wuqr_ref[...])
    kn = _dot(ckvn, wk_ref[...])
    vt = _dot(ckvn, wv_ref[...]).T.astype(bf16)

    za = jax.nn.gelu(za)
    u = za[:, :GROUP_W]
    v = _rms(za[:, GROUP_W:], avg_ref[...]).astype(bf16)
    lane = lax.broadcasted_iota(jnp.int32, (CHUNK, GROUP_W), 1)
    for c in range(TM // CHUNK):
        vc = v[c * CHUNK:(c + 1) * CHUNK, :]
        sv = abias_ref[...]
        for hd in range(A_HEADS):
            r = _dot(aws_ref[hd], vc)
            sv = sv + jnp.where((lane >= hd * HEAD_DIM) & (lane < (hd + 1) * HEAD_DIM), r, 0.0)
        ya_ref[c * CHUNK:(c + 1) * CHUNK, :] = u[c * CHUNK:(c + 1) * CHUNK, :] * sv

    bg = zb[:, :GROUP_W]
    y = zb[:, GROUP_W:2 * GROUP_W] * zb[:, 2 * GROUP_W:]
    yh = zh[:, :GROUP_W] * zh[:, GROUP_W:]
    y_before = jnp.where(j == 0, 0.0, yh[7:8, :])
    y_after = jnp.where(j == SEQ // TM - 1, 0.0, yh[8:9, :])
    row = lax.broadcasted_iota(jnp.int32, (TM, GROUP_W), 0)
    y_m1 = jnp.where(row == 0, y_before, pltpu.roll(y, 1, axis=0))
    y_p1 = jnp.where(row == TM - 1, y_after, pltpu.roll(y, TM - 1, axis=0))
    wc = bconv_ref[...]
    yb_ref[...] = bg * (wc[0:1, :] * y_m1 + wc[1:2, :] * y + wc[2:3, :] * y_p1)

    for hd in range(D_HEADS):
        sl = slice(hd * LANES, (hd + 1) * LANES)
        qt_ref[hd] = ((q_pre[:, sl] * cos + q_rot[:, sl] * sin) * MLA_QSCALE).T.astype(bf16)
        k4_ref[hd] = (kn[:, sl] + kr).astype(bf16)
        vt_ref[hd] = vt[hd * V_DIM:(hd + 1) * V_DIM, :]


def _inproj(x, lw, cosq, sinq):
    nt = T_ALL // TM
    tps = SEQ // TM
    full = lambda shape: pl.BlockSpec(shape, lambda i: (0,) * len(shape))
    in_specs = [
        pl.BlockSpec((TM, D_MODEL), lambda i: (i, 0)),
        pl.BlockSpec((8, D_MODEL), lambda i: (jnp.maximum(i * (TM // 8) - 1, 0), 0)),
        pl.BlockSpec((8, D_MODEL), lambda i: (jnp.minimum((i + 1) * (TM // 8), T_ALL // 8 - 1), 0)),
        full((1, D_MODEL)),
        full((D_MODEL, N_IN_PACK)),
        full((1, GROUP_W)),
        full((A_HEADS, CHUNK, CHUNK)),
        full((CHUNK, GROUP_W)),
        full((3, GROUP_W)),
        full((1, Q_RANK)),
        full((1, KV_RANK)),
        full((Q_RANK, D_HEADS * LANES)),
        full((Q_RANK, D_HEADS * LANES)),
        full((KV_RANK, D_HEADS * LANES)),
        full((KV_RANK, D_HEADS * V_DIM)),
        pl.BlockSpec((TM, LANES), lambda i: (i % tps, 0)),
        pl.BlockSpec((TM, LANES), lambda i: (i % tps, 0)),
    ]
    out_specs = [
        pl.BlockSpec((TM, GROUP_W), lambda i: (i, 0)),
        pl.BlockSpec((TM, GROUP_W), lambda i: (i, 0)),
        pl.BlockSpec((TM, C_COLS), lambda i: (i, 0)),
        pl.BlockSpec((None, D_HEADS, LANES, TM), lambda i: (i // tps, 0, 0, i % tps)),
        pl.BlockSpec((None, D_HEADS, TM, LANES), lambda i: (i // tps, 0, i % tps, 0)),
        pl.BlockSpec((None, D_HEADS, V_DIM, TM), lambda i: (i // tps, 0, 0, i % tps)),
    ]
    out_shape = [
        jax.ShapeDtypeStruct((T_ALL, GROUP_W), f32),
        jax.ShapeDtypeStruct((T_ALL, GROUP_W), f32),
        jax.ShapeDtypeStruct((T_ALL, C_COLS), bf16),
        jax.ShapeDtypeStruct((NSEQ, D_HEADS, LANES, SEQ), bf16),
        jax.ShapeDtypeStruct((NSEQ, D_HEADS, SEQ, LANES), bf16),
        jax.ShapeDtypeStruct((NSEQ, D_HEADS, V_DIM, SEQ), bf16),
    ]
    return pl.pallas_call(
        _inproj_kernel, grid=(nt,), in_specs=in_specs, out_specs=out_specs, out_shape=out_shape,
        name="inproj",
        compiler_params=pltpu.CompilerParams(dimension_semantics=("parallel",),
                                             vmem_limit_bytes=VMEM_LIMIT),
    )(x, x, x, lw["g1"], lw["w_in"], lw["a_vg"], lw["a_ws"], lw["a_bias"], lw["b_conv"],
      lw["d_qg"], lw["d_kvg"], lw["w_uq"], lw["w_uq_rot"], lw["w_k"], lw["w_v"], cosq, sinq)


def _natten_kernel(q_ref, k_ref, v_ref, bias_ref, o_ref):
    jb = pl.program_id(1)
    lane = lax.broadcasted_iota(jnp.int32, (GRID_W, GROUP_W), 1)
    head_of_lane = lane // HEAD_DIM
    nkeys = WIN_R * GRID_W

    jobs = []
    for i in range(ROWS_PER_STEP):
        r = jb * ROWS_PER_STEP + i
        rs = jnp.clip(r - WIN_R // 2, 0, ROWS - WIN_R)
        d = rs - r + (WIN_R - 1)
        kstart = pl.multiple_of(rs * GRID_W, GRID_W)
        q = q_ref[i * GRID_W:(i + 1) * GRID_W, :]
        zero = jnp.zeros_like(q)
        qq = jnp.concatenate([jnp.where(head_of_lane == h, q, zero) for h in range(C_HEADS)], axis=0)
        jobs.append((_dot_nt(qq, k_ref[pl.ds(kstart, nkeys), :]), d, kstart))
    for i, (sc, d, kstart) in enumerate(jobs):
        sc = sc + bias_ref[d]
        m = jnp.max(sc, axis=-1, keepdims=True)
        p = jnp.exp(sc - m)
        l = jnp.sum(p, axis=-1, keepdims=True)
        o = _dot(p.astype(bf16), v_ref[pl.ds(kstart, nkeys), :]) / l
        out = jnp.zeros((GRID_W, GROUP_W), f32)
        for h in range(C_HEADS):
            out = jnp.where(head_of_lane == h, o[h * GRID_W:(h + 1) * GRID_W], out)
        o_ref[i * GRID_W:(i + 1) * GRID_W, :] = out


def _natten(zc, bias):
    nb = ROWS // ROWS_PER_STEP
    tq = ROWS_PER_STEP * GRID_W
    return pl.pallas_call(
        _natten_kernel, grid=(NSEQ, nb),
        in_specs=[
            pl.BlockSpec((tq, GROUP_W), lambda b, j: (b * nb + j, 0)),
            pl.BlockSpec((SEQ, GROUP_W), lambda b, j: (b, 1)),
            pl.BlockSpec((SEQ, GROUP_W), lambda b, j: (b, 2)),
            pl.BlockSpec((WIN_R, C_HEADS * GRID_W, WIN_R * GRID_W), lambda b, j: (0, 0, 0)),
        ],
        out_specs=pl.BlockSpec((tq, GROUP_W), lambda b, j: (b * nb + j, 0)),
        out_shape=jax.ShapeDtypeStruct((T_ALL, GROUP_W), f32),
        name="natten",
        compiler_params=pltpu.CompilerParams(dimension_semantics=("parallel", "arbitrary"),
                                             vmem_limit_bytes=VMEM_LIMIT),
    )(zc, zc, zc, bias)


def _mla_kernel(qt_ref, k_ref, vt_ref, o_ref, s_scr):
    nchunk = SEQ // TK
    chains = [(t, h) for t in range(MLA_QTILES) for h in range(D_HEADS)]
    nc = len(chains)
    qts = [qt_ref[h, :, t * TQ:(t + 1) * TQ] for t, h in chains]

    def qk(slot, c, ci):
        ks = pl.multiple_of(ci * TK, TK)
        s_scr[slot, c] = _dot(k_ref[chains[c][1], pl.ds(ks, TK), :], qts[c])

    ones = jnp.ones((L_ROWS, TK), bf16)

    def softmax_pv(slot, c, ci, stat):
        m, acc = stat
        ks = pl.multiple_of(ci * TK, TK)
        vt = jnp.concatenate([vt_ref[chains[c][1], :, pl.ds(ks, TK)], ones], axis=0)
        s = s_scr[slot, c]
        m_new = jnp.maximum(m, jnp.max(s, axis=0, keepdims=True))
        alpha = jnp.exp2(m - m_new)
        p = jnp.exp2(s - m_new)
        acc = acc * alpha + _dot(vt, p.astype(bf16))
        return m_new, acc

    def half(cur, ci, stats, prefetch=True):
        out = []
        for c in range(nc):
            if prefetch:
                qk(1 - cur, c, ci + 1)
            if c >= MLA_LAG:
                out.append(softmax_pv(cur, c - MLA_LAG, ci, stats[c - MLA_LAG]))
        for c in range(nc - MLA_LAG, nc):
            out.append(softmax_pv(cur, c, ci, stats[c]))
        return tuple(out)

    def body(cp, stats):
        for u in range(MLA_UNROLL):
            stats = half(u % 2, MLA_UNROLL * cp + u, stats)
        return stats

    for c in range(nc):
        qk(0, c, 0)
    init = tuple((jnp.full((1, TQ), NEG, f32), jnp.zeros((V_DIM + L_ROWS, TQ), f32)) for _ in chains)
    res = lax.fori_loop(0, nchunk // MLA_UNROLL - 1, body, init)
    for u in range(MLA_UNROLL):
        res = half(u % 2, nchunk - MLA_UNROLL + u, res, prefetch=u < MLA_UNROLL - 1)
    for t in range(MLA_QTILES):
        o_ref[t * TQ:(t + 1) * TQ, :] = jnp.concatenate(
            [acc[:V_DIM] / acc[V_DIM:V_DIM + 1] for (_, acc) in res[t * D_HEADS:(t + 1) * D_HEADS]], axis=0).T


def _mla(qt, k4, vt):
    tq = TQ * MLA_QTILES
    nq = SEQ // tq
    return pl.pallas_call(
        _mla_kernel, grid=(NSEQ, nq),
        in_specs=[
            pl.BlockSpec((None, D_HEADS, LANES, tq), lambda b, i: (b, 0, 0, i)),
            pl.BlockSpec((None, D_HEADS, SEQ, LANES), lambda b, i: (b, 0, 0, 0)),
            pl.BlockSpec((None, D_HEADS, V_DIM, SEQ), lambda b, i: (b, 0, 0, 0)),
        ],
        out_specs=pl.BlockSpec((tq, GROUP_W), lambda b, i: (b * nq + i, 0)),
        out_shape=jax.ShapeDtypeStruct((T_ALL, GROUP_W), f32),
        scratch_shapes=[pltpu.VMEM((2, D_HEADS * MLA_QTILES, TK, TQ), f32)],
        name="mla",
        compiler_params=pltpu.CompilerParams(
            dimension_semantics=("parallel", "arbitrary"),
            vmem_limit_bytes=VMEM_LIMIT),
    )(qt, k4, vt)


def _outproj_kernel(ya_ref, yb_ref, yc_ref, yd_ref, x_ref, og_ref, wo_ref, g2_ref, rwt_ref,
                    xn_ref, h2_ref, aff_ref):
    y = jnp.concatenate(
        [_rms(y_ref[...], og_ref[:, g * GROUP_W:(g + 1) * GROUP_W]).astype(bf16)
         for g, y_ref in enumerate((ya_ref, yb_ref, yc_ref, yd_ref))], axis=1)
    acc = x_ref[...] + _dot(y, wo_ref[...])
    xn_ref[...] = acc
    h2 = _rms(acc, g2_ref[...]).astype(bf16)
    h2_ref[...] = h2
    logits = _dot_nt(rwt_ref[...], h2)
    e = jnp.exp(logits - jnp.max(logits, axis=0, keepdims=True))
    aff_ref[...] = e / jnp.sum(e, axis=0, keepdims=True)


def _outproj(ya, yb, yc, yd, x, lw):
    nt = T_ALL // TM
    full = lambda shape: pl.BlockSpec(shape, lambda i: (0,) * len(shape))
    yspec = pl.BlockSpec((TM, GROUP_W), lambda i: (i, 0))
    xspec = pl.BlockSpec((TM, D_MODEL), lambda i: (i, 0))
    return pl.pallas_call(
        _outproj_kernel, grid=(nt,),
        in_specs=[yspec, yspec, yspec, yspec, xspec, full((1, D_MODEL)), full((D_MODEL, D_MODEL)),
                  full((1, D_MODEL)), full((N_EXPERTS, D_MODEL))],
        out_specs=[xspec, xspec, pl.BlockSpec((N_EXPERTS, TM), lambda i: (0, i))],
        out_shape=[jax.ShapeDtypeStruct((T_ALL, D_MODEL), f32),
                   jax.ShapeDtypeStruct((T_ALL, D_MODEL), bf16),
                   jax.ShapeDtypeStruct((N_EXPERTS, T_ALL), f32)],
        name="outproj",
        compiler_params=pltpu.CompilerParams(dimension_semantics=("parallel",),
                                             vmem_limit_bytes=VMEM_LIMIT),
    )(ya, yb, yc, yd, x, lw["og"], lw["w_o"], lw["g2"], lw["rwt"])


def _route_kernel(aff_ref, tri_ref, posm_ref, base_ref, thr_scr, need_scr, runeq_scr, runsel_scr, *, cap):
    j = pl.program_id(0)

    @pl.when(j == 0)
    def _():
        def refine(i, thr):
            cand = thr | jnp.left_shift(jnp.int32(1), 30 - i)
            bits = pltpu.bitcast(aff_ref[...], jnp.int32)
            cnt = jnp.sum(jnp.where(bits >= cand, 1.0, 0.0), axis=1, keepdims=True)
            return jnp.where(cnt >= cap, cand, thr)

        thr = lax.fori_loop(0, 31, refine, jnp.zeros((N_EXPERTS, 1), jnp.int32))
        bits = pltpu.bitcast(aff_ref[...], jnp.int32)
        above = jnp.sum(jnp.where(bits > thr, 1.0, 0.0), axis=1, keepdims=True)
        thr_scr[...] = jnp.broadcast_to(thr, (N_EXPERTS, LANES))
        need_scr[...] = jnp.broadcast_to(cap - above, (N_EXPERTS, LANES))
        runeq_scr[...] = jnp.zeros((N_EXPERTS, LANES), f32)
        runsel_scr[...] = jnp.zeros((N_EXPERTS, LANES), f32)

    thr = thr_scr[:, 0:1]
    need = need_scr[:, 0:1]
    tiles = []
    for u in range(ROUTE_TILES):
        t0 = pl.multiple_of((j * ROUTE_TILES + u) * TT, TT)
        bits = pltpu.bitcast(aff_ref[:, pl.ds(t0, TT)], jnp.int32)
        eq = bits == thr
        eq_f = jnp.where(eq, 1.0, 0.0)
        tiles.append((bits, eq, eq_f, _dot(eq_f.astype(bf16), tri_ref[...])))
    sels = []
    runeq = runeq_scr[:, 0:1]
    for bits, eq, eq_f, eq_before in tiles:
        sel = (bits > thr) | (eq & (runeq + eq_before < need))
        sel_f = jnp.where(sel, 1.0, 0.0)
        sels.append((sel, sel_f, _dot(sel_f.astype(bf16), tri_ref[...])))
        runeq = runeq + jnp.sum(eq_f, axis=1, keepdims=True)
    runeq_scr[...] = jnp.broadcast_to(runeq, (N_EXPERTS, LANES))
    base = runsel_scr[...]
    for u, (sel, sel_f, sel_before) in enumerate(sels):
        posm_ref[:, u * TT:(u + 1) * TT] = jnp.where(sel, base[:, 0:1] + sel_before, -1.0)
        base_ref[u] = base
        base = base + jnp.sum(sel_f, axis=1, keepdims=True)
    runsel_scr[...] = base


def _route(aff, row0, nrows, tri):
    cap = CAPACITY_FACTOR * nrows // N_EXPERTS
    nt = nrows // TT
    blk0 = row0 // nrows
    vec = pltpu.VMEM((N_EXPERTS, LANES), f32)
    return pl.pallas_call(
        functools.partial(_route_kernel, cap=cap), grid=(nt // ROUTE_TILES,),
        in_specs=[pl.BlockSpec((N_EXPERTS, nrows), lambda j: (0, blk0)),
                  pl.BlockSpec((TT, TT), lambda j: (0, 0))],
        out_specs=[pl.BlockSpec((N_EXPERTS, ROUTE_TILES * TT), lambda j: (0, j)),
                   pl.BlockSpec((ROUTE_TILES, N_EXPERTS, LANES), lambda j: (j, 0, 0))],
        out_shape=[jax.ShapeDtypeStruct((N_EXPERTS, nrows), f32),
                   jax.ShapeDtypeStruct((nt, N_EXPERTS, LANES), f32)],
        scratch_shapes=[pltpu.VMEM((N_EXPERTS, LANES), jnp.int32), vec, vec, vec],
        name="route",
        compiler_params=pltpu.CompilerParams(dimension_semantics=("arbitrary",)),
    )(aff, tri)


def _window_class(count, windows, align):
    longest = jnp.max(count, axis=1) + (align - 1)
    return sum((longest > w).astype(jnp.int32) for w in windows[:-1])


def _dispatch_kernel(big_ref, first_ref, posm_ref, h2_ref, xe_hbm, stage, sem, carry, *, cap, ntiles):
    j = pl.program_id(0)
    slot = j % 2

    @pl.when(j == 0)
    def _():
        carry[...] = jnp.zeros_like(carry)
        pad = DISP_WINDOWS[-1]
        stage[1, 0:pad, :] = jnp.zeros((pad, D_MODEL), bf16)
        pads = [pltpu.make_async_copy(stage.at[1, pl.ds(0, pad)], xe_hbm.at[e, pl.ds(cap, pad)], sem.at[1])
                for e in range(N_EXPERTS)]
        for p in pads:
            p.start()
        for p in pads:
            p.wait()

    def run_start(jj, e):
        return first_ref[jj * N_EXPERTS + e]

    def window_copy(jj, e, sl, win):
        a = pl.multiple_of(run_start(jj, e) // ROW_ALIGN * ROW_ALIGN, ROW_ALIGN)
        return pltpu.make_async_copy(stage.at[sl, pl.ds(e * win, win)], xe_hbm.at[e, pl.ds(a, win)], sem.at[sl])

    def for_tile_window(jj, fn):
        for flag, win in enumerate(DISP_WINDOWS):
            @pl.when(big_ref[jj] == flag)
            def _():
                fn(win)

    def wait_all(jj, sl):
        def go(win):
            for e in range(N_EXPERTS):
                window_copy(jj, e, sl, win).wait()
        for_tile_window(jj, go)

    def compact(win):
        row = lax.broadcasted_iota(jnp.int32, (win, TT), 0)
        starts = [run_start(j, e) // ROW_ALIGN * ROW_ALIGN for e in range(N_EXPERTS)]
        onehot = jnp.concatenate(
            [jnp.where((row + starts[e]).astype(f32) == posm_ref[e:e + 1, :], 1.0, 0.0)
             for e in range(N_EXPERTS)], axis=0).astype(bf16)
        rows = _dot(onehot, h2_ref[...])
        head = lax.broadcasted_iota(jnp.int32, (ROW_ALIGN, D_MODEL), 0)
        for e in range(N_EXPERTS):
            lead = run_start(j, e) - starts[e]
            w = rows[e * win:(e + 1) * win]
            top = w[:ROW_ALIGN] + jnp.where(head < lead, carry[e], 0.0)
            stage[slot, e * win:e * win + ROW_ALIGN, :] = top.astype(bf16)
            stage[slot, e * win + ROW_ALIGN:(e + 1) * win, :] = w[ROW_ALIGN:].astype(bf16)
        for e in range(N_EXPERTS):
            end = jnp.where(j + 1 < ntiles, run_start(jnp.minimum(j + 1, ntiles - 1), e), cap)
            off = jnp.minimum(end // ROW_ALIGN * ROW_ALIGN - starts[e], win - ROW_ALIGN)
            off = pl.multiple_of(off, ROW_ALIGN)
            carry[e] = stage[slot, pl.ds(e * win + off, ROW_ALIGN), :].astype(f32)

    for_tile_window(j, compact)

    @pl.when(j > 0)
    def _():
        wait_all(j - 1, 1 - slot)

    def start_all(win):
        for e in range(N_EXPERTS):
            window_copy(j, e, slot, win).start()
    for_tile_window(j, start_all)

    @pl.when(j == ntiles - 1)
    def _():
        wait_all(j, slot)


def _dispatch(h2, posm, first, count, row0, nrows):
    cap = CAPACITY_FACTOR * nrows // N_EXPERTS
    nt = nrows // TT
    blk0 = row0 // TT
    big = _window_class(count, DISP_WINDOWS, ROW_ALIGN)
    return pl.pallas_call(
        functools.partial(_dispatch_kernel, cap=cap, ntiles=nt),
        grid_spec=pltpu.PrefetchScalarGridSpec(
            num_scalar_prefetch=2, grid=(nt,),
            in_specs=[pl.BlockSpec((N_EXPERTS, TT), lambda j, *_: (0, j)),
                      pl.BlockSpec((TT, D_MODEL), lambda j, *_: (blk0 + j, 0))],
            out_specs=pl.BlockSpec(memory_space=pl.ANY),
            scratch_shapes=[pltpu.VMEM((2, N_EXPERTS * DISP_WINDOWS[-1], D_MODEL), bf16),
                            pltpu.SemaphoreType.DMA((2,)),
                            pltpu.VMEM((N_EXPERTS, ROW_ALIGN, D_MODEL), f32)],
        ),
        out_shape=jax.ShapeDtypeStruct((N_EXPERTS, cap + DISP_WINDOWS[-1], D_MODEL), bf16),
        name="dispatch",
        compiler_params=pltpu.CompilerParams(dimension_semantics=("arbitrary",), vmem_limit_bytes=VMEM_LIMIT),
    )(big, first.reshape(-1), posm, h2)


def _ffn_kernel(xe_ref, rw_ref, wg_ref, wu_ref, wd_ref, ye_ref):
    e = pl.program_id(0)
    x = xe_ref[...]
    logits = _dot(x, rw_ref[...])
    lane = lax.broadcasted_iota(jnp.int32, logits.shape, 1)
    logits = jnp.where(lane < N_EXPERTS, logits, NEG)
    ex = jnp.exp(logits - jnp.max(logits, axis=-1, keepdims=True))
    gate = jnp.sum(jnp.where(lane == e, ex, 0.0), axis=-1, keepdims=True) / jnp.sum(ex, axis=-1, keepdims=True)
    hg = _dot(x, wg_ref[...].astype(bf16))
    hu = _dot(x, wu_ref[...].astype(bf16))
    h = (hg * (1.0 / (1.0 + jnp.exp(-hg))) * hu).astype(bf16)
    ye_ref[...] = (_dot(h, wd_ref[...].astype(bf16)) * gate).astype(bf16)


def _ffn(xe, lw, nrows):
    cap = CAPACITY_FACTOR * nrows // N_EXPERTS
    layer = lw["layer"]
    wspec = pl.BlockSpec((None, None, D_MODEL, D_EXPERT), lambda e, t: (layer, e, 0, 0))
    return pl.pallas_call(
        _ffn_kernel, grid=(N_EXPERTS, cap // TME),
        in_specs=[
            pl.BlockSpec((None, TME, D_MODEL), lambda e, t: (e, t, 0)),
            pl.BlockSpec((D_MODEL, LANES), lambda e, t: (0, 0)),
            wspec, wspec,
            pl.BlockSpec((None, None, D_EXPERT, D_MODEL), lambda e, t: (layer, e, 0, 0)),
        ],
        out_specs=pl.BlockSpec((None, TME, D_MODEL), lambda e, t: (e, t, 0)),
        out_shape=jax.ShapeDtypeStruct((N_EXPERTS, cap, D_MODEL), bf16),
        name="expert_ffn",
        compiler_params=pltpu.CompilerParams(dimension_semantics=("parallel", "arbitrary"),
                                             vmem_limit_bytes=VMEM_LIMIT),
    )(xe, lw["rw"], lw["e_wg"], lw["e_wu"], lw["e_wd"])


def _combine_kernel(big_ref, first_ref, posm_ref, x_ref, ye_hbm, g_ref, o_ref, buf, sem, *, cap, ntiles, final):
    j = pl.program_id(0)

    def window_start(jj, e, win):
        return jnp.minimum(first_ref[jj * N_EXPERTS + e] // COMB_ALIGN * COMB_ALIGN, cap - win)

    def window_copy(jj, e, slot, win):
        a = pl.multiple_of(window_start(jj, e, win), COMB_ALIGN)
        return pltpu.make_async_copy(ye_hbm.at[e, pl.ds(a, win)], buf.at[slot, pl.ds(e * win, win)],
                                     sem.at[slot])

    def for_tile_window(jj, fn):
        for flag, win in enumerate(COMB_WINDOWS):
            @pl.when(big_ref[jj] == flag)
            def _():
                fn(win)

    def start_all(jj, slot):
        def go(win):
            for e in range(N_EXPERTS):
                window_copy(jj, e, slot, win).start()
        for_tile_window(jj, go)

    @pl.when(j == 0)
    def _():
        for ahead in range(COMB_BUFS - 1):
            start_all(ahead, ahead)

    slot = j % COMB_BUFS

    def wait_all(win):
        for e in range(N_EXPERTS):
            window_copy(j, e, slot, win).wait()
    for_tile_window(j, wait_all)

    @pl.when(j + COMB_BUFS - 1 < ntiles)
    def _():
        start_all(j + COMB_BUFS - 1, (j + COMB_BUFS - 1) % COMB_BUFS)

    def add_rows(win):
        row = lax.broadcasted_iota(jnp.int32, (win, TT), 0)
        onehot = jnp.concatenate(
            [jnp.where((row + window_start(j, e, win)).astype(f32) == posm_ref[e:e + 1, :], 1.0, 0.0)
             for e in range(N_EXPERTS)], axis=0).astype(bf16)
        out = x_ref[...] + _dot_tn(onehot, buf[slot, 0:N_EXPERTS * win, :])
        o_ref[...] = _rms(out, g_ref[...]) if final else out
    for_tile_window(j, add_rows)


def _combine(x, ye, posm, first, count, row0, nrows, final_g=None):
    cap = CAPACITY_FACTOR * nrows // N_EXPERTS
    nt = nrows // TT
    blk0 = row0 // TT
    final = final_g is not None
    big = _window_class(count, COMB_WINDOWS, COMB_ALIGN)
    gain = final_g if final else jnp.ones((1, D_MODEL), f32)
    return pl.pallas_call(
        functools.partial(_combine_kernel, cap=cap, ntiles=nt, final=final),
        grid_spec=pltpu.PrefetchScalarGridSpec(
            num_scalar_prefetch=2, grid=(nt,),
            in_specs=[pl.BlockSpec((N_EXPERTS, TT), lambda j, *_: (0, j)),
                      pl.BlockSpec((TT, D_MODEL), lambda j, *_: (blk0 + j, 0)),
                      pl.BlockSpec(memory_space=pl.ANY),
                      pl.BlockSpec((1, D_MODEL), lambda j, *_: (0, 0))],
            out_specs=pl.BlockSpec((TT, D_MODEL), lambda j, *_: ((0 if final else blk0) + j, 0)),
            scratch_shapes=[pltpu.VMEM((COMB_BUFS, N_EXPERTS * COMB_WINDOWS[-1], D_MODEL), bf16),
                            pltpu.SemaphoreType.DMA((COMB_BUFS,))],
        ),
        out_shape=jax.ShapeDtypeStruct((nrows, D_MODEL) if final else x.shape, f32),
        input_output_aliases={} if final else {3: 0},
        name="combine",
        compiler_params=pltpu.CompilerParams(dimension_semantics=("arbitrary",), vmem_limit_bytes=VMEM_LIMIT),
    )(big, first.reshape(-1), posm, x, ye, gain)


def _stack_kernel(a_ref, b_ref, o_ref, *, na_blocks):
    o_ref[...] = jnp.where(pl.program_id(0) < na_blocks, a_ref[...], b_ref[...])


def _stack_rows(a, b):
    na, nb = a.shape[0] // TM, b.shape[0] // TM
    return pl.pallas_call(
        functools.partial(_stack_kernel, na_blocks=na), grid=(na + nb,),
        in_specs=[pl.BlockSpec((TM, D_MODEL), lambda i: (jnp.minimum(i, na - 1), 0)),
                  pl.BlockSpec((TM, D_MODEL), lambda i: (jnp.maximum(i - na, 0), 0))],
        out_specs=pl.BlockSpec((TM, D_MODEL), lambda i: (i, 0)),
        out_shape=jax.ShapeDtypeStruct((a.shape[0] + b.shape[0], D_MODEL), a.dtype),
        name="stack_rows",
        compiler_params=pltpu.CompilerParams(dimension_semantics=("arbitrary",)),
    )(a, b)


def _natten_bias_table(rpb):
    c = np.arange(GRID_W)
    cs = np.clip(c - WIN_C // 2, 0, GRID_W - WIN_C)
    kc = np.arange(GRID_W)
    col_mask = (kc[None, :] >= cs[:, None]) & (kc[None, :] < cs[:, None] + WIN_C)
    dc = np.clip(kc[None, :] - c[:, None], -(WIN_C - 1), WIN_C - 1) + (WIN_C - 1)
    tb = jnp.where(col_mask[None, None], rpb[:, :, dc].astype(f32), NEG)
    dw = np.arange(WIN_R)[:, None] + np.arange(WIN_R)[None, :]
    t = tb[:, dw]
    t = t.transpose(1, 0, 3, 2, 4).reshape(WIN_R, C_HEADS * GRID_W, WIN_R * GRID_W)
    return t


def _rope_slabs():
    inv = 1.0 / (ROPE_THETA ** (jnp.arange(0, ROPE_DIM, 2, dtype=f32) / ROPE_DIM))
    ang = jnp.arange(SEQ, dtype=f32)[:, None] * inv[None, :]
    cos, sin = jnp.cos(ang), jnp.sin(ang)
    ones = jnp.ones((SEQ, NOPE_DIM), f32)
    zpad = jnp.zeros((SEQ, LANES - NOPE_DIM - ROPE_DIM), f32)
    cosq = jnp.concatenate([ones, cos, cos, zpad], axis=1)
    sinq = jnp.concatenate([0.0 * ones, sin, sin, zpad], axis=1)
    return cosq, sinq


def _pack_layer(i, norm1_g, w_in, a_vnorm_g, a_ws, a_bs, b_conv, c_rpb, d_qnorm_g, d_kvnorm_g,
                d_w_uq, d_w_ukv, out_norm_g, w_o, norm2_g, router_w, e_w_gate, e_w_up, e_w_down):
    half = ROPE_DIM // 2
    w = w_in[i]
    kr0 = OFF_D + Q_RANK + KV_RANK
    kr = w[:, kr0:kr0 + ROPE_DIM]
    z64 = jnp.zeros((D_MODEL, NOPE_DIM), f32)
    z32 = jnp.zeros((D_MODEL, LANES - NOPE_DIM - ROPE_DIM), f32)
    kr_slab = jnp.concatenate([z64, kr, z32], axis=1)
    kr_rot = jnp.concatenate([z64, -kr[:, half:], kr[:, :half], z32], axis=1)
    w_pack = jnp.concatenate([w[:, :kr0], kr_slab, kr_rot], axis=1).astype(bf16)

    uq = d_w_uq[i].reshape(Q_RANK, D_HEADS, NOPE_DIM + ROPE_DIM)
    nope, rope = uq[..., :NOPE_DIM], uq[..., NOPE_DIM:]
    zq = jnp.zeros((Q_RANK, D_HEADS, LANES - NOPE_DIM - ROPE_DIM), f32)
    w_uq = jnp.concatenate([nope, rope, zq], axis=-1).reshape(Q_RANK, D_HEADS * LANES)
    w_uq_rot = jnp.concatenate([0.0 * nope, -rope[..., half:], rope[..., :half], zq],
                               axis=-1).reshape(Q_RANK, D_HEADS * LANES)
    ukv = d_w_ukv[i].reshape(KV_RANK, D_HEADS, NOPE_DIM + V_DIM)
    w_k = jnp.concatenate([ukv[..., :NOPE_DIM], jnp.zeros((KV_RANK, D_HEADS, LANES - NOPE_DIM), f32)],
                          axis=-1).reshape(KV_RANK, D_HEADS * LANES)
    w_v = ukv[..., NOPE_DIM:].reshape(KV_RANK, D_HEADS * V_DIM)
    rw = jnp.concatenate([router_w[i], jnp.zeros((D_MODEL, LANES - N_EXPERTS), f32)], axis=1)
    return {
        "g1": norm1_g[i].reshape(1, D_MODEL),
        "w_in": w_pack,
        "a_vg": a_vnorm_g[i].reshape(1, GROUP_W),
        "a_ws": a_ws[i].astype(bf16),
        "a_bias": jnp.repeat(a_bs[i].T, HEAD_DIM, axis=1),
        "b_conv": b_conv[i],
        "c_bias": _natten_bias_table(c_rpb[i]),
        "d_qg": d_qnorm_g[i].reshape(1, Q_RANK),
        "d_kvg": d_kvnorm_g[i].reshape(1, KV_RANK),
        "w_uq": w_uq.astype(bf16),
        "w_uq_rot": w_uq_rot.astype(bf16),
        "w_k": w_k.astype(bf16),
        "w_v": w_v.astype(bf16),
        "og": out_norm_g[i].reshape(1, D_MODEL),
        "w_o": w_o[i].astype(bf16),
        "g2": norm2_g[i].reshape(1, D_MODEL),
        "rw": rw.astype(bf16),
        "rwt": router_w[i].T.astype(bf16),
        "layer": i,
        "e_wg": e_w_gate,
        "e_wu": e_w_up,
        "e_wd": e_w_down,
    }


def _moe(xn, h2, aff, lw, tri, final_g=None):
    x = xn
    outs = []
    for row0, nrows in ((0, BATCH * SEQ), (BATCH * SEQ, DEC_BATCH * SEQ)):
        cap = CAPACITY_FACTOR * nrows // N_EXPERTS
        posm, base = _route(aff, row0, nrows, tri)
        first = base[:, :, 0].astype(jnp.int32)
        count = jnp.diff(first, axis=0, append=jnp.full((1, N_EXPERTS), cap, jnp.int32))
        xe = _dispatch(h2, posm, first, count, row0, nrows)
        ye = _ffn(xe, lw, nrows)
        if final_g is None:
            x = _combine(x, ye, posm, first, count, row0, nrows)
        else:
            outs.append(_combine(xn, ye, posm, first, count, row0, nrows, final_g))
    return x if final_g is None else outs


def kernel(x_prompt, x_sample, norm1_g, w_in, a_vnorm_g, a_ws, a_bs, b_conv, c_rpb, d_qnorm_g, d_kvnorm_g, d_w_uq, d_w_ukv, out_norm_g, w_o, norm2_g, router_w, e_w_gate, e_w_up, e_w_down, final_norm_g):
    n_p = BATCH * SEQ
    n_s = DEC_BATCH * SEQ
    x = _stack_rows(x_prompt.reshape(n_p, D_MODEL), x_sample.reshape(n_s, D_MODEL))
    cosq, sinq = _rope_slabs()
    tri = jnp.asarray(np.triu(np.ones((TT, TT), np.float32), 1), bf16)
    for i in range(DEPTH):
        lw = _pack_layer(i, norm1_g, w_in, a_vnorm_g, a_ws, a_bs, b_conv, c_rpb, d_qnorm_g, d_kvnorm_g,
                         d_w_uq, d_w_ukv, out_norm_g, w_o, norm2_g, router_w, e_w_gate, e_w_up, e_w_down)
        ya, yb, zc, qt, k4, vt = _inproj(x, lw, cosq, sinq)
        yc = _natten(zc, lw["c_bias"])
        yd = _mla(qt, k4, vt)
        xn, h2, aff = _outproj(ya, yb, yc, yd, x, lw)
        x = _moe(xn, h2, aff, lw, tri, final_norm_g.reshape(1, D_MODEL) if i == DEPTH - 1 else None)
    y_p, y_s = x
    return (y_p.reshape(BATCH, SEQ, D_MODEL), y_s.reshape(DEC_BATCH, SEQ, D_MODEL))
```

```python
import functools
import math

import numpy as np
import jax
import jax.numpy as jnp
from jax import lax
from jax.experimental import pallas as pl
from jax.experimental.pallas import tpu as pltpu

D_MODEL = 1024
BATCH = 4
SEQ = 8192
DEPTH = 4
DEC_BATCH = 2
GROUP_W = 256
HEAD_DIM = 64
A_HEADS = 4
CHUNK = 128
C_HEADS = 4
GRID_W = 64
WIN_R = 8
WIN_C = 16
D_HEADS = 4
Q_RANK = 256
KV_RANK = 128
NOPE_DIM = 64
ROPE_DIM = 32
V_DIM = 64
ROPE_THETA = 10000.0
N_EXPERTS = 16
D_EXPERT = 1024
CAPACITY_FACTOR = 2
EPS = 1e-6

NSEQ = BATCH + DEC_BATCH
T_ALL = NSEQ * SEQ
ROWS = SEQ // GRID_W
LANES = 128
HALO = 8
NEG = -1e30

A_COLS = 2 * GROUP_W
B_COLS = 3 * GROUP_W
C_COLS = 3 * GROUP_W
OFF_B = A_COLS
OFF_C = OFF_B + B_COLS
OFF_D = OFF_C + C_COLS
D_PACK = Q_RANK + KV_RANK + 2 * LANES
N_IN_PACK = OFF_D + D_PACK

TM = 1024
TQ = 256
TK = 512
L_ROWS = 16
MLA_UNROLL = 4
MLA_QTILES = 2
MLA_LAG = 1
MLA_QSCALE = (NOPE_DIM + ROPE_DIM) ** -0.5 * math.log2(math.e)
ROWS_PER_STEP = 16
TME = 1024
TT = 256
ROUTE_TILES = 8
ROW_ALIGN = 16
DISP_WINDOWS = (80, 144, TT + ROW_ALIGN)
COMB_ALIGN = ROW_ALIGN
COMB_WINDOWS = (80, 144, TT + COMB_ALIGN)
COMB_BUFS = 3
VMEM_LIMIT = 56 * 1024 * 1024

f32 = jnp.float32
bf16 = jnp.bfloat16


def _rms(x, g):
    return x * lax.rsqrt(jnp.mean(x * x, axis=-1, keepdims=True) + EPS) * g


def _dot(a, b):
    return jnp.dot(a, b, preferred_element_type=f32)


def _dot_nt(a, b):
    return lax.dot_general(a, b, (((1,), (1,)), ((), ())), preferred_element_type=f32)


def _dot_tn(a, b):
    return lax.dot_general(a, b, (((0,), (0,)), ((), ())), preferred_element_type=f32)


def _inproj_kernel(x_ref, xp_ref, xn_ref, g1_ref, win_ref, avg_ref, aws_ref, abias_ref,
                   bconv_ref, qg_ref, kvg_ref, wuq_ref, wuqr_ref, wk_ref, wv_ref,
                   cos_ref, sin_ref,
                   ya_ref, yb_ref, zc_ref, qt_ref, k4_ref, vt_ref):
    i = pl.program_id(0)
    j = i % (SEQ // TM)
    g1 = g1_ref[...]
    h = _rms(x_ref[...], g1).astype(bf16)

    za = _dot(h, win_ref[:, 0:A_COLS])
    zb = _dot(h, win_ref[:, OFF_B:OFF_B + B_COLS])
    zc = _dot(h, win_ref[:, OFF_C:OFF_C + C_COLS])
    zd = _dot(h, win_ref[:, OFF_D:OFF_D + D_PACK])
    xh = jnp.concatenate([xp_ref[...], xn_ref[...]], axis=0)
    hh = _rms(xh, g1).astype(bf16)
    zh = _dot(hh, win_ref[:, OFF_B + GROUP_W:OFF_B + B_COLS])

    zc_ref[:, 0:GROUP_W] = (zc[:, 0:GROUP_W] * (HEAD_DIM ** -0.5)).astype(bf16)
    zc_ref[:, GROUP_W:] = zc[:, GROUP_W:].astype(bf16)

    cqn = _rms(zd[:, 0:Q_RANK], qg_ref[...]).astype(bf16)
    ckvn = _rms(zd[:, Q_RANK:Q_RANK + KV_RANK], kvg_ref[...]).astype(bf16)
    cos = cos_ref[...]
    sin = sin_ref[...]
    o = Q_RANK + KV_RANK
    kr = zd[:, o:o + LANES] * cos + zd[:, o + LANES:o + 2 * LANES] * sin
    q_pre = _dot(cqn, wuq_ref[...])
    q_rot = _dot(cqn, wuqr_ref[...])
    kn = _dot(ckvn, wk_ref[...])
    vt = _dot(ckvn, wv_ref[...]).T.astype(bf16)

    za = jax.nn.gelu(za)
    u = za[:, :GROUP_W]
    v = _rms(za[:, GROUP_W:], avg_ref[...]).astype(bf16)
    lane = lax.broadcasted_iota(jnp.int32, (CHUNK, GROUP_W), 1)
    for c in range(TM // CHUNK):
        vc = v[c * CHUNK:(c + 1) * CHUNK, :]
        sv = abias_ref[...]
        for hd in range(A_HEADS):
            r = _dot(aws_ref[hd], vc)
            sv = sv + jnp.where((lane >= hd * HEAD_DIM) & (lane < (hd + 1) * HEAD_DIM), r, 0.0)
        ya_ref[c * CHUNK:(c + 1) * CHUNK, :] = u[c * CHUNK:(c + 1) * CHUNK, :] * sv

    bg = zb[:, :GROUP_W]
    y = zb[:, GROUP_W:2 * GROUP_W] * zb[:, 2 * GROUP_W:]
    yh = zh[:, :GROUP_W] * zh[:, GROUP_W:]
    y_before = jnp.where(j == 0, 0.0, yh[HALO - 1:HALO, :])
    y_after = jnp.where(j == SEQ // TM - 1, 0.0, yh[HALO:HALO + 1, :])
    row = lax.broadcasted_iota(jnp.int32, (TM, GROUP_W), 0)
    y_m1 = jnp.where(row == 0, y_before, pltpu.roll(y, 1, axis=0))
    y_p1 = jnp.where(row == TM - 1, y_after, pltpu.roll(y, TM - 1, axis=0))
    wc = bconv_ref[...]
    yb_ref[...] = bg * (wc[0:1, :] * y_m1 + wc[1:2, :] * y + wc[2:3, :] * y_p1)

    for hd in range(D_HEADS):
        sl = slice(hd * LANES, (hd + 1) * LANES)
        qt_ref[hd] = ((q_pre[:, sl] * cos + q_rot[:, sl] * sin) * MLA_QSCALE).T.astype(bf16)
        k4_ref[hd] = (kn[:, sl] + kr).astype(bf16)
        vt_ref[hd] = vt[hd * V_DIM:(hd + 1) * V_DIM, :]


def _inproj(x, lw, cosq, sinq):
    nt = T_ALL // TM
    tps = SEQ // TM
    full = lambda shape: pl.BlockSpec(shape, lambda i: (0,) * len(shape))
    in_specs = [
        pl.BlockSpec((TM, D_MODEL), lambda i: (i, 0)),
        pl.BlockSpec((HALO, D_MODEL), lambda i: (jnp.maximum(i * (TM // HALO) - 1, 0), 0)),
        pl.BlockSpec((HALO, D_MODEL), lambda i: (jnp.minimum((i + 1) * (TM // HALO), T_ALL // HALO - 1), 0)),
        full((1, D_MODEL)),
        full((D_MODEL, N_IN_PACK)),
        full((1, GROUP_W)),
        full((A_HEADS, CHUNK, CHUNK)),
        full((CHUNK, GROUP_W)),
        full((3, GROUP_W)),
        full((1, Q_RANK)),
        full((1, KV_RANK)),
        full((Q_RANK, D_HEADS * LANES)),
        full((Q_RANK, D_HEADS * LANES)),
        full((KV_RANK, D_HEADS * LANES)),
        full((KV_RANK, D_HEADS * V_DIM)),
        pl.BlockSpec((TM, LANES), lambda i: (i % tps, 0)),
        pl.BlockSpec((TM, LANES), lambda i: (i % tps, 0)),
    ]
    out_specs = [
        pl.BlockSpec((TM, GROUP_W), lambda i: (i, 0)),
        pl.BlockSpec((TM, GROUP_W), lambda i: (i, 0)),
        pl.BlockSpec((TM, C_COLS), lambda i: (i, 0)),
        pl.BlockSpec((None, D_HEADS, LANES, TM), lambda i: (i // tps, 0, 0, i % tps)),
        pl.BlockSpec((None, D_HEADS, TM, LANES), lambda i: (i // tps, 0, i % tps, 0)),
        pl.BlockSpec((None, D_HEADS, V_DIM, TM), lambda i: (i // tps, 0, 0, i % tps)),
    ]
    out_shape = [
        jax.ShapeDtypeStruct((T_ALL, GROUP_W), f32),
        jax.ShapeDtypeStruct((T_ALL, GROUP_W), f32),
        jax.ShapeDtypeStruct((T_ALL, C_COLS), bf16),
        jax.ShapeDtypeStruct((NSEQ, D_HEADS, LANES, SEQ), bf16),
        jax.ShapeDtypeStruct((NSEQ, D_HEADS, SEQ, LANES), bf16),
        jax.ShapeDtypeStruct((NSEQ, D_HEADS, V_DIM, SEQ), bf16),
    ]
    return pl.pallas_call(
        _inproj_kernel, grid=(nt,), in_specs=in_specs, out_specs=out_specs, out_shape=out_shape,
        name="inproj",
        compiler_params=pltpu.CompilerParams(dimension_semantics=("parallel",),
                                             vmem_limit_bytes=VMEM_LIMIT),
    )(x, x, x, lw["g1"], lw["w_in"], lw["a_vg"], lw["a_ws"], lw["a_bias"], lw["b_conv"],
      lw["d_qg"], lw["d_kvg"], lw["w_uq"], lw["w_uq_rot"], lw["w_k"], lw["w_v"], cosq, sinq)


def _natten_kernel(q_ref, k_ref, v_ref, bias_ref, o_ref):
    jb = pl.program_id(1)
    lane = lax.broadcasted_iota(jnp.int32, (GRID_W, GROUP_W), 1)
    head_of_lane = lane // HEAD_DIM
    nkeys = WIN_R * GRID_W

    jobs = []
    for i in range(ROWS_PER_STEP):
        r = jb * ROWS_PER_STEP + i
        rs = jnp.clip(r - WIN_R // 2, 0, ROWS - WIN_R)
        d = rs - r + (WIN_R - 1)
        kstart = pl.multiple_of(rs * GRID_W, GRID_W)
        q = q_ref[i * GRID_W:(i + 1) * GRID_W, :]
        zero = jnp.zeros_like(q)
        qq = jnp.concatenate([jnp.where(head_of_lane == h, q, zero) for h in range(C_HEADS)], axis=0)
        jobs.append((_dot_nt(qq, k_ref[pl.ds(kstart, nkeys), :]), d, kstart))
    for i, (sc, d, kstart) in enumerate(jobs):
        sc = sc + bias_ref[d]
        m = jnp.max(sc, axis=-1, keepdims=True)
        p = jnp.exp(sc - m)
        l = jnp.sum(p, axis=-1, keepdims=True)
        o = _dot(p.astype(bf16), v_ref[pl.ds(kstart, nkeys), :]) / l
        out = jnp.zeros((GRID_W, GROUP_W), f32)
        for h in range(C_HEADS):
            out = jnp.where(head_of_lane == h, o[h * GRID_W:(h + 1) * GRID_W], out)
        o_ref[i * GRID_W:(i + 1) * GRID_W, :] = out


def _natten(zc, bias):
    nb = ROWS // ROWS_PER_STEP
    tq = ROWS_PER_STEP * GRID_W
    return pl.pallas_call(
        _natten_kernel, grid=(NSEQ, nb),
        in_specs=[
            pl.BlockSpec((tq, GROUP_W), lambda b, j: (b * nb + j, 0)),
            pl.BlockSpec((SEQ, GROUP_W), lambda b, j: (b, 1)),
            pl.BlockSpec((SEQ, GROUP_W), lambda b, j: (b, 2)),
            pl.BlockSpec((WIN_R, C_HEADS * GRID_W, WIN_R * GRID_W), lambda b, j: (0, 0, 0)),
        ],
        out_specs=pl.BlockSpec((tq, GROUP_W), lambda b, j: (b * nb + j, 0)),
        out_shape=jax.ShapeDtypeStruct((T_ALL, GROUP_W), f32),
        name="natten",
        compiler_params=pltpu.CompilerParams(dimension_semantics=("parallel", "arbitrary"),
                                             vmem_limit_bytes=VMEM_LIMIT),
    )(zc, zc, zc, bias)


def _mla_kernel(qt_ref, k_ref, vt_ref, o_ref, s_scr):
    nchunk = SEQ // TK
    chains = [(t, h) for t in range(MLA_QTILES) for h in range(D_HEADS)]
    nc = len(chains)
    qts = [qt_ref[h, :, t * TQ:(t + 1) * TQ] for t, h in chains]

    def qk(slot, c, ci):
        ks = pl.multiple_of(ci * TK, TK)
        s_scr[slot, c] = _dot(k_ref[chains[c][1], pl.ds(ks, TK), :], qts[c])

    ones = jnp.ones((L_ROWS, TK), bf16)

    def softmax_pv(slot, c, ci, stat):
        m, acc = stat
        ks = pl.multiple_of(ci * TK, TK)
        vt = jnp.concatenate([vt_ref[chains[c][1], :, pl.ds(ks, TK)], ones], axis=0)
        s = s_scr[slot, c]
        m_new = jnp.maximum(m, jnp.max(s, axis=0, keepdims=True))
        alpha = jnp.exp2(m - m_new)
        p = jnp.exp2(s - m_new)
        acc = acc * alpha + _dot(vt, p.astype(bf16))
        return m_new, acc

    def half(cur, ci, stats, prefetch=True):
        out = []
        for c in range(nc):
            if prefetch:
                qk(1 - cur, c, ci + 1)
            if c >= MLA_LAG:
                out.append(softmax_pv(cur, c - MLA_LAG, ci, stats[c - MLA_LAG]))
        for c in range(nc - MLA_LAG, nc):
            out.append(softmax_pv(cur, c, ci, stats[c]))
        return tuple(out)

    def body(cp, stats):
        for u in range(MLA_UNROLL):
            stats = half(u % 2, MLA_UNROLL * cp + u, stats)
        return stats

    for c in range(nc):
        qk(0, c, 0)
    init = tuple((jnp.full((1, TQ), NEG, f32), jnp.zeros((V_DIM + L_ROWS, TQ), f32)) for _ in chains)
    res = lax.fori_loop(0, nchunk // MLA_UNROLL - 1, body, init)
    for u in range(MLA_UNROLL):
        res = half(u % 2, nchunk - MLA_UNROLL + u, res, prefetch=u < MLA_UNROLL - 1)
    for t in range(MLA_QTILES):
        o_ref[t * TQ:(t + 1) * TQ, :] = jnp.concatenate(
            [acc[:V_DIM] / acc[V_DIM:V_DIM + 1] for (_, acc) in res[t * D_HEADS:(t + 1) * D_HEADS]], axis=0).T


def _mla(qt, k4, vt):
    tq = TQ * MLA_QTILES
    nq = SEQ // tq
    return pl.pallas_call(
        _mla_kernel, grid=(NSEQ, nq),
        in_specs=[
            pl.BlockSpec((None, D_HEADS, LANES, tq), lambda b, i: (b, 0, 0, i)),
            pl.BlockSpec((None, D_HEADS, SEQ, LANES), lambda b, i: (b, 0, 0, 0)),
            pl.BlockSpec((None, D_HEADS, V_DIM, SEQ), lambda b, i: (b, 0, 0, 0)),
        ],
        out_specs=pl.BlockSpec((tq, GROUP_W), lambda b, i: (b * nq + i, 0)),
        out_shape=jax.ShapeDtypeStruct((T_ALL, GROUP_W), f32),
        scratch_shapes=[pltpu.VMEM((2, D_HEADS * MLA_QTILES, TK, TQ), f32)],
        name="mla",
        compiler_params=pltpu.CompilerParams(
            dimension_semantics=("parallel", "arbitrary"),
            vmem_limit_bytes=VMEM_LIMIT),
    )(qt, k4, vt)


def _outproj_kernel(ya_ref, yb_ref, yc_ref, yd_ref, x_ref, og_ref, wo_ref, g2_ref, rwt_ref,
                    xn_ref, h2_ref, aff_ref):
    y = jnp.concatenate(
        [_rms(y_ref[...], og_ref[:, g * GROUP_W:(g + 1) * GROUP_W]).astype(bf16)
         for g, y_ref in enumerate((ya_ref, yb_ref, yc_ref, yd_ref))], axis=1)
    acc = x_ref[...] + _dot(y, wo_ref[...])
    xn_ref[...] = acc
    h2 = _rms(acc, g2_ref[...]).astype(bf16)
    h2_ref[...] = h2
    logits = _dot_nt(rwt_ref[...], h2)
    e = jnp.exp(logits - jnp.max(logits, axis=0, keepdims=True))
    aff_ref[...] = e / jnp.sum(e, axis=0, keepdims=True)


def _outproj(ya, yb, yc, yd, x, lw):
    nt = T_ALL // TM
    full = lambda shape: pl.BlockSpec(shape, lambda i: (0,) * len(shape))
    yspec = pl.BlockSpec((TM, GROUP_W), lambda i: (i, 0))
    xspec = pl.BlockSpec((TM, D_MODEL), lambda i: (i, 0))
    return pl.pallas_call(
        _outproj_kernel, grid=(nt,),
        in_specs=[yspec, yspec, yspec, yspec, xspec, full((1, D_MODEL)), full((D_MODEL, D_MODEL)),
                  full((1, D_MODEL)), full((N_EXPERTS, D_MODEL))],
        out_specs=[xspec, xspec, pl.BlockSpec((N_EXPERTS, TM), lambda i: (0, i))],
        out_shape=[jax.ShapeDtypeStruct((T_ALL, D_MODEL), f32),
                   jax.ShapeDtypeStruct((T_ALL, D_MODEL), bf16),
                   jax.ShapeDtypeStruct((N_EXPERTS, T_ALL), f32)],
        name="outproj",
        compiler_params=pltpu.CompilerParams(dimension_semantics=("parallel",),
                                             vmem_limit_bytes=VMEM_LIMIT),
    )(ya, yb, yc, yd, x, lw["og"], lw["w_o"], lw["g2"], lw["rwt"])


def _route_kernel(aff_ref, tri_ref, posm_ref, base_ref, thr_scr, need_scr, runeq_scr, runsel_scr, *, cap):
    j = pl.program_id(0)

    @pl.when(j == 0)
    def _():
        def refine(i, thr):
            cand = thr | jnp.left_shift(jnp.int32(1), 30 - i)
            bits = pltpu.bitcast(aff_ref[...], jnp.int32)
            cnt = jnp.sum(jnp.where(bits >= cand, 1.0, 0.0), axis=1, keepdims=True)
            return jnp.where(cnt >= cap, cand, thr)

        thr = lax.fori_loop(0, 31, refine, jnp.zeros((N_EXPERTS, 1), jnp.int32))
        bits = pltpu.bitcast(aff_ref[...], jnp.int32)
        above = jnp.sum(jnp.where(bits > thr, 1.0, 0.0), axis=1, keepdims=True)
        thr_scr[...] = jnp.broadcast_to(thr, (N_EXPERTS, LANES))
        need_scr[...] = jnp.broadcast_to(cap - above, (N_EXPERTS, LANES))
        runeq_scr[...] = jnp.zeros((N_EXPERTS, LANES), f32)
        runsel_scr[...] = jnp.zeros((N_EXPERTS, LANES), f32)

    thr = thr_scr[:, 0:1]
    need = need_scr[:, 0:1]
    tiles = []
    for u in range(ROUTE_TILES):
        t0 = pl.multiple_of((j * ROUTE_TILES + u) * TT, TT)
        bits = pltpu.bitcast(aff_ref[:, pl.ds(t0, TT)], jnp.int32)
        eq = bits == thr
        eq_f = jnp.where(eq, 1.0, 0.0)
        tiles.append((bits, eq, eq_f, _dot(eq_f.astype(bf16), tri_ref[...])))
    sels = []
    runeq = runeq_scr[:, 0:1]
    for bits, eq, eq_f, eq_before in tiles:
        sel = (bits > thr) | (eq & (runeq + eq_before < need))
        sel_f = jnp.where(sel, 1.0, 0.0)
        sels.append((sel, sel_f, _dot(sel_f.astype(bf16), tri_ref[...])))
        runeq = runeq + jnp.sum(eq_f, axis=1, keepdims=True)
    runeq_scr[...] = jnp.broadcast_to(runeq, (N_EXPERTS, LANES))
    base = runsel_scr[...]
    for u, (sel, sel_f, sel_before) in enumerate(sels):
        posm_ref[:, u * TT:(u + 1) * TT] = jnp.where(sel, base[:, 0:1] + sel_before, -1.0)
        base_ref[u] = base
        base = base + jnp.sum(sel_f, axis=1, keepdims=True)
    runsel_scr[...] = base


def _route(aff, row0, nrows, tri):
    cap = CAPACITY_FACTOR * nrows // N_EXPERTS
    nt = nrows // TT
    blk0 = row0 // nrows
    vec = pltpu.VMEM((N_EXPERTS, LANES), f32)
    return pl.pallas_call(
        functools.partial(_route_kernel, cap=cap), grid=(nt // ROUTE_TILES,),
        in_specs=[pl.BlockSpec((N_EXPERTS, nrows), lambda j: (0, blk0)),
                  pl.BlockSpec((TT, TT), lambda j: (0, 0))],
        out_specs=[pl.BlockSpec((N_EXPERTS, ROUTE_TILES * TT), lambda j: (0, j)),
                   pl.BlockSpec((ROUTE_TILES, N_EXPERTS, LANES), lambda j: (j, 0, 0))],
        out_shape=[jax.ShapeDtypeStruct((N_EXPERTS, nrows), f32),
                   jax.ShapeDtypeStruct((nt, N_EXPERTS, LANES), f32)],
        scratch_shapes=[pltpu.VMEM((N_EXPERTS, LANES), jnp.int32), vec, vec, vec],
        name="route",
        compiler_params=pltpu.CompilerParams(dimension_semantics=("arbitrary",)),
    )(aff, tri)


def _window_class(count, windows, align):
    longest = jnp.max(count, axis=1) + (align - 1)
    return sum((longest > w).astype(jnp.int32) for w in windows[:-1])


def _dispatch_kernel(big_ref, first_ref, posm_ref, h2_ref, xe_hbm, stage, sem, carry, *, cap, ntiles):
    j = pl.program_id(0)
    slot = j % 2

    @pl.when(j == 0)
    def _():
        carry[...] = jnp.zeros_like(carry)
        pad = DISP_WINDOWS[-1]
        stage[1, 0:pad, :] = jnp.zeros((pad, D_MODEL), bf16)
        pads = [pltpu.make_async_copy(stage.at[1, pl.ds(0, pad)], xe_hbm.at[e, pl.ds(cap, pad)], sem.at[1])
                for e in range(N_EXPERTS)]
        for p in pads:
            p.start()
        for p in pads:
            p.wait()

    def run_start(jj, e):
        return first_ref[jj * N_EXPERTS + e]

    def window_copy(jj, e, sl, win):
        a = pl.multiple_of(run_start(jj, e) // ROW_ALIGN * ROW_ALIGN, ROW_ALIGN)
        return pltpu.make_async_copy(stage.at[sl, pl.ds(e * win, win)], xe_hbm.at[e, pl.ds(a, win)], sem.at[sl])

    def for_tile_window(jj, fn):
        for flag, win in enumerate(DISP_WINDOWS):
            @pl.when(big_ref[jj] == flag)
            def _():
                fn(win)

    def wait_all(jj, sl):
        def go(win):
            for e in range(N_EXPERTS):
                window_copy(jj, e, sl, win).wait()
        for_tile_window(jj, go)

    def compact(win):
        row = lax.broadcasted_iota(jnp.int32, (win, TT), 0)
        starts = [run_start(j, e) // ROW_ALIGN * ROW_ALIGN for e in range(N_EXPERTS)]
        onehot = jnp.concatenate(
            [jnp.where((row + starts[e]).astype(f32) == posm_ref[e:e + 1, :], 1.0, 0.0)
             for e in range(N_EXPERTS)], axis=0).astype(bf16)
        rows = _dot(onehot, h2_ref[...])
        head = lax.broadcasted_iota(jnp.int32, (ROW_ALIGN, D_MODEL), 0)
        for e in range(N_EXPERTS):
            lead = run_start(j, e) - starts[e]
            w = rows[e * win:(e + 1) * win]
            top = w[:ROW_ALIGN] + jnp.where(head < lead, carry[e], 0.0)
            stage[slot, e * win:e * win + ROW_ALIGN, :] = top.astype(bf16)
            stage[slot, e * win + ROW_ALIGN:(e + 1) * win, :] = w[ROW_ALIGN:].astype(bf16)
        for e in range(N_EXPERTS):
            end = jnp.where(j + 1 < ntiles, run_start(jnp.minimum(j + 1, ntiles - 1), e), cap)
            off = jnp.minimum(end // ROW_ALIGN * ROW_ALIGN - starts[e], win - ROW_ALIGN)
            off = pl.multiple_of(off, ROW_ALIGN)
            carry[e] = stage[slot, pl.ds(e * win + off, ROW_ALIGN), :].astype(f32)

    for_tile_window(j, compact)

    @pl.when(j > 0)
    def _():
        wait_all(j - 1, 1 - slot)

    def start_all(win):
        for e in range(N_EXPERTS):
            window_copy(j, e, slot, win).start()
    for_tile_window(j, start_all)

    @pl.when(j == ntiles - 1)
    def _():
        wait_all(j, slot)


def _dispatch(h2, posm, first, count, row0, nrows):
    cap = CAPACITY_FACTOR * nrows // N_EXPERTS
    nt = nrows // TT
    blk0 = row0 // TT
    big = _window_class(count, DISP_WINDOWS, ROW_ALIGN)
    return pl.pallas_call(
        functools.partial(_dispatch_kernel, cap=cap, ntiles=nt),
        grid_spec=pltpu.PrefetchScalarGridSpec(
            num_scalar_prefetch=2, grid=(nt,),
            in_specs=[pl.BlockSpec((N_EXPERTS, TT), lambda j, *_: (0, j)),
                      pl.BlockSpec((TT, D_MODEL), lambda j, *_: (blk0 + j, 0))],
            out_specs=pl.BlockSpec(memory_space=pl.ANY),
            scratch_shapes=[pltpu.VMEM((2, N_EXPERTS * DISP_WINDOWS[-1], D_MODEL), bf16),
                            pltpu.SemaphoreType.DMA((2,)),
                            pltpu.VMEM((N_EXPERTS, ROW_ALIGN, D_MODEL), f32)],
        ),
        out_shape=jax.ShapeDtypeStruct((N_EXPERTS, cap + DISP_WINDOWS[-1], D_MODEL), bf16),
        name="dispatch",
        compiler_params=pltpu.CompilerParams(dimension_semantics=("arbitrary",), vmem_limit_bytes=VMEM_LIMIT),
    )(big, first.reshape(-1), posm, h2)


def _ffn_kernel(xe_ref, rw_ref, wg_ref, wu_ref, wd_ref, ye_ref):
    e = pl.program_id(0)
    x = xe_ref[...]
    logits = _dot(x, rw_ref[...])
    lane = lax.broadcasted_iota(jnp.int32, logits.shape, 1)
    logits = jnp.where(lane < N_EXPERTS, logits, NEG)
    ex = jnp.exp(logits - jnp.max(logits, axis=-1, keepdims=True))
    gate = jnp.sum(jnp.where(lane == e, ex, 0.0), axis=-1, keepdims=True) / jnp.sum(ex, axis=-1, keepdims=True)
    hg = _dot(x, wg_ref[...].astype(bf16))
    hu = _dot(x, wu_ref[...].astype(bf16))
    h = (hg * (1.0 / (1.0 + jnp.exp(-hg))) * hu).astype(bf16)
    ye_ref[...] = (_dot(h, wd_ref[...].astype(bf16)) * gate).astype(bf16)


def _ffn(xe, lw, nrows):
    cap = CAPACITY_FACTOR * nrows // N_EXPERTS
    layer = lw["layer"]
    wspec = pl.BlockSpec((None, None, D_MODEL, D_EXPERT), lambda e, t: (layer, e, 0, 0))
    return pl.pallas_call(
        _ffn_kernel, grid=(N_EXPERTS, cap // TME),
        in_specs=[
            pl.BlockSpec((None, TME, D_MODEL), lambda e, t: (e, t, 0)),
            pl.BlockSpec((D_MODEL, LANES), lambda e, t: (0, 0)),
            wspec, wspec,
            pl.BlockSpec((None, None, D_EXPERT, D_MODEL), lambda e, t: (layer, e, 0, 0)),
        ],
        out_specs=pl.BlockSpec((None, TME, D_MODEL), lambda e, t: (e, t, 0)),
        out_shape=jax.ShapeDtypeStruct((N_EXPERTS, cap, D_MODEL), bf16),
        name="expert_ffn",
        compiler_params=pltpu.CompilerParams(dimension_semantics=("parallel", "arbitrary"),
                                             vmem_limit_bytes=VMEM_LIMIT),
    )(xe, lw["rw"], lw["e_wg"], lw["e_wu"], lw["e_wd"])


def _combine_kernel(big_ref, first_ref, posm_ref, x_ref, ye_hbm, g_ref, o_ref, buf, sem, *, cap, ntiles, final):
    j = pl.program_id(0)

    def window_start(jj, e, win):
        return jnp.minimum(first_ref[jj * N_EXPERTS + e] // COMB_ALIGN * COMB_ALIGN, cap - win)

    def window_copy(jj, e, slot, win):
        a = pl.multiple_of(window_start(jj, e, win), COMB_ALIGN)
        return pltpu.make_async_copy(ye_hbm.at[e, pl.ds(a, win)], buf.at[slot, pl.ds(e * win, win)],
                                     sem.at[slot])

    def for_tile_window(jj, fn):
        for flag, win in enumerate(COMB_WINDOWS):
            @pl.when(big_ref[jj] == flag)
            def _():
                fn(win)

    def start_all(jj, slot):
        def go(win):
            for e in range(N_EXPERTS):
                window_copy(jj, e, slot, win).start()
        for_tile_window(jj, go)

    @pl.when(j == 0)
    def _():
        for ahead in range(COMB_BUFS - 1):
            start_all(ahead, ahead)

    slot = j % COMB_BUFS

    def wait_all(win):
        for e in range(N_EXPERTS):
            window_copy(j, e, slot, win).wait()
    for_tile_window(j, wait_all)

    @pl.when(j + COMB_BUFS - 1 < ntiles)
    def _():
        start_all(j + COMB_BUFS - 1, (j + COMB_BUFS - 1) % COMB_BUFS)

    def add_rows(win):
        row = lax.broadcasted_iota(jnp.int32, (win, TT), 0)
        onehot = jnp.concatenate(
            [jnp.where((row + window_start(j, e, win)).astype(f32) == posm_ref[e:e + 1, :], 1.0, 0.0)
             for e in range(N_EXPERTS)], axis=0).astype(bf16)
        out = x_ref[...] + _dot_tn(onehot, buf[slot, 0:N_EXPERTS * win, :])
        o_ref[...] = _rms(out, g_ref[...]) if final else out
    for_tile_window(j, add_rows)


def _combine(x, ye, posm, first, count, row0, nrows, final_g=None):
    cap = CAPACITY_FACTOR * nrows // N_EXPERTS
    nt = nrows // TT
    blk0 = row0 // TT
    final = final_g is not None
    big = _window_class(count, COMB_WINDOWS, COMB_ALIGN)
    gain = final_g if final else jnp.ones((1, D_MODEL), f32)
    return pl.pallas_call(
        functools.partial(_combine_kernel, cap=cap, ntiles=nt, final=final),
        grid_spec=pltpu.PrefetchScalarGridSpec(
            num_scalar_prefetch=2, grid=(nt,),
            in_specs=[pl.BlockSpec((N_EXPERTS, TT), lambda j, *_: (0, j)),
                      pl.BlockSpec((TT, D_MODEL), lambda j, *_: (blk0 + j, 0)),
                      pl.BlockSpec(memory_space=pl.ANY),
                      pl.BlockSpec((1, D_MODEL), lambda j, *_: (0, 0))],
            out_specs=pl.BlockSpec((TT, D_MODEL), lambda j, *_: ((0 if final else blk0) + j, 0)),
            scratch_shapes=[pltpu.VMEM((COMB_BUFS, N_EXPERTS * COMB_WINDOWS[-1], D_MODEL), bf16),
                            pltpu.SemaphoreType.DMA((COMB_BUFS,))],
        ),
        out_shape=jax.ShapeDtypeStruct((nrows, D_MODEL) if final else x.shape, f32),
        input_output_aliases={} if final else {3: 0},
        name="combine",
        compiler_params=pltpu.CompilerParams(dimension_semantics=("arbitrary",), vmem_limit_bytes=VMEM_LIMIT),
    )(big, first.reshape(-1), posm, x, ye, gain)


def _stack_kernel(a_ref, b_ref, o_ref, *, na_blocks):
    o_ref[...] = jnp.where(pl.program_id(0) < na_blocks, a_ref[...], b_ref[...])


def _stack_rows(a, b):
    na, nb = a.shape[0] // TM, b.shape[0] // TM
    return pl.pallas_call(
        functools.partial(_stack_kernel, na_blocks=na), grid=(na + nb,),
        in_specs=[pl.BlockSpec((TM, D_MODEL), lambda i: (jnp.minimum(i, na - 1), 0)),
                  pl.BlockSpec((TM, D_MODEL), lambda i: (jnp.maximum(i - na, 0), 0))],
        out_specs=pl.BlockSpec((TM, D_MODEL), lambda i: (i, 0)),
        out_shape=jax.ShapeDtypeStruct((a.shape[0] + b.shape[0], D_MODEL), a.dtype),
        name="stack_rows",
        compiler_params=pltpu.CompilerParams(dimension_semantics=("arbitrary",)),
    )(a, b)


def _natten_bias_table(rpb):
    c = np.arange(GRID_W)
    cs = np.clip(c - WIN_C // 2, 0, GRID_W - WIN_C)
    kc = np.arange(GRID_W)
    col_mask = (kc[None, :] >= cs[:, None]) & (kc[None, :] < cs[:, None] + WIN_C)
    dc = np.clip(kc[None, :] - c[:, None], -(WIN_C - 1), WIN_C - 1) + (WIN_C - 1)
    tb = jnp.where(col_mask[None, None], rpb[:, :, dc].astype(f32), NEG)
    dw = np.arange(WIN_R)[:, None] + np.arange(WIN_R)[None, :]
    t = tb[:, dw]
    t = t.transpose(1, 0, 3, 2, 4).reshape(WIN_R, C_HEADS * GRID_W, WIN_R * GRID_W)
    return t


def _rope_slabs():
    inv = 1.0 / (ROPE_THETA ** (jnp.arange(0, ROPE_DIM, 2, dtype=f32) / ROPE_DIM))
    ang = jnp.arange(SEQ, dtype=f32)[:, None] * inv[None, :]
    cos, sin = jnp.cos(ang), jnp.sin(ang)
    ones = jnp.ones((SEQ, NOPE_DIM), f32)
    zpad = jnp.zeros((SEQ, LANES - NOPE_DIM - ROPE_DIM), f32)
    cosq = jnp.concatenate([ones, cos, cos, zpad], axis=1)
    sinq = jnp.concatenate([0.0 * ones, sin, sin, zpad], axis=1)
    return cosq, sinq


def _pack_layer(i, norm1_g, w_in, a_vnorm_g, a_ws, a_bs, b_conv, c_rpb, d_qnorm_g, d_kvnorm_g,
                d_w_uq, d_w_ukv, out_norm_g, w_o, norm2_g, router_w, e_w_gate, e_w_up, e_w_down):
    half = ROPE_DIM // 2
    w = w_in[i]
    kr0 = OFF_D + Q_RANK + KV_RANK
    kr = w[:, kr0:kr0 + ROPE_DIM]
    z64 = jnp.zeros((D_MODEL, NOPE_DIM), f32)
    z32 = jnp.zeros((D_MODEL, LANES - NOPE_DIM - ROPE_DIM), f32)
    kr_slab = jnp.concatenate([z64, kr, z32], axis=1)
    kr_rot = jnp.concatenate([z64, -kr[:, half:], kr[:, :half], z32], axis=1)
    w_pack = jnp.concatenate([w[:, :kr0], kr_slab, kr_rot], axis=1).astype(bf16)

    uq = d_w_uq[i].reshape(Q_RANK, D_HEADS, NOPE_DIM + ROPE_DIM)
    nope, rope = uq[..., :NOPE_DIM], uq[..., NOPE_DIM:]
    zq = jnp.zeros((Q_RANK, D_HEADS, LANES - NOPE_DIM - ROPE_DIM), f32)
    w_uq = jnp.concatenate([nope, rope, zq], axis=-1).reshape(Q_RANK, D_HEADS * LANES)
    w_uq_rot = jnp.concatenate([0.0 * nope, -rope[..., half:], rope[..., :half], zq],
                               axis=-1).reshape(Q_RANK, D_HEADS * LANES)
    ukv = d_w_ukv[i].reshape(KV_RANK, D_HEADS, NOPE_DIM + V_DIM)
    w_k = jnp.concatenate([ukv[..., :NOPE_DIM], jnp.zeros((KV_RANK, D_HEADS, LANES - NOPE_DIM), f32)],
                          axis=-1).reshape(KV_RANK, D_HEADS * LANES)
    w_v = ukv[..., NOPE_DIM:].reshape(KV_RANK, D_HEADS * V_DIM)
    rw = jnp.concatenate([router_w[i], jnp.zeros((D_MODEL, LANES - N_EXPERTS), f32)], axis=1)
    return {
        "g1": norm1_g[i].reshape(1, D_MODEL),
        "w_in": w_pack,
        "a_vg": a_vnorm_g[i].reshape(1, GROUP_W),
        "a_ws": a_ws[i].astype(bf16),
        "a_bias": jnp.repeat(a_bs[i].T, HEAD_DIM, axis=1),
        "b_conv": b_conv[i],
        "c_bias": _natten_bias_table(c_rpb[i]),
        "d_qg": d_qnorm_g[i].reshape(1, Q_RANK),
        "d_kvg": d_kvnorm_g[i].reshape(1, KV_RANK),
        "w_uq": w_uq.astype(bf16),
        "w_uq_rot": w_uq_rot.astype(bf16),
        "w_k": w_k.astype(bf16),
        "w_v": w_v.astype(bf16),
        "og": out_norm_g[i].reshape(1, D_MODEL),
        "w_o": w_o[i].astype(bf16),
        "g2": norm2_g[i].reshape(1, D_MODEL),
        "rw": rw.astype(bf16),
        "rwt": router_w[i].T.astype(bf16),
        "layer": i,
        "e_wg": e_w_gate,
        "e_wu": e_w_up,
        "e_wd": e_w_down,
    }


def _moe(xn, h2, aff, lw, tri, final_g=None):
    x = xn
    outs = []
    for row0, nrows in ((0, BATCH * SEQ), (BATCH * SEQ, DEC_BATCH * SEQ)):
        cap = CAPACITY_FACTOR * nrows // N_EXPERTS
        posm, base = _route(aff, row0, nrows, tri)
        first = base[:, :, 0].astype(jnp.int32)
        count = jnp.diff(first, axis=0, append=jnp.full((1, N_EXPERTS), cap, jnp.int32))
        xe = _dispatch(h2, posm, first, count, row0, nrows)
        ye = _ffn(xe, lw, nrows)
        if final_g is None:
            x = _combine(x, ye, posm, first, count, row0, nrows)
        else:
            outs.append(_combine(xn, ye, posm, first, count, row0, nrows, final_g))
    return x if final_g is None else outs


def kernel(x_prompt, x_sample, norm1_g, w_in, a_vnorm_g, a_ws, a_bs, b_conv, c_rpb, d_qnorm_g, d_kvnorm_g, d_w_uq, d_w_ukv, out_norm_g, w_o, norm2_g, router_w, e_w_gate, e_w_up, e_w_down, final_norm_g):
    n_p = BATCH * SEQ
    n_s = DEC_BATCH * SEQ
    x = _stack_rows(x_prompt.reshape(n_p, D_MODEL), x_sample.reshape(n_s, D_MODEL))
    cosq, sinq = _rope_slabs()
    tri = jnp.asarray(np.triu(np.ones((TT, TT), np.float32), 1), bf16)
    for i in range(DEPTH):
        lw = _pack_layer(i, norm1_g, w_in, a_vnorm_g, a_ws, a_bs, b_conv, c_rpb, d_qnorm_g, d_kvnorm_g,
                         d_w_uq, d_w_ukv, out_norm_g, w_o, norm2_g, router_w, e_w_gate, e_w_up, e_w_down)
        ya, yb, zc, qt, k4, vt = _inproj(x, lw, cosq, sinq)
        yc = _natten(zc, lw["c_bias"])
        yd = _mla(qt, k4, vt)
        xn, h2, aff = _outproj(ya, yb, yc, yd, x, lw)
        x = _moe(xn, h2, aff, lw, tri, final_norm_g.reshape(1, D_MODEL) if i == DEPTH - 1 else None)
    y_p, y_s = x
    return (y_p.reshape(BATCH, SEQ, D_MODEL), y_s.reshape(DEC_BATCH, SEQ, D_MODEL))
```

```python
import functools
import math

import numpy as np
import jax
import jax.numpy as jnp
from jax import lax
from jax.experimental import pallas as pl
from jax.experimental.pallas import tpu as pltpu

D_MODEL = 1024
BATCH = 4
SEQ = 8192
DEPTH = 4
DEC_BATCH = 2
GROUP_W = 256
HEAD_DIM = 64
A_HEADS = 4
CHUNK = 128
C_HEADS = 4
GRID_W = 64
WIN_R = 8
WIN_C = 16
D_HEADS = 4
Q_RANK = 256
KV_RANK = 128
NOPE_DIM = 64
ROPE_DIM = 32
V_DIM = 64
ROPE_THETA = 10000.0
N_EXPERTS = 16
D_EXPERT = 1024
CAPACITY_FACTOR = 2
EPS = 1e-6

NSEQ = BATCH + DEC_BATCH
T_ALL = NSEQ * SEQ
ROWS = SEQ // GRID_W
LANES = 128
HALO = 8
NEG = -1e30

A_COLS = 2 * GROUP_W
B_COLS = 3 * GROUP_W
C_COLS = 3 * GROUP_W
OFF_B = A_COLS
OFF_C = OFF_B + B_COLS
OFF_D = OFF_C + C_COLS
D_PACK = Q_RANK + KV_RANK + 2 * LANES
N_IN_PACK = OFF_D + D_PACK

TM = 1024
TQ = 256
TK = 512
L_ROWS = 16
MLA_UNROLL = 4
MLA_QTILES = 2
MLA_LAG = 1
MLA_QSCALE = (NOPE_DIM + ROPE_DIM) ** -0.5 * math.log2(math.e)
ROWS_PER_STEP = 32
TME = 1024
TT = 256
ROUTE_TILES = 8
ROW_ALIGN = 16
DISP_WINDOWS = (80, 144, TT + ROW_ALIGN)
COMB_ALIGN = ROW_ALIGN
COMB_WINDOWS = (80, 144, TT + COMB_ALIGN)
COMB_BUFS = 3
VMEM_LIMIT = 56 * 1024 * 1024

f32 = jnp.float32
bf16 = jnp.bfloat16


def _rms(x, g):
    return x * lax.rsqrt(jnp.mean(x * x, axis=-1, keepdims=True) + EPS) * g


def _dot(a, b):
    return jnp.dot(a, b, preferred_element_type=f32)


def _dot_nt(a, b):
    return lax.dot_general(a, b, (((1,), (1,)), ((), ())), preferred_element_type=f32)


def _dot_tn(a, b):
    return lax.dot_general(a, b, (((0,), (0,)), ((), ())), preferred_element_type=f32)


def _inproj_kernel(x_ref, xp_ref, xn_ref, g1_ref, win_ref, avg_ref, aws_ref, abias_ref,
                   bconv_ref, qg_ref, kvg_ref, wuq_ref, wuqr_ref, wk_ref, wv_ref,
                   cos_ref, sin_ref,
                   ya_ref, yb_ref, zc_ref, qt_ref, k4_ref, vt_ref):
    i = pl.program_id(0)
    j = i % (SEQ // TM)
    g1 = g1_ref[...]
    h = _rms(x_ref[...], g1).astype(bf16)

    za = _dot(h, win_ref[:, 0:A_COLS])
    zb = _dot(h, win_ref[:, OFF_B:OFF_B + B_COLS])
    zc = _dot(h, win_ref[:, OFF_C:OFF_C + C_COLS])
    zd = _dot(h, win_ref[:, OFF_D:OFF_D + D_PACK])
    xh = jnp.concatenate([xp_ref[...], xn_ref[...]], axis=0)
    hh = _rms(xh, g1).astype(bf16)
    zh = _dot(hh, win_ref[:, OFF_B + GROUP_W:OFF_B + B_COLS])

    zc_ref[:, 0:GROUP_W] = (zc[:, 0:GROUP_W] * (HEAD_DIM ** -0.5)).astype(bf16)
    zc_ref[:, GROUP_W:] = zc[:, GROUP_W:].astype(bf16)

    cqn = _rms(zd[:, 0:Q_RANK], qg_ref[...]).astype(bf16)
    ckvn = _rms(zd[:, Q_RANK:Q_RANK + KV_RANK], kvg_ref[...]).astype(bf16)
    cos = cos_ref[...]
    sin = sin_ref[...]
    o = Q_RANK + KV_RANK
    kr = zd[:, o:o + LANES] * cos + zd[:, o + LANES:o + 2 * LANES] * sin
    q_pre = _dot(cqn, wuq_ref[...])
    q_rot = _dot(cqn, wuqr_ref[...])
    kn = _dot(ckvn, wk_ref[...])
    vt = _dot(ckvn, wv_ref[...]).T.astype(bf16)

    za = jax.nn.gelu(za)
    u = za[:, :GROUP_W]
    v = _rms(za[:, GROUP_W:], avg_ref[...]).astype(bf16)
    lane = lax.broadcasted_iota(jnp.int32, (CHUNK, GROUP_W), 1)
    for c in range(TM // CHUNK):
        vc = v[c * CHUNK:(c + 1) * CHUNK, :]
        sv = abias_ref[...]
        for hd in range(A_HEADS):
            r = _dot(aws_ref[hd], vc)
            sv = sv + jnp.where((lane >= hd * HEAD_DIM) & (lane < (hd + 1) * HEAD_DIM), r, 0.0)
        ya_ref[c * CHUNK:(c + 1) * CHUNK, :] = u[c * CHUNK:(c + 1) * CHUNK, :] * sv

    bg = zb[:, :GROUP_W]
    y = zb[:, GROUP_W:2 * GROUP_W] * zb[:, 2 * GROUP_W:]
    yh = zh[:, :GROUP_W] * zh[:, GROUP_W:]
    y_before = jnp.where(j == 0, 0.0, yh[HALO - 1:HALO, :])
    y_after = jnp.where(j == SEQ // TM - 1, 0.0, yh[HALO:HALO + 1, :])
    row = lax.broadcasted_iota(jnp.int32, (TM, GROUP_W), 0)
    y_m1 = jnp.where(row == 0, y_before, pltpu.roll(y, 1, axis=0))
    y_p1 = jnp.where(row == TM - 1, y_after, pltpu.roll(y, TM - 1, axis=0))
    wc = bconv_ref[...]
    yb_ref[...] = bg * (wc[0:1, :] * y_m1 + wc[1:2, :] * y + wc[2:3, :] * y_p1)

    for hd in range(D_HEADS):
        sl = slice(hd * LANES, (hd + 1) * LANES)
        qt_ref[hd] = ((q_pre[:, sl] * cos + q_rot[:, sl] * sin) * MLA_QSCALE).T.astype(bf16)
        k4_ref[hd] = (kn[:, sl] + kr).astype(bf16)
        vt_ref[hd] = vt[hd * V_DIM:(hd + 1) * V_DIM, :]


def _inproj(x, lw, cosq, sinq):
    nt = T_ALL // TM
    tps = SEQ // TM
    full = lambda shape: pl.BlockSpec(shape, lambda i: (0,) * len(shape))
    in_specs = [
        pl.BlockSpec((TM, D_MODEL), lambda i: (i, 0)),
        pl.BlockSpec((HALO, D_MODEL), lambda i: (jnp.maximum(i * (TM // HALO) - 1, 0), 0)),
        pl.BlockSpec((HALO, D_MODEL), lambda i: (jnp.minimum((i + 1) * (TM // HALO), T_ALL // HALO - 1), 0)),
        full((1, D_MODEL)),
        full((D_MODEL, N_IN_PACK)),
        full((1, GROUP_W)),
        full((A_HEADS, CHUNK, CHUNK)),
        full((CHUNK, GROUP_W)),
        full((3, GROUP_W)),
        full((1, Q_RANK)),
        full((1, KV_RANK)),
        full((Q_RANK, D_HEADS * LANES)),
        full((Q_RANK, D_HEADS * LANES)),
        full((KV_RANK, D_HEADS * LANES)),
        full((KV_RANK, D_HEADS * V_DIM)),
        pl.BlockSpec((TM, LANES), lambda i: (i % tps, 0)),
        pl.BlockSpec((TM, LANES), lambda i: (i % tps, 0)),
    ]
    out_specs = [
        pl.BlockSpec((TM, GROUP_W), lambda i: (i, 0)),
        pl.BlockSpec((TM, GROUP_W), lambda i: (i, 0)),
        pl.BlockSpec((TM, C_COLS), lambda i: (i, 0)),
        pl.BlockSpec((None, D_HEADS, LANES, TM), lambda i: (i // tps, 0, 0, i % tps)),
        pl.BlockSpec((None, D_HEADS, TM, LANES), lambda i: (i // tps, 0, i % tps, 0)),
        pl.BlockSpec((None, D_HEADS, V_DIM, TM), lambda i: (i // tps, 0, 0, i % tps)),
    ]
    out_shape = [
        jax.ShapeDtypeStruct((T_ALL, GROUP_W), f32),
        jax.ShapeDtypeStruct((T_ALL, GROUP_W), f32),
        jax.ShapeDtypeStruct((T_ALL, C_COLS), bf16),
        jax.ShapeDtypeStruct((NSEQ, D_HEADS, LANES, SEQ), bf16),
        jax.ShapeDtypeStruct((NSEQ, D_HEADS, SEQ, LANES), bf16),
        jax.ShapeDtypeStruct((NSEQ, D_HEADS, V_DIM, SEQ), bf16),
    ]
    return pl.pallas_call(
        _inproj_kernel, grid=(nt,), in_specs=in_specs, out_specs=out_specs, out_shape=out_shape,
        name="inproj",
        compiler_params=pltpu.CompilerParams(dimension_semantics=("parallel",),
                                             vmem_limit_bytes=VMEM_LIMIT),
    )(x, x, x, lw["g1"], lw["w_in"], lw["a_vg"], lw["a_ws"], lw["a_bias"], lw["b_conv"],
      lw["d_qg"], lw["d_kvg"], lw["w_uq"], lw["w_uq_rot"], lw["w_k"], lw["w_v"], cosq, sinq)


def _natten_kernel(q_ref, k_ref, v_ref, bias_ref, o_ref):
    jb = pl.program_id(1)
    lane = lax.broadcasted_iota(jnp.int32, (GRID_W, GROUP_W), 1)
    head_of_lane = lane // HEAD_DIM
    nkeys = WIN_R * GRID_W

    jobs = []
    for i in range(ROWS_PER_STEP):
        r = jb * ROWS_PER_STEP + i
        rs = jnp.clip(r - WIN_R // 2, 0, ROWS - WIN_R)
        d = rs - r + (WIN_R - 1)
        kstart = pl.multiple_of(rs * GRID_W, GRID_W)
        q = q_ref[i * GRID_W:(i + 1) * GRID_W, :]
        zero = jnp.zeros_like(q)
        qq = jnp.concatenate([jnp.where(head_of_lane == h, q, zero) for h in range(C_HEADS)], axis=0)
        jobs.append((_dot_nt(qq, k_ref[pl.ds(kstart, nkeys), :]), d, kstart))
    for i, (sc, d, kstart) in enumerate(jobs):
        sc = sc + bias_ref[d]
        m = jnp.max(sc, axis=-1, keepdims=True)
        p = jnp.exp(sc - m)
        l = jnp.sum(p, axis=-1, keepdims=True)
        o = _dot(p.astype(bf16), v_ref[pl.ds(kstart, nkeys), :]) / l
        out = jnp.zeros((GRID_W, GROUP_W), f32)
        for h in range(C_HEADS):
            out = jnp.where(head_of_lane == h, o[h * GRID_W:(h + 1) * GRID_W], out)
        o_ref[i * GRID_W:(i + 1) * GRID_W, :] = out


def _natten(zc, bias):
    nb = ROWS // ROWS_PER_STEP
    tq = ROWS_PER_STEP * GRID_W
    return pl.pallas_call(
        _natten_kernel, grid=(NSEQ, nb),
        in_specs=[
            pl.BlockSpec((tq, GROUP_W), lambda b, j: (b * nb + j, 0)),
            pl.BlockSpec((SEQ, GROUP_W), lambda b, j: (b, 1)),
            pl.BlockSpec((SEQ, GROUP_W), lambda b, j: (b, 2)),
            pl.BlockSpec((WIN_R, C_HEADS * GRID_W, WIN_R * GRID_W), lambda b, j: (0, 0, 0)),
        ],
        out_specs=pl.BlockSpec((tq, GROUP_W), lambda b, j: (b * nb + j, 0)),
        out_shape=jax.ShapeDtypeStruct((T_ALL, GROUP_W), f32),
        name="natten",
        compiler_params=pltpu.CompilerParams(dimension_semantics=("parallel", "arbitrary"),
                                             vmem_limit_bytes=VMEM_LIMIT),
    )(zc, zc, zc, bias)


def _mla_kernel(qt_ref, k_ref, vt_ref, o_ref, s_scr):
    nchunk = SEQ // TK
    chains = [(t, h) for t in range(MLA_QTILES) for h in range(D_HEADS)]
    nc = len(chains)
    qts = [qt_ref[h, :, t * TQ:(t + 1) * TQ] for t, h in chains]

    def qk(slot, c, ci):
        ks = pl.multiple_of(ci * TK, TK)
        s_scr[slot, c] = _dot(k_ref[chains[c][1], pl.ds(ks, TK), :], qts[c])

    ones = jnp.ones((L_ROWS, TK), bf16)

    def softmax_pv(slot, c, ci, stat):
        m, acc = stat
        ks = pl.multiple_of(ci * TK, TK)
        vt = jnp.concatenate([vt_ref[chains[c][1], :, pl.ds(ks, TK)], ones], axis=0)
        s = s_scr[slot, c]
        m_new = jnp.maximum(m, jnp.max(s, axis=0, keepdims=True))
        alpha = jnp.exp2(m - m_new)
        p = jnp.exp2(s - m_new)
        acc = acc * alpha + _dot(vt, p.astype(bf16))
        return m_new, acc

    def half(cur, ci, stats, prefetch=True):
        out = []
        for c in range(nc):
            if prefetch:
                qk(1 - cur, c, ci + 1)
            if c >= MLA_LAG:
                out.append(softmax_pv(cur, c - MLA_LAG, ci, stats[c - MLA_LAG]))
        for c in range(nc - MLA_LAG, nc):
            out.append(softmax_pv(cur, c, ci, stats[c]))
        return tuple(out)

    def body(cp, stats):
        for u in range(MLA_UNROLL):
            stats = half(u % 2, MLA_UNROLL * cp + u, stats)
        return stats

    for c in range(nc):
        qk(0, c, 0)
    init = tuple((jnp.full((1, TQ), NEG, f32), jnp.zeros((V_DIM + L_ROWS, TQ), f32)) for _ in chains)
    res = lax.fori_loop(0, nchunk // MLA_UNROLL - 1, body, init)
    for u in range(MLA_UNROLL):
        res = half(u % 2, nchunk - MLA_UNROLL + u, res, prefetch=u < MLA_UNROLL - 1)
    for t in range(MLA_QTILES):
        o_ref[t * TQ:(t + 1) * TQ, :] = jnp.concatenate(
            [acc[:V_DIM] / acc[V_DIM:V_DIM + 1] for (_, acc) in res[t * D_HEADS:(t + 1) * D_HEADS]], axis=0).T


def _mla(qt, k4, vt):
    tq = TQ * MLA_QTILES
    nq = SEQ // tq
    return pl.pallas_call(
        _mla_kernel, grid=(NSEQ, nq),
        in_specs=[
            pl.BlockSpec((None, D_HEADS, LANES, tq), lambda b, i: (b, 0, 0, i)),
            pl.BlockSpec((None, D_HEADS, SEQ, LANES), lambda b, i: (b, 0, 0, 0)),
            pl.BlockSpec((None, D_HEADS, V_DIM, SEQ), lambda b, i: (b, 0, 0, 0)),
        ],
        out_specs=pl.BlockSpec((tq, GROUP_W), lambda b, i: (b * nq + i, 0)),
        out_shape=jax.ShapeDtypeStruct((T_ALL, GROUP_W), f32),
        scratch_shapes=[pltpu.VMEM((2, D_HEADS * MLA_QTILES, TK, TQ), f32)],
        name="mla",
        compiler_params=pltpu.CompilerParams(
            dimension_semantics=("parallel", "arbitrary"),
            vmem_limit_bytes=VMEM_LIMIT),
    )(qt, k4, vt)


def _outproj_kernel(ya_ref, yb_ref, yc_ref, yd_ref, x_ref, og_ref, wo_ref, g2_ref, rwt_ref,
                    xn_ref, h2_ref, aff_ref):
    y = jnp.concatenate(
        [_rms(y_ref[...], og_ref[:, g * GROUP_W:(g + 1) * GROUP_W]).astype(bf16)
         for g, y_ref in enumerate((ya_ref, yb_ref, yc_ref, yd_ref))], axis=1)
    acc = x_ref[...] + _dot(y, wo_ref[...])
    xn_ref[...] = acc
    h2 = _rms(acc, g2_ref[...]).astype(bf16)
    h2_ref[...] = h2
    logits = _dot_nt(rwt_ref[...], h2)
    e = jnp.exp(logits - jnp.max(logits, axis=0, keepdims=True))
    aff_ref[...] = e / jnp.sum(e, axis=0, keepdims=True)


def _outproj(ya, yb, yc, yd, x, lw):
    nt = T_ALL // TM
    full = lambda shape: pl.BlockSpec(shape, lambda i: (0,) * len(shape))
    yspec = pl.BlockSpec((TM, GROUP_W), lambda i: (i, 0))
    xspec = pl.BlockSpec((TM, D_MODEL), lambda i: (i, 0))
    return pl.pallas_call(
        _outproj_kernel, grid=(nt,),
        in_specs=[yspec, yspec, yspec, yspec, xspec, full((1, D_MODEL)), full((D_MODEL, D_MODEL)),
                  full((1, D_MODEL)), full((N_EXPERTS, D_MODEL))],
        out_specs=[xspec, xspec, pl.BlockSpec((N_EXPERTS, TM), lambda i: (0, i))],
        out_shape=[jax.ShapeDtypeStruct((T_ALL, D_MODEL), f32),
                   jax.ShapeDtypeStruct((T_ALL, D_MODEL), bf16),
                   jax.ShapeDtypeStruct((N_EXPERTS, T_ALL), f32)],
        name="outproj",
        compiler_params=pltpu.CompilerParams(dimension_semantics=("parallel",),
                                             vmem_limit_bytes=VMEM_LIMIT),
    )(ya, yb, yc, yd, x, lw["og"], lw["w_o"], lw["g2"], lw["rwt"])


def _route_kernel(aff_ref, tri_ref, posm_ref, base_ref, thr_scr, need_scr, runeq_scr, runsel_scr, *, cap):
    j = pl.program_id(0)

    @pl.when(j == 0)
    def _():
        def refine(i, thr):
            cand = thr | jnp.left_shift(jnp.int32(1), 30 - i)
            bits = pltpu.bitcast(aff_ref[...], jnp.int32)
            cnt = jnp.sum(jnp.where(bits >= cand, 1.0, 0.0), axis=1, keepdims=True)
            return jnp.where(cnt >= cap, cand, thr)

        thr = lax.fori_loop(0, 31, refine, jnp.zeros((N_EXPERTS, 1), jnp.int32))
        bits = pltpu.bitcast(aff_ref[...], jnp.int32)
        above = jnp.sum(jnp.where(bits > thr, 1.0, 0.0), axis=1, keepdims=True)
        thr_scr[...] = jnp.broadcast_to(thr, (N_EXPERTS, LANES))
        need_scr[...] = jnp.broadcast_to(cap - above, (N_EXPERTS, LANES))
        runeq_scr[...] = jnp.zeros((N_EXPERTS, LANES), f32)
        runsel_scr[...] = jnp.zeros((N_EXPERTS, LANES), f32)

    thr = thr_scr[:, 0:1]
    need = need_scr[:, 0:1]
    tiles = []
    for u in range(ROUTE_TILES):
        t0 = pl.multiple_of((j * ROUTE_TILES + u) * TT, TT)
        bits = pltpu.bitcast(aff_ref[:, pl.ds(t0, TT)], jnp.int32)
        eq = bits == thr
        eq_f = jnp.where(eq, 1.0, 0.0)
        tiles.append((bits, eq, eq_f, _dot(eq_f.astype(bf16), tri_ref[...])))
    sels = []
    runeq = runeq_scr[:, 0:1]
    for bits, eq, eq_f, eq_before in tiles:
        sel = (bits > thr) | (eq & (runeq + eq_before < need))
        sel_f = jnp.where(sel, 1.0, 0.0)
        sels.append((sel, sel_f, _dot(sel_f.astype(bf16), tri_ref[...])))
        runeq = runeq + jnp.sum(eq_f, axis=1, keepdims=True)
    runeq_scr[...] = jnp.broadcast_to(runeq, (N_EXPERTS, LANES))
    base = runsel_scr[...]
    for u, (sel, sel_f, sel_before) in enumerate(sels):
        posm_ref[:, u * TT:(u + 1) * TT] = jnp.where(sel, base[:, 0:1] + sel_before, -1.0)
        base_ref[u] = base
        base = base + jnp.sum(sel_f, axis=1, keepdims=True)
    runsel_scr[...] = base


def _route(aff, row0, nrows, tri):
    cap = CAPACITY_FACTOR * nrows // N_EXPERTS
    nt = nrows // TT
    blk0 = row0 // nrows
    vec = pltpu.VMEM((N_EXPERTS, LANES), f32)
    return pl.pallas_call(
        functools.partial(_route_kernel, cap=cap), grid=(nt // ROUTE_TILES,),
        in_specs=[pl.BlockSpec((N_EXPERTS, nrows), lambda j: (0, blk0)),
                  pl.BlockSpec((TT, TT), lambda j: (0, 0))],
        out_specs=[pl.BlockSpec((N_EXPERTS, ROUTE_TILES * TT), lambda j: (0, j)),
                   pl.BlockSpec((ROUTE_TILES, N_EXPERTS, LANES), lambda j: (j, 0, 0))],
        out_shape=[jax.ShapeDtypeStruct((N_EXPERTS, nrows), f32),
                   jax.ShapeDtypeStruct((nt, N_EXPERTS, LANES), f32)],
        scratch_shapes=[pltpu.VMEM((N_EXPERTS, LANES), jnp.int32), vec, vec, vec],
        name="route",
        compiler_params=pltpu.CompilerParams(dimension_semantics=("arbitrary",)),
    )(aff, tri)


def _window_class(count, windows, align):
    longest = jnp.max(count, axis=1) + (align - 1)
    return sum((longest > w).astype(jnp.int32) for w in windows[:-1])


def _dispatch_kernel(big_ref, first_ref, posm_ref, h2_ref, xe_hbm, stage, sem, carry, *, cap, ntiles):
    j = pl.program_id(0)
    slot = j % 2

    @pl.when(j == 0)
    def _():
        carry[...] = jnp.zeros_like(carry)
        pad = DISP_WINDOWS[-1]
        stage[1, 0:pad, :] = jnp.zeros((pad, D_MODEL), bf16)
        pads = [pltpu.make_async_copy(stage.at[1, pl.ds(0, pad)], xe_hbm.at[e, pl.ds(cap, pad)], sem.at[1])
                for e in range(N_EXPERTS)]
        for p in pads:
            p.start()
        for p in pads:
            p.wait()

    def run_start(jj, e):
        return first_ref[jj * N_EXPERTS + e]

    def window_copy(jj, e, sl, win):
        a = pl.multiple_of(run_start(jj, e) // ROW_ALIGN * ROW_ALIGN, ROW_ALIGN)
        return pltpu.make_async_copy(stage.at[sl, pl.ds(e * win, win)], xe_hbm.at[e, pl.ds(a, win)], sem.at[sl])

    def for_tile_window(jj, fn):
        for flag, win in enumerate(DISP_WINDOWS):
            @pl.when(big_ref[jj] == flag)
            def _():
                fn(win)

    def wait_all(jj, sl):
        def go(win):
            for e in range(N_EXPERTS):
                window_copy(jj, e, sl, win).wait()
        for_tile_window(jj, go)

    def compact(win):
        row = lax.broadcasted_iota(jnp.int32, (win, TT), 0)
        starts = [run_start(j, e) // ROW_ALIGN * ROW_ALIGN for e in range(N_EXPERTS)]
        onehot = jnp.concatenate(
            [jnp.where((row + starts[e]).astype(f32) == posm_ref[e:e + 1, :], 1.0, 0.0)
             for e in range(N_EXPERTS)], axis=0).astype(bf16)
        rows = _dot(onehot, h2_ref[...])
        head = lax.broadcasted_iota(jnp.int32, (ROW_ALIGN, D_MODEL), 0)
        for e in range(N_EXPERTS):
            lead = run_start(j, e) - starts[e]
            w = rows[e * win:(e + 1) * win]
            top = w[:ROW_ALIGN] + jnp.where(head < lead, carry[e], 0.0)
            stage[slot, e * win:e * win + ROW_ALIGN, :] = top.astype(bf16)
            stage[slot, e * win + ROW_ALIGN:(e + 1) * win, :] = w[ROW_ALIGN:].astype(bf16)
        for e in range(N_EXPERTS):
            end = jnp.where(j + 1 < ntiles, run_start(jnp.minimum(j + 1, ntiles - 1), e), cap)
            off = jnp.minimum(end // ROW_ALIGN * ROW_ALIGN - starts[e], win - ROW_ALIGN)
            off = pl.multiple_of(off, ROW_ALIGN)
            carry[e] = stage[slot, pl.ds(e * win + off, ROW_ALIGN), :].astype(f32)

    for_tile_window(j, compact)

    @pl.when(j > 0)
    def _():
        wait_all(j - 1, 1 - slot)

    def start_all(win):
        for e in range(N_EXPERTS):
            window_copy(j, e, slot, win).start()
    for_tile_window(j, start_all)

    @pl.when(j == ntiles - 1)
    def _():
        wait_all(j, slot)


def _dispatch(h2, posm, first, count, row0, nrows):
    cap = CAPACITY_FACTOR * nrows // N_EXPERTS
    nt = nrows // TT
    blk0 = row0 // TT
    big = _window_class(count, DISP_WINDOWS, ROW_ALIGN)
    return pl.pallas_call(
        functools.partial(_dispatch_kernel, cap=cap, ntiles=nt),
        grid_spec=pltpu.PrefetchScalarGridSpec(
            num_scalar_prefetch=2, grid=(nt,),
            in_specs=[pl.BlockSpec((N_EXPERTS, TT), lambda j, *_: (0, j)),
                      pl.BlockSpec((TT, D_MODEL), lambda j, *_: (blk0 + j, 0))],
            out_specs=pl.BlockSpec(memory_space=pl.ANY),
            scratch_shapes=[pltpu.VMEM((2, N_EXPERTS * DISP_WINDOWS[-1], D_MODEL), bf16),
                            pltpu.SemaphoreType.DMA((2,)),
                            pltpu.VMEM((N_EXPERTS, ROW_ALIGN, D_MODEL), f32)],
        ),
        out_shape=jax.ShapeDtypeStruct((N_EXPERTS, cap + DISP_WINDOWS[-1], D_MODEL), bf16),
        name="dispatch",
        compiler_params=pltpu.CompilerParams(dimension_semantics=("arbitrary",), vmem_limit_bytes=VMEM_LIMIT),
    )(big, first.reshape(-1), posm, h2)


def _ffn_kernel(xe_ref, rw_ref, wg_ref, wu_ref, wd_ref, ye_ref):
    e = pl.program_id(0)
    x = xe_ref[...]
    hg = _dot(x, wg_ref[...].astype(bf16))
    hu = _dot(x, wu_ref[...].astype(bf16))
    logits = _dot(x, rw_ref[...])
    h = (hg * (1.0 / (1.0 + jnp.exp(-hg))) * hu).astype(bf16)
    lane = lax.broadcasted_iota(jnp.int32, logits.shape, 1)
    logits = jnp.where(lane < N_EXPERTS, logits, NEG)
    ex = jnp.exp(logits - jnp.max(logits, axis=-1, keepdims=True))
    gate = jnp.sum(jnp.where(lane == e, ex, 0.0), axis=-1, keepdims=True) / jnp.sum(ex, axis=-1, keepdims=True)
    ye_ref[...] = (_dot(h, wd_ref[...].astype(bf16)) * gate).astype(bf16)


def _ffn(xe, lw, nrows):
    cap = CAPACITY_FACTOR * nrows // N_EXPERTS
    layer = lw["layer"]
    wspec = pl.BlockSpec((None, None, D_MODEL, D_EXPERT), lambda e, t: (layer, e, 0, 0))
    return pl.pallas_call(
        _ffn_kernel, grid=(N_EXPERTS, cap // TME),
        in_specs=[
            pl.BlockSpec((None, TME, D_MODEL), lambda e, t: (e, t, 0)),
            pl.BlockSpec((D_MODEL, LANES), lambda e, t: (0, 0)),
            wspec, wspec,
            pl.BlockSpec((None, None, D_EXPERT, D_MODEL), lambda e, t: (layer, e, 0, 0)),
        ],
        out_specs=pl.BlockSpec((None, TME, D_MODEL), lambda e, t: (e, t, 0)),
        out_shape=jax.ShapeDtypeStruct((N_EXPERTS, cap, D_MODEL), bf16),
        name="expert_ffn",
        compiler_params=pltpu.CompilerParams(dimension_semantics=("parallel", "arbitrary"),
                                             vmem_limit_bytes=VMEM_LIMIT),
    )(xe, lw["rw"], lw["e_wg"], lw["e_wu"], lw["e_wd"])


def _combine_kernel(big_ref, first_ref, posm_ref, x_ref, ye_hbm, g_ref, o_ref, buf, sem, *, cap, ntiles, final):
    j = pl.program_id(0)

    def window_start(jj, e, win):
        return jnp.minimum(first_ref[jj * N_EXPERTS + e] // COMB_ALIGN * COMB_ALIGN, cap - win)

    def window_copy(jj, e, slot, win):
        a = pl.multiple_of(window_start(jj, e, win), COMB_ALIGN)
        return pltpu.make_async_copy(ye_hbm.at[e, pl.ds(a, win)], buf.at[slot, pl.ds(e * win, win)],
                                     sem.at[slot])

    def for_tile_window(jj, fn):
        for flag, win in enumerate(COMB_WINDOWS):
            @pl.when(big_ref[jj] == flag)
            def _():
                fn(win)

    def start_all(jj, slot):
        def go(win):
            for e in range(N_EXPERTS):
                window_copy(jj, e, slot, win).start()
        for_tile_window(jj, go)

    @pl.when(j == 0)
    def _():
        for ahead in range(COMB_BUFS - 1):
            start_all(ahead, ahead)

    slot = j % COMB_BUFS

    def wait_all(win):
        for e in range(N_EXPERTS):
            window_copy(j, e, slot, win).wait()
    for_tile_window(j, wait_all)

    @pl.when(j + COMB_BUFS - 1 < ntiles)
    def _():
        start_all(j + COMB_BUFS - 1, (j + COMB_BUFS - 1) % COMB_BUFS)

    def add_rows(win):
        row = lax.broadcasted_iota(jnp.int32, (win, TT), 0)
        onehot = jnp.concatenate(
            [jnp.where((row + window_start(j, e, win)).astype(f32) == posm_ref[e:e + 1, :], 1.0, 0.0)
             for e in range(N_EXPERTS)], axis=0).astype(bf16)
        out = x_ref[...] + _dot_tn(onehot, buf[slot, 0:N_EXPERTS * win, :])
        o_ref[...] = _rms(out, g_ref[...]) if final else out
    for_tile_window(j, add_rows)


def _combine(x, ye, posm, first, count, row0, nrows, final_g=None):
    cap = CAPACITY_FACTOR * nrows // N_EXPERTS
    nt = nrows // TT
    blk0 = row0 // TT
    final = final_g is not None
    big = _window_class(count, COMB_WINDOWS, COMB_ALIGN)
    gain = final_g if final else jnp.ones((1, D_MODEL), f32)
    return pl.pallas_call(
        functools.partial(_combine_kernel, cap=cap, ntiles=nt, final=final),
        grid_spec=pltpu.PrefetchScalarGridSpec(
            num_scalar_prefetch=2, grid=(nt,),
            in_specs=[pl.BlockSpec((N_EXPERTS, TT), lambda j, *_: (0, j)),
                      pl.BlockSpec((TT, D_MODEL), lambda j, *_: (blk0 + j, 0)),
                      pl.BlockSpec(memory_space=pl.ANY),
                      pl.BlockSpec((1, D_MODEL), lambda j, *_: (0, 0))],
            out_specs=pl.BlockSpec((TT, D_MODEL), lambda j, *_: ((0 if final else blk0) + j, 0)),
            scratch_shapes=[pltpu.VMEM((COMB_BUFS, N_EXPERTS * COMB_WINDOWS[-1], D_MODEL), bf16),
                            pltpu.SemaphoreType.DMA((COMB_BUFS,))],
        ),
        out_shape=jax.ShapeDtypeStruct((nrows, D_MODEL) if final else x.shape, f32),
        input_output_aliases={} if final else {3: 0},
        name="combine",
        compiler_params=pltpu.CompilerParams(dimension_semantics=("arbitrary",), vmem_limit_bytes=VMEM_LIMIT),
    )(big, first.reshape(-1), posm, x, ye, gain)


def _stack_kernel(a_ref, b_ref, o_ref, *, na_blocks):
    o_ref[...] = jnp.where(pl.program_id(0) < na_blocks, a_ref[...], b_ref[...])


def _stack_rows(a, b):
    na, nb = a.shape[0] // TM, b.shape[0] // TM
    return pl.pallas_call(
        functools.partial(_stack_kernel, na_blocks=na), grid=(na + nb,),
        in_specs=[pl.BlockSpec((TM, D_MODEL), lambda i: (jnp.minimum(i, na - 1), 0)),
                  pl.BlockSpec((TM, D_MODEL), lambda i: (jnp.maximum(i - na, 0), 0))],
        out_specs=pl.BlockSpec((TM, D_MODEL), lambda i: (i, 0)),
        out_shape=jax.ShapeDtypeStruct((a.shape[0] + b.shape[0], D_MODEL), a.dtype),
        name="stack_rows",
        compiler_params=pltpu.CompilerParams(dimension_semantics=("arbitrary",)),
    )(a, b)


def _natten_bias_table(rpb):
    c = np.arange(GRID_W)
    cs = np.clip(c - WIN_C // 2, 0, GRID_W - WIN_C)
    kc = np.arange(GRID_W)
    col_mask = (kc[None, :] >= cs[:, None]) & (kc[None, :] < cs[:, None] + WIN_C)
    dc = np.clip(kc[None, :] - c[:, None], -(WIN_C - 1), WIN_C - 1) + (WIN_C - 1)
    tb = jnp.where(col_mask[None, None], rpb[:, :, dc].astype(f32), NEG)
    dw = np.arange(WIN_R)[:, None] + np.arange(WIN_R)[None, :]
    t = tb[:, dw]
    t = t.transpose(1, 0, 3, 2, 4).reshape(WIN_R, C_HEADS * GRID_W, WIN_R * GRID_W)
    return t


def _rope_slabs():
    inv = 1.0 / (ROPE_THETA ** (jnp.arange(0, ROPE_DIM, 2, dtype=f32) / ROPE_DIM))
    ang = jnp.arange(SEQ, dtype=f32)[:, None] * inv[None, :]
    cos, sin = jnp.cos(ang), jnp.sin(ang)
    ones = jnp.ones((SEQ, NOPE_DIM), f32)
    zpad = jnp.zeros((SEQ, LANES - NOPE_DIM - ROPE_DIM), f32)
    cosq = jnp.concatenate([ones, cos, cos, zpad], axis=1)
    sinq = jnp.concatenate([0.0 * ones, sin, sin, zpad], axis=1)
    return cosq, sinq


def _pack_layer(i, norm1_g, w_in, a_vnorm_g, a_ws, a_bs, b_conv, c_rpb, d_qnorm_g, d_kvnorm_g,
                d_w_uq, d_w_ukv, out_norm_g, w_o, norm2_g, router_w, e_w_gate, e_w_up, e_w_down):
    half = ROPE_DIM // 2
    w = w_in[i]
    kr0 = OFF_D + Q_RANK + KV_RANK
    kr = w[:, kr0:kr0 + ROPE_DIM]
    z64 = jnp.zeros((D_MODEL, NOPE_DIM), f32)
    z32 = jnp.zeros((D_MODEL, LANES - NOPE_DIM - ROPE_DIM), f32)
    kr_slab = jnp.concatenate([z64, kr, z32], axis=1)
    kr_rot = jnp.concatenate([z64, -kr[:, half:], kr[:, :half], z32], axis=1)
    w_pack = jnp.concatenate([w[:, :kr0], kr_slab, kr_rot], axis=1).astype(bf16)

    uq = d_w_uq[i].reshape(Q_RANK, D_HEADS, NOPE_DIM + ROPE_DIM)
    nope, rope = uq[..., :NOPE_DIM], uq[..., NOPE_DIM:]
    zq = jnp.zeros((Q_RANK, D_HEADS, LANES - NOPE_DIM - ROPE_DIM), f32)
    w_uq = jnp.concatenate([nope, rope, zq], axis=-1).reshape(Q_RANK, D_HEADS * LANES)
    w_uq_rot = jnp.concatenate([0.0 * nope, -rope[..., half:], rope[..., :half], zq],
                               axis=-1).reshape(Q_RANK, D_HEADS * LANES)
    ukv = d_w_ukv[i].reshape(KV_RANK, D_HEADS, NOPE_DIM + V_DIM)
    w_k = jnp.concatenate([ukv[..., :NOPE_DIM], jnp.zeros((KV_RANK, D_HEADS, LANES - NOPE_DIM), f32)],
                          axis=-1).reshape(KV_RANK, D_HEADS * LANES)
    w_v = ukv[..., NOPE_DIM:].reshape(KV_RANK, D_HEADS * V_DIM)
    rw = jnp.concatenate([router_w[i], jnp.zeros((D_MODEL, LANES - N_EXPERTS), f32)], axis=1)
    return {
        "g1": norm1_g[i].reshape(1, D_MODEL),
        "w_in": w_pack,
        "a_vg": a_vnorm_g[i].reshape(1, GROUP_W),
        "a_ws": a_ws[i].astype(bf16),
        "a_bias": jnp.repeat(a_bs[i].T, HEAD_DIM, axis=1),
        "b_conv": b_conv[i],
        "c_bias": _natten_bias_table(c_rpb[i]),
        "d_qg": d_qnorm_g[i].reshape(1, Q_RANK),
        "d_kvg": d_kvnorm_g[i].reshape(1, KV_RANK),
        "w_uq": w_uq.astype(bf16),
        "w_uq_rot": w_uq_rot.astype(bf16),
        "w_k": w_k.astype(bf16),
        "w_v": w_v.astype(bf16),
        "og": out_norm_g[i].reshape(1, D_MODEL),
        "w_o": w_o[i].astype(bf16),
        "g2": norm2_g[i].reshape(1, D_MODEL),
        "rw": rw.astype(bf16),
        "rwt": router_w[i].T.astype(bf16),
        "layer": i,
        "e_wg": e_w_gate,
        "e_wu": e_w_up,
        "e_wd": e_w_down,
    }


def _moe(xn, h2, aff, lw, tri, final_g=None):
    x = xn
    outs = []
    for row0, nrows in ((0, BATCH * SEQ), (BATCH * SEQ, DEC_BATCH * SEQ)):
        cap = CAPACITY_FACTOR * nrows // N_EXPERTS
        posm, base = _route(aff, row0, nrows, tri)
        first = base[:, :, 0].astype(jnp.int32)
        count = jnp.diff(first, axis=0, append=jnp.full((1, N_EXPERTS), cap, jnp.int32))
        xe = _dispatch(h2, posm, first, count, row0, nrows)
        ye = _ffn(xe, lw, nrows)
        if final_g is None:
            x = _combine(x, ye, posm, first, count, row0, nrows)
        else:
            outs.append(_combine(xn, ye, posm, first, count, row0, nrows, final_g))
    return x if final_g is None else outs


def kernel(x_prompt, x_sample, norm1_g, w_in, a_vnorm_g, a_ws, a_bs, b_conv, c_rpb, d_qnorm_g, d_kvnorm_g, d_w_uq, d_w_ukv, out_norm_g, w_o, norm2_g, router_w, e_w_gate, e_w_up, e_w_down, final_norm_g):
    n_p = BATCH * SEQ
    n_s = DEC_BATCH * SEQ
    x = _stack_rows(x_prompt.reshape(n_p, D_MODEL), x_sample.reshape(n_s, D_MODEL))
    cosq, sinq = _rope_slabs()
    tri = jnp.asarray(np.triu(np.ones((TT, TT), np.float32), 1), bf16)
    for i in range(DEPTH):
        lw = _pack_layer(i, norm1_g, w_in, a_vnorm_g, a_ws, a_bs, b_conv, c_rpb, d_qnorm_g, d_kvnorm_g,
                         d_w_uq, d_w_ukv, out_norm_g, w_o, norm2_g, router_w, e_w_gate, e_w_up, e_w_down)
        ya, yb, zc, qt, k4, vt = _inproj(x, lw, cosq, sinq)
        yc = _natten(zc, lw["c_bias"])
        yd = _mla(qt, k4, vt)
        xn, h2, aff = _outproj(ya, yb, yc, yd, x, lw)
        x = _moe(xn, h2, aff, lw, tri, final_norm_g.reshape(1, D_MODEL) if i == DEPTH - 1 else None)
    y_p, y_s = x
    return (y_p.reshape(BATCH, SEQ, D_MODEL), y_s.reshape(DEC_BATCH, SEQ, D_MODEL))
```

```python
import functools
import math

import numpy as np
import jax
import jax.numpy as jnp
from jax import lax
from jax.experimental import pallas as pl
from jax.experimental.pallas import tpu as pltpu

D_MODEL = 1024
BATCH = 4
SEQ = 8192
DEPTH = 4
DEC_BATCH = 2
GROUP_W = 256
HEAD_DIM = 64
A_HEADS = 4
CHUNK = 128
C_HEADS = 4
GRID_W = 64
WIN_R = 8
WIN_C = 16
D_HEADS = 4
Q_RANK = 256
KV_RANK = 128
NOPE_DIM = 64
ROPE_DIM = 32
V_DIM = 64
ROPE_THETA = 10000.0
N_EXPERTS = 16
D_EXPERT = 1024
CAPACITY_FACTOR = 2
EPS = 1e-6

NSEQ = BATCH + DEC_BATCH
T_ALL = NSEQ * SEQ
ROWS = SEQ // GRID_W
LANES = 128
HALO = 8
NEG = -1e30

A_COLS = 2 * GROUP_W
B_COLS = 3 * GROUP_W
C_COLS = 3 * GROUP_W
OFF_B = A_COLS
OFF_C = OFF_B + B_COLS
OFF_D = OFF_C + C_COLS
D_PACK = Q_RANK + KV_RANK + 2 * LANES
N_IN_PACK = OFF_D + D_PACK

TM = 1024
XBUFS = 3
TQ = 256
TK = 512
L_ROWS = 16
MLA_UNROLL = 4
MLA_QTILES = 2
MLA_LAG = 1
MLA_QSCALE = (NOPE_DIM + ROPE_DIM) ** -0.5 * math.log2(math.e)
ROWS_PER_STEP = 32
TME = 1024
TT = 256
ROUTE_TILES = 8
ROW_ALIGN = 16
DISP_WINDOWS = (80, 144, TT + ROW_ALIGN)
COMB_ALIGN = ROW_ALIGN
COMB_WINDOWS = (80, 144, TT + COMB_ALIGN)
COMB_BUFS = 3
VMEM_LIMIT = 56 * 1024 * 1024

f32 = jnp.float32
bf16 = jnp.bfloat16


def _rms(x, g):
    return x * lax.rsqrt(jnp.mean(x * x, axis=-1, keepdims=True) + EPS) * g


def _dot(a, b):
    return jnp.dot(a, b, preferred_element_type=f32)


def _dot_nt(a, b):
    return lax.dot_general(a, b, (((1,), (1,)), ((), ())), preferred_element_type=f32)


def _dot_tn(a, b):
    return lax.dot_general(a, b, (((0,), (0,)), ((), ())), preferred_element_type=f32)


def _inproj_kernel(x_ref, xp_ref, xn_ref, g1_ref, win_ref, avg_ref, aws_ref, abias_ref,
                   bconv_ref, qg_ref, kvg_ref, wuq_ref, wuqr_ref, wk_ref, wv_ref,
                   cos_ref, sin_ref,
                   ya_ref, yb_ref, zc_ref, qt_ref, k4_ref, vt_ref):
    i = pl.program_id(0)
    j = i % (SEQ // TM)
    g1 = g1_ref[...]
    h = _rms(x_ref[...], g1).astype(bf16)

    za = _dot(h, win_ref[:, 0:A_COLS])
    zb = _dot(h, win_ref[:, OFF_B:OFF_B + B_COLS])
    zc = _dot(h, win_ref[:, OFF_C:OFF_C + C_COLS])
    zd = _dot(h, win_ref[:, OFF_D:OFF_D + D_PACK])
    xh = jnp.concatenate([xp_ref[...], xn_ref[...]], axis=0)
    hh = _rms(xh, g1).astype(bf16)
    zh = _dot(hh, win_ref[:, OFF_B + GROUP_W:OFF_B + B_COLS])

    zc_ref[:, 0:GROUP_W] = (zc[:, 0:GROUP_W] * (HEAD_DIM ** -0.5)).astype(bf16)
    zc_ref[:, GROUP_W:] = zc[:, GROUP_W:].astype(bf16)

    cqn = _rms(zd[:, 0:Q_RANK], qg_ref[...]).astype(bf16)
    ckvn = _rms(zd[:, Q_RANK:Q_RANK + KV_RANK], kvg_ref[...]).astype(bf16)
    cos = cos_ref[...]
    sin = sin_ref[...]
    o = Q_RANK + KV_RANK
    kr = zd[:, o:o + LANES] * cos + zd[:, o + LANES:o + 2 * LANES] * sin
    q_pre = _dot(cqn, wuq_ref[...])
    q_rot = _dot(cqn, wuqr_ref[...])
    kn = _dot(ckvn, wk_ref[...])
    vt = _dot(ckvn, wv_ref[...]).T.astype(bf16)

    za = jax.nn.gelu(za)
    u = za[:, :GROUP_W]
    v = _rms(za[:, GROUP_W:], avg_ref[...]).astype(bf16)
    lane = lax.broadcasted_iota(jnp.int32, (CHUNK, GROUP_W), 1)
    for c in range(TM // CHUNK):
        vc = v[c * CHUNK:(c + 1) * CHUNK, :]
        sv = abias_ref[...]
        for hd in range(A_HEADS):
            r = _dot(aws_ref[hd], vc)
            sv = sv + jnp.where((lane >= hd * HEAD_DIM) & (lane < (hd + 1) * HEAD_DIM), r, 0.0)
        ya_ref[c * CHUNK:(c + 1) * CHUNK, :] = u[c * CHUNK:(c + 1) * CHUNK, :] * sv

    bg = zb[:, :GROUP_W]
    y = zb[:, GROUP_W:2 * GROUP_W] * zb[:, 2 * GROUP_W:]
    yh = zh[:, :GROUP_W] * zh[:, GROUP_W:]
    y_before = jnp.where(j == 0, 0.0, yh[HALO - 1:HALO, :])
    y_after = jnp.where(j == SEQ // TM - 1, 0.0, yh[HALO:HALO + 1, :])
    row = lax.broadcasted_iota(jnp.int32, (TM, GROUP_W), 0)
    y_m1 = jnp.where(row == 0, y_before, pltpu.roll(y, 1, axis=0))
    y_p1 = jnp.where(row == TM - 1, y_after, pltpu.roll(y, TM - 1, axis=0))
    wc = bconv_ref[...]
    yb_ref[...] = bg * (wc[0:1, :] * y_m1 + wc[1:2, :] * y + wc[2:3, :] * y_p1)

    for hd in range(D_HEADS):
        sl = slice(hd * LANES, (hd + 1) * LANES)
        qt_ref[hd] = ((q_pre[:, sl] * cos + q_rot[:, sl] * sin) * MLA_QSCALE).T.astype(bf16)
        k4_ref[hd] = (kn[:, sl] + kr).astype(bf16)
        vt_ref[hd] = vt[hd * V_DIM:(hd + 1) * V_DIM, :]


def _inproj(x, lw, cosq, sinq):
    nt = T_ALL // TM
    tps = SEQ // TM
    full = lambda shape: pl.BlockSpec(shape, lambda i: (0,) * len(shape))
    in_specs = [
        pl.BlockSpec((TM, D_MODEL), lambda i: (i, 0)),
        pl.BlockSpec((HALO, D_MODEL), lambda i: (jnp.maximum(i * (TM // HALO) - 1, 0), 0)),
        pl.BlockSpec((HALO, D_MODEL), lambda i: (jnp.minimum((i + 1) * (TM // HALO), T_ALL // HALO - 1), 0)),
        full((1, D_MODEL)),
        full((D_MODEL, N_IN_PACK)),
        full((1, GROUP_W)),
        full((A_HEADS, CHUNK, CHUNK)),
        full((CHUNK, GROUP_W)),
        full((3, GROUP_W)),
        full((1, Q_RANK)),
        full((1, KV_RANK)),
        full((Q_RANK, D_HEADS * LANES)),
        full((Q_RANK, D_HEADS * LANES)),
        full((KV_RANK, D_HEADS * LANES)),
        full((KV_RANK, D_HEADS * V_DIM)),
        pl.BlockSpec((TM, LANES), lambda i: (i % tps, 0)),
        pl.BlockSpec((TM, LANES), lambda i: (i % tps, 0)),
    ]
    out_specs = [
        pl.BlockSpec((TM, GROUP_W), lambda i: (i, 0)),
        pl.BlockSpec((TM, GROUP_W), lambda i: (i, 0)),
        pl.BlockSpec((TM, C_COLS), lambda i: (i, 0)),
        pl.BlockSpec((None, D_HEADS, LANES, TM), lambda i: (i // tps, 0, 0, i % tps)),
        pl.BlockSpec((None, D_HEADS, TM, LANES), lambda i: (i // tps, 0, i % tps, 0)),
        pl.BlockSpec((None, D_HEADS, V_DIM, TM), lambda i: (i // tps, 0, 0, i % tps)),
    ]
    out_shape = [
        jax.ShapeDtypeStruct((T_ALL, GROUP_W), f32),
        jax.ShapeDtypeStruct((T_ALL, GROUP_W), f32),
        jax.ShapeDtypeStruct((T_ALL, C_COLS), bf16),
        jax.ShapeDtypeStruct((NSEQ, D_HEADS, LANES, SEQ), bf16),
        jax.ShapeDtypeStruct((NSEQ, D_HEADS, SEQ, LANES), bf16),
        jax.ShapeDtypeStruct((NSEQ, D_HEADS, V_DIM, SEQ), bf16),
    ]
    return pl.pallas_call(
        _inproj_kernel, grid=(nt,), in_specs=in_specs, out_specs=out_specs, out_shape=out_shape,
        name="inproj",
        compiler_params=pltpu.CompilerParams(dimension_semantics=("parallel",),
                                             vmem_limit_bytes=VMEM_LIMIT),
    )(x, x, x, lw["g1"], lw["w_in"], lw["a_vg"], lw["a_ws"], lw["a_bias"], lw["b_conv"],
      lw["d_qg"], lw["d_kvg"], lw["w_uq"], lw["w_uq_rot"], lw["w_k"], lw["w_v"], cosq, sinq)


def _natten_kernel(q_ref, k_ref, v_ref, bias_ref, o_ref):
    jb = pl.program_id(1)
    lane = lax.broadcasted_iota(jnp.int32, (GRID_W, GROUP_W), 1)
    head_of_lane = lane // HEAD_DIM
    nkeys = WIN_R * GRID_W

    jobs = []
    for i in range(ROWS_PER_STEP):
        r = jb * ROWS_PER_STEP + i
        rs = jnp.clip(r - WIN_R // 2, 0, ROWS - WIN_R)
        d = rs - r + (WIN_R - 1)
        kstart = pl.multiple_of(rs * GRID_W, GRID_W)
        q = q_ref[i * GRID_W:(i + 1) * GRID_W, :]
        zero = jnp.zeros_like(q)
        qq = jnp.concatenate([jnp.where(head_of_lane == h, q, zero) for h in range(C_HEADS)], axis=0)
        jobs.append((_dot_nt(qq, k_ref[pl.ds(kstart, nkeys), :]), d, kstart))
    for i, (sc, d, kstart) in enumerate(jobs):
        sc = sc + bias_ref[d]
        m = jnp.max(sc, axis=-1, keepdims=True)
        p = jnp.exp(sc - m)
        l = jnp.sum(p, axis=-1, keepdims=True)
        o = _dot(p.astype(bf16), v_ref[pl.ds(kstart, nkeys), :]) / l
        out = jnp.zeros((GRID_W, GROUP_W), f32)
        for h in range(C_HEADS):
            out = jnp.where(head_of_lane == h, o[h * GRID_W:(h + 1) * GRID_W], out)
        o_ref[i * GRID_W:(i + 1) * GRID_W, :] = out


def _natten(zc, bias):
    nb = ROWS // ROWS_PER_STEP
    tq = ROWS_PER_STEP * GRID_W
    return pl.pallas_call(
        _natten_kernel, grid=(NSEQ, nb),
        in_specs=[
            pl.BlockSpec((tq, GROUP_W), lambda b, j: (b * nb + j, 0)),
            pl.BlockSpec((SEQ, GROUP_W), lambda b, j: (b, 1)),
            pl.BlockSpec((SEQ, GROUP_W), lambda b, j: (b, 2)),
            pl.BlockSpec((WIN_R, C_HEADS * GRID_W, WIN_R * GRID_W), lambda b, j: (0, 0, 0)),
        ],
        out_specs=pl.BlockSpec((tq, GROUP_W), lambda b, j: (b * nb + j, 0)),
        out_shape=jax.ShapeDtypeStruct((T_ALL, GROUP_W), f32),
        name="natten",
        compiler_params=pltpu.CompilerParams(dimension_semantics=("parallel", "arbitrary"),
                                             vmem_limit_bytes=VMEM_LIMIT),
    )(zc, zc, zc, bias)


def _mla_kernel(qt_ref, k_ref, vt_ref, o_ref, s_scr):
    nchunk = SEQ // TK
    chains = [(t, h) for t in range(MLA_QTILES) for h in range(D_HEADS)]
    nc = len(chains)
    qts = [qt_ref[h, :, t * TQ:(t + 1) * TQ] for t, h in chains]

    def qk(slot, c, ci):
        ks = pl.multiple_of(ci * TK, TK)
        s_scr[slot, c] = _dot(k_ref[chains[c][1], pl.ds(ks, TK), :], qts[c])

    ones = jnp.ones((L_ROWS, TK), bf16)

    def softmax_pv(slot, c, ci, stat):
        m, acc = stat
        ks = pl.multiple_of(ci * TK, TK)
        vt = jnp.concatenate([vt_ref[chains[c][1], :, pl.ds(ks, TK)], ones], axis=0)
        s = s_scr[slot, c]
        m_new = jnp.maximum(m, jnp.max(s, axis=0, keepdims=True))
        alpha = jnp.exp2(m - m_new)
        p = jnp.exp2(s - m_new)
        acc = acc * alpha + _dot(vt, p.astype(bf16))
        return m_new, acc

    def half(cur, ci, stats, prefetch=True):
        out = []
        for c in range(nc):
            if prefetch:
                qk(1 - cur, c, ci + 1)
            if c >= MLA_LAG:
                out.append(softmax_pv(cur, c - MLA_LAG, ci, stats[c - MLA_LAG]))
        for c in range(nc - MLA_LAG, nc):
            out.append(softmax_pv(cur, c, ci, stats[c]))
        return tuple(out)

    def body(cp, stats):
        for u in range(MLA_UNROLL):
            stats = half(u % 2, MLA_UNROLL * cp + u, stats)
        return stats

    for c in range(nc):
        qk(0, c, 0)
    init = tuple((jnp.full((1, TQ), NEG, f32), jnp.zeros((V_DIM + L_ROWS, TQ), f32)) for _ in chains)
    res = lax.fori_loop(0, nchunk // MLA_UNROLL - 1, body, init)
    for u in range(MLA_UNROLL):
        res = half(u % 2, nchunk - MLA_UNROLL + u, res, prefetch=u < MLA_UNROLL - 1)
    for t in range(MLA_QTILES):
        o_ref[t * TQ:(t + 1) * TQ, :] = jnp.concatenate(
            [acc[:V_DIM] / acc[V_DIM:V_DIM + 1] for (_, acc) in res[t * D_HEADS:(t + 1) * D_HEADS]], axis=0).T


def _mla(qt, k4, vt):
    tq = TQ * MLA_QTILES
    nq = SEQ // tq
    return pl.pallas_call(
        _mla_kernel, grid=(NSEQ, nq),
        in_specs=[
            pl.BlockSpec((None, D_HEADS, LANES, tq), lambda b, i: (b, 0, 0, i)),
            pl.BlockSpec((None, D_HEADS, SEQ, LANES), lambda b, i: (b, 0, 0, 0)),
            pl.BlockSpec((None, D_HEADS, V_DIM, SEQ), lambda b, i: (b, 0, 0, 0)),
        ],
        out_specs=pl.BlockSpec((tq, GROUP_W), lambda b, i: (b * nq + i, 0)),
        out_shape=jax.ShapeDtypeStruct((T_ALL, GROUP_W), f32),
        scratch_shapes=[pltpu.VMEM((2, D_HEADS * MLA_QTILES, TK, TQ), f32)],
        name="mla",
        compiler_params=pltpu.CompilerParams(
            dimension_semantics=("parallel", "arbitrary"),
            vmem_limit_bytes=VMEM_LIMIT),
    )(qt, k4, vt)


def _outproj_kernel(ya_ref, yb_ref, yc_ref, yd_ref, x_hbm, og_ref, wo_ref, g2_ref, rwt_ref,
                    xn_ref, h2_ref, aff_ref, xbuf, sem, *, ntiles):
    i = pl.program_id(0)

    def x_copy(step, slot):
        return pltpu.make_async_copy(x_hbm.at[pl.ds(pl.multiple_of(step * TM, TM), TM)], xbuf.at[slot],
                                     sem.at[slot])

    @pl.when(i == 0)
    def _():
        for ahead in range(XBUFS - 1):
            x_copy(ahead, ahead).start()

    slot = i % XBUFS
    x_copy(i, slot).wait()

    @pl.when(i + XBUFS - 1 < ntiles)
    def _():
        x_copy(i + XBUFS - 1, (i + XBUFS - 1) % XBUFS).start()

    y = jnp.concatenate(
        [_rms(y_ref[...], og_ref[:, g * GROUP_W:(g + 1) * GROUP_W]).astype(bf16)
         for g, y_ref in enumerate((ya_ref, yb_ref, yc_ref, yd_ref))], axis=1)
    acc = xbuf[slot] + _dot(y, wo_ref[...])
    xn_ref[...] = acc
    h2 = _rms(acc, g2_ref[...]).astype(bf16)
    h2_ref[...] = h2
    logits = _dot_nt(rwt_ref[...], h2)
    e = jnp.exp(logits - jnp.max(logits, axis=0, keepdims=True))
    aff_ref[...] = e / jnp.sum(e, axis=0, keepdims=True)


def _outproj(ya, yb, yc, yd, x, lw):
    nt = T_ALL // TM
    full = lambda shape: pl.BlockSpec(shape, lambda i: (0,) * len(shape))
    yspec = pl.BlockSpec((TM, GROUP_W), lambda i: (i, 0))
    xspec = pl.BlockSpec((TM, D_MODEL), lambda i: (i, 0))
    return pl.pallas_call(
        functools.partial(_outproj_kernel, ntiles=nt), grid=(nt,),
        in_specs=[yspec, yspec, yspec, yspec, pl.BlockSpec(memory_space=pl.ANY), full((1, D_MODEL)),
                  full((D_MODEL, D_MODEL)), full((1, D_MODEL)), full((N_EXPERTS, D_MODEL))],
        out_specs=[xspec, xspec, pl.BlockSpec((N_EXPERTS, TM), lambda i: (0, i))],
        out_shape=[jax.ShapeDtypeStruct((T_ALL, D_MODEL), f32),
                   jax.ShapeDtypeStruct((T_ALL, D_MODEL), bf16),
                   jax.ShapeDtypeStruct((N_EXPERTS, T_ALL), f32)],
        scratch_shapes=[pltpu.VMEM((XBUFS, TM, D_MODEL), f32), pltpu.SemaphoreType.DMA((XBUFS,))],
        name="outproj",
        compiler_params=pltpu.CompilerParams(dimension_semantics=("arbitrary",),
                                             vmem_limit_bytes=VMEM_LIMIT),
    )(ya, yb, yc, yd, x, lw["og"], lw["w_o"], lw["g2"], lw["rwt"])


def _route_kernel(aff_ref, tri_ref, posm_ref, base_ref, thr_scr, need_scr, runeq_scr, runsel_scr, *, cap):
    j = pl.program_id(0)

    @pl.when(j == 0)
    def _():
        def refine(i, thr):
            cand = thr | jnp.left_shift(jnp.int32(1), 30 - i)
            bits = pltpu.bitcast(aff_ref[...], jnp.int32)
            cnt = jnp.sum(jnp.where(bits >= cand, 1.0, 0.0), axis=1, keepdims=True)
            return jnp.where(cnt >= cap, cand, thr)

        thr = lax.fori_loop(0, 31, refine, jnp.zeros((N_EXPERTS, 1), jnp.int32))
        bits = pltpu.bitcast(aff_ref[...], jnp.int32)
        above = jnp.sum(jnp.where(bits > thr, 1.0, 0.0), axis=1, keepdims=True)
        thr_scr[...] = jnp.broadcast_to(thr, (N_EXPERTS, LANES))
        need_scr[...] = jnp.broadcast_to(cap - above, (N_EXPERTS, LANES))
        runeq_scr[...] = jnp.zeros((N_EXPERTS, LANES), f32)
        runsel_scr[...] = jnp.zeros((N_EXPERTS, LANES), f32)

    thr = thr_scr[:, 0:1]
    need = need_scr[:, 0:1]
    tiles = []
    for u in range(ROUTE_TILES):
        t0 = pl.multiple_of((j * ROUTE_TILES + u) * TT, TT)
        bits = pltpu.bitcast(aff_ref[:, pl.ds(t0, TT)], jnp.int32)
        eq = bits == thr
        eq_f = jnp.where(eq, 1.0, 0.0)
        tiles.append((bits, eq, eq_f, _dot(eq_f.astype(bf16), tri_ref[...])))
    sels = []
    runeq = runeq_scr[:, 0:1]
    for bits, eq, eq_f, eq_before in tiles:
        sel = (bits > thr) | (eq & (runeq + eq_before < need))
        sel_f = jnp.where(sel, 1.0, 0.0)
        sels.append((sel, sel_f, _dot(sel_f.astype(bf16), tri_ref[...])))
        runeq = runeq + jnp.sum(eq_f, axis=1, keepdims=True)
    runeq_scr[...] = jnp.broadcast_to(runeq, (N_EXPERTS, LANES))
    base = runsel_scr[...]
    for u, (sel, sel_f, sel_before) in enumerate(sels):
        posm_ref[:, u * TT:(u + 1) * TT] = jnp.where(sel, base[:, 0:1] + sel_before, -1.0)
        base_ref[u] = base
        base = base + jnp.sum(sel_f, axis=1, keepdims=True)
    runsel_scr[...] = base


def _route(aff, row0, nrows, tri):
    cap = CAPACITY_FACTOR * nrows // N_EXPERTS
    nt = nrows // TT
    blk0 = row0 // nrows
    vec = pltpu.VMEM((N_EXPERTS, LANES), f32)
    return pl.pallas_call(
        functools.partial(_route_kernel, cap=cap), grid=(nt // ROUTE_TILES,),
        in_specs=[pl.BlockSpec((N_EXPERTS, nrows), lambda j: (0, blk0)),
                  pl.BlockSpec((TT, TT), lambda j: (0, 0))],
        out_specs=[pl.BlockSpec((N_EXPERTS, ROUTE_TILES * TT), lambda j: (0, j)),
                   pl.BlockSpec((ROUTE_TILES, N_EXPERTS, LANES), lambda j: (j, 0, 0))],
        out_shape=[jax.ShapeDtypeStruct((N_EXPERTS, nrows), f32),
                   jax.ShapeDtypeStruct((nt, N_EXPERTS, LANES), f32)],
        scratch_shapes=[pltpu.VMEM((N_EXPERTS, LANES), jnp.int32), vec, vec, vec],
        name="route",
        compiler_params=pltpu.CompilerParams(dimension_semantics=("arbitrary",)),
    )(aff, tri)


def _window_class(count, windows, align):
    longest = jnp.max(count, axis=1) + (align - 1)
    return sum((longest > w).astype(jnp.int32) for w in windows[:-1])


def _dispatch_kernel(big_ref, first_ref, posm_ref, h2_ref, xe_hbm, stage, sem, carry, *, cap, ntiles):
    j = pl.program_id(0)
    slot = j % 2

    @pl.when(j == 0)
    def _():
        carry[...] = jnp.zeros_like(carry)
        pad = DISP_WINDOWS[-1]
        stage[1, 0:pad, :] = jnp.zeros((pad, D_MODEL), bf16)
        pads = [pltpu.make_async_copy(stage.at[1, pl.ds(0, pad)], xe_hbm.at[e, pl.ds(cap, pad)], sem.at[1])
                for e in range(N_EXPERTS)]
        for p in pads:
            p.start()
        for p in pads:
            p.wait()

    def run_start(jj, e):
        return first_ref[jj * N_EXPERTS + e]

    def window_copy(jj, e, sl, win):
        a = pl.multiple_of(run_start(jj, e) // ROW_ALIGN * ROW_ALIGN, ROW_ALIGN)
        return pltpu.make_async_copy(stage.at[sl, pl.ds(e * win, win)], xe_hbm.at[e, pl.ds(a, win)], sem.at[sl])

    def for_tile_window(jj, fn):
        for flag, win in enumerate(DISP_WINDOWS):
            @pl.when(big_ref[jj] == flag)
            def _():
                fn(win)

    def wait_all(jj, sl):
        def go(win):
            for e in range(N_EXPERTS):
                window_copy(jj, e, sl, win).wait()
        for_tile_window(jj, go)

    def compact(win):
        row = lax.broadcasted_iota(jnp.int32, (win, TT), 0)
        starts = [run_start(j, e) // ROW_ALIGN * ROW_ALIGN for e in range(N_EXPERTS)]
        onehot = jnp.concatenate(
            [jnp.where((row + starts[e]).astype(f32) == posm_ref[e:e + 1, :], 1.0, 0.0)
             for e in range(N_EXPERTS)], axis=0).astype(bf16)
        rows = _dot(onehot, h2_ref[...])
        head = lax.broadcasted_iota(jnp.int32, (ROW_ALIGN, D_MODEL), 0)
        for e in range(N_EXPERTS):
            lead = run_start(j, e) - starts[e]
            w = rows[e * win:(e + 1) * win]
            top = w[:ROW_ALIGN] + jnp.where(head < lead, carry[e], 0.0)
            stage[slot, e * win:e * win + ROW_ALIGN, :] = top.astype(bf16)
            stage[slot, e * win + ROW_ALIGN:(e + 1) * win, :] = w[ROW_ALIGN:].astype(bf16)
        for e in range(N_EXPERTS):
            end = jnp.where(j + 1 < ntiles, run_start(jnp.minimum(j + 1, ntiles - 1), e), cap)
            off = jnp.minimum(end // ROW_ALIGN * ROW_ALIGN - starts[e], win - ROW_ALIGN)
            off = pl.multiple_of(off, ROW_ALIGN)
            carry[e] = stage[slot, pl.ds(e * win + off, ROW_ALIGN), :].astype(f32)

    for_tile_window(j, compact)

    @pl.when(j > 0)
    def _():
        wait_all(j - 1, 1 - slot)

    def start_all(win):
        for e in range(N_EXPERTS):
            window_copy(j, e, slot, win).start()
    for_tile_window(j, start_all)

    @pl.when(j == ntiles - 1)
    def _():
        wait_all(j, slot)


def _dispatch(h2, posm, first, count, row0, nrows):
    cap = CAPACITY_FACTOR * nrows // N_EXPERTS
    nt = nrows // TT
    blk0 = row0 // TT
    big = _window_class(count, DISP_WINDOWS, ROW_ALIGN)
    return pl.pallas_call(
        functools.partial(_dispatch_kernel, cap=cap, ntiles=nt),
        grid_spec=pltpu.PrefetchScalarGridSpec(
            num_scalar_prefetch=2, grid=(nt,),
            in_specs=[pl.BlockSpec((N_EXPERTS, TT), lambda j, *_: (0, j)),
                      pl.BlockSpec((TT, D_MODEL), lambda j, *_: (blk0 + j, 0))],
            out_specs=pl.BlockSpec(memory_space=pl.ANY),
            scratch_shapes=[pltpu.VMEM((2, N_EXPERTS * DISP_WINDOWS[-1], D_MODEL), bf16),
                            pltpu.SemaphoreType.DMA((2,)),
                            pltpu.VMEM((N_EXPERTS, ROW_ALIGN, D_MODEL), f32)],
        ),
        out_shape=jax.ShapeDtypeStruct((N_EXPERTS, cap + DISP_WINDOWS[-1], D_MODEL), bf16),
        name="dispatch",
        compiler_params=pltpu.CompilerParams(dimension_semantics=("arbitrary",), vmem_limit_bytes=VMEM_LIMIT),
    )(big, first.reshape(-1), posm, h2)


def _ffn_kernel(xe_ref, rw_ref, wg_ref, wu_ref, wd_ref, ye_ref):
    e = pl.program_id(0)
    x = xe_ref[...]
    hg = _dot(x, wg_ref[...].astype(bf16))
    hu = _dot(x, wu_ref[...].astype(bf16))
    logits = _dot(x, rw_ref[...])
    h = (hg * (1.0 / (1.0 + jnp.exp(-hg))) * hu).astype(bf16)
    lane = lax.broadcasted_iota(jnp.int32, logits.shape, 1)
    logits = jnp.where(lane < N_EXPERTS, logits, NEG)
    ex = jnp.exp(logits - jnp.max(logits, axis=-1, keepdims=True))
    gate = jnp.sum(jnp.where(lane == e, ex, 0.0), axis=-1, keepdims=True) / jnp.sum(ex, axis=-1, keepdims=True)
    ye_ref[...] = (_dot(h, wd_ref[...].astype(bf16)) * gate).astype(bf16)


def _ffn(xe, lw, nrows):
    cap = CAPACITY_FACTOR * nrows // N_EXPERTS
    layer = lw["layer"]
    wspec = pl.BlockSpec((None, None, D_MODEL, D_EXPERT), lambda e, t: (layer, e, 0, 0))
    return pl.pallas_call(
        _ffn_kernel, grid=(N_EXPERTS, cap // TME),
        in_specs=[
            pl.BlockSpec((None, TME, D_MODEL), lambda e, t: (e, t, 0)),
            pl.BlockSpec((D_MODEL, LANES), lambda e, t: (0, 0)),
            wspec, wspec,
            pl.BlockSpec((None, None, D_EXPERT, D_MODEL), lambda e, t: (layer, e, 0, 0)),
        ],
        out_specs=pl.BlockSpec((None, TME, D_MODEL), lambda e, t: (e, t, 0)),
        out_shape=jax.ShapeDtypeStruct((N_EXPERTS, cap, D_MODEL), bf16),
        name="expert_ffn",
        compiler_params=pltpu.CompilerParams(dimension_semantics=("parallel", "arbitrary"),
                                             vmem_limit_bytes=VMEM_LIMIT),
    )(xe, lw["rw"], lw["e_wg"], lw["e_wu"], lw["e_wd"])


def _combine_kernel(big_ref, first_ref, posm_ref, x_ref, ye_hbm, g_ref, o_ref, buf, sem, *, cap, ntiles, final):
    j = pl.program_id(0)

    def window_start(jj, e, win):
        return jnp.minimum(first_ref[jj * N_EXPERTS + e] // COMB_ALIGN * COMB_ALIGN, cap - win)

    def window_copy(jj, e, slot, win):
        a = pl.multiple_of(window_start(jj, e, win), COMB_ALIGN)
        return pltpu.make_async_copy(ye_hbm.at[e, pl.ds(a, win)], buf.at[slot, pl.ds(e * win, win)],
                                     sem.at[slot])

    def for_tile_window(jj, fn):
        for flag, win in enumerate(COMB_WINDOWS):
            @pl.when(big_ref[jj] == flag)
            def _():
                fn(win)

    def start_all(jj, slot):
        def go(win):
            for e in range(N_EXPERTS):
                window_copy(jj, e, slot, win).start()
        for_tile_window(jj, go)

    @pl.when(j == 0)
    def _():
        for ahead in range(COMB_BUFS - 1):
            start_all(ahead, ahead)

    slot = j % COMB_BUFS

    def wait_all(win):
        for e in range(N_EXPERTS):
            window_copy(j, e, slot, win).wait()
    for_tile_window(j, wait_all)

    @pl.when(j + COMB_BUFS - 1 < ntiles)
    def _():
        start_all(j + COMB_BUFS - 1, (j + COMB_BUFS - 1) % COMB_BUFS)

    def add_rows(win):
        row = lax.broadcasted_iota(jnp.int32, (win, TT), 0)
        onehot = jnp.concatenate(
            [jnp.where((row + window_start(j, e, win)).astype(f32) == posm_ref[e:e + 1, :], 1.0, 0.0)
             for e in range(N_EXPERTS)], axis=0).astype(bf16)
        out = x_ref[...] + _dot_tn(onehot, buf[slot, 0:N_EXPERTS * win, :])
        o_ref[...] = _rms(out, g_ref[...]) if final else out
    for_tile_window(j, add_rows)


def _combine(x, ye, posm, first, count, row0, nrows, final_g=None):
    cap = CAPACITY_FACTOR * nrows // N_EXPERTS
    nt = nrows // TT
    blk0 = row0 // TT
    final = final_g is not None
    big = _window_class(count, COMB_WINDOWS, COMB_ALIGN)
    gain = final_g if final else jnp.ones((1, D_MODEL), f32)
    return pl.pallas_call(
        functools.partial(_combine_kernel, cap=cap, ntiles=nt, final=final),
        grid_spec=pltpu.PrefetchScalarGridSpec(
            num_scalar_prefetch=2, grid=(nt,),
            in_specs=[pl.BlockSpec((N_EXPERTS, TT), lambda j, *_: (0, j)),
                      pl.BlockSpec((TT, D_MODEL), lambda j, *_: (blk0 + j, 0)),
                      pl.BlockSpec(memory_space=pl.ANY),
                      pl.BlockSpec((1, D_MODEL), lambda j, *_: (0, 0))],
            out_specs=pl.BlockSpec((TT, D_MODEL), lambda j, *_: ((0 if final else blk0) + j, 0)),
            scratch_shapes=[pltpu.VMEM((COMB_BUFS, N_EXPERTS * COMB_WINDOWS[-1], D_MODEL), bf16),
                            pltpu.SemaphoreType.DMA((COMB_BUFS,))],
        ),
        out_shape=jax.ShapeDtypeStruct((nrows, D_MODEL) if final else x.shape, f32),
        input_output_aliases={} if final else {3: 0},
        name="combine",
        compiler_params=pltpu.CompilerParams(dimension_semantics=("arbitrary",), vmem_limit_bytes=VMEM_LIMIT),
    )(big, first.reshape(-1), posm, x, ye, gain)


def _stack_kernel(a_ref, b_ref, o_ref, *, na_blocks):
    o_ref[...] = jnp.where(pl.program_id(0) < na_blocks, a_ref[...], b_ref[...])


def _stack_rows(a, b):
    na, nb = a.shape[0] // TM, b.shape[0] // TM
    return pl.pallas_call(
        functools.partial(_stack_kernel, na_blocks=na), grid=(na + nb,),
        in_specs=[pl.BlockSpec((TM, D_MODEL), lambda i: (jnp.minimum(i, na - 1), 0)),
                  pl.BlockSpec((TM, D_MODEL), lambda i: (jnp.maximum(i - na, 0), 0))],
        out_specs=pl.BlockSpec((TM, D_MODEL), lambda i: (i, 0)),
        out_shape=jax.ShapeDtypeStruct((a.shape[0] + b.shape[0], D_MODEL), a.dtype),
        name="stack_rows",
        compiler_params=pltpu.CompilerParams(dimension_semantics=("arbitrary",)),
    )(a, b)


def _natten_bias_table(rpb):
    c = np.arange(GRID_W)
    cs = np.clip(c - WIN_C // 2, 0, GRID_W - WIN_C)
    kc = np.arange(GRID_W)
    col_mask = (kc[None, :] >= cs[:, None]) & (kc[None, :] < cs[:, None] + WIN_C)
    dc = np.clip(kc[None, :] - c[:, None], -(WIN_C - 1), WIN_C - 1) + (WIN_C - 1)
    tb = jnp.where(col_mask[None, None], rpb[:, :, dc].astype(f32), NEG)
    dw = np.arange(WIN_R)[:, None] + np.arange(WIN_R)[None, :]
    t = tb[:, dw]
    t = t.transpose(1, 0, 3, 2, 4).reshape(WIN_R, C_HEADS * GRID_W, WIN_R * GRID_W)
    return t


def _rope_slabs():
    inv = 1.0 / (ROPE_THETA ** (jnp.arange(0, ROPE_DIM, 2, dtype=f32) / ROPE_DIM))
    ang = jnp.arange(SEQ, dtype=f32)[:, None] * inv[None, :]
    cos, sin = jnp.cos(ang), jnp.sin(ang)
    ones = jnp.ones((SEQ, NOPE_DIM), f32)
    zpad = jnp.zeros((SEQ, LANES - NOPE_DIM - ROPE_DIM), f32)
    cosq = jnp.concatenate([ones, cos, cos, zpad], axis=1)
    sinq = jnp.concatenate([0.0 * ones, sin, sin, zpad], axis=1)
    return cosq, sinq


def _pack_layer(i, norm1_g, w_in, a_vnorm_g, a_ws, a_bs, b_conv, c_rpb, d_qnorm_g, d_kvnorm_g,
                d_w_uq, d_w_ukv, out_norm_g, w_o, norm2_g, router_w, e_w_gate, e_w_up, e_w_down):
    half = ROPE_DIM // 2
    w = w_in[i]
    kr0 = OFF_D + Q_RANK + KV_RANK
    kr = w[:, kr0:kr0 + ROPE_DIM]
    z64 = jnp.zeros((D_MODEL, NOPE_DIM), f32)
    z32 = jnp.zeros((D_MODEL, LANES - NOPE_DIM - ROPE_DIM), f32)
    kr_slab = jnp.concatenate([z64, kr, z32], axis=1)
    kr_rot = jnp.concatenate([z64, -kr[:, half:], kr[:, :half], z32], axis=1)
    w_pack = jnp.concatenate([w[:, :kr0], kr_slab, kr_rot], axis=1).astype(bf16)

    uq = d_w_uq[i].reshape(Q_RANK, D_HEADS, NOPE_DIM + ROPE_DIM)
    nope, rope = uq[..., :NOPE_DIM], uq[..., NOPE_DIM:]
    zq = jnp.zeros((Q_RANK, D_HEADS, LANES - NOPE_DIM - ROPE_DIM), f32)
    w_uq = jnp.concatenate([nope, rope, zq], axis=-1).reshape(Q_RANK, D_HEADS * LANES)
    w_uq_rot = jnp.concatenate([0.0 * nope, -rope[..., half:], rope[..., :half], zq],
                               axis=-1).reshape(Q_RANK, D_HEADS * LANES)
    ukv = d_w_ukv[i].reshape(KV_RANK, D_HEADS, NOPE_DIM + V_DIM)
    w_k = jnp.concatenate([ukv[..., :NOPE_DIM], jnp.zeros((KV_RANK, D_HEADS, LANES - NOPE_DIM), f32)],
                          axis=-1).reshape(KV_RANK, D_HEADS * LANES)
    w_v = ukv[..., NOPE_DIM:].reshape(KV_RANK, D_HEADS * V_DIM)
    rw = jnp.concatenate([router_w[i], jnp.zeros((D_MODEL, LANES - N_EXPERTS), f32)], axis=1)
    return {
        "g1": norm1_g[i].reshape(1, D_MODEL),
        "w_in": w_pack,
        "a_vg": a_vnorm_g[i].reshape(1, GROUP_W),
        "a_ws": a_ws[i].astype(bf16),
        "a_bias": jnp.repeat(a_bs[i].T, HEAD_DIM, axis=1),
        "b_conv": b_conv[i],
        "c_bias": _natten_bias_table(c_rpb[i]),
        "d_qg": d_qnorm_g[i].reshape(1, Q_RANK),
        "d_kvg": d_kvnorm_g[i].reshape(1, KV_RANK),
        "w_uq": w_uq.astype(bf16),
        "w_uq_rot": w_uq_rot.astype(bf16),
        "w_k": w_k.astype(bf16),
        "w_v": w_v.astype(bf16),
        "og": out_norm_g[i].reshape(1, D_MODEL),
        "w_o": w_o[i].astype(bf16),
        "g2": norm2_g[i].reshape(1, D_MODEL),
        "rw": rw.astype(bf16),
        "rwt": router_w[i].T.astype(bf16),
        "layer": i,
        "e_wg": e_w_gate,
        "e_wu": e_w_up,
        "e_wd": e_w_down,
    }


def _moe(xn, h2, aff, lw, tri, final_g=None):
    x = xn
    outs = []
    for row0, nrows in ((0, BATCH * SEQ), (BATCH * SEQ, DEC_BATCH * SEQ)):
        cap = CAPACITY_FACTOR * nrows // N_EXPERTS
        posm, base = _route(aff, row0, nrows, tri)
        first = base[:, :, 0].astype(jnp.int32)
        count = jnp.diff(first, axis=0, append=jnp.full((1, N_EXPERTS), cap, jnp.int32))
        xe = _dispatch(h2, posm, first, count, row0, nrows)
        ye = _ffn(xe, lw, nrows)
        if final_g is None:
            x = _combine(x, ye, posm, first, count, row0, nrows)
        else:
            outs.append(_combine(xn, ye, posm, first, count, row0, nrows, final_g))
    return x if final_g is None else outs


def kernel(x_prompt, x_sample, norm1_g, w_in, a_vnorm_g, a_ws, a_bs, b_conv, c_rpb, d_qnorm_g, d_kvnorm_g, d_w_uq, d_w_ukv, out_norm_g, w_o, norm2_g, router_w, e_w_gate, e_w_up, e_w_down, final_norm_g):
    n_p = BATCH * SEQ
    n_s = DEC_BATCH * SEQ
    x = _stack_rows(x_prompt.reshape(n_p, D_MODEL), x_sample.reshape(n_s, D_MODEL))
    cosq, sinq = _rope_slabs()
    tri = jnp.asarray(np.triu(np.ones((TT, TT), np.float32), 1), bf16)
    for i in range(DEPTH):
        lw = _pack_layer(i, norm1_g, w_in, a_vnorm_g, a_ws, a_bs, b_conv, c_rpb, d_qnorm_g, d_kvnorm_g,
                         d_w_uq, d_w_ukv, out_norm_g, w_o, norm2_g, router_w, e_w_gate, e_w_up, e_w_down)
        ya, yb, zc, qt, k4, vt = _inproj(x, lw, cosq, sinq)
        yc = _natten(zc, lw["c_bias"])
        yd = _mla(qt, k4, vt)
        xn, h2, aff = _outproj(ya, yb, yc, yd, x, lw)
        x = _moe(xn, h2, aff, lw, tri, final_norm_g.reshape(1, D_MODEL) if i == DEPTH - 1 else None)
    y_p, y_s = x
    return (y_p.reshape(BATCH, SEQ, D_MODEL), y_s.reshape(DEC_BATCH, SEQ, D_MODEL))
```
